```python
import jax
import jax.numpy as jnp
from jax import lax
import numpy as np

D_MODEL = 1024
BATCH = 2
SEQ = 8192
DEPTH = 1

GRID_W = 64
Q_BLOCK = 128
ROPE_THETA = 10000.0
NORM_EPS = 1e-6

MLA_HEADS = 8
MLA_Q_LORA = 256
MLA_KV_LORA = 128
MLA_NOPE = 64
MLA_ROPE = 32
MLA_V = 64
MLA_WIDTH = MLA_HEADS * MLA_V

GQA_HEADS = 8
GQA_KV_HEADS = 2
GQA_HEAD_DIM = 64
GQA_GROUP = GQA_HEADS // GQA_KV_HEADS
GQA_WIDTH = GQA_HEADS * GQA_HEAD_DIM
GQA_KV_WIDTH = GQA_KV_HEADS * GQA_HEAD_DIM

MIX_WIDTH = MLA_WIDTH + GQA_WIDTH
IN_COLS = MLA_Q_LORA + MLA_KV_LORA + MLA_ROPE + GQA_WIDTH + 2 * GQA_KV_WIDTH
IN_SPLITS = (MLA_Q_LORA,
             MLA_Q_LORA + MLA_KV_LORA,
             MLA_Q_LORA + MLA_KV_LORA + MLA_ROPE,
             MLA_Q_LORA + MLA_KV_LORA + MLA_ROPE + GQA_WIDTH,
             MLA_Q_LORA + MLA_KV_LORA + MLA_ROPE + GQA_WIDTH + GQA_KV_WIDTH)

N_EXPERTS = 256
TOP_K = 8
N_GROUPS = 8
TOPK_GROUPS = 4
EXPERT_FF = 256
SHARED_FF = 256
ROUTED_SCALE = 2.5
EXPERT_BLOCK = 128

N_MOD = 6

kernel_name = "hybrid_mla_gqa_moe_encoder_layer"


def rms_norm(x, g):
    xf = x.astype(jnp.float32)
    y = xf * lax.rsqrt(jnp.mean(xf * xf, axis=-1, keepdims=True) + NORM_EPS)
    return (y * g.astype(jnp.float32)).astype(x.dtype)


def swiglu(x, w_gate, w_up, w_down):
    return (jax.nn.silu(x @ w_gate) * (x @ w_up)) @ w_down


def axial_rope_tables(n, dim):
    rows = n // GRID_W
    row = jnp.broadcast_to(jnp.arange(rows, dtype=jnp.float32)[:, None], (rows, GRID_W)).reshape(n)
    col = jnp.broadcast_to(jnp.arange(GRID_W, dtype=jnp.float32)[None, :], (rows, GRID_W)).reshape(n)
    n_freq = dim // 4
    inv_freq = ROPE_THETA ** (-jnp.arange(n_freq, dtype=jnp.float32) / n_freq)
    ang = jnp.concatenate([row[:, None] * inv_freq, col[:, None] * inv_freq], axis=-1)
    return jnp.cos(ang), jnp.sin(ang)


def apply_rope(x, cos, sin):
    d = x.shape[-1]
    xp = x.reshape(*x.shape[:-1], d // 2, 2)
    bshape = (1, cos.shape[0]) + (1,) * (x.ndim - 3) + (cos.shape[1],)
    c = cos.reshape(bshape).astype(x.dtype)
    s = sin.reshape(bshape).astype(x.dtype)
    x0, x1 = xp[..., 0], xp[..., 1]
    return jnp.stack([x0 * c - x1 * s, x0 * s + x1 * c], axis=-1).reshape(x.shape)


def sweep_query_blocks(block_fn, *qs):
    b, s = qs[0].shape[:2]
    nb = s // Q_BLOCK
    blocked = tuple(jnp.moveaxis(q.reshape(b, nb, Q_BLOCK, *q.shape[2:]), 1, 0) for q in qs)
    out = lax.map(lambda args: block_fn(*args), blocked)
    out = jnp.moveaxis(out, 0, 1)
    return out.reshape(b, s, *out.shape[3:])


def mla_attention(q_lat, kv_lat, k_rope, g_q_a, w_q_b, g_kv_a, w_kv_b, cos, sin):
    b, s, _ = q_lat.shape
    q = (rms_norm(q_lat, g_q_a) @ w_q_b).reshape(b, s, MLA_HEADS, MLA_NOPE + MLA_ROPE)
    q_nope = q[..., :MLA_NOPE]
    q_pe = apply_rope(q[..., MLA_NOPE:], cos, sin)
    kv = (rms_norm(kv_lat, g_kv_a) @ w_kv_b).reshape(b, s, MLA_HEADS, MLA_NOPE + MLA_V)
    k_nope, v = kv[..., :MLA_NOPE], kv[..., MLA_NOPE:]
    k_pe = apply_rope(k_rope, cos, sin)
    scale = (MLA_NOPE + MLA_ROPE) ** -0.5

    def block(qn, qr):
        sc = (jnp.einsum('bqhd,bkhd->bhqk', qn, k_nope)
              + jnp.einsum('bqhr,bkr->bhqk', qr, k_pe))
        p = jax.nn.softmax(sc.astype(jnp.float32) * scale, axis=-1).astype(v.dtype)
        return jnp.einsum('bhqk,bkhd->bqhd', p, v)

    o = sweep_query_blocks(block, q_nope, q_pe)
    return o.reshape(b, s, MLA_WIDTH)


def gqa_attention(q, k, v, g_q_head, g_k_head, cos, sin):
    b, s, _ = q.shape
    q = rms_norm(q.reshape(b, s, GQA_HEADS, GQA_HEAD_DIM), g_q_head)
    q = apply_rope(q, cos, sin).reshape(b, s, GQA_KV_HEADS, GQA_GROUP, GQA_HEAD_DIM)
    k = apply_rope(rms_norm(k.reshape(b, s, GQA_KV_HEADS, GQA_HEAD_DIM), g_k_head), cos, sin)
    v = v.reshape(b, s, GQA_KV_HEADS, GQA_HEAD_DIM)
    scale = GQA_HEAD_DIM ** -0.5

    def block(qb):
        sc = jnp.einsum('bqhgd,bkhd->bhgqk', qb, k)
        p = jax.nn.softmax(sc.astype(jnp.float32) * scale, axis=-1).astype(v.dtype)
        return jnp.einsum('bhgqk,bkhd->bqhgd', p, v)

    o = sweep_query_blocks(block, q)
    return o.reshape(b, s, GQA_WIDTH)


def routed_experts(t, top_idx, top_w, w_gate_e, w_up_e, w_down_e):
    n, d = t.shape
    a = n * TOP_K
    flat_e = top_idx.reshape(a).astype(jnp.int32)
    flat_tok = jnp.repeat(jnp.arange(n, dtype=jnp.int32), TOP_K)
    flat_w = top_w.reshape(a).astype(t.dtype)
    counts = jnp.zeros((N_EXPERTS,), jnp.int32).at[flat_e].add(1)
    padded = (counts + EXPERT_BLOCK - 1) // EXPERT_BLOCK * EXPERT_BLOCK
    pad_end = jnp.cumsum(padded)
    pad_start = pad_end - padded
    raw_start = jnp.cumsum(counts) - counts
    order = jnp.argsort(flat_e)
    sorted_e = flat_e[order]
    dest = pad_start[sorted_e] + (jnp.arange(a, dtype=jnp.int32) - raw_start[sorted_e])
    n_blocks = (a + EXPERT_BLOCK - 1) // EXPERT_BLOCK + N_EXPERTS
    p_rows = n_blocks * EXPERT_BLOCK
    buf_tok = jnp.full((p_rows,), n, jnp.int32).at[dest].set(flat_tok[order])
    buf_w = jnp.zeros((p_rows,), t.dtype).at[dest].set(flat_w[order])
    block_starts = jnp.arange(n_blocks, dtype=jnp.int32) * EXPERT_BLOCK
    block_expert = jnp.minimum(jnp.searchsorted(pad_end, block_starts, side='right'),
                               N_EXPERTS - 1).astype(jnp.int32)
    t_pad = jnp.concatenate([t, jnp.zeros((1, d), t.dtype)], axis=0)

    def expert_block(args):
        tok, e, wts = args
        xb = t_pad[tok]
        return swiglu(xb, w_gate_e[e], w_up_e[e], w_down_e[e]) * wts[:, None]

    out = lax.map(expert_block, (buf_tok.reshape(n_blocks, EXPERT_BLOCK), block_expert,
                                 buf_w.reshape(n_blocks, EXPERT_BLOCK)))
    return jnp.zeros((n + 1, d), t.dtype).at[buf_tok].add(out.reshape(p_rows, d))[:n]


def moe_ffn(h, w_router, b_router, w_gate_e, w_up_e, w_down_e, w_gate_s, w_up_s, w_down_s):
    b, s, d = h.shape
    n = b * s
    t = h.reshape(n, d)
    scores = jax.nn.sigmoid(t.astype(jnp.float32) @ w_router.astype(jnp.float32))
    biased = scores + b_router.astype(jnp.float32)
    grouped = biased.reshape(n, N_GROUPS, N_EXPERTS // N_GROUPS)
    group_score = lax.top_k(grouped, 2)[0].sum(axis=-1)
    _, top_groups = lax.top_k(group_score, TOPK_GROUPS)
    group_mask = jnp.any(top_groups[:, :, None] == jnp.arange(N_GROUPS)[None, None, :], axis=1)
    expert_mask = jnp.repeat(group_mask, N_EXPERTS // N_GROUPS, axis=1)
    _, top_idx = lax.top_k(jnp.where(expert_mask, biased, -jnp.inf), TOP_K)
    top_w = jnp.take_along_axis(scores, top_idx, axis=1)
    top_w = top_w / jnp.sum(top_w, axis=-1, keepdims=True) * ROUTED_SCALE
    routed = routed_experts(t, top_idx, top_w, w_gate_e, w_up_e, w_down_e)
    shared = swiglu(t, w_gate_s, w_up_s, w_down_s)
    return (routed + shared).reshape(b, s, d)


def hybrid_layer(x, c, w_ada, b_ada, g_pre_mix, g_post_mix, g_pre_ffn, g_post_ffn, w_in,
                 g_q_a, w_q_b, g_kv_a, w_kv_b, g_q_head, g_k_head, g_mla_out, g_gqa_out,
                 w_out, w_router, b_router, w_gate_e, w_up_e, w_down_e,
                 w_gate_s, w_up_s, w_down_s):
    b, s, _ = x.shape
    mod = (jax.nn.silu(c) @ w_ada + b_ada)[:, None, :]
    shift_m, scale_m, gate_m, shift_f, scale_f, gate_f = jnp.split(mod, N_MOD, axis=-1)

    h = rms_norm(x, g_pre_mix) * (1 + scale_m) + shift_m
    proj = h @ w_in
    q_lat, kv_lat, k_rope, q_g, k_g, v_g = jnp.split(proj, IN_SPLITS, axis=-1)
    cos_m, sin_m = axial_rope_tables(s, MLA_ROPE)
    cos_g, sin_g = axial_rope_tables(s, GQA_HEAD_DIM)
    o_mla = mla_attention(q_lat, kv_lat, k_rope, g_q_a, w_q_b, g_kv_a, w_kv_b, cos_m, sin_m)
    o_gqa = gqa_attention(q_g, k_g, v_g, g_q_head, g_k_head, cos_g, sin_g)
    mixed = jnp.concatenate([rms_norm(o_mla, g_mla_out), rms_norm(o_gqa, g_gqa_out)],
                            axis=-1) @ w_out
    x = x + gate_m * rms_norm(mixed, g_post_mix)

    h = rms_norm(x, g_pre_ffn) * (1 + scale_f) + shift_f
    y = moe_ffn(h, w_router, b_router, w_gate_e, w_up_e, w_down_e, w_gate_s, w_up_s, w_down_s)
    return x + gate_f * rms_norm(y, g_post_ffn)


def setup_inputs(seed: int = 0) -> dict:
    key = jax.random.key(seed)
    ks = jax.random.split(key, 26)
    L, D = DEPTH, D_MODEL

    def nrm(k, shape, scale):
        return jax.random.normal(k, shape, jnp.float32) * scale

    def gain(k, shape):
        return 1.0 + 0.02 * jax.random.normal(k, shape, jnp.float32)

    return {
        "x": nrm(ks[0], (BATCH, SEQ, D), 1.0),
        "c": nrm(ks[1], (BATCH, D), 1.0),
        "w_ada": nrm(ks[2], (L, D, N_MOD * D), 0.5 * D ** -0.5),
        "b_ada": nrm(ks[3], (L, N_MOD * D), 0.02),
        "g_pre_mix": gain(ks[4], (L, D)),
        "g_post_mix": gain(ks[5], (L, D)),
        "g_pre_ffn": gain(ks[6], (L, D)),
        "g_post_ffn": gain(ks[7], (L, D)),
        "w_in": nrm(ks[8], (L, D, IN_COLS), D ** -0.5),
        "g_q_a": gain(ks[9], (L, MLA_Q_LORA)),
        "w_q_b": nrm(ks[10], (L, MLA_Q_LORA, MLA_HEADS * (MLA_NOPE + MLA_ROPE)), MLA_Q_LORA ** -0.5),
        "g_kv_a": gain(ks[11], (L, MLA_KV_LORA)),
        "w_kv_b": nrm(ks[12], (L, MLA_KV_LORA, MLA_HEADS * (MLA_NOPE + MLA_V)), MLA_KV_LORA ** -0.5),
        "g_q_head": gain(ks[13], (L, GQA_HEAD_DIM)),
        "g_k_head": gain(ks[14], (L, GQA_HEAD_DIM)),
        "g_mla_out": gain(ks[15], (L, MLA_WIDTH)),
        "g_gqa_out": gain(ks[16], (L, GQA_WIDTH)),
        "w_out": nrm(ks[17], (L, MIX_WIDTH, D), MIX_WIDTH ** -0.5),
        "w_router": nrm(ks[18], (L, D, N_EXPERTS), D ** -0.5),
        "b_router": nrm(ks[19], (L, N_EXPERTS), 0.01),
        "w_gate_e": nrm(ks[20], (L, N_EXPERTS, D, EXPERT_FF), D ** -0.5),
        "w_up_e": nrm(ks[21], (L, N_EXPERTS, D, EXPERT_FF), D ** -0.5),
        "w_down_e": nrm(ks[22], (L, N_EXPERTS, EXPERT_FF, D), EXPERT_FF ** -0.5),
        "w_gate_s": nrm(ks[23], (L, D, SHARED_FF), D ** -0.5),
        "w_up_s": nrm(ks[24], (L, D, SHARED_FF), D ** -0.5),
        "w_down_s": nrm(ks[25], (L, SHARED_FF, D), SHARED_FF ** -0.5),
    }


def reference(x, c, w_ada, b_ada, g_pre_mix, g_post_mix, g_pre_ffn, g_post_ffn, w_in,
              g_q_a, w_q_b, g_kv_a, w_kv_b, g_q_head, g_k_head, g_mla_out, g_gqa_out,
              w_out, w_router, b_router, w_gate_e, w_up_e, w_down_e,
              w_gate_s, w_up_s, w_down_s):
    for l in range(DEPTH):
        x = hybrid_layer(x, c, w_ada[l], b_ada[l], g_pre_mix[l], g_post_mix[l], g_pre_ffn[l],
                         g_post_ffn[l], w_in[l], g_q_a[l], w_q_b[l], g_kv_a[l], w_kv_b[l],
                         g_q_head[l], g_k_head[l], g_mla_out[l], g_gqa_out[l], w_out[l],
                         w_router[l], b_router[l], w_gate_e[l], w_up_e[l], w_down_e[l],
                         w_gate_s[l], w_up_s[l], w_down_s[l])
    return x
```

```python
import functools

import jax
import jax.numpy as jnp
import numpy as np
from jax import lax
from jax.experimental import pallas as pl
from jax.experimental.pallas import tpu as pltpu

D_MODEL = 1024
GRID_W = 64
ROPE_THETA = 10000.0
NORM_EPS = 1e-6

MLA_HEADS = 8
MLA_Q_LORA = 256
MLA_KV_LORA = 128
MLA_NOPE = 64
MLA_ROPE = 32
MLA_V = 64

GQA_HEADS = 8
GQA_KV_HEADS = 2
GQA_HEAD_DIM = 64
GQA_GROUP = GQA_HEADS // GQA_KV_HEADS

N_HEADS = MLA_HEADS + GQA_HEADS
N_KV = MLA_HEADS + GQA_KV_HEADS
HEAD_PAD = 128
V_DIM = 64

N_EXPERTS = 256
TOP_K = 8
N_GROUPS = 8
GROUP_SIZE = N_EXPERTS // N_GROUPS
TOPK_GROUPS = 4
EXPERT_FF = 256
ROUTED_SCALE = 2.5
N_MOD = 6

OFF_QLAT = 0
OFF_KVLAT = OFF_QLAT + MLA_Q_LORA
OFF_KROPE = OFF_KVLAT + MLA_KV_LORA
OFF_QG = OFF_KROPE + MLA_ROPE
OFF_KG = OFF_QG + GQA_HEADS * GQA_HEAD_DIM
OFF_VG = OFF_KG + GQA_KV_HEADS * GQA_HEAD_DIM
IN_COLS = OFF_VG + GQA_KV_HEADS * GQA_HEAD_DIM

T_PROJ = 512
T_Q = 256
T_POST = 512
T_ROW = 256
T_M = 256
PACK_ROWS = 4
VMEM_LIMIT = 56 * 1024 * 1024

_NT = (((1,), (1,)), ((), ()))
_F32 = jnp.float32
_BF16 = jnp.bfloat16
_U32 = jnp.uint32
_HI_MASK = np.uint32(0xFFFF0000)


def _rsqrt_mean_sq(v, axis):
    return lax.rsqrt(jnp.mean(v * v, axis=axis, keepdims=True) + NORM_EPS)


def _pack_pair(lo, hi):
    lo_b = pltpu.bitcast(lo.astype(_BF16).astype(_F32), _U32) >> 16
    hi_b = pltpu.bitcast(hi.astype(_BF16).astype(_F32), _U32) & _HI_MASK
    return lo_b | hi_b


def _unpack_pair(u):
    return pltpu.bitcast(u << 16, _F32), pltpu.bitcast(u & _HI_MASK, _F32)


def _ada_kernel(c_ref, w_ref, b_ref, o_ref):
    c = c_ref[...]
    a = (c * jax.nn.sigmoid(c)).astype(_BF16)
    o_ref[...] = jnp.dot(a, w_ref[...].astype(_BF16), preferred_element_type=_F32) + b_ref[...]


def _ada_mod(c, w_ada, b_ada):
    b, d = c.shape
    cols = w_ada.shape[1]
    rows = 8
    tn = 1536
    c_pad = jnp.zeros((rows, d), _F32).at[:b].set(c)
    out = pl.pallas_call(
        _ada_kernel,
        out_shape=jax.ShapeDtypeStruct((rows, cols), _F32),
        grid=(cols // tn,),
        in_specs=[pl.BlockSpec((rows, d), lambda j: (0, 0)),
                  pl.BlockSpec((d, tn), lambda j: (0, j)),
                  pl.BlockSpec((1, tn), lambda j: (0, j))],
        out_specs=pl.BlockSpec((rows, tn), lambda j: (0, j)),
        compiler_params=pltpu.CompilerParams(vmem_limit_bytes=VMEM_LIMIT),
        name="ada_mod",
    )(c_pad, w_ada, b_ada.reshape(1, cols))
    return out[:b].reshape(b, N_MOD, d)


def _rope(x0, x1, cos, sin):
    return x0 * cos - x1 * sin, x0 * sin + x1 * cos


def _proj_kernel(x_ref, mod_ref, gpre_ref, win_ref, gqa_ref, wqb_ref, gkva_ref, wkvb_ref,
                 gqh_ref, gkh_ref, cosm_ref, sinm_ref, cosg_ref, sing_ref,
                 qT_ref, k_ref, vT_ref):
    x = x_ref[0]
    t = x.shape[0]
    shift = mod_ref[0, 0:1, :]
    scale = mod_ref[0, 1:2, :]
    h = x * _rsqrt_mean_sq(x, -1) * gpre_ref[...] * (1.0 + scale) + shift
    pT = lax.dot_general(win_ref[...], h.astype(_BF16), _NT, preferred_element_type=_F32)

    cm, sm = cosm_ref[...], sinm_ref[...]
    cg, sg = cosg_ref[...], sing_ref[...]
    half_m = MLA_ROPE // 2
    half_g = GQA_HEAD_DIM // 2

    ql = pT[OFF_QLAT:OFF_QLAT + MLA_Q_LORA]
    qn = (ql * _rsqrt_mean_sq(ql, 0) * gqa_ref[...]).astype(_BF16)
    qT = jnp.dot(wqb_ref[...], qn, preferred_element_type=_F32)
    sc_m = float((MLA_NOPE + MLA_ROPE) ** -0.5)
    n_nope = MLA_HEADS * MLA_NOPE
    n_half = MLA_HEADS * half_m
    q0, q1 = _rope(qT[n_nope:n_nope + n_half], qT[n_nope + n_half:],
                   jnp.tile(cm, (MLA_HEADS, 1)), jnp.tile(sm, (MLA_HEADS, 1)))
    zq_m = jnp.zeros((HEAD_PAD - MLA_NOPE - MLA_ROPE, t), _BF16)
    for hd in range(MLA_HEADS):
        qT_ref[0, hd, 0:MLA_NOPE, :] = (qT[hd * MLA_NOPE:(hd + 1) * MLA_NOPE] * sc_m).astype(_BF16)
        qT_ref[0, hd, MLA_NOPE:MLA_NOPE + half_m, :] = (q0[hd * half_m:(hd + 1) * half_m] * sc_m).astype(_BF16)
        qT_ref[0, hd, MLA_NOPE + half_m:MLA_NOPE + MLA_ROPE, :] = (
            q1[hd * half_m:(hd + 1) * half_m] * sc_m).astype(_BF16)
        qT_ref[0, hd, MLA_NOPE + MLA_ROPE:, :] = zq_m

    kl = pT[OFF_KVLAT:OFF_KVLAT + MLA_KV_LORA]
    kn = (kl * _rsqrt_mean_sq(kl, 0) * gkva_ref[...]).astype(_BF16)
    kvT = jnp.dot(wkvb_ref[...], kn, preferred_element_type=_F32)
    k0, k1 = _rope(pT[OFF_KROPE:OFF_KROPE + half_m], pT[OFF_KROPE + half_m:OFF_KROPE + MLA_ROPE], cm, sm)
    zk_m = jnp.zeros((HEAD_PAD - MLA_NOPE - MLA_ROPE, t), _F32)
    for hd in range(MLA_HEADS):
        kTh = jnp.concatenate([kvT[hd * MLA_NOPE:(hd + 1) * MLA_NOPE], k0, k1, zk_m], axis=0)
        k_ref[0, hd] = kTh.T.astype(_BF16)
        vT_ref[0, hd, 0] = kvT[n_nope + hd * MLA_V:n_nope + (hd + 1) * MLA_V].astype(_BF16)

    sc_g = float(GQA_HEAD_DIM ** -0.5)
    zq_g = jnp.zeros((HEAD_PAD - GQA_HEAD_DIM, t), _BF16)
    for hd in range(GQA_HEADS):
        blk = pT[OFF_QG + hd * GQA_HEAD_DIM:OFF_QG + (hd + 1) * GQA_HEAD_DIM]
        bn = blk * _rsqrt_mean_sq(blk, 0) * gqh_ref[...]
        g0, g1 = _rope(bn[:half_g], bn[half_g:], cg, sg)
        qT_ref[0, MLA_HEADS + hd, 0:half_g, :] = (g0 * sc_g).astype(_BF16)
        qT_ref[0, MLA_HEADS + hd, half_g:GQA_HEAD_DIM, :] = (g1 * sc_g).astype(_BF16)
        qT_ref[0, MLA_HEADS + hd, GQA_HEAD_DIM:, :] = zq_g

    zk_g = jnp.zeros((HEAD_PAD - GQA_HEAD_DIM, t), _F32)
    for hd in range(GQA_KV_HEADS):
        blk = pT[OFF_KG + hd * GQA_HEAD_DIM:OFF_KG + (hd + 1) * GQA_HEAD_DIM]
        bn = blk * _rsqrt_mean_sq(blk, 0) * gkh_ref[...]
        g0, g1 = _rope(bn[:half_g], bn[half_g:], cg, sg)
        kTh = jnp.concatenate([g0, g1, zk_g], axis=0)
        k_ref[0, MLA_HEADS + hd] = kTh.T.astype(_BF16)
        vT_ref[0, MLA_HEADS + hd, 0] = pT[OFF_VG + hd * V_DIM:OFF_VG + (hd + 1) * V_DIM].astype(_BF16)


def _project(x, mod, gpre, winT, gqa, wqbT, gkva, wkvbT, gqh, gkh, cosm, sinm, cosg, sing):
    b, s, d = x.shape
    t = T_PROJ
    nt = s // t
    full = lambda a: pl.BlockSpec(a.shape, lambda bi, i: (0,) * a.ndim)
    tab = lambda a: pl.BlockSpec((a.shape[0], t), lambda bi, i: (0, i))
    return pl.pallas_call(
        _proj_kernel,
        out_shape=(jax.ShapeDtypeStruct((b, N_HEADS, HEAD_PAD, s), _BF16),
                   jax.ShapeDtypeStruct((b, N_KV, s, HEAD_PAD), _BF16),
                   jax.ShapeDtypeStruct((b, N_KV, nt, V_DIM, t), _BF16)),
        grid=(b, nt),
        in_specs=[pl.BlockSpec((1, t, d), lambda bi, i: (bi, i, 0)),
                  pl.BlockSpec((1, N_MOD, d), lambda bi, i: (bi, 0, 0)),
                  full(gpre), full(winT), full(gqa), full(wqbT), full(gkva), full(wkvbT),
                  full(gqh), full(gkh), tab(cosm), tab(sinm), tab(cosg), tab(sing)],
        out_specs=(pl.BlockSpec((1, N_HEADS, HEAD_PAD, t), lambda bi, i: (bi, 0, 0, i)),
                   pl.BlockSpec((1, N_KV, t, HEAD_PAD), lambda bi, i: (bi, 0, i, 0)),
                   pl.BlockSpec((1, N_KV, 1, V_DIM, t), lambda bi, i: (bi, 0, i, 0, 0))),
        compiler_params=pltpu.CompilerParams(
            dimension_semantics=("arbitrary", "arbitrary"), vmem_limit_bytes=VMEM_LIMIT),
        name="in_proj",
    )(x, mod, gpre, winT, gqa, wqbT, gkva, wkvbT, gqh, gkh, cosm, sinm, cosg, sing)


def _attn_kernel(qT_ref, k_ref, vT_ref, o_ref, *, n_chunks, tk):
    qT = qT_ref[0, 0]
    tq = qT.shape[1]

    def chunk(j, carry):
        m, l, acc = carry
        kc = k_ref[0, 0, pl.ds(pl.multiple_of(j * tk, tk), tk), :]
        s = jnp.dot(kc, qT, preferred_element_type=_F32)
        m_new = jnp.maximum(m, jnp.max(s, axis=0, keepdims=True))
        alpha = jnp.exp(m - m_new)
        p = jnp.exp(s - m_new)
        l = alpha * l + jnp.sum(p, axis=0, keepdims=True)
        acc = alpha * acc + jnp.dot(vT_ref[0, 0, j], p.astype(_BF16), preferred_element_type=_F32)
        return m_new, l, acc

    def pair(jj, carry):
        return chunk(2 * jj + 1, chunk(2 * jj, carry))

    init = (jnp.full((1, tq), -1e30, _F32), jnp.zeros((1, tq), _F32), jnp.zeros((V_DIM, tq), _F32))
    _, l, acc = lax.fori_loop(0, n_chunks // 2, pair, init)
    o_ref[0, 0] = acc / l


def _kv_index(h):
    return jnp.where(h < MLA_HEADS, h, MLA_HEADS + (h - MLA_HEADS) // GQA_GROUP)


def _attention(qT, k, vT):
    b, nh, dp, s = qT.shape
    n_chunks, tk = vT.shape[2], vT.shape[4]
    assert n_chunks % 2 == 0
    return pl.pallas_call(
        functools.partial(_attn_kernel, n_chunks=n_chunks, tk=tk),
        out_shape=jax.ShapeDtypeStruct((b, nh, V_DIM, s), _F32),
        grid=(b, nh, s // T_Q),
        in_specs=[pl.BlockSpec((1, 1, dp, T_Q), lambda bi, h, i: (bi, h, 0, i)),
                  pl.BlockSpec((1, 1, s, dp), lambda bi, h, i: (bi, _kv_index(h), 0, 0)),
                  pl.BlockSpec((1, 1, n_chunks, V_DIM, tk), lambda bi, h, i: (bi, _kv_index(h), 0, 0, 0))],
        out_specs=pl.BlockSpec((1, 1, V_DIM, T_Q), lambda bi, h, i: (bi, h, 0, i)),
        compiler_params=pltpu.CompilerParams(
            dimension_semantics=("arbitrary", "arbitrary", "arbitrary"), vmem_limit_bytes=VMEM_LIMIT),
        name="attention",
    )(qT, k, vT)


def _first_argmax(v, rows, n):
    m = jnp.max(v, axis=0, keepdims=True)
    idx = jnp.min(jnp.where(v == m, rows, n), axis=0, keepdims=True)
    return m, idx


def _post_kernel(x_ref, o_ref, mod_ref, gmla_ref, ggqa_ref, wout_ref, gpost_ref, gffn_ref,
                 wrh_ref, wrl_ref, br_ref,
                 x1_ref, hp_ref, idx_ref, w_ref, rank_ref, cnt_ref, carry_ref):
    first = jnp.logical_and(pl.program_id(0) == 0, pl.program_id(1) == 0)

    @pl.when(first)
    def _():
        carry_ref[...] = jnp.zeros_like(carry_ref)

    x = x_ref[0]
    t = x.shape[0]
    gate_m = mod_ref[0, 2:3, :]
    shift_f = mod_ref[0, 3:4, :]
    scale_f = mod_ref[0, 4:5, :]

    o = o_ref[0].reshape(N_HEADS * V_DIM, t)
    half = MLA_HEADS * V_DIM
    om, og = o[:half], o[half:]
    mixedT = jnp.concatenate([om * _rsqrt_mean_sq(om, 0) * gmla_ref[...],
                              og * _rsqrt_mean_sq(og, 0) * ggqa_ref[...]], axis=0).astype(_BF16)
    y = jnp.dot(wout_ref[...], mixedT, preferred_element_type=_F32).T
    x1 = x + gate_m * (y * _rsqrt_mean_sq(y, -1) * gpost_ref[...])
    x1_ref[0] = x1
    h2 = x1 * _rsqrt_mean_sq(x1, -1) * gffn_ref[...] * (1.0 + scale_f) + shift_f

    for c in range(PACK_ROWS):
        hp_ref[:, c, :] = _pack_pair(h2[:, 256 * c:256 * c + 128], h2[:, 256 * c + 128:256 * c + 256])

    hh = h2.astype(_BF16)
    hl = (h2 - hh.astype(_F32)).astype(_BF16)
    wrh = wrh_ref[...]
    logits = (lax.dot_general(wrh, hh, _NT, preferred_element_type=_F32)
              + lax.dot_general(wrh, hl, _NT, preferred_element_type=_F32)
              + lax.dot_general(wrl_ref[...], hh, _NT, preferred_element_type=_F32))
    scores = jax.nn.sigmoid(logits)
    biased = scores + br_ref[...]
    neg = -jnp.inf

    rows_g = lax.broadcasted_iota(jnp.int32, (GROUP_SIZE, t), 0)
    gscores = []
    for g in range(N_GROUPS):
        v = biased[g * GROUP_SIZE:(g + 1) * GROUP_SIZE]
        m1, i1 = _first_argmax(v, rows_g, GROUP_SIZE)
        m2 = jnp.max(jnp.where(rows_g == i1, neg, v), axis=0, keepdims=True)
        gscores.append(m1 + m2)
    cur = jnp.concatenate(gscores, axis=0)
    rows_8 = lax.broadcasted_iota(jnp.int32, (N_GROUPS, t), 0)
    gsel = jnp.zeros((N_GROUPS, t), _F32)
    for _ in range(TOPK_GROUPS):
        _, gi = _first_argmax(cur, rows_8, N_GROUPS)
        hit = rows_8 == gi
        gsel = jnp.where(hit, 1.0, gsel)
        cur = jnp.where(hit, neg, cur)
    emask = jnp.concatenate(
        [jnp.broadcast_to(gsel[g:g + 1], (GROUP_SIZE, t)) for g in range(N_GROUPS)], axis=0) > 0.5
    masked = jnp.where(emask, biased, neg)

    rows_e = lax.broadcasted_iota(jnp.int32, (N_EXPERTS, t), 0)
    idxs, ws = [], []
    for _ in range(TOP_K):
        _, ei = _first_argmax(masked, rows_e, N_EXPERTS)
        hit = rows_e == ei
        idxs.append(ei)
        ws.append(jnp.sum(jnp.where(hit, scores, 0.0), axis=0, keepdims=True))
        masked = jnp.where(hit, neg, masked)
    wsum = ws[0]
    for wk in ws[1:]:
        wsum = wsum + wk

    onehot = jnp.zeros((N_EXPERTS, t), _F32)
    for ei in idxs:
        onehot = onehot + jnp.where(rows_e == ei, 1.0, 0.0)
    upper = (lax.broadcasted_iota(jnp.int32, (t, t), 0) < lax.broadcasted_iota(jnp.int32, (t, t), 1))
    before = jnp.dot(onehot.astype(_BF16), jnp.where(upper, 1.0, 0.0).astype(_BF16),
                     preferred_element_type=_F32) + carry_ref[...]
    for kk in range(TOP_K):
        idx_ref[kk:kk + 1, :] = idxs[kk]
        w_ref[kk:kk + 1, :] = ws[kk] / wsum * ROUTED_SCALE
        rank_ref[kk:kk + 1, :] = jnp.sum(
            jnp.where(rows_e == idxs[kk], before, 0.0), axis=0, keepdims=True).astype(jnp.int32)
    total = carry_ref[...] + jnp.sum(onehot, axis=1, keepdims=True)
    carry_ref[...] = total
    cnt_ref[...] = jnp.broadcast_to(total, cnt_ref.shape)


def _post_attention(x, oT, mod, gmla, ggqa, woutT, gpost, gffn, wrh, wrl, br):
    b, s, d = x.shape
    n = b * s
    t = T_POST
    nt = s // t
    full = lambda a: pl.BlockSpec(a.shape, lambda bi, i: (0,) * a.ndim)
    tok = lambda bi, i: (0, bi * nt + i)
    return pl.pallas_call(
        _post_kernel,
        out_shape=(jax.ShapeDtypeStruct((b, s, d), _F32),
                   jax.ShapeDtypeStruct((n, PACK_ROWS, 128), _U32),
                   jax.ShapeDtypeStruct((TOP_K, n), jnp.int32),
                   jax.ShapeDtypeStruct((TOP_K, n), _F32),
                   jax.ShapeDtypeStruct((TOP_K, n), jnp.int32),
                   jax.ShapeDtypeStruct((N_EXPERTS, 128), _F32)),
        grid=(b, nt),
        in_specs=[pl.BlockSpec((1, t, d), lambda bi, i: (bi, i, 0)),
                  pl.BlockSpec((1, N_HEADS, V_DIM, t), lambda bi, i: (bi, 0, 0, i)),
                  pl.BlockSpec((1, N_MOD, d), lambda bi, i: (bi, 0, 0)),
                  full(gmla), full(ggqa), full(woutT), full(gpost), full(gffn),
                  full(wrh), full(wrl), full(br)],
        out_specs=(pl.BlockSpec((1, t, d), lambda bi, i: (bi, i, 0)),
                   pl.BlockSpec((t, PACK_ROWS, 128), lambda bi, i: (bi * nt + i, 0, 0)),
                   pl.BlockSpec((TOP_K, t), tok),
                   pl.BlockSpec((TOP_K, t), tok),
                   pl.BlockSpec((TOP_K, t), tok),
                   pl.BlockSpec((N_EXPERTS, 128), lambda bi, i: (0, 0))),
        scratch_shapes=[pltpu.VMEM((N_EXPERTS, 1), _F32)],
        compiler_params=pltpu.CompilerParams(
            dimension_semantics=("arbitrary", "arbitrary"), vmem_limit_bytes=VMEM_LIMIT),
        name="out_proj_router",
    )(x, oT, mod, gmla, ggqa, woutT, gpost, gffn, wrh, wrl, br)


def _row_copy(src_row, dst_row, sem):
    return pltpu.make_async_copy(src_row, dst_row, sem)


def _dispatch_kernel(dest_ref, h_ref, xs_in_ref, xs_ref, sem):
    del xs_in_ref
    t_rows = h_ref.shape[0]

    def issue(t, _):
        for kk in range(TOP_K):
            _row_copy(h_ref.at[t], xs_ref.at[dest_ref[0, 0, t * TOP_K + kk]], sem).start()
        return 0

    lax.fori_loop(0, t_rows, issue, 0)

    def drain(t, _):
        for kk in range(TOP_K):
            _row_copy(h_ref.at[0], xs_ref.at[0], sem).wait()
        return 0

    lax.fori_loop(0, t_rows, drain, 0)


def _dispatch(dest_tiles, hp, n_slots):
    n = hp.shape[0]
    t = T_ROW
    xs0 = jnp.zeros((n_slots, PACK_ROWS, 128), _U32)
    return pl.pallas_call(
        _dispatch_kernel,
        out_shape=jax.ShapeDtypeStruct((n_slots, PACK_ROWS, 128), _U32),
        grid=(n // t,),
        in_specs=[pl.BlockSpec((1, 1, t * TOP_K), lambda i: (i, 0, 0), memory_space=pltpu.SMEM),
                  pl.BlockSpec((t, PACK_ROWS, 128), lambda i: (i, 0, 0)),
                  pl.BlockSpec(memory_space=pl.ANY)],
        out_specs=pl.BlockSpec(memory_space=pl.ANY),
        scratch_shapes=[pltpu.SemaphoreType.DMA(())],
        input_output_aliases={2: 0},
        compiler_params=pltpu.CompilerParams(
            dimension_semantics=("arbitrary",), vmem_limit_bytes=VMEM_LIMIT),
        name="dispatch",
    )(dest_tiles, hp, xs0)


def _unpack_rows(ref, rows):
    parts = []
    for c in range(PACK_ROWS):
        lo, hi = _unpack_pair(ref[pl.ds(0, rows), c, :])
        parts += [lo.astype(_BF16), hi.astype(_BF16)]
    return jnp.concatenate(parts, axis=1)


def _expert_kernel(bexp_ref, bmap_ref, nused_ref, x_ref, wg_ref, wu_ref, wd_ref, y_ref, wgu_s, wd_s):
    del bmap_ref
    b = pl.program_id(0)
    e = bexp_ref[b]
    e_prev = bexp_ref[jnp.maximum(b - 1, 0)]

    @pl.when(jnp.logical_or(b == 0, e != e_prev))
    def _():
        wgu_s[:, :EXPERT_FF] = wg_ref[0].astype(_BF16)
        wgu_s[:, EXPERT_FF:] = wu_ref[0].astype(_BF16)
        wd_s[...] = wd_ref[0].astype(_BF16)

    @pl.when(b < nused_ref[0])
    def _():
        xb = _unpack_rows(x_ref, T_M)
        gu = jnp.dot(xb, wgu_s[...], preferred_element_type=_F32)
        g, u = gu[:, :EXPERT_FF], gu[:, EXPERT_FF:]
        mid = (g * jax.nn.sigmoid(g) * u).astype(_BF16)
        y = jnp.dot(mid, wd_s[...], preferred_element_type=_F32)
        for c in range(PACK_ROWS):
            y_ref[:, c, :] = _pack_pair(y[:, 256 * c:256 * c + 128], y[:, 256 * c + 128:256 * c + 256])

    @pl.when(b >= nused_ref[0])
    def _():
        y_ref[...] = jnp.zeros_like(y_ref)


def _experts(bexp, bmap, nused, xs, wg, wu, wd):
    n_slots = xs.shape[0]
    nb = n_slots // T_M
    d, ff = wg.shape[1], wg.shape[2]
    return pl.pallas_call(
        _expert_kernel,
        out_shape=jax.ShapeDtypeStruct((n_slots, PACK_ROWS, 128), _U32),
        grid_spec=pltpu.PrefetchScalarGridSpec(
            num_scalar_prefetch=3,
            grid=(nb,),
            in_specs=[pl.BlockSpec((T_M, PACK_ROWS, 128), lambda i, be, bm, nu: (bm[i], 0, 0)),
                      pl.BlockSpec((1, d, ff), lambda i, be, bm, nu: (be[i], 0, 0)),
                      pl.BlockSpec((1, d, ff), lambda i, be, bm, nu: (be[i], 0, 0)),
                      pl.BlockSpec((1, ff, d), lambda i, be, bm, nu: (be[i], 0, 0))],
            out_specs=pl.BlockSpec((T_M, PACK_ROWS, 128), lambda i, be, bm, nu: (i, 0, 0)),
            scratch_shapes=[pltpu.VMEM((d, 2 * ff), _BF16), pltpu.VMEM((ff, d), _BF16)]),
        compiler_params=pltpu.CompilerParams(
            dimension_semantics=("arbitrary",), vmem_limit_bytes=VMEM_LIMIT),
        name="experts",
    )(bexp, bmap, nused, xs, wg, wu, wd)


def _combine_kernel(dest_ref, ys_ref, hp_ref, w_ref, x1_ref, mod_ref, wgus_ref, wds_ref, gpost_ref,
                    o_ref, gbuf, sem):
    t_rows = hp_ref.shape[0]

    def issue(t, _):
        for kk in range(TOP_K):
            _row_copy(ys_ref.at[dest_ref[0, 0, t * TOP_K + kk]], gbuf.at[kk * t_rows + t], sem).start()
        return 0

    lax.fori_loop(0, t_rows, issue, 0)

    hb = _unpack_rows(hp_ref, t_rows)
    gu = jnp.dot(hb, wgus_ref[...], preferred_element_type=_F32)
    g, u = gu[:, :EXPERT_FF], gu[:, EXPERT_FF:]
    mid = (g * jax.nn.sigmoid(g) * u).astype(_BF16)
    shared = jnp.dot(mid, wds_ref[...], preferred_element_type=_F32)

    def drain(t, _):
        for kk in range(TOP_K):
            _row_copy(ys_ref.at[0], gbuf.at[0], sem).wait()
        return 0

    lax.fori_loop(0, t_rows, drain, 0)

    w = w_ref[...]
    cols = []
    for c in range(PACK_ROWS):
        acc_lo = jnp.zeros((t_rows, 128), _F32)
        acc_hi = jnp.zeros((t_rows, 128), _F32)
        for kk in range(TOP_K):
            lo, hi = _unpack_pair(gbuf[pl.ds(kk * t_rows, t_rows), c, :])
            wk = w[:, kk:kk + 1]
            acc_lo = acc_lo + wk * lo
            acc_hi = acc_hi + wk * hi
        cols += [acc_lo, acc_hi]
    y = jnp.concatenate(cols, axis=1) + shared
    gate_f = mod_ref[0, 5:6, :]
    o_ref[...] = x1_ref[...] + gate_f * (y * _rsqrt_mean_sq(y, -1) * gpost_ref[...])


def _combine(dest_tiles, ys, hp, w_tok, x1, mod, wgus, wds, gpost, tiles_per_batch):
    n, d = x1.shape
    t = T_ROW
    full = lambda a: pl.BlockSpec(a.shape, lambda i: (0,) * a.ndim)
    return pl.pallas_call(
        _combine_kernel,
        out_shape=jax.ShapeDtypeStruct((n, d), _F32),
        grid=(n // t,),
        in_specs=[pl.BlockSpec((1, 1, t * TOP_K), lambda i: (i, 0, 0), memory_space=pltpu.SMEM),
                  pl.BlockSpec(memory_space=pl.ANY),
                  pl.BlockSpec((t, PACK_ROWS, 128), lambda i: (i, 0, 0)),
                  pl.BlockSpec((t, TOP_K), lambda i: (i, 0)),
                  pl.BlockSpec((t, d), lambda i: (i, 0)),
                  pl.BlockSpec((1, N_MOD, d), lambda i: (i // tiles_per_batch, 0, 0)),
                  full(wgus), full(wds), full(gpost)],
        out_specs=pl.BlockSpec((t, d), lambda i: (i, 0)),
        scratch_shapes=[pltpu.VMEM((TOP_K * t, PACK_ROWS, 128), _U32), pltpu.SemaphoreType.DMA(())],
        compiler_params=pltpu.CompilerParams(
            dimension_semantics=("arbitrary",), vmem_limit_bytes=VMEM_LIMIT),
        name="combine",
    )(dest_tiles, ys, hp, w_tok, x1, mod, wgus, wds, gpost)


def _deinterleave(n):
    return np.concatenate([np.arange(0, n, 2), np.arange(1, n, 2)])


def _rope_tables(s, dim):
    rows = s // GRID_W
    row = jnp.broadcast_to(jnp.arange(rows, dtype=_F32)[:, None], (rows, GRID_W)).reshape(s)
    col = jnp.broadcast_to(jnp.arange(GRID_W, dtype=_F32)[None, :], (rows, GRID_W)).reshape(s)
    n_freq = dim // 4
    inv_freq = ROPE_THETA ** (-jnp.arange(n_freq, dtype=_F32) / n_freq)
    ang = jnp.concatenate([row[:, None] * inv_freq, col[:, None] * inv_freq], axis=-1)
    return jnp.cos(ang).T, jnp.sin(ang).T


def _in_proj_perm():
    perm = list(range(OFF_KROPE))
    perm += list(OFF_KROPE + _deinterleave(MLA_ROPE))
    for hd in range(GQA_HEADS):
        perm += list(OFF_QG + hd * GQA_HEAD_DIM + _deinterleave(GQA_HEAD_DIM))
    for hd in range(GQA_KV_HEADS):
        perm += list(OFF_KG + hd * GQA_HEAD_DIM + _deinterleave(GQA_HEAD_DIM))
    perm += list(range(OFF_VG, IN_COLS))
    return np.asarray(perm)


def _q_b_perm():
    per = MLA_NOPE + MLA_ROPE
    nope = [hd * per + j for hd in range(MLA_HEADS) for j in range(MLA_NOPE)]
    even = [hd * per + MLA_NOPE + 2 * i for hd in range(MLA_HEADS) for i in range(MLA_ROPE // 2)]
    odd = [hd * per + MLA_NOPE + 2 * i + 1 for hd in range(MLA_HEADS) for i in range(MLA_ROPE // 2)]
    return np.asarray(nope + even + odd)


def _kv_b_perm():
    per = MLA_NOPE + MLA_V
    kk = [hd * per + j for hd in range(MLA_HEADS) for j in range(MLA_NOPE)]
    vv = [hd * per + MLA_NOPE + j for hd in range(MLA_HEADS) for j in range(MLA_V)]
    return np.asarray(kk + vv)


def _layer(x, c, w_ada, b_ada, g_pre_mix, g_post_mix, g_pre_ffn, g_post_ffn, w_in, g_q_a, w_q_b,
           g_kv_a, w_kv_b, g_q_head, g_k_head, g_mla_out, g_gqa_out, w_out, w_router, b_router,
           w_gate_e, w_up_e, w_down_e, w_gate_s, w_up_s, w_down_s):
    b, s, d = x.shape
    n = b * s
    col = lambda g: g.reshape(-1, 1).astype(_F32)
    row = lambda g: g.reshape(1, -1).astype(_F32)

    mod = _ada_mod(c, w_ada, b_ada)

    winT = w_in[:, _in_proj_perm()].T.astype(_BF16)
    wqbT = w_q_b[:, _q_b_perm()].T.astype(_BF16)
    wkvbT = w_kv_b[:, _kv_b_perm()].T.astype(_BF16)
    perm_h = _deinterleave(GQA_HEAD_DIM)
    cosm, sinm = _rope_tables(s, MLA_ROPE)
    cosg, sing = _rope_tables(s, GQA_HEAD_DIM)

    qT, k, vT = _project(x, mod, row(g_pre_mix), winT, col(g_q_a), wqbT, col(g_kv_a), wkvbT,
                         col(g_q_head[perm_h]), col(g_k_head[perm_h]), cosm, sinm, cosg, sing)
    oT = _attention(qT, k, vT)

    wr_t = w_router.T.astype(_F32)
    wrh = wr_t.astype(_BF16)
    wrl = (wr_t - wrh.astype(_F32)).astype(_BF16)
    x1, hp, idx_t, w_t, rank_t, cnt = _post_attention(
        x, oT, mod, col(g_mla_out), col(g_gqa_out), w_out.T.astype(_BF16), row(g_post_mix),
        row(g_pre_ffn), wrh, wrl, col(b_router))

    counts = cnt[:, 0].astype(jnp.int32)
    blocks_e = (counts + T_M - 1) // T_M
    blk_end = jnp.cumsum(blocks_e)
    start = (blk_end - blocks_e) * T_M
    n_blocks = n * TOP_K // T_M + N_EXPERTS
    nused = blk_end[-1:]
    bidx = jnp.minimum(jnp.arange(n_blocks, dtype=jnp.int32), nused[0] - 1)
    bexp = jnp.minimum(jnp.searchsorted(blk_end, bidx, side="right"), N_EXPERTS - 1).astype(jnp.int32)
    dest = start[idx_t] + rank_t
    dest_tiles = dest.T.reshape(n // T_ROW, 1, T_ROW * TOP_K)

    xs = _dispatch(dest_tiles, hp, n_blocks * T_M)
    ys = _experts(bexp, bidx, nused.astype(jnp.int32), xs, w_gate_e, w_up_e, w_down_e)
    wgus = jnp.concatenate([w_gate_s, w_up_s], axis=1).astype(_BF16)
    out = _combine(dest_tiles, ys, hp, w_t.T, x1.reshape(n, d), mod, wgus, w_down_s.astype(_BF16),
                   row(g_post_ffn), s // T_ROW)
    return out.reshape(b, s, d)


def kernel(x, c, w_ada, b_ada, g_pre_mix, g_post_mix, g_pre_ffn, g_post_ffn, w_in, g_q_a, w_q_b, g_kv_a, w_kv_b, g_q_head, g_k_head, g_mla_out, g_gqa_out, w_out, w_router, b_router, w_gate_e, w_up_e, w_down_e, w_gate_s, w_up_s, w_down_s):
    depth = w_ada.shape[0]
    for l in range(depth):
        x = _layer(x, c, w_ada[l], b_ada[l], g_pre_mix[l], g_post_mix[l], g_pre_ffn[l], g_post_ffn[l],
                   w_in[l], g_q_a[l], w_q_b[l], g_kv_a[l], w_kv_b[l], g_q_head[l], g_k_head[l],
                   g_mla_out[l], g_gqa_out[l], w_out[l], w_router[l], b_router[l], w_gate_e[l],
                   w_up_e[l], w_down_e[l], w_gate_s[l], w_up_s[l], w_down_s[l])
    return x
```

```python
import functools

import jax
import jax.numpy as jnp
import numpy as np
from jax import lax
from jax.experimental import pallas as pl
from jax.experimental.pallas import tpu as pltpu

D_MODEL = 1024
GRID_W = 64
ROPE_THETA = 10000.0
NORM_EPS = 1e-6

MLA_HEADS = 8
MLA_Q_LORA = 256
MLA_KV_LORA = 128
MLA_NOPE = 64
MLA_ROPE = 32
MLA_V = 64

GQA_HEADS = 8
GQA_KV_HEADS = 2
GQA_HEAD_DIM = 64
GQA_GROUP = GQA_HEADS // GQA_KV_HEADS

N_HEADS = MLA_HEADS + GQA_HEADS
N_KV = MLA_HEADS + GQA_KV_HEADS
HEAD_PAD = 128
V_DIM = 64

N_EXPERTS = 256
TOP_K = 8
N_GROUPS = 8
GROUP_SIZE = N_EXPERTS // N_GROUPS
TOPK_GROUPS = 4
EXPERT_FF = 256
ROUTED_SCALE = 2.5
N_MOD = 6

OFF_QLAT = 0
OFF_KVLAT = OFF_QLAT + MLA_Q_LORA
OFF_KROPE = OFF_KVLAT + MLA_KV_LORA
OFF_QG = OFF_KROPE + MLA_ROPE
OFF_KG = OFF_QG + GQA_HEADS * GQA_HEAD_DIM
OFF_VG = OFF_KG + GQA_KV_HEADS * GQA_HEAD_DIM
IN_COLS = OFF_VG + GQA_KV_HEADS * GQA_HEAD_DIM

T_PROJ = 512
T_Q = 512
T_POST = 512
T_ROW = 256
T_M = 256
PACK_ROWS = 4
N_SCORE_SLOTS = 4
LOG2_E = 1.4426950408889634
VMEM_LIMIT = 56 * 1024 * 1024

_NT = (((1,), (1,)), ((), ()))
_F32 = jnp.float32
_BF16 = jnp.bfloat16
_U32 = jnp.uint32
_HI_MASK = np.uint32(0xFFFF0000)


def _rsqrt_mean_sq(v, axis):
    return lax.rsqrt(jnp.mean(v * v, axis=axis, keepdims=True) + NORM_EPS)


def _pack_pair(lo, hi):
    lo_b = pltpu.bitcast(lo.astype(_BF16).astype(_F32), _U32) >> 16
    hi_b = pltpu.bitcast(hi.astype(_BF16).astype(_F32), _U32) & _HI_MASK
    return lo_b | hi_b


def _unpack_pair(u):
    return pltpu.bitcast(u << 16, _F32), pltpu.bitcast(u & _HI_MASK, _F32)


def _ada_kernel(c_ref, w_ref, b_ref, o_ref):
    c = c_ref[...]
    a = (c * jax.nn.sigmoid(c)).astype(_BF16)
    o_ref[...] = jnp.dot(a, w_ref[...].astype(_BF16), preferred_element_type=_F32) + b_ref[...]


def _ada_mod(c, w_ada, b_ada):
    b, d = c.shape
    cols = w_ada.shape[1]
    rows = 8
    tn = 1536
    c_pad = jnp.zeros((rows, d), _F32).at[:b].set(c)
    out = pl.pallas_call(
        _ada_kernel,
        out_shape=jax.ShapeDtypeStruct((rows, cols), _F32),
        grid=(cols // tn,),
        in_specs=[pl.BlockSpec((rows, d), lambda j: (0, 0)),
                  pl.BlockSpec((d, tn), lambda j: (0, j)),
                  pl.BlockSpec((1, tn), lambda j: (0, j))],
        out_specs=pl.BlockSpec((rows, tn), lambda j: (0, j)),
        compiler_params=pltpu.CompilerParams(vmem_limit_bytes=VMEM_LIMIT),
        name="ada_mod",
    )(c_pad, w_ada, b_ada.reshape(1, cols))
    return out[:b].reshape(b, N_MOD, d)


def _rope(x0, x1, cos, sin):
    return x0 * cos - x1 * sin, x0 * sin + x1 * cos


def _proj_kernel(x_ref, mod_ref, gpre_ref, win_ref, gqa_ref, wqb_ref, gkva_ref, wkvb_ref,
                 gqh_ref, gkh_ref, cosm_ref, sinm_ref, cosg_ref, sing_ref,
                 qT_ref, k_ref, vT_ref):
    x = x_ref[0]
    t = x.shape[0]
    shift = mod_ref[0, 0:1, :]
    scale = mod_ref[0, 1:2, :]
    h = x * _rsqrt_mean_sq(x, -1) * gpre_ref[...] * (1.0 + scale) + shift
    pT = lax.dot_general(win_ref[...], h.astype(_BF16), _NT, preferred_element_type=_F32)

    cm, sm = cosm_ref[...], sinm_ref[...]
    cg, sg = cosg_ref[...], sing_ref[...]
    half_m = MLA_ROPE // 2
    half_g = GQA_HEAD_DIM // 2

    ql = pT[OFF_QLAT:OFF_QLAT + MLA_Q_LORA]
    qn = (ql * _rsqrt_mean_sq(ql, 0) * gqa_ref[...]).astype(_BF16)
    qT = jnp.dot(wqb_ref[...], qn, preferred_element_type=_F32)
    sc_m = float((MLA_NOPE + MLA_ROPE) ** -0.5 * LOG2_E)
    n_nope = MLA_HEADS * MLA_NOPE
    n_half = MLA_HEADS * half_m
    q0, q1 = _rope(qT[n_nope:n_nope + n_half], qT[n_nope + n_half:],
                   jnp.tile(cm, (MLA_HEADS, 1)), jnp.tile(sm, (MLA_HEADS, 1)))
    zq_m = jnp.zeros((HEAD_PAD - MLA_NOPE - MLA_ROPE, t), _BF16)
    for hd in range(MLA_HEADS):
        qT_ref[0, hd, 0:MLA_NOPE, :] = (qT[hd * MLA_NOPE:(hd + 1) * MLA_NOPE] * sc_m).astype(_BF16)
        qT_ref[0, hd, MLA_NOPE:MLA_NOPE + half_m, :] = (q0[hd * half_m:(hd + 1) * half_m] * sc_m).astype(_BF16)
        qT_ref[0, hd, MLA_NOPE + half_m:MLA_NOPE + MLA_ROPE, :] = (
            q1[hd * half_m:(hd + 1) * half_m] * sc_m).astype(_BF16)
        qT_ref[0, hd, MLA_NOPE + MLA_ROPE:, :] = zq_m

    kl = pT[OFF_KVLAT:OFF_KVLAT + MLA_KV_LORA]
    kn = (kl * _rsqrt_mean_sq(kl, 0) * gkva_ref[...]).astype(_BF16)
    kvT = jnp.dot(wkvb_ref[...], kn, preferred_element_type=_F32)
    k0, k1 = _rope(pT[OFF_KROPE:OFF_KROPE + half_m], pT[OFF_KROPE + half_m:OFF_KROPE + MLA_ROPE], cm, sm)
    zk_m = jnp.zeros((HEAD_PAD - MLA_NOPE - MLA_ROPE, t), _F32)
    for hd in range(MLA_HEADS):
        kTh = jnp.concatenate([kvT[hd * MLA_NOPE:(hd + 1) * MLA_NOPE], k0, k1, zk_m], axis=0)
        k_ref[0, hd] = kTh.T.astype(_BF16)
        vT_ref[0, hd, 0] = kvT[n_nope + hd * MLA_V:n_nope + (hd + 1) * MLA_V].astype(_BF16)

    sc_g = float(GQA_HEAD_DIM ** -0.5 * LOG2_E)
    zq_g = jnp.zeros((HEAD_PAD - GQA_HEAD_DIM, t), _BF16)
    for hd in range(GQA_HEADS):
        blk = pT[OFF_QG + hd * GQA_HEAD_DIM:OFF_QG + (hd + 1) * GQA_HEAD_DIM]
        bn = blk * _rsqrt_mean_sq(blk, 0) * gqh_ref[...]
        g0, g1 = _rope(bn[:half_g], bn[half_g:], cg, sg)
        qT_ref[0, MLA_HEADS + hd, 0:half_g, :] = (g0 * sc_g).astype(_BF16)
        qT_ref[0, MLA_HEADS + hd, half_g:GQA_HEAD_DIM, :] = (g1 * sc_g).astype(_BF16)
        qT_ref[0, MLA_HEADS + hd, GQA_HEAD_DIM:, :] = zq_g

    zk_g = jnp.zeros((HEAD_PAD - GQA_HEAD_DIM, t), _F32)
    for hd in range(GQA_KV_HEADS):
        blk = pT[OFF_KG + hd * GQA_HEAD_DIM:OFF_KG + (hd + 1) * GQA_HEAD_DIM]
        bn = blk * _rsqrt_mean_sq(blk, 0) * gkh_ref[...]
        g0, g1 = _rope(bn[:half_g], bn[half_g:], cg, sg)
        kTh = jnp.concatenate([g0, g1, zk_g], axis=0)
        k_ref[0, MLA_HEADS + hd] = kTh.T.astype(_BF16)
        vT_ref[0, MLA_HEADS + hd, 0] = pT[OFF_VG + hd * V_DIM:OFF_VG + (hd + 1) * V_DIM].astype(_BF16)


def _project(x, mod, gpre, winT, gqa, wqbT, gkva, wkvbT, gqh, gkh, cosm, sinm, cosg, sing):
    b, s, d = x.shape
    t = T_PROJ
    nt = s // t
    full = lambda a: pl.BlockSpec(a.shape, lambda bi, i: (0,) * a.ndim)
    tab = lambda a: pl.BlockSpec((a.shape[0], t), lambda bi, i: (0, i))
    return pl.pallas_call(
        _proj_kernel,
        out_shape=(jax.ShapeDtypeStruct((b, N_HEADS, HEAD_PAD, s), _BF16),
                   jax.ShapeDtypeStruct((b, N_KV, s, HEAD_PAD), _BF16),
                   jax.ShapeDtypeStruct((b, N_KV, nt, V_DIM, t), _BF16)),
        grid=(b, nt),
        in_specs=[pl.BlockSpec((1, t, d), lambda bi, i: (bi, i, 0)),
                  pl.BlockSpec((1, N_MOD, d), lambda bi, i: (bi, 0, 0)),
                  full(gpre), full(winT), full(gqa), full(wqbT), full(gkva), full(wkvbT),
                  full(gqh), full(gkh), tab(cosm), tab(sinm), tab(cosg), tab(sing)],
        out_specs=(pl.BlockSpec((1, N_HEADS, HEAD_PAD, t), lambda bi, i: (bi, 0, 0, i)),
                   pl.BlockSpec((1, N_KV, t, HEAD_PAD), lambda bi, i: (bi, 0, i, 0)),
                   pl.BlockSpec((1, N_KV, 1, V_DIM, t), lambda bi, i: (bi, 0, i, 0, 0))),
        compiler_params=pltpu.CompilerParams(
            dimension_semantics=("arbitrary", "arbitrary"), vmem_limit_bytes=VMEM_LIMIT),
        name="in_proj",
    )(x, mod, gpre, winT, gqa, wqbT, gkva, wkvbT, gqh, gkh, cosm, sinm, cosg, sing)


def _attn_kernel(qT_ref, k_ref, vT_ref, o_ref, *slots, n_chunks, tk):
    qT = qT_ref[0, 0]
    tq = qT.shape[1]
    ns = len(slots)

    def scores(j, slot):
        kc = k_ref[0, 0, pl.ds(pl.multiple_of(j * tk, tk), tk), :]
        s = jnp.dot(kc, qT, preferred_element_type=_F32)
        slots[slot][...] = s
        return jnp.max(s, axis=0, keepdims=True)

    def accumulate(j, slot, mc, carry):
        m, l, acc = carry
        m_new = jnp.maximum(m, mc)
        alpha = jnp.exp2(m - m_new)
        p = jnp.exp2(slots[slot][...] - m_new)
        l = alpha * l + jnp.sum(p, axis=0, keepdims=True)
        acc = alpha * acc + jnp.dot(vT_ref[0, 0, j], p.astype(_BF16), preferred_element_type=_F32)
        return m_new, l, acc

    def group(jj, state):
        mcs, carry = state
        mcs = list(mcs)
        j = ns * jj
        for u in range(ns):
            nxt = scores(j + u + 2, (u + 2) % ns)
            carry = accumulate(j + u, u, mcs[u], carry)
            mcs[(u + 2) % ns] = nxt
        return tuple(mcs), carry

    carry = (jnp.full((1, tq), -1e30, _F32), jnp.zeros((1, tq), _F32), jnp.zeros((V_DIM, tq), _F32))
    zero = jnp.zeros((1, tq), _F32)
    mcs = (scores(0, 0), scores(1, 1)) + (zero,) * (ns - 2)
    n_loop = (n_chunks - 2) // ns
    mcs, carry = lax.fori_loop(0, n_loop, group, (mcs, carry))
    mcs = list(mcs)
    for j in range(n_loop * ns, n_chunks):
        if j + 2 < n_chunks:
            mcs[(j + 2) % ns] = scores(j + 2, (j + 2) % ns)
        carry = accumulate(j, j % ns, mcs[j % ns], carry)
    _, l, acc = carry
    o_ref[0, 0] = acc / l


def _kv_index(h):
    return jnp.where(h < MLA_HEADS, h, MLA_HEADS + (h - MLA_HEADS) // GQA_GROUP)


def _attention(qT, k, vT):
    b, nh, dp, s = qT.shape
    n_chunks, tk = vT.shape[2], vT.shape[4]
    assert n_chunks > N_SCORE_SLOTS
    return pl.pallas_call(
        functools.partial(_attn_kernel, n_chunks=n_chunks, tk=tk),
        out_shape=jax.ShapeDtypeStruct((b, nh, V_DIM, s), _F32),
        grid=(b, nh, s // T_Q),
        in_specs=[pl.BlockSpec((1, 1, dp, T_Q), lambda bi, h, i: (bi, h, 0, i)),
                  pl.BlockSpec((1, 1, s, dp), lambda bi, h, i: (bi, _kv_index(h), 0, 0)),
                  pl.BlockSpec((1, 1, n_chunks, V_DIM, tk), lambda bi, h, i: (bi, _kv_index(h), 0, 0, 0))],
        out_specs=pl.BlockSpec((1, 1, V_DIM, T_Q), lambda bi, h, i: (bi, h, 0, i)),
        scratch_shapes=[pltpu.VMEM((tk, T_Q), _F32) for _ in range(N_SCORE_SLOTS)],
        compiler_params=pltpu.CompilerParams(
            dimension_semantics=("arbitrary", "arbitrary", "arbitrary"), vmem_limit_bytes=VMEM_LIMIT),
        name="attention",
    )(qT, k, vT)


def _first_argmax(v, rows, n):
    m = jnp.max(v, axis=0, keepdims=True)
    idx = jnp.min(jnp.where(v == m, rows, n), axis=0, keepdims=True)
    return m, idx


def _post_kernel(x_ref, o_ref, mod_ref, gmla_ref, ggqa_ref, wout_ref, gpost_ref, gffn_ref,
                 wrh_ref, wrl_ref, br_ref,
                 x1_ref, hp_ref, idx_ref, w_ref, rank_ref, cnt_ref, carry_ref):
    first = jnp.logical_and(pl.program_id(0) == 0, pl.program_id(1) == 0)

    @pl.when(first)
    def _():
        carry_ref[...] = jnp.zeros_like(carry_ref)

    x = x_ref[0]
    t = x.shape[0]
    gate_m = mod_ref[0, 2:3, :]
    shift_f = mod_ref[0, 3:4, :]
    scale_f = mod_ref[0, 4:5, :]

    o = o_ref[0].reshape(N_HEADS * V_DIM, t)
    half = MLA_HEADS * V_DIM
    om, og = o[:half], o[half:]
    mixedT = jnp.concatenate([om * _rsqrt_mean_sq(om, 0) * gmla_ref[...],
                              og * _rsqrt_mean_sq(og, 0) * ggqa_ref[...]], axis=0).astype(_BF16)
    y = jnp.dot(wout_ref[...], mixedT, preferred_element_type=_F32).T
    x1 = x + gate_m * (y * _rsqrt_mean_sq(y, -1) * gpost_ref[...])
    x1_ref[0] = x1
    h2 = x1 * _rsqrt_mean_sq(x1, -1) * gffn_ref[...] * (1.0 + scale_f) + shift_f

    for c in range(PACK_ROWS):
        hp_ref[:, c, :] = _pack_pair(h2[:, 256 * c:256 * c + 128], h2[:, 256 * c + 128:256 * c + 256])

    hh = h2.astype(_BF16)
    hl = (h2 - hh.astype(_F32)).astype(_BF16)
    wrh = wrh_ref[...]
    logits = (lax.dot_general(wrh, hh, _NT, preferred_element_type=_F32)
              + lax.dot_general(wrh, hl, _NT, preferred_element_type=_F32)
              + lax.dot_general(wrl_ref[...], hh, _NT, preferred_element_type=_F32))
    scores = jax.nn.sigmoid(logits)
    biased = scores + br_ref[...]
    neg = -jnp.inf

    rows_g = lax.broadcasted_iota(jnp.int32, (GROUP_SIZE, t), 0)
    gscores = []
    for g in range(N_GROUPS):
        v = biased[g * GROUP_SIZE:(g + 1) * GROUP_SIZE]
        m1, i1 = _first_argmax(v, rows_g, GROUP_SIZE)
        m2 = jnp.max(jnp.where(rows_g == i1, neg, v), axis=0, keepdims=True)
        gscores.append(m1 + m2)
    cur = jnp.concatenate(gscores, axis=0)
    rows_8 = lax.broadcasted_iota(jnp.int32, (N_GROUPS, t), 0)
    gsel = jnp.zeros((N_GROUPS, t), _F32)
    for _ in range(TOPK_GROUPS):
        _, gi = _first_argmax(cur, rows_8, N_GROUPS)
        hit = rows_8 == gi
        gsel = jnp.where(hit, 1.0, gsel)
        cur = jnp.where(hit, neg, cur)
    emask = jnp.concatenate(
        [jnp.broadcast_to(gsel[g:g + 1], (GROUP_SIZE, t)) for g in range(N_GROUPS)], axis=0) > 0.5
    masked = jnp.where(emask, biased, neg)

    rows_e = lax.broadcasted_iota(jnp.int32, (N_EXPERTS, t), 0)
    idxs, ws = [], []
    for _ in range(TOP_K):
        _, ei = _first_argmax(masked, rows_e, N_EXPERTS)
        hit = rows_e == ei
        idxs.append(ei)
        ws.append(jnp.sum(jnp.where(hit, scores, 0.0), axis=0, keepdims=True))
        masked = jnp.where(hit, neg, masked)
    wsum = ws[0]
    for wk in ws[1:]:
        wsum = wsum + wk

    onehot = jnp.zeros((N_EXPERTS, t), _F32)
    for ei in idxs:
        onehot = onehot + jnp.where(rows_e == ei, 1.0, 0.0)
    upper = (lax.broadcasted_iota(jnp.int32, (t, t), 0) < lax.broadcasted_iota(jnp.int32, (t, t), 1))
    before = jnp.dot(onehot.astype(_BF16), jnp.where(upper, 1.0, 0.0).astype(_BF16),
                     preferred_element_type=_F32) + carry_ref[...]
    for kk in range(TOP_K):
        idx_ref[kk:kk + 1, :] = idxs[kk]
        w_ref[kk:kk + 1, :] = ws[kk] / wsum * ROUTED_SCALE
        rank_ref[kk:kk + 1, :] = jnp.sum(
            jnp.where(rows_e == idxs[kk], before, 0.0), axis=0, keepdims=True).astype(jnp.int32)
    total = carry_ref[...] + jnp.sum(onehot, axis=1, keepdims=True)
    carry_ref[...] = total
    cnt_ref[...] = jnp.broadcast_to(total, cnt_ref.shape)


def _post_attention(x, oT, mod, gmla, ggqa, woutT, gpost, gffn, wrh, wrl, br):
    b, s, d = x.shape
    n = b * s
    t = T_POST
    nt = s // t
    full = lambda a: pl.BlockSpec(a.shape, lambda bi, i: (0,) * a.ndim)
    tok = lambda bi, i: (0, bi * nt + i)
    return pl.pallas_call(
        _post_kernel,
        out_shape=(jax.ShapeDtypeStruct((b, s, d), _F32),
                   jax.ShapeDtypeStruct((n, PACK_ROWS, 128), _U32),
                   jax.ShapeDtypeStruct((TOP_K, n), jnp.int32),
                   jax.ShapeDtypeStruct((TOP_K, n), _F32),
                   jax.ShapeDtypeStruct((TOP_K, n), jnp.int32),
                   jax.ShapeDtypeStruct((N_EXPERTS, 128), _F32)),
        grid=(b, nt),
        in_specs=[pl.BlockSpec((1, t, d), lambda bi, i: (bi, i, 0)),
                  pl.BlockSpec((1, N_HEADS, V_DIM, t), lambda bi, i: (bi, 0, 0, i)),
                  pl.BlockSpec((1, N_MOD, d), lambda bi, i: (bi, 0, 0)),
                  full(gmla), full(ggqa), full(woutT), full(gpost), full(gffn),
                  full(wrh), full(wrl), full(br)],
        out_specs=(pl.BlockSpec((1, t, d), lambda bi, i: (bi, i, 0)),
                   pl.BlockSpec((t, PACK_ROWS, 128), lambda bi, i: (bi * nt + i, 0, 0)),
                   pl.BlockSpec((TOP_K, t), tok),
                   pl.BlockSpec((TOP_K, t), tok),
                   pl.BlockSpec((TOP_K, t), tok),
                   pl.BlockSpec((N_EXPERTS, 128), lambda bi, i: (0, 0))),
        scratch_shapes=[pltpu.VMEM((N_EXPERTS, 1), _F32)],
        compiler_params=pltpu.CompilerParams(
            dimension_semantics=("arbitrary", "arbitrary"), vmem_limit_bytes=VMEM_LIMIT),
        name="out_proj_router",
    )(x, oT, mod, gmla, ggqa, woutT, gpost, gffn, wrh, wrl, br)


def _slot_kernel(idx_ref, rank_ref, start_ref, dest_ref):
    t = idx_ref.shape[1]
    rows_e = lax.broadcasted_iota(jnp.int32, (N_EXPERTS, t), 0)
    start = start_ref[...]
    for kk in range(TOP_K):
        base = jnp.sum(jnp.where(rows_e == idx_ref[kk:kk + 1, :], start, 0.0), axis=0, keepdims=True)
        dest_ref[kk:kk + 1, :] = rank_ref[kk:kk + 1, :] + base.astype(jnp.int32)


def _slots(idx_t, rank_t, start):
    n = idx_t.shape[1]
    t = T_POST
    return pl.pallas_call(
        _slot_kernel,
        out_shape=jax.ShapeDtypeStruct((TOP_K, n), jnp.int32),
        grid=(n // t,),
        in_specs=[pl.BlockSpec((TOP_K, t), lambda i: (0, i)),
                  pl.BlockSpec((TOP_K, t), lambda i: (0, i)),
                  pl.BlockSpec((N_EXPERTS, 1), lambda i: (0, 0))],
        out_specs=pl.BlockSpec((TOP_K, t), lambda i: (0, i)),
        compiler_params=pltpu.CompilerParams(dimension_semantics=("arbitrary",)),
        name="slot_index",
    )(idx_t, rank_t, start.astype(_F32).reshape(N_EXPERTS, 1))


def _row_copy(src_row, dst_row, sem):
    return pltpu.make_async_copy(src_row, dst_row, sem)


def _dispatch_kernel(dest_ref, h_ref, xs_in_ref, xs_ref, sem):
    del xs_in_ref
    t_rows = h_ref.shape[0]

    def issue(t, _):
        for kk in range(TOP_K):
            _row_copy(h_ref.at[t], xs_ref.at[dest_ref[0, 0, t * TOP_K + kk]], sem).start()
        return 0

    lax.fori_loop(0, t_rows, issue, 0)

    def drain(t, _):
        for kk in range(TOP_K):
            _row_copy(h_ref.at[0], xs_ref.at[0], sem).wait()
        return 0

    lax.fori_loop(0, t_rows, drain, 0)


def _dispatch(dest_tiles, hp, n_slots):
    n = hp.shape[0]
    t = T_ROW
    xs0 = jnp.zeros((n_slots, PACK_ROWS, 128), _U32)
    return pl.pallas_call(
        _dispatch_kernel,
        out_shape=jax.ShapeDtypeStruct((n_slots, PACK_ROWS, 128), _U32),
        grid=(n // t,),
        in_specs=[pl.BlockSpec((1, 1, t * TOP_K), lambda i: (i, 0, 0), memory_space=pltpu.SMEM),
                  pl.BlockSpec((t, PACK_ROWS, 128), lambda i: (i, 0, 0)),
                  pl.BlockSpec(memory_space=pl.ANY)],
        out_specs=pl.BlockSpec(memory_space=pl.ANY),
        scratch_shapes=[pltpu.SemaphoreType.DMA(())],
        input_output_aliases={2: 0},
        compiler_params=pltpu.CompilerParams(
            dimension_semantics=("arbitrary",), vmem_limit_bytes=VMEM_LIMIT),
        name="dispatch",
    )(dest_tiles, hp, xs0)


def _unpack_rows(ref, rows):
    parts = []
    for c in range(PACK_ROWS):
        lo, hi = _unpack_pair(ref[pl.ds(0, rows), c, :])
        parts += [lo.astype(_BF16), hi.astype(_BF16)]
    return jnp.concatenate(parts, axis=1)


def _expert_kernel(bexp_ref, bmap_ref, nused_ref, x_ref, wg_ref, wu_ref, wd_ref, y_ref, wgu_s, wd_s):
    del bmap_ref
    b = pl.program_id(0)
    e = bexp_ref[b]
    e_prev = bexp_ref[jnp.maximum(b - 1, 0)]

    @pl.when(jnp.logical_or(b == 0, e != e_prev))
    def _():
        wgu_s[:, :EXPERT_FF] = wg_ref[0].astype(_BF16)
        wgu_s[:, EXPERT_FF:] = wu_ref[0].astype(_BF16)
        wd_s[...] = wd_ref[0].astype(_BF16)

    @pl.when(b < nused_ref[0])
    def _():
        xb = _unpack_rows(x_ref, T_M)
        gu = jnp.dot(xb, wgu_s[...], preferred_element_type=_F32)
        g, u = gu[:, :EXPERT_FF], gu[:, EXPERT_FF:]
        mid = (g * jax.nn.sigmoid(g) * u).astype(_BF16)
        y = jnp.dot(mid, wd_s[...], preferred_element_type=_F32)
        for c in range(PACK_ROWS):
            y_ref[:, c, :] = _pack_pair(y[:, 256 * c:256 * c + 128], y[:, 256 * c + 128:256 * c + 256])

    @pl.when(b >= nused_ref[0])
    def _():
        y_ref[...] = jnp.zeros_like(y_ref)


def _experts(bexp, bmap, nused, xs, wg, wu, wd):
    n_slots = xs.shape[0]
    nb = n_slots // T_M
    d, ff = wg.shape[1], wg.shape[2]
    return pl.pallas_call(
        _expert_kernel,
        out_shape=jax.ShapeDtypeStruct((n_slots, PACK_ROWS, 128), _U32),
        grid_spec=pltpu.PrefetchScalarGridSpec(
            num_scalar_prefetch=3,
            grid=(nb,),
            in_specs=[pl.BlockSpec((T_M, PACK_ROWS, 128), lambda i, be, bm, nu: (bm[i], 0, 0)),
                      pl.BlockSpec((1, d, ff), lambda i, be, bm, nu: (be[i], 0, 0)),
                      pl.BlockSpec((1, d, ff), lambda i, be, bm, nu: (be[i], 0, 0)),
                      pl.BlockSpec((1, ff, d), lambda i, be, bm, nu: (be[i], 0, 0))],
            out_specs=pl.BlockSpec((T_M, PACK_ROWS, 128), lambda i, be, bm, nu: (i, 0, 0)),
            scratch_shapes=[pltpu.VMEM((d, 2 * ff), _BF16), pltpu.VMEM((ff, d), _BF16)]),
        compiler_params=pltpu.CompilerParams(
            dimension_semantics=("arbitrary",), vmem_limit_bytes=VMEM_LIMIT),
        name="experts",
    )(bexp, bmap, nused, xs, wg, wu, wd)


def _combine_kernel(dest_ref, ys_ref, hp_ref, w_ref, x1_ref, mod_ref, wgus_ref, wds_ref, gpost_ref,
                    o_ref, gbuf, sem):
    t_rows = hp_ref.shape[0]

    def issue(t, _):
        for kk in range(TOP_K):
            _row_copy(ys_ref.at[dest_ref[0, 0, t * TOP_K + kk]], gbuf.at[kk * t_rows + t], sem).start()
        return 0

    lax.fori_loop(0, t_rows, issue, 0)

    hb = _unpack_rows(hp_ref, t_rows)
    gu = jnp.dot(hb, wgus_ref[...], preferred_element_type=_F32)
    g, u = gu[:, :EXPERT_FF], gu[:, EXPERT_FF:]
    mid = (g * jax.nn.sigmoid(g) * u).astype(_BF16)
    shared = jnp.dot(mid, wds_ref[...], preferred_element_type=_F32)

    def drain(t, _):
        for kk in range(TOP_K):
            _row_copy(ys_ref.at[0], gbuf.at[0], sem).wait()
        return 0

    lax.fori_loop(0, t_rows, drain, 0)

    w = w_ref[...]
    cols = []
    for c in range(PACK_ROWS):
        acc_lo = jnp.zeros((t_rows, 128), _F32)
        acc_hi = jnp.zeros((t_rows, 128), _F32)
        for kk in range(TOP_K):
            lo, hi = _unpack_pair(gbuf[pl.ds(kk * t_rows, t_rows), c, :])
            wk = w[:, kk:kk + 1]
            acc_lo = acc_lo + wk * lo
            acc_hi = acc_hi + wk * hi
        cols += [acc_lo, acc_hi]
    y = jnp.concatenate(cols, axis=1) + shared
    gate_f = mod_ref[0, 5:6, :]
    o_ref[...] = x1_ref[...] + gate_f * (y * _rsqrt_mean_sq(y, -1) * gpost_ref[...])


def _combine(dest_tiles, ys, hp, w_tok, x1, mod, wgus, wds, gpost, tiles_per_batch):
    n, d = x1.shape
    t = T_ROW
    full = lambda a: pl.BlockSpec(a.shape, lambda i: (0,) * a.ndim)
    return pl.pallas_call(
        _combine_kernel,
        out_shape=jax.ShapeDtypeStruct((n, d), _F32),
        grid=(n // t,),
        in_specs=[pl.BlockSpec((1, 1, t * TOP_K), lambda i: (i, 0, 0), memory_space=pltpu.SMEM),
                  pl.BlockSpec(memory_space=pl.ANY),
                  pl.BlockSpec((t, PACK_ROWS, 128), lambda i: (i, 0, 0)),
                  pl.BlockSpec((t, TOP_K), lambda i: (i, 0)),
                  pl.BlockSpec((t, d), lambda i: (i, 0)),
                  pl.BlockSpec((1, N_MOD, d), lambda i: (i // tiles_per_batch, 0, 0)),
                  full(wgus), full(wds), full(gpost)],
        out_specs=pl.BlockSpec((t, d), lambda i: (i, 0)),
        scratch_shapes=[pltpu.VMEM((TOP_K * t, PACK_ROWS, 128), _U32), pltpu.SemaphoreType.DMA(())],
        compiler_params=pltpu.CompilerParams(
            dimension_semantics=("arbitrary",), vmem_limit_bytes=VMEM_LIMIT),
        name="combine",
    )(dest_tiles, ys, hp, w_tok, x1, mod, wgus, wds, gpost)


def _deinterleave(n):
    return np.concatenate([np.arange(0, n, 2), np.arange(1, n, 2)])


def _rope_tables(s, dim):
    rows = s // GRID_W
    row = jnp.broadcast_to(jnp.arange(rows, dtype=_F32)[:, None], (rows, GRID_W)).reshape(s)
    col = jnp.broadcast_to(jnp.arange(GRID_W, dtype=_F32)[None, :], (rows, GRID_W)).reshape(s)
    n_freq = dim // 4
    inv_freq = ROPE_THETA ** (-jnp.arange(n_freq, dtype=_F32) / n_freq)
    ang = jnp.concatenate([row[:, None] * inv_freq, col[:, None] * inv_freq], axis=-1)
    return jnp.cos(ang).T, jnp.sin(ang).T


def _in_proj_perm():
    perm = list(range(OFF_KROPE))
    perm += list(OFF_KROPE + _deinterleave(MLA_ROPE))
    for hd in range(GQA_HEADS):
        perm += list(OFF_QG + hd * GQA_HEAD_DIM + _deinterleave(GQA_HEAD_DIM))
    for hd in range(GQA_KV_HEADS):
        perm += list(OFF_KG + hd * GQA_HEAD_DIM + _deinterleave(GQA_HEAD_DIM))
    perm += list(range(OFF_VG, IN_COLS))
    return np.asarray(perm)


def _q_b_perm():
    per = MLA_NOPE + MLA_ROPE
    nope = [hd * per + j for hd in range(MLA_HEADS) for j in range(MLA_NOPE)]
    even = [hd * per + MLA_NOPE + 2 * i for hd in range(MLA_HEADS) for i in range(MLA_ROPE // 2)]
    odd = [hd * per + MLA_NOPE + 2 * i + 1 for hd in range(MLA_HEADS) for i in range(MLA_ROPE // 2)]
    return np.asarray(nope + even + odd)


def _kv_b_perm():
    per = MLA_NOPE + MLA_V
    kk = [hd * per + j for hd in range(MLA_HEADS) for j in range(MLA_NOPE)]
    vv = [hd * per + MLA_NOPE + j for hd in range(MLA_HEADS) for j in range(MLA_V)]
    return np.asarray(kk + vv)


def _layer(x, c, w_ada, b_ada, g_pre_mix, g_post_mix, g_pre_ffn, g_post_ffn, w_in, g_q_a, w_q_b,
           g_kv_a, w_kv_b, g_q_head, g_k_head, g_mla_out, g_gqa_out, w_out, w_router, b_router,
           w_gate_e, w_up_e, w_down_e, w_gate_s, w_up_s, w_down_s):
    b, s, d = x.shape
    n = b * s
    col = lambda g: g.reshape(-1, 1).astype(_F32)
    row = lambda g: g.reshape(1, -1).astype(_F32)

    mod = _ada_mod(c, w_ada, b_ada)

    winT = w_in[:, _in_proj_perm()].T.astype(_BF16)
    wqbT = w_q_b[:, _q_b_perm()].T.astype(_BF16)
    wkvbT = w_kv_b[:, _kv_b_perm()].T.astype(_BF16)
    perm_h = _deinterleave(GQA_HEAD_DIM)
    cosm, sinm = _rope_tables(s, MLA_ROPE)
    cosg, sing = _rope_tables(s, GQA_HEAD_DIM)

    qT, k, vT = _project(x, mod, row(g_pre_mix), winT, col(g_q_a), wqbT, col(g_kv_a), wkvbT,
                         col(g_q_head[perm_h]), col(g_k_head[perm_h]), cosm, sinm, cosg, sing)
    oT = _attention(qT, k, vT)

    wr_t = w_router.T.astype(_F32)
    wrh = wr_t.astype(_BF16)
    wrl = (wr_t - wrh.astype(_F32)).astype(_BF16)
    x1, hp, idx_t, w_t, rank_t, cnt = _post_attention(
        x, oT, mod, col(g_mla_out), col(g_gqa_out), w_out.T.astype(_BF16), row(g_post_mix),
        row(g_pre_ffn), wrh, wrl, col(b_router))

    counts = cnt[:, 0].astype(jnp.int32)
    blocks_e = (counts + T_M - 1) // T_M
    blk_end = jnp.cumsum(blocks_e)
    start = (blk_end - blocks_e) * T_M
    n_blocks = n * TOP_K // T_M + N_EXPERTS
    nused = blk_end[-1:]
    bidx = jnp.minimum(jnp.arange(n_blocks, dtype=jnp.int32), nused[0] - 1)
    bexp = jnp.minimum(jnp.searchsorted(blk_end, bidx, side="right"), N_EXPERTS - 1).astype(jnp.int32)
    dest = _slots(idx_t, rank_t, start)
    dest_tiles = dest.T.reshape(n // T_ROW, 1, T_ROW * TOP_K)

    xs = _dispatch(dest_tiles, hp, n_blocks * T_M)
    ys = _experts(bexp, bidx, nused.astype(jnp.int32), xs, w_gate_e, w_up_e, w_down_e)
    wgus = jnp.concatenate([w_gate_s, w_up_s], axis=1).astype(_BF16)
    out = _combine(dest_tiles, ys, hp, w_t.T, x1.reshape(n, d), mod, wgus, w_down_s.astype(_BF16),
                   row(g_post_ffn), s // T_ROW)
    return out.reshape(b, s, d)


def kernel(x, c, w_ada, b_ada, g_pre_mix, g_post_mix, g_pre_ffn, g_post_ffn, w_in, g_q_a, w_q_b, g_kv_a, w_kv_b, g_q_head, g_k_head, g_mla_out, g_gqa_out, w_out, w_router, b_router, w_gate_e, w_up_e, w_down_e, w_gate_s, w_up_s, w_down_s):
    depth = w_ada.shape[0]
    for l in range(depth):
        x = _layer(x, c, w_ada[l], b_ada[l], g_pre_mix[l], g_post_mix[l], g_pre_ffn[l], g_post_ffn[l],
                   w_in[l], g_q_a[l], w_q_b[l], g_kv_a[l], w_kv_b[l], g_q_head[l], g_k_head[l],
                   g_mla_out[l], g_gqa_out[l], w_out[l], w_router[l], b_router[l], w_gate_e[l],
                   w_up_e[l], w_down_e[l], w_gate_s[l], w_up_s[l], w_down_s[l])
    return x
```

```python
import functools

import jax
import jax.numpy as jnp
import numpy as np
from jax import lax
from jax.experimental import pallas as pl
from jax.experimental.pallas import tpu as pltpu

D_MODEL = 1024
GRID_W = 64
ROPE_THETA = 10000.0
NORM_EPS = 1e-6

MLA_HEADS = 8
MLA_Q_LORA = 256
MLA_KV_LORA = 128
MLA_NOPE = 64
MLA_ROPE = 32
MLA_V = 64

GQA_HEADS = 8
GQA_KV_HEADS = 2
GQA_HEAD_DIM = 64
GQA_GROUP = GQA_HEADS // GQA_KV_HEADS

N_HEADS = MLA_HEADS + GQA_HEADS
N_KV = MLA_HEADS + GQA_KV_HEADS
HEAD_PAD = 128
V_DIM = 64

N_EXPERTS = 256
TOP_K = 8
N_GROUPS = 8
GROUP_SIZE = N_EXPERTS // N_GROUPS
TOPK_GROUPS = 4
EXPERT_FF = 256
ROUTED_SCALE = 2.5
N_MOD = 6

OFF_QLAT = 0
OFF_KVLAT = OFF_QLAT + MLA_Q_LORA
OFF_KROPE = OFF_KVLAT + MLA_KV_LORA
OFF_QG = OFF_KROPE + MLA_ROPE
OFF_KG = OFF_QG + GQA_HEADS * GQA_HEAD_DIM
OFF_VG = OFF_KG + GQA_KV_HEADS * GQA_HEAD_DIM
IN_COLS = OFF_VG + GQA_KV_HEADS * GQA_HEAD_DIM

T_PROJ = 512
T_Q = 512
T_POST = 512
T_ROW = 256
T_M = 256
PACK_ROWS = 4
N_SCORE_SLOTS = 4
LOG2_E = 1.4426950408889634
VMEM_LIMIT = 56 * 1024 * 1024

_NT = (((1,), (1,)), ((), ()))
_F32 = jnp.float32
_BF16 = jnp.bfloat16
_U32 = jnp.uint32
_HI_MASK = np.uint32(0xFFFF0000)


def _rsqrt_mean_sq(v, axis):
    return lax.rsqrt(jnp.mean(v * v, axis=axis, keepdims=True) + NORM_EPS)


def _pack_pair(lo, hi):
    lo_b = pltpu.bitcast(lo.astype(_BF16).astype(_F32), _U32) >> 16
    hi_b = pltpu.bitcast(hi.astype(_BF16).astype(_F32), _U32) & _HI_MASK
    return lo_b | hi_b


def _unpack_pair(u):
    return pltpu.bitcast(u << 16, _F32), pltpu.bitcast(u & _HI_MASK, _F32)


def _token_rows(ref, first, count=1):
    return ref.at[pl.ds(pl.multiple_of(first * PACK_ROWS, PACK_ROWS), count * PACK_ROWS), :]


def _chunk_of_tokens(first, count, c):
    return pl.ds(first * PACK_ROWS + c, count, stride=PACK_ROWS)


def _store_packed(ref, val):
    rows = val.shape[0]
    for c in range(PACK_ROWS):
        ref[_chunk_of_tokens(0, rows, c), :] = _pack_pair(
            val[:, 256 * c:256 * c + 128], val[:, 256 * c + 128:256 * c + 256])


def _load_packed(ref, first, rows):
    parts = []
    for c in range(PACK_ROWS):
        parts += list(_unpack_pair(ref[_chunk_of_tokens(first, rows, c), :]))
    return parts


def _ada_kernel(c_ref, w_ref, b_ref, o_ref):
    c = c_ref[...]
    a = (c * jax.nn.sigmoid(c)).astype(_BF16)
    o_ref[...] = jnp.dot(a, w_ref[...].astype(_BF16), preferred_element_type=_F32) + b_ref[...]


def _ada_mod(c, w_ada, b_ada):
    b, d = c.shape
    cols = w_ada.shape[1]
    rows = 8
    tn = 1536
    c_pad = jnp.zeros((rows, d), _F32).at[:b].set(c)
    out = pl.pallas_call(
        _ada_kernel,
        out_shape=jax.ShapeDtypeStruct((rows, cols), _F32),
        grid=(cols // tn,),
        in_specs=[pl.BlockSpec((rows, d), lambda j: (0, 0)),
                  pl.BlockSpec((d, tn), lambda j: (0, j)),
                  pl.BlockSpec((1, tn), lambda j: (0, j))],
        out_specs=pl.BlockSpec((rows, tn), lambda j: (0, j)),
        compiler_params=pltpu.CompilerParams(vmem_limit_bytes=VMEM_LIMIT),
        name="ada_mod",
    )(c_pad, w_ada, b_ada.reshape(1, cols))
    return out[:b].reshape(b, N_MOD, d)


def _rope(x0, x1, cos, sin):
    return x0 * cos - x1 * sin, x0 * sin + x1 * cos


def _proj_kernel(x_ref, mod_ref, gpre_ref, win_ref, gqa_ref, wqb_ref, gkva_ref, wkvb_ref,
                 gqh_ref, gkh_ref, cosm_ref, sinm_ref, cosg_ref, sing_ref,
                 qT_ref, k_ref, vT_ref):
    x = x_ref[0]
    t = x.shape[0]
    shift = mod_ref[0, 0:1, :]
    scale = mod_ref[0, 1:2, :]
    h = x * _rsqrt_mean_sq(x, -1) * gpre_ref[...] * (1.0 + scale) + shift
    pT = lax.dot_general(win_ref[...], h.astype(_BF16), _NT, preferred_element_type=_F32)

    cm, sm = cosm_ref[...], sinm_ref[...]
    cg, sg = cosg_ref[...], sing_ref[...]
    half_m = MLA_ROPE // 2
    half_g = GQA_HEAD_DIM // 2

    ql = pT[OFF_QLAT:OFF_QLAT + MLA_Q_LORA]
    qn = (ql * _rsqrt_mean_sq(ql, 0) * gqa_ref[...]).astype(_BF16)
    qT = jnp.dot(wqb_ref[...], qn, preferred_element_type=_F32)
    sc_m = float((MLA_NOPE + MLA_ROPE) ** -0.5 * LOG2_E)
    n_nope = MLA_HEADS * MLA_NOPE
    n_half = MLA_HEADS * half_m
    q0, q1 = _rope(qT[n_nope:n_nope + n_half], qT[n_nope + n_half:],
                   jnp.tile(cm, (MLA_HEADS, 1)), jnp.tile(sm, (MLA_HEADS, 1)))
    zq_m = jnp.zeros((HEAD_PAD - MLA_NOPE - MLA_ROPE, t), _BF16)
    for hd in range(MLA_HEADS):
        qT_ref[0, hd, 0:MLA_NOPE, :] = (qT[hd * MLA_NOPE:(hd + 1) * MLA_NOPE] * sc_m).astype(_BF16)
        qT_ref[0, hd, MLA_NOPE:MLA_NOPE + half_m, :] = (q0[hd * half_m:(hd + 1) * half_m] * sc_m).astype(_BF16)
        qT_ref[0, hd, MLA_NOPE + half_m:MLA_NOPE + MLA_ROPE, :] = (
            q1[hd * half_m:(hd + 1) * half_m] * sc_m).astype(_BF16)
        qT_ref[0, hd, MLA_NOPE + MLA_ROPE:, :] = zq_m

    kl = pT[OFF_KVLAT:OFF_KVLAT + MLA_KV_LORA]
    kn = (kl * _rsqrt_mean_sq(kl, 0) * gkva_ref[...]).astype(_BF16)
    kvT = jnp.dot(wkvb_ref[...], kn, preferred_element_type=_F32)
    k0, k1 = _rope(pT[OFF_KROPE:OFF_KROPE + half_m], pT[OFF_KROPE + half_m:OFF_KROPE + MLA_ROPE], cm, sm)
    zk_m = jnp.zeros((HEAD_PAD - MLA_NOPE - MLA_ROPE, t), _F32)
    for hd in range(MLA_HEADS):
        kTh = jnp.concatenate([kvT[hd * MLA_NOPE:(hd + 1) * MLA_NOPE], k0, k1, zk_m], axis=0)
        k_ref[0, hd] = kTh.T.astype(_BF16)
        vT_ref[0, hd, 0] = kvT[n_nope + hd * MLA_V:n_nope + (hd + 1) * MLA_V].astype(_BF16)

    sc_g = float(GQA_HEAD_DIM ** -0.5 * LOG2_E)
    zq_g = jnp.zeros((HEAD_PAD - GQA_HEAD_DIM, t), _BF16)
    for hd in range(GQA_HEADS):
        blk = pT[OFF_QG + hd * GQA_HEAD_DIM:OFF_QG + (hd + 1) * GQA_HEAD_DIM]
        bn = blk * _rsqrt_mean_sq(blk, 0) * gqh_ref[...]
        g0, g1 = _rope(bn[:half_g], bn[half_g:], cg, sg)
        qT_ref[0, MLA_HEADS + hd, 0:half_g, :] = (g0 * sc_g).astype(_BF16)
        qT_ref[0, MLA_HEADS + hd, half_g:GQA_HEAD_DIM, :] = (g1 * sc_g).astype(_BF16)
        qT_ref[0, MLA_HEADS + hd, GQA_HEAD_DIM:, :] = zq_g

    zk_g = jnp.zeros((HEAD_PAD - GQA_HEAD_DIM, t), _F32)
    for hd in range(GQA_KV_HEADS):
        blk = pT[OFF_KG + hd * GQA_HEAD_DIM:OFF_KG + (hd + 1) * GQA_HEAD_DIM]
        bn = blk * _rsqrt_mean_sq(blk, 0) * gkh_ref[...]
        g0, g1 = _rope(bn[:half_g], bn[half_g:], cg, sg)
        kTh = jnp.concatenate([g0, g1, zk_g], axis=0)
        k_ref[0, MLA_HEADS + hd] = kTh.T.astype(_BF16)
        vT_ref[0, MLA_HEADS + hd, 0] = pT[OFF_VG + hd * V_DIM:OFF_VG + (hd + 1) * V_DIM].astype(_BF16)


def _project(x, mod, gpre, winT, gqa, wqbT, gkva, wkvbT, gqh, gkh, cosm, sinm, cosg, sing):
    b, s, d = x.shape
    t = T_PROJ
    nt = s // t
    full = lambda a: pl.BlockSpec(a.shape, lambda bi, i: (0,) * a.ndim)
    tab = lambda a: pl.BlockSpec((a.shape[0], t), lambda bi, i: (0, i))
    return pl.pallas_call(
        _proj_kernel,
        out_shape=(jax.ShapeDtypeStruct((b, N_HEADS, HEAD_PAD, s), _BF16),
                   jax.ShapeDtypeStruct((b, N_KV, s, HEAD_PAD), _BF16),
                   jax.ShapeDtypeStruct((b, N_KV, nt, V_DIM, t), _BF16)),
        grid=(b, nt),
        in_specs=[pl.BlockSpec((1, t, d), lambda bi, i: (bi, i, 0)),
                  pl.BlockSpec((1, N_MOD, d), lambda bi, i: (bi, 0, 0)),
                  full(gpre), full(winT), full(gqa), full(wqbT), full(gkva), full(wkvbT),
                  full(gqh), full(gkh), tab(cosm), tab(sinm), tab(cosg), tab(sing)],
        out_specs=(pl.BlockSpec((1, N_HEADS, HEAD_PAD, t), lambda bi, i: (bi, 0, 0, i)),
                   pl.BlockSpec((1, N_KV, t, HEAD_PAD), lambda bi, i: (bi, 0, i, 0)),
                   pl.BlockSpec((1, N_KV, 1, V_DIM, t), lambda bi, i: (bi, 0, i, 0, 0))),
        compiler_params=pltpu.CompilerParams(
            dimension_semantics=("arbitrary", "arbitrary"), vmem_limit_bytes=VMEM_LIMIT),
        name="in_proj",
    )(x, mod, gpre, winT, gqa, wqbT, gkva, wkvbT, gqh, gkh, cosm, sinm, cosg, sing)


def _attn_kernel(qT_ref, k_ref, vT_ref, o_ref, *slots, n_chunks, tk):
    qT = qT_ref[0, 0]
    tq = qT.shape[1]
    ns = len(slots)

    def scores(j, slot):
        kc = k_ref[0, 0, pl.ds(pl.multiple_of(j * tk, tk), tk), :]
        s = jnp.dot(kc, qT, preferred_element_type=_F32)
        slots[slot][...] = s
        return jnp.max(s, axis=0, keepdims=True)

    def accumulate(j, slot, mc, carry):
        m, l, acc = carry
        m_new = jnp.maximum(m, mc)
        alpha = jnp.exp2(m - m_new)
        p = jnp.exp2(slots[slot][...] - m_new)
        l = alpha * l + jnp.sum(p, axis=0, keepdims=True)
        acc = alpha * acc + jnp.dot(vT_ref[0, 0, j], p.astype(_BF16), preferred_element_type=_F32)
        return m_new, l, acc

    def group(jj, state):
        mcs, carry = state
        mcs = list(mcs)
        j = ns * jj
        for u in range(ns):
            nxt = scores(j + u + 2, (u + 2) % ns)
            carry = accumulate(j + u, u, mcs[u], carry)
            mcs[(u + 2) % ns] = nxt
        return tuple(mcs), carry

    carry = (jnp.full((1, tq), -1e30, _F32), jnp.zeros((1, tq), _F32), jnp.zeros((V_DIM, tq), _F32))
    zero = jnp.zeros((1, tq), _F32)
    mcs = (scores(0, 0), scores(1, 1)) + (zero,) * (ns - 2)
    n_loop = (n_chunks - 2) // ns
    mcs, carry = lax.fori_loop(0, n_loop, group, (mcs, carry))
    mcs = list(mcs)
    for j in range(n_loop * ns, n_chunks):
        if j + 2 < n_chunks:
            mcs[(j + 2) % ns] = scores(j + 2, (j + 2) % ns)
        carry = accumulate(j, j % ns, mcs[j % ns], carry)
    _, l, acc = carry
    o_ref[0, 0] = acc / l


def _kv_index(h):
    return jnp.where(h < MLA_HEADS, h, MLA_HEADS + (h - MLA_HEADS) // GQA_GROUP)


def _attention(qT, k, vT):
    b, nh, dp, s = qT.shape
    n_chunks, tk = vT.shape[2], vT.shape[4]
    assert n_chunks > N_SCORE_SLOTS
    return pl.pallas_call(
        functools.partial(_attn_kernel, n_chunks=n_chunks, tk=tk),
        out_shape=jax.ShapeDtypeStruct((b, nh, V_DIM, s), _F32),
        grid=(b, nh, s // T_Q),
        in_specs=[pl.BlockSpec((1, 1, dp, T_Q), lambda bi, h, i: (bi, h, 0, i)),
                  pl.BlockSpec((1, 1, s, dp), lambda bi, h, i: (bi, _kv_index(h), 0, 0)),
                  pl.BlockSpec((1, 1, n_chunks, V_DIM, tk), lambda bi, h, i: (bi, _kv_index(h), 0, 0, 0))],
        out_specs=pl.BlockSpec((1, 1, V_DIM, T_Q), lambda bi, h, i: (bi, h, 0, i)),
        scratch_shapes=[pltpu.VMEM((tk, T_Q), _F32) for _ in range(N_SCORE_SLOTS)],
        compiler_params=pltpu.CompilerParams(
            dimension_semantics=("arbitrary", "arbitrary", "arbitrary"), vmem_limit_bytes=VMEM_LIMIT),
        name="attention",
    )(qT, k, vT)


def _first_argmax(v, rows, n):
    m = jnp.max(v, axis=0, keepdims=True)
    idx = jnp.min(jnp.where(v == m, rows, n), axis=0, keepdims=True)
    return m, idx


def _post_kernel(x_ref, o_ref, mod_ref, gmla_ref, ggqa_ref, wout_ref, gpost_ref, gffn_ref,
                 wrh_ref, wrl_ref, br_ref,
                 x1_ref, hp_ref, idx_ref, w_ref, rank_ref, cnt_ref, carry_ref):
    first = jnp.logical_and(pl.program_id(0) == 0, pl.program_id(1) == 0)

    @pl.when(first)
    def _():
        carry_ref[...] = jnp.zeros_like(carry_ref)

    x = x_ref[0]
    t = x.shape[0]
    gate_m = mod_ref[0, 2:3, :]
    shift_f = mod_ref[0, 3:4, :]
    scale_f = mod_ref[0, 4:5, :]

    o = o_ref[0].reshape(N_HEADS * V_DIM, t)
    half = MLA_HEADS * V_DIM
    om, og = o[:half], o[half:]
    mixedT = jnp.concatenate([om * _rsqrt_mean_sq(om, 0) * gmla_ref[...],
                              og * _rsqrt_mean_sq(og, 0) * ggqa_ref[...]], axis=0).astype(_BF16)
    y = jnp.dot(wout_ref[...], mixedT, preferred_element_type=_F32).T
    x1 = x + gate_m * (y * _rsqrt_mean_sq(y, -1) * gpost_ref[...])
    x1_ref[0] = x1
    h2 = x1 * _rsqrt_mean_sq(x1, -1) * gffn_ref[...] * (1.0 + scale_f) + shift_f

    _store_packed(hp_ref, h2)

    hh = h2.astype(_BF16)
    hl = (h2 - hh.astype(_F32)).astype(_BF16)
    wrh = wrh_ref[...]
    logits = (lax.dot_general(wrh, hh, _NT, preferred_element_type=_F32)
              + lax.dot_general(wrh, hl, _NT, preferred_element_type=_F32)
              + lax.dot_general(wrl_ref[...], hh, _NT, preferred_element_type=_F32))
    scores = jax.nn.sigmoid(logits)
    biased = scores + br_ref[...]
    neg = -jnp.inf

    rows_g = lax.broadcasted_iota(jnp.int32, (GROUP_SIZE, t), 0)
    gscores = []
    for g in range(N_GROUPS):
        v = biased[g * GROUP_SIZE:(g + 1) * GROUP_SIZE]
        m1, i1 = _first_argmax(v, rows_g, GROUP_SIZE)
        m2 = jnp.max(jnp.where(rows_g == i1, neg, v), axis=0, keepdims=True)
        gscores.append(m1 + m2)
    cur = jnp.concatenate(gscores, axis=0)
    rows_8 = lax.broadcasted_iota(jnp.int32, (N_GROUPS, t), 0)
    gsel = jnp.zeros((N_GROUPS, t), _F32)
    for _ in range(TOPK_GROUPS):
        _, gi = _first_argmax(cur, rows_8, N_GROUPS)
        hit = rows_8 == gi
        gsel = jnp.where(hit, 1.0, gsel)
        cur = jnp.where(hit, neg, cur)
    emask = jnp.concatenate(
        [jnp.broadcast_to(gsel[g:g + 1], (GROUP_SIZE, t)) for g in range(N_GROUPS)], axis=0) > 0.5
    masked = jnp.where(emask, biased, neg)

    rows_e = lax.broadcasted_iota(jnp.int32, (N_EXPERTS, t), 0)
    idxs, ws = [], []
    for _ in range(TOP_K):
        _, ei = _first_argmax(masked, rows_e, N_EXPERTS)
        hit = rows_e == ei
        idxs.append(ei)
        ws.append(jnp.sum(jnp.where(hit, scores, 0.0), axis=0, keepdims=True))
        masked = jnp.where(hit, neg, masked)
    wsum = ws[0]
    for wk in ws[1:]:
        wsum = wsum + wk

    onehot = jnp.zeros((N_EXPERTS, t), _F32)
    for ei in idxs:
        onehot = onehot + jnp.where(rows_e == ei, 1.0, 0.0)
    upper = (lax.broadcasted_iota(jnp.int32, (t, t), 0) < lax.broadcasted_iota(jnp.int32, (t, t), 1))
    before = jnp.dot(onehot.astype(_BF16), jnp.where(upper, 1.0, 0.0).astype(_BF16),
                     preferred_element_type=_F32) + carry_ref[...]
    for kk in range(TOP_K):
        idx_ref[kk:kk + 1, :] = idxs[kk]
        w_ref[kk:kk + 1, :] = ws[kk] / wsum * ROUTED_SCALE
        rank_ref[kk:kk + 1, :] = jnp.sum(
            jnp.where(rows_e == idxs[kk], before, 0.0), axis=0, keepdims=True).astype(jnp.int32)
    total = carry_ref[...] + jnp.sum(onehot, axis=1, keepdims=True)
    carry_ref[...] = total
    cnt_ref[...] = jnp.broadcast_to(total, cnt_ref.shape)


def _post_attention(x, oT, mod, gmla, ggqa, woutT, gpost, gffn, wrh, wrl, br):
    b, s, d = x.shape
    n = b * s
    t = T_POST
    nt = s // t
    full = lambda a: pl.BlockSpec(a.shape, lambda bi, i: (0,) * a.ndim)
    tok = lambda bi, i: (0, bi * nt + i)
    return pl.pallas_call(
        _post_kernel,
        out_shape=(jax.ShapeDtypeStruct((b, s, d), _F32),
                   jax.ShapeDtypeStruct((n * PACK_ROWS, 128), _U32),
                   jax.ShapeDtypeStruct((TOP_K, n), jnp.int32),
                   jax.ShapeDtypeStruct((TOP_K, n), _F32),
                   jax.ShapeDtypeStruct((TOP_K, n), jnp.int32),
                   jax.ShapeDtypeStruct((N_EXPERTS, 128), _F32)),
        grid=(b, nt),
        in_specs=[pl.BlockSpec((1, t, d), lambda bi, i: (bi, i, 0)),
                  pl.BlockSpec((1, N_HEADS, V_DIM, t), lambda bi, i: (bi, 0, 0, i)),
                  pl.BlockSpec((1, N_MOD, d), lambda bi, i: (bi, 0, 0)),
                  full(gmla), full(ggqa), full(woutT), full(gpost), full(gffn),
                  full(wrh), full(wrl), full(br)],
        out_specs=(pl.BlockSpec((1, t, d), lambda bi, i: (bi, i, 0)),
                   pl.BlockSpec((t * PACK_ROWS, 128), lambda bi, i: (bi * nt + i, 0)),
                   pl.BlockSpec((TOP_K, t), tok),
                   pl.BlockSpec((TOP_K, t), tok),
                   pl.BlockSpec((TOP_K, t), tok),
                   pl.BlockSpec((N_EXPERTS, 128), lambda bi, i: (0, 0))),
        scratch_shapes=[pltpu.VMEM((N_EXPERTS, 1), _F32)],
        compiler_params=pltpu.CompilerParams(
            dimension_semantics=("arbitrary", "arbitrary"), vmem_limit_bytes=VMEM_LIMIT),
        name="out_proj_router",
    )(x, oT, mod, gmla, ggqa, woutT, gpost, gffn, wrh, wrl, br)


def _slot_kernel(idx_ref, rank_ref, start_ref, dest_ref):
    t = idx_ref.shape[1]
    rows_e = lax.broadcasted_iota(jnp.int32, (N_EXPERTS, t), 0)
    start = start_ref[...]
    for kk in range(TOP_K):
        base = jnp.sum(jnp.where(rows_e == idx_ref[kk:kk + 1, :], start, 0.0), axis=0, keepdims=True)
        dest_ref[kk:kk + 1, :] = rank_ref[kk:kk + 1, :] + base.astype(jnp.int32)


def _slots(idx_t, rank_t, start):
    n = idx_t.shape[1]
    t = T_POST
    return pl.pallas_call(
        _slot_kernel,
        out_shape=jax.ShapeDtypeStruct((TOP_K, n), jnp.int32),
        grid=(n // t,),
        in_specs=[pl.BlockSpec((TOP_K, t), lambda i: (0, i)),
                  pl.BlockSpec((TOP_K, t), lambda i: (0, i)),
                  pl.BlockSpec((N_EXPERTS, 1), lambda i: (0, 0))],
        out_specs=pl.BlockSpec((TOP_K, t), lambda i: (0, i)),
        compiler_params=pltpu.CompilerParams(dimension_semantics=("arbitrary",)),
        name="slot_index",
    )(idx_t, rank_t, start.astype(_F32).reshape(N_EXPERTS, 1))


def _row_copy(src, src_row, dst, dst_row, sem):
    return pltpu.make_async_copy(_token_rows(src, src_row), _token_rows(dst, dst_row), sem)


def _dispatch_kernel(dest_ref, h_ref, xs_in_ref, xs_ref, sem):
    del xs_in_ref
    t_rows = h_ref.shape[0] // PACK_ROWS

    def issue(t, _):
        for kk in range(TOP_K):
            _row_copy(h_ref, t, xs_ref, dest_ref[0, 0, t * TOP_K + kk], sem).start()
        return 0

    lax.fori_loop(0, t_rows, issue, 0)

    def drain(t, _):
        for kk in range(TOP_K):
            _row_copy(h_ref, 0, xs_ref, 0, sem).wait()
        return 0

    lax.fori_loop(0, t_rows, drain, 0)


def _dispatch(dest_tiles, hp, n_slots):
    n = hp.shape[0] // PACK_ROWS
    t = T_ROW
    xs0 = jnp.zeros((n_slots * PACK_ROWS, 128), _U32)
    return pl.pallas_call(
        _dispatch_kernel,
        out_shape=jax.ShapeDtypeStruct((n_slots * PACK_ROWS, 128), _U32),
        grid=(n // t,),
        in_specs=[pl.BlockSpec((1, 1, t * TOP_K), lambda i: (i, 0, 0), memory_space=pltpu.SMEM),
                  pl.BlockSpec((t * PACK_ROWS, 128), lambda i: (i, 0)),
                  pl.BlockSpec(memory_space=pl.ANY)],
        out_specs=pl.BlockSpec(memory_space=pl.ANY),
        scratch_shapes=[pltpu.SemaphoreType.DMA(())],
        input_output_aliases={2: 0},
        compiler_params=pltpu.CompilerParams(
            dimension_semantics=("arbitrary",), vmem_limit_bytes=VMEM_LIMIT),
        name="dispatch",
    )(dest_tiles, hp, xs0)


def _swiglu(xb, wgu, wd):
    gu = jnp.dot(xb, wgu, preferred_element_type=_F32)
    g, u = gu[:, :EXPERT_FF], gu[:, EXPERT_FF:]
    mid = (g * jax.nn.sigmoid(g) * u).astype(_BF16)
    return jnp.dot(mid, wd, preferred_element_type=_F32)


def _load_packed_bf16(ref, rows):
    return jnp.concatenate([p.astype(_BF16) for p in _load_packed(ref, 0, rows)], axis=1)


def _expert_kernel(bexp_ref, bmap_ref, nused_ref, x_ref, wg_ref, wu_ref, wd_ref, y_ref, wgu_s, wd_s):
    del bmap_ref
    b = pl.program_id(0)
    e = bexp_ref[b]
    e_prev = bexp_ref[jnp.maximum(b - 1, 0)]

    @pl.when(jnp.logical_or(b == 0, e != e_prev))
    def _():
        wgu_s[:, :EXPERT_FF] = wg_ref[0].astype(_BF16)
        wgu_s[:, EXPERT_FF:] = wu_ref[0].astype(_BF16)
        wd_s[...] = wd_ref[0].astype(_BF16)

    @pl.when(b < nused_ref[0])
    def _():
        _store_packed(y_ref, _swiglu(_load_packed_bf16(x_ref, T_M), wgu_s[...], wd_s[...]))

    @pl.when(b >= nused_ref[0])
    def _():
        y_ref[...] = jnp.zeros_like(y_ref)


def _experts(bexp, bmap, nused, xs, wg, wu, wd):
    n_slots = xs.shape[0] // PACK_ROWS
    nb = n_slots // T_M
    d, ff = wg.shape[1], wg.shape[2]
    blk = T_M * PACK_ROWS
    return pl.pallas_call(
        _expert_kernel,
        out_shape=jax.ShapeDtypeStruct(xs.shape, _U32),
        grid_spec=pltpu.PrefetchScalarGridSpec(
            num_scalar_prefetch=3,
            grid=(nb,),
            in_specs=[pl.BlockSpec((blk, 128), lambda i, be, bm, nu: (bm[i], 0)),
                      pl.BlockSpec((1, d, ff), lambda i, be, bm, nu: (be[i], 0, 0)),
                      pl.BlockSpec((1, d, ff), lambda i, be, bm, nu: (be[i], 0, 0)),
                      pl.BlockSpec((1, ff, d), lambda i, be, bm, nu: (be[i], 0, 0))],
            out_specs=pl.BlockSpec((blk, 128), lambda i, be, bm, nu: (i, 0)),
            scratch_shapes=[pltpu.VMEM((d, 2 * ff), _BF16), pltpu.VMEM((ff, d), _BF16)]),
        compiler_params=pltpu.CompilerParams(
            dimension_semantics=("arbitrary",), vmem_limit_bytes=VMEM_LIMIT),
        name="experts",
    )(bexp, bmap, nused, xs, wg, wu, wd)


def _combine_kernel(dest_ref, ys_ref, hp_ref, w_ref, x1_ref, mod_ref, wgus_ref, wds_ref, gpost_ref,
                    o_ref, gbuf, sem):
    t_rows = hp_ref.shape[0] // PACK_ROWS

    def issue(t, _):
        for kk in range(TOP_K):
            _row_copy(ys_ref, dest_ref[0, 0, t * TOP_K + kk], gbuf, kk * t_rows + t, sem).start()
        return 0

    lax.fori_loop(0, t_rows, issue, 0)

    shared = _swiglu(_load_packed_bf16(hp_ref, t_rows), wgus_ref[...], wds_ref[...])

    def drain(t, _):
        for kk in range(TOP_K):
            _row_copy(ys_ref, 0, gbuf, 0, sem).wait()
        return 0

    lax.fori_loop(0, t_rows, drain, 0)

    w = w_ref[...]
    cols = None
    for kk in range(TOP_K):
        wk = w[:, kk:kk + 1]
        parts = [wk * p for p in _load_packed(gbuf, kk * t_rows, t_rows)]
        cols = parts if cols is None else [a + b for a, b in zip(cols, parts)]
    y = jnp.concatenate(cols, axis=1) + shared
    gate_f = mod_ref[0, 5:6, :]
    o_ref[...] = x1_ref[...] + gate_f * (y * _rsqrt_mean_sq(y, -1) * gpost_ref[...])


def _combine(dest_tiles, ys, hp, w_tok, x1, mod, wgus, wds, gpost, tiles_per_batch):
    n, d = x1.shape
    t = T_ROW
    full = lambda a: pl.BlockSpec(a.shape, lambda i: (0,) * a.ndim)
    return pl.pallas_call(
        _combine_kernel,
        out_shape=jax.ShapeDtypeStruct((n, d), _F32),
        grid=(n // t,),
        in_specs=[pl.BlockSpec((1, 1, t * TOP_K), lambda i: (i, 0, 0), memory_space=pltpu.SMEM),
                  pl.BlockSpec(memory_space=pl.ANY),
                  pl.BlockSpec((t * PACK_ROWS, 128), lambda i: (i, 0)),
                  pl.BlockSpec((t, TOP_K), lambda i: (i, 0)),
                  pl.BlockSpec((t, d), lambda i: (i, 0)),
                  pl.BlockSpec((1, N_MOD, d), lambda i: (i // tiles_per_batch, 0, 0)),
                  full(wgus), full(wds), full(gpost)],
        out_specs=pl.BlockSpec((t, d), lambda i: (i, 0)),
        scratch_shapes=[pltpu.VMEM((TOP_K * t * PACK_ROWS, 128), _U32), pltpu.SemaphoreType.DMA(())],
        compiler_params=pltpu.CompilerParams(
            dimension_semantics=("arbitrary",), vmem_limit_bytes=VMEM_LIMIT),
        name="combine",
    )(dest_tiles, ys, hp, w_tok, x1, mod, wgus, wds, gpost)


def _deinterleave(n):
    return np.concatenate([np.arange(0, n, 2), np.arange(1, n, 2)])


def _rope_tables(s, dim):
    rows = s // GRID_W
    row = jnp.broadcast_to(jnp.arange(rows, dtype=_F32)[:, None], (rows, GRID_W)).reshape(s)
    col = jnp.broadcast_to(jnp.arange(GRID_W, dtype=_F32)[None, :], (rows, GRID_W)).reshape(s)
    n_freq = dim // 4
    inv_freq = ROPE_THETA ** (-jnp.arange(n_freq, dtype=_F32) / n_freq)
    ang = jnp.concatenate([row[:, None] * inv_freq, col[:, None] * inv_freq], axis=-1)
    return jnp.cos(ang).T, jnp.sin(ang).T


def _in_proj_perm():
    perm = list(range(OFF_KROPE))
    perm += list(OFF_KROPE + _deinterleave(MLA_ROPE))
    for hd in range(GQA_HEADS):
        perm += list(OFF_QG + hd * GQA_HEAD_DIM + _deinterleave(GQA_HEAD_DIM))
    for hd in range(GQA_KV_HEADS):
        perm += list(OFF_KG + hd * GQA_HEAD_DIM + _deinterleave(GQA_HEAD_DIM))
    perm += list(range(OFF_VG, IN_COLS))
    return np.asarray(perm)


def _q_b_perm():
    per = MLA_NOPE + MLA_ROPE
    nope = [hd * per + j for hd in range(MLA_HEADS) for j in range(MLA_NOPE)]
    even = [hd * per + MLA_NOPE + 2 * i for hd in range(MLA_HEADS) for i in range(MLA_ROPE // 2)]
    odd = [hd * per + MLA_NOPE + 2 * i + 1 for hd in range(MLA_HEADS) for i in range(MLA_ROPE // 2)]
    return np.asarray(nope + even + odd)


def _kv_b_perm():
    per = MLA_NOPE + MLA_V
    kk = [hd * per + j for hd in range(MLA_HEADS) for j in range(MLA_NOPE)]
    vv = [hd * per + MLA_NOPE + j for hd in range(MLA_HEADS) for j in range(MLA_V)]
    return np.asarray(kk + vv)


def _layer(x, c, w_ada, b_ada, g_pre_mix, g_post_mix, g_pre_ffn, g_post_ffn, w_in, g_q_a, w_q_b,
           g_kv_a, w_kv_b, g_q_head, g_k_head, g_mla_out, g_gqa_out, w_out, w_router, b_router,
           w_gate_e, w_up_e, w_down_e, w_gate_s, w_up_s, w_down_s):
    b, s, d = x.shape
    n = b * s
    col = lambda g: g.reshape(-1, 1).astype(_F32)
    row = lambda g: g.reshape(1, -1).astype(_F32)

    mod = _ada_mod(c, w_ada, b_ada)

    winT = w_in[:, _in_proj_perm()].T.astype(_BF16)
    wqbT = w_q_b[:, _q_b_perm()].T.astype(_BF16)
    wkvbT = w_kv_b[:, _kv_b_perm()].T.astype(_BF16)
    perm_h = _deinterleave(GQA_HEAD_DIM)
    cosm, sinm = _rope_tables(s, MLA_ROPE)
    cosg, sing = _rope_tables(s, GQA_HEAD_DIM)

    qT, k, vT = _project(x, mod, row(g_pre_mix), winT, col(g_q_a), wqbT, col(g_kv_a), wkvbT,
                         col(g_q_head[perm_h]), col(g_k_head[perm_h]), cosm, sinm, cosg, sing)
    oT = _attention(qT, k, vT)

    wr_t = w_router.T.astype(_F32)
    wrh = wr_t.astype(_BF16)
    wrl = (wr_t - wrh.astype(_F32)).astype(_BF16)
    x1, hp, idx_t, w_t, rank_t, cnt = _post_attention(
        x, oT, mod, col(g_mla_out), col(g_gqa_out), w_out.T.astype(_BF16), row(g_post_mix),
        row(g_pre_ffn), wrh, wrl, col(b_router))

    counts = cnt[:, 0].astype(jnp.int32)
    blocks_e = (counts + T_M - 1) // T_M
    blk_end = jnp.cumsum(blocks_e)
    start = (blk_end - blocks_e) * T_M
    n_blocks = n * TOP_K // T_M + N_EXPERTS
    nused = blk_end[-1:]
    bidx = jnp.minimum(jnp.arange(n_blocks, dtype=jnp.int32), nused[0] - 1)
    bexp = jnp.minimum(jnp.sum(blk_end[None, :] <= bidx[:, None], axis=1), N_EXPERTS - 1).astype(jnp.int32)
    dest = _slots(idx_t, rank_t, start)
    dest_tiles = dest.T.reshape(n // T_ROW, 1, T_ROW * TOP_K)

    xs = _dispatch(dest_tiles, hp, n_blocks * T_M)
    ys = _experts(bexp, bidx, nused.astype(jnp.int32), xs, w_gate_e, w_up_e, w_down_e)
    wgus = jnp.concatenate([w_gate_s, w_up_s], axis=1).astype(_BF16)
    out = _combine(dest_tiles, ys, hp, w_t.T, x1.reshape(n, d), mod, wgus, w_down_s.astype(_BF16),
                   row(g_post_ffn), s // T_ROW)
    return out.reshape(b, s, d)


def kernel(x, c, w_ada, b_ada, g_pre_mix, g_post_mix, g_pre_ffn, g_post_ffn, w_in, g_q_a, w_q_b, g_kv_a, w_kv_b, g_q_head, g_k_head, g_mla_out, g_gqa_out, w_out, w_router, b_router, w_gate_e, w_up_e, w_down_e, w_gate_s, w_up_s, w_down_s):
    depth = w_ada.shape[0]
    for l in range(depth):
        x = _layer(x, c, w_ada[l], b_ada[l], g_pre_mix[l], g_post_mix[l], g_pre_ffn[l], g_post_ffn[l],
                   w_in[l], g_q_a[l], w_q_b[l], g_kv_a[l], w_kv_b[l], g_q_head[l], g_k_head[l],
                   g_mla_out[l], g_gqa_out[l], w_out[l], w_router[l], b_router[l], w_gate_e[l],
                   w_up_e[l], w_down_e[l], w_gate_s[l], w_up_s[l], w_down_s[l])
    return x
```

```python
import functools

import jax
import jax.numpy as jnp
import numpy as np
from jax import lax
from jax.experimental import pallas as pl
from jax.experimental.pallas import tpu as pltpu

D_MODEL = 1024
GRID_W = 64
ROPE_THETA = 10000.0
NORM_EPS = 1e-6

MLA_HEADS = 8
MLA_Q_LORA = 256
MLA_KV_LORA = 128
MLA_NOPE = 64
MLA_ROPE = 32
MLA_V = 64

GQA_HEADS = 8
GQA_KV_HEADS = 2
GQA_HEAD_DIM = 64
GQA_GROUP = GQA_HEADS // GQA_KV_HEADS

N_HEADS = MLA_HEADS + GQA_HEADS
N_KV = MLA_HEADS + GQA_KV_HEADS
HEAD_PAD = 128
V_DIM = 64

N_EXPERTS = 256
TOP_K = 8
N_GROUPS = 8
GROUP_SIZE = N_EXPERTS // N_GROUPS
TOPK_GROUPS = 4
EXPERT_FF = 256
ROUTED_SCALE = 2.5
N_MOD = 6

OFF_QLAT = 0
OFF_KVLAT = OFF_QLAT + MLA_Q_LORA
OFF_KROPE = OFF_KVLAT + MLA_KV_LORA
OFF_QG = OFF_KROPE + MLA_ROPE
OFF_KG = OFF_QG + GQA_HEADS * GQA_HEAD_DIM
OFF_VG = OFF_KG + GQA_KV_HEADS * GQA_HEAD_DIM
IN_COLS = OFF_VG + GQA_KV_HEADS * GQA_HEAD_DIM

T_PROJ = 512
T_Q = 512
T_POST = 512
T_ROW = 256
T_M = 256
PACK_ROWS = 4
N_SCORE_SLOTS = 4
LOG2_E = 1.4426950408889634
VMEM_LIMIT = 56 * 1024 * 1024

_NT = (((1,), (1,)), ((), ()))
_F32 = jnp.float32
_BF16 = jnp.bfloat16
_U32 = jnp.uint32
_HI_MASK = np.uint32(0xFFFF0000)


def _rsqrt_mean_sq(v, axis):
    return lax.rsqrt(jnp.mean(v * v, axis=axis, keepdims=True) + NORM_EPS)


def _pack_pair(lo, hi):
    lo_b = pltpu.bitcast(lo.astype(_BF16).astype(_F32), _U32) >> 16
    hi_b = pltpu.bitcast(hi.astype(_BF16).astype(_F32), _U32) & _HI_MASK
    return lo_b | hi_b


def _unpack_pair(u):
    return pltpu.bitcast(u << 16, _F32), pltpu.bitcast(u & _HI_MASK, _F32)


def _token_rows(ref, first, count=1):
    return ref.at[pl.ds(pl.multiple_of(first * PACK_ROWS, PACK_ROWS), count * PACK_ROWS), :]


def _chunk_of_tokens(first, count, c):
    return pl.ds(first * PACK_ROWS + c, count, stride=PACK_ROWS)


def _store_packed(ref, val):
    rows = val.shape[0]
    for c in range(PACK_ROWS):
        ref[_chunk_of_tokens(0, rows, c), :] = _pack_pair(
            val[:, 256 * c:256 * c + 128], val[:, 256 * c + 128:256 * c + 256])


def _load_packed(ref, first, rows):
    parts = []
    for c in range(PACK_ROWS):
        parts += list(_unpack_pair(ref[_chunk_of_tokens(first, rows, c), :]))
    return parts


def _ada_kernel(c_ref, w_ref, b_ref, o_ref):
    c = c_ref[...]
    a = (c * jax.nn.sigmoid(c)).astype(_BF16)
    o_ref[...] = jnp.dot(a, w_ref[...].astype(_BF16), preferred_element_type=_F32) + b_ref[...]


def _ada_mod(c, w_ada, b_ada):
    b, d = c.shape
    cols = w_ada.shape[1]
    rows = 8
    tn = 1536
    c_pad = jnp.zeros((rows, d), _F32).at[:b].set(c)
    out = pl.pallas_call(
        _ada_kernel,
        out_shape=jax.ShapeDtypeStruct((rows, cols), _F32),
        grid=(cols // tn,),
        in_specs=[pl.BlockSpec((rows, d), lambda j: (0, 0)),
                  pl.BlockSpec((d, tn), lambda j: (0, j)),
                  pl.BlockSpec((1, tn), lambda j: (0, j))],
        out_specs=pl.BlockSpec((rows, tn), lambda j: (0, j)),
        compiler_params=pltpu.CompilerParams(vmem_limit_bytes=VMEM_LIMIT),
        name="ada_mod",
    )(c_pad, w_ada, b_ada.reshape(1, cols))
    return out[:b].reshape(b, N_MOD, d)


def _rope(x0, x1, cos, sin):
    return x0 * cos - x1 * sin, x0 * sin + x1 * cos


def _proj_kernel(x_ref, mod_ref, gpre_ref, win_ref, gqa_ref, wqb_ref, gkva_ref, wkvb_ref,
                 gqh_ref, gkh_ref, cosm_ref, sinm_ref, cosg_ref, sing_ref,
                 qT_ref, k_ref, vT_ref):
    x = x_ref[0]
    t = x.shape[0]
    shift = mod_ref[0, 0:1, :]
    scale = mod_ref[0, 1:2, :]
    h = x * _rsqrt_mean_sq(x, -1) * gpre_ref[...] * (1.0 + scale) + shift
    pT = lax.dot_general(win_ref[...], h.astype(_BF16), _NT, preferred_element_type=_F32)

    cm, sm = cosm_ref[...], sinm_ref[...]
    cg, sg = cosg_ref[...], sing_ref[...]
    half_m = MLA_ROPE // 2
    half_g = GQA_HEAD_DIM // 2

    ql = pT[OFF_QLAT:OFF_QLAT + MLA_Q_LORA]
    qn = (ql * _rsqrt_mean_sq(ql, 0) * gqa_ref[...]).astype(_BF16)
    qT = jnp.dot(wqb_ref[...], qn, preferred_element_type=_F32)
    sc_m = float((MLA_NOPE + MLA_ROPE) ** -0.5 * LOG2_E)
    n_nope = MLA_HEADS * MLA_NOPE
    n_half = MLA_HEADS * half_m
    q0, q1 = _rope(qT[n_nope:n_nope + n_half], qT[n_nope + n_half:],
                   jnp.tile(cm, (MLA_HEADS, 1)), jnp.tile(sm, (MLA_HEADS, 1)))
    zq_m = jnp.zeros((HEAD_PAD - MLA_NOPE - MLA_ROPE, t), _BF16)
    for hd in range(MLA_HEADS):
        qT_ref[0, hd, 0:MLA_NOPE, :] = (qT[hd * MLA_NOPE:(hd + 1) * MLA_NOPE] * sc_m).astype(_BF16)
        qT_ref[0, hd, MLA_NOPE:MLA_NOPE + half_m, :] = (q0[hd * half_m:(hd + 1) * half_m] * sc_m).astype(_BF16)
        qT_ref[0, hd, MLA_NOPE + half_m:MLA_NOPE + MLA_ROPE, :] = (
            q1[hd * half_m:(hd + 1) * half_m] * sc_m).astype(_BF16)
        qT_ref[0, hd, MLA_NOPE + MLA_ROPE:, :] = zq_m

    kl = pT[OFF_KVLAT:OFF_KVLAT + MLA_KV_LORA]
    kn = (kl * _rsqrt_mean_sq(kl, 0) * gkva_ref[...]).astype(_BF16)
    kvT = jnp.dot(wkvb_ref[...], kn, preferred_element_type=_F32)
    k0, k1 = _rope(pT[OFF_KROPE:OFF_KROPE + half_m], pT[OFF_KROPE + half_m:OFF_KROPE + MLA_ROPE], cm, sm)
    zk_m = jnp.zeros((HEAD_PAD - MLA_NOPE - MLA_ROPE, t), _F32)
    for hd in range(MLA_HEADS):
        kTh = jnp.concatenate([kvT[hd * MLA_NOPE:(hd + 1) * MLA_NOPE], k0, k1, zk_m], axis=0)
        k_ref[0, hd] = kTh.T.astype(_BF16)
        vT_ref[0, hd, 0] = kvT[n_nope + hd * MLA_V:n_nope + (hd + 1) * MLA_V].astype(_BF16)

    sc_g = float(GQA_HEAD_DIM ** -0.5 * LOG2_E)
    zq_g = jnp.zeros((HEAD_PAD - GQA_HEAD_DIM, t), _BF16)
    for hd in range(GQA_HEADS):
        blk = pT[OFF_QG + hd * GQA_HEAD_DIM:OFF_QG + (hd + 1) * GQA_HEAD_DIM]
        bn = blk * _rsqrt_mean_sq(blk, 0) * gqh_ref[...]
        g0, g1 = _rope(bn[:half_g], bn[half_g:], cg, sg)
        qT_ref[0, MLA_HEADS + hd, 0:half_g, :] = (g0 * sc_g).astype(_BF16)
        qT_ref[0, MLA_HEADS + hd, half_g:GQA_HEAD_DIM, :] = (g1 * sc_g).astype(_BF16)
        qT_ref[0, MLA_HEADS + hd, GQA_HEAD_DIM:, :] = zq_g

    zk_g = jnp.zeros((HEAD_PAD - GQA_HEAD_DIM, t), _F32)
    for hd in range(GQA_KV_HEADS):
        blk = pT[OFF_KG + hd * GQA_HEAD_DIM:OFF_KG + (hd + 1) * GQA_HEAD_DIM]
        bn = blk * _rsqrt_mean_sq(blk, 0) * gkh_ref[...]
        g0, g1 = _rope(bn[:half_g], bn[half_g:], cg, sg)
        kTh = jnp.concatenate([g0, g1, zk_g], axis=0)
        k_ref[0, MLA_HEADS + hd] = kTh.T.astype(_BF16)
        vT_ref[0, MLA_HEADS + hd, 0] = pT[OFF_VG + hd * V_DIM:OFF_VG + (hd + 1) * V_DIM].astype(_BF16)


def _project(x, mod, gpre, winT, gqa, wqbT, gkva, wkvbT, gqh, gkh, cosm, sinm, cosg, sing):
    b, s, d = x.shape
    t = T_PROJ
    nt = s // t
    full = lambda a: pl.BlockSpec(a.shape, lambda bi, i: (0,) * a.ndim)
    tab = lambda a: pl.BlockSpec((a.shape[0], t), lambda bi, i: (0, i))
    return pl.pallas_call(
        _proj_kernel,
        out_shape=(jax.ShapeDtypeStruct((b, N_HEADS, HEAD_PAD, s), _BF16),
                   jax.ShapeDtypeStruct((b, N_KV, s, HEAD_PAD), _BF16),
                   jax.ShapeDtypeStruct((b, N_KV, nt, V_DIM, t), _BF16)),
        grid=(b, nt),
        in_specs=[pl.BlockSpec((1, t, d), lambda bi, i: (bi, i, 0)),
                  pl.BlockSpec((1, N_MOD, d), lambda bi, i: (bi, 0, 0)),
                  full(gpre), full(winT), full(gqa), full(wqbT), full(gkva), full(wkvbT),
                  full(gqh), full(gkh), tab(cosm), tab(sinm), tab(cosg), tab(sing)],
        out_specs=(pl.BlockSpec((1, N_HEADS, HEAD_PAD, t), lambda bi, i: (bi, 0, 0, i)),
                   pl.BlockSpec((1, N_KV, t, HEAD_PAD), lambda bi, i: (bi, 0, i, 0)),
                   pl.BlockSpec((1, N_KV, 1, V_DIM, t), lambda bi, i: (bi, 0, i, 0, 0))),
        compiler_params=pltpu.CompilerParams(
            dimension_semantics=("arbitrary", "arbitrary"), vmem_limit_bytes=VMEM_LIMIT),
        name="in_proj",
    )(x, mod, gpre, winT, gqa, wqbT, gkva, wkvbT, gqh, gkh, cosm, sinm, cosg, sing)


def _attn_kernel(qT_ref, k_ref, vT_ref, o_ref, *slots, n_chunks, tk):
    qT = qT_ref[0, 0]
    tq = qT.shape[1]
    ns = len(slots)

    def scores(j, slot):
        kc = k_ref[0, 0, pl.ds(pl.multiple_of(j * tk, tk), tk), :]
        s = jnp.dot(kc, qT, preferred_element_type=_F32)
        slots[slot][...] = s
        return jnp.max(s, axis=0, keepdims=True)

    def accumulate(j, slot, mc, carry):
        m, l, acc = carry
        m_new = jnp.maximum(m, mc)
        alpha = jnp.exp2(m - m_new)
        p = jnp.exp2(slots[slot][...] - m_new)
        l = alpha * l + jnp.sum(p, axis=0, keepdims=True)
        acc = alpha * acc + jnp.dot(vT_ref[0, 0, j], p.astype(_BF16), preferred_element_type=_F32)
        return m_new, l, acc

    def group(jj, state):
        mcs, carry = state
        mcs = list(mcs)
        j = ns * jj
        for u in range(ns):
            nxt = scores(j + u + 2, (u + 2) % ns)
            carry = accumulate(j + u, u, mcs[u], carry)
            mcs[(u + 2) % ns] = nxt
        return tuple(mcs), carry

    carry = (jnp.full((1, tq), -1e30, _F32), jnp.zeros((1, tq), _F32), jnp.zeros((V_DIM, tq), _F32))
    zero = jnp.zeros((1, tq), _F32)
    mcs = (scores(0, 0), scores(1, 1)) + (zero,) * (ns - 2)
    n_loop = (n_chunks - 2) // ns
    mcs, carry = lax.fori_loop(0, n_loop, group, (mcs, carry))
    mcs = list(mcs)
    for j in range(n_loop * ns, n_chunks):
        if j + 2 < n_chunks:
            mcs[(j + 2) % ns] = scores(j + 2, (j + 2) % ns)
        carry = accumulate(j, j % ns, mcs[j % ns], carry)
    _, l, acc = carry
    o_ref[0, 0] = acc / l


def _kv_index(h):
    return jnp.where(h < MLA_HEADS, h, MLA_HEADS + (h - MLA_HEADS) // GQA_GROUP)


def _attention(qT, k, vT):
    b, nh, dp, s = qT.shape
    n_chunks, tk = vT.shape[2], vT.shape[4]
    assert n_chunks > N_SCORE_SLOTS
    return pl.pallas_call(
        functools.partial(_attn_kernel, n_chunks=n_chunks, tk=tk),
        out_shape=jax.ShapeDtypeStruct((b, nh, V_DIM, s), _F32),
        grid=(b, nh, s // T_Q),
        in_specs=[pl.BlockSpec((1, 1, dp, T_Q), lambda bi, h, i: (bi, h, 0, i)),
                  pl.BlockSpec((1, 1, s, dp), lambda bi, h, i: (bi, _kv_index(h), 0, 0)),
                  pl.BlockSpec((1, 1, n_chunks, V_DIM, tk), lambda bi, h, i: (bi, _kv_index(h), 0, 0, 0))],
        out_specs=pl.BlockSpec((1, 1, V_DIM, T_Q), lambda bi, h, i: (bi, h, 0, i)),
        scratch_shapes=[pltpu.VMEM((tk, T_Q), _F32) for _ in range(N_SCORE_SLOTS)],
        compiler_params=pltpu.CompilerParams(
            dimension_semantics=("arbitrary", "arbitrary", "arbitrary"), vmem_limit_bytes=VMEM_LIMIT),
        name="attention",
    )(qT, k, vT)


def _first_argmax(v, rows, n):
    m = jnp.max(v, axis=0, keepdims=True)
    idx = jnp.min(jnp.where(v == m, rows, n), axis=0, keepdims=True)
    return m, idx


def _post_kernel(x_ref, o_ref, mod_ref, gmla_ref, ggqa_ref, wout_ref, gpost_ref, gffn_ref,
                 wrh_ref, wrl_ref, br_ref,
                 x1_ref, hp_ref, idx_ref, w_ref, rank_ref, cnt_ref, carry_ref):
    first = jnp.logical_and(pl.program_id(0) == 0, pl.program_id(1) == 0)

    @pl.when(first)
    def _():
        carry_ref[...] = jnp.zeros_like(carry_ref)

    x = x_ref[0]
    t = x.shape[0]
    gate_m = mod_ref[0, 2:3, :]
    shift_f = mod_ref[0, 3:4, :]
    scale_f = mod_ref[0, 4:5, :]

    o = o_ref[0].reshape(N_HEADS * V_DIM, t)
    half = MLA_HEADS * V_DIM
    om, og = o[:half], o[half:]
    mixedT = jnp.concatenate([om * _rsqrt_mean_sq(om, 0) * gmla_ref[...],
                              og * _rsqrt_mean_sq(og, 0) * ggqa_ref[...]], axis=0).astype(_BF16)
    y = jnp.dot(wout_ref[...], mixedT, preferred_element_type=_F32).T
    x1 = x + gate_m * (y * _rsqrt_mean_sq(y, -1) * gpost_ref[...])
    x1_ref[0] = x1
    h2 = x1 * _rsqrt_mean_sq(x1, -1) * gffn_ref[...] * (1.0 + scale_f) + shift_f

    _store_packed(hp_ref, h2)

    hh = h2.astype(_BF16)
    hl = (h2 - hh.astype(_F32)).astype(_BF16)
    wrh = wrh_ref[...]
    logits = (lax.dot_general(wrh, hh, _NT, preferred_element_type=_F32)
              + lax.dot_general(wrh, hl, _NT, preferred_element_type=_F32)
              + lax.dot_general(wrl_ref[...], hh, _NT, preferred_element_type=_F32))
    scores = jax.nn.sigmoid(logits)
    biased = scores + br_ref[...]
    neg = -jnp.inf

    rows_g = lax.broadcasted_iota(jnp.int32, (GROUP_SIZE, t), 0)
    gscores = []
    for g in range(N_GROUPS):
        v = biased[g * GROUP_SIZE:(g + 1) * GROUP_SIZE]
        m1, i1 = _first_argmax(v, rows_g, GROUP_SIZE)
        m2 = jnp.max(jnp.where(rows_g == i1, neg, v), axis=0, keepdims=True)
        gscores.append(m1 + m2)
    cur = jnp.concatenate(gscores, axis=0)
    rows_8 = lax.broadcasted_iota(jnp.int32, (N_GROUPS, t), 0)
    gsel = jnp.zeros((N_GROUPS, t), _F32)
    for _ in range(TOPK_GROUPS):
        _, gi = _first_argmax(cur, rows_8, N_GROUPS)
        hit = rows_8 == gi
        gsel = jnp.where(hit, 1.0, gsel)
        cur = jnp.where(hit, neg, cur)
    emask = jnp.concatenate(
        [jnp.broadcast_to(gsel[g:g + 1], (GROUP_SIZE, t)) for g in range(N_GROUPS)], axis=0) > 0.5
    masked = jnp.where(emask, biased, neg)

    rows_e = lax.broadcasted_iota(jnp.int32, (N_EXPERTS, t), 0)
    idxs, ws = [], []
    for _ in range(TOP_K):
        _, ei = _first_argmax(masked, rows_e, N_EXPERTS)
        hit = rows_e == ei
        idxs.append(ei)
        ws.append(jnp.sum(jnp.where(hit, scores, 0.0), axis=0, keepdims=True))
        masked = jnp.where(hit, neg, masked)
    wsum = ws[0]
    for wk in ws[1:]:
        wsum = wsum + wk

    onehot = jnp.zeros((N_EXPERTS, t), _F32)
    for ei in idxs:
        onehot = onehot + jnp.where(rows_e == ei, 1.0, 0.0)
    upper = (lax.broadcasted_iota(jnp.int32, (t, t), 0) < lax.broadcasted_iota(jnp.int32, (t, t), 1))
    before = jnp.dot(onehot.astype(_BF16), jnp.where(upper, 1.0, 0.0).astype(_BF16),
                     preferred_element_type=_F32) + carry_ref[...]
    for kk in range(TOP_K):
        idx_ref[kk:kk + 1, :] = idxs[kk]
        w_ref[kk:kk + 1, :] = ws[kk] / wsum * ROUTED_SCALE
        rank_ref[kk:kk + 1, :] = jnp.sum(
            jnp.where(rows_e == idxs[kk], before, 0.0), axis=0, keepdims=True).astype(jnp.int32)
    total = carry_ref[...] + jnp.sum(onehot, axis=1, keepdims=True)
    carry_ref[...] = total
    cnt_ref[...] = jnp.broadcast_to(total, cnt_ref.shape)


def _post_attention(x, oT, mod, gmla, ggqa, woutT, gpost, gffn, wrh, wrl, br):
    b, s, d = x.shape
    n = b * s
    t = T_POST
    nt = s // t
    full = lambda a: pl.BlockSpec(a.shape, lambda bi, i: (0,) * a.ndim)
    tok = lambda bi, i: (0, bi * nt + i)
    return pl.pallas_call(
        _post_kernel,
        out_shape=(jax.ShapeDtypeStruct((b, s, d), _F32),
                   jax.ShapeDtypeStruct((n * PACK_ROWS, 128), _U32),
                   jax.ShapeDtypeStruct((TOP_K, n), jnp.int32),
                   jax.ShapeDtypeStruct((TOP_K, n), _F32),
                   jax.ShapeDtypeStruct((TOP_K, n), jnp.int32),
                   jax.ShapeDtypeStruct((N_EXPERTS, 128), _F32)),
        grid=(b, nt),
        in_specs=[pl.BlockSpec((1, t, d), lambda bi, i: (bi, i, 0)),
                  pl.BlockSpec((1, N_HEADS, V_DIM, t), lambda bi, i: (bi, 0, 0, i)),
                  pl.BlockSpec((1, N_MOD, d), lambda bi, i: (bi, 0, 0)),
                  full(gmla), full(ggqa), full(woutT), full(gpost), full(gffn),
                  full(wrh), full(wrl), full(br)],
        out_specs=(pl.BlockSpec((1, t, d), lambda bi, i: (bi, i, 0)),
                   pl.BlockSpec((t * PACK_ROWS, 128), lambda bi, i: (bi * nt + i, 0)),
                   pl.BlockSpec((TOP_K, t), tok),
                   pl.BlockSpec((TOP_K, t), tok),
                   pl.BlockSpec((TOP_K, t), tok),
                   pl.BlockSpec((N_EXPERTS, 128), lambda bi, i: (0, 0))),
        scratch_shapes=[pltpu.VMEM((N_EXPERTS, 1), _F32)],
        compiler_params=pltpu.CompilerParams(
            dimension_semantics=("arbitrary", "arbitrary"), vmem_limit_bytes=VMEM_LIMIT),
        name="out_proj_router",
    )(x, oT, mod, gmla, ggqa, woutT, gpost, gffn, wrh, wrl, br)


def _slot_kernel(idx_ref, rank_ref, start_ref, dest_ref):
    t = idx_ref.shape[1]
    rows_e = lax.broadcasted_iota(jnp.int32, (N_EXPERTS, t), 0)
    start = start_ref[...]
    for kk in range(TOP_K):
        base = jnp.sum(jnp.where(rows_e == idx_ref[kk:kk + 1, :], start, 0.0), axis=0, keepdims=True)
        dest_ref[kk:kk + 1, :] = rank_ref[kk:kk + 1, :] + base.astype(jnp.int32)


def _slots(idx_t, rank_t, start):
    n = idx_t.shape[1]
    t = T_POST
    return pl.pallas_call(
        _slot_kernel,
        out_shape=jax.ShapeDtypeStruct((TOP_K, n), jnp.int32),
        grid=(n // t,),
        in_specs=[pl.BlockSpec((TOP_K, t), lambda i: (0, i)),
                  pl.BlockSpec((TOP_K, t), lambda i: (0, i)),
                  pl.BlockSpec((N_EXPERTS, 1), lambda i: (0, 0))],
        out_specs=pl.BlockSpec((TOP_K, t), lambda i: (0, i)),
        compiler_params=pltpu.CompilerParams(dimension_semantics=("arbitrary",)),
        name="slot_index",
    )(idx_t, rank_t, start.astype(_F32).reshape(N_EXPERTS, 1))


def _row_copy(src, src_row, dst, dst_row, sem):
    return pltpu.make_async_copy(_token_rows(src, src_row), _token_rows(dst, dst_row), sem)


def _dispatch_kernel(dest_ref, h_ref, xs_in_ref, xs_ref, sem):
    del xs_in_ref
    t_rows = h_ref.shape[0] // PACK_ROWS

    def issue(t, _):
        for kk in range(TOP_K):
            _row_copy(h_ref, t, xs_ref, dest_ref[0, 0, t * TOP_K + kk], sem).start(priority=kk % 2)
        return 0

    lax.fori_loop(0, t_rows, issue, 0)

    def drain(t, _):
        for kk in range(TOP_K):
            _row_copy(h_ref, 0, xs_ref, 0, sem).wait()
        return 0

    lax.fori_loop(0, t_rows, drain, 0)


def _dispatch(dest_tiles, hp, n_slots):
    n = hp.shape[0] // PACK_ROWS
    t = T_ROW
    xs0 = jnp.zeros((n_slots * PACK_ROWS, 128), _U32)
    return pl.pallas_call(
        _dispatch_kernel,
        out_shape=jax.ShapeDtypeStruct((n_slots * PACK_ROWS, 128), _U32),
        grid=(n // t,),
        in_specs=[pl.BlockSpec((1, 1, t * TOP_K), lambda i: (i, 0, 0), memory_space=pltpu.SMEM),
                  pl.BlockSpec((t * PACK_ROWS, 128), lambda i: (i, 0)),
                  pl.BlockSpec(memory_space=pl.ANY)],
        out_specs=pl.BlockSpec(memory_space=pl.ANY),
        scratch_shapes=[pltpu.SemaphoreType.DMA(())],
        input_output_aliases={2: 0},
        compiler_params=pltpu.CompilerParams(
            dimension_semantics=("arbitrary",), vmem_limit_bytes=VMEM_LIMIT),
        name="dispatch",
    )(dest_tiles, hp, xs0)


def _swiglu(xb, wgu, wd):
    gu = jnp.dot(xb, wgu, preferred_element_type=_F32)
    g, u = gu[:, :EXPERT_FF], gu[:, EXPERT_FF:]
    mid = (g * jax.nn.sigmoid(g) * u).astype(_BF16)
    return jnp.dot(mid, wd, preferred_element_type=_F32)


def _load_packed_bf16(ref, rows):
    return jnp.concatenate([p.astype(_BF16) for p in _load_packed(ref, 0, rows)], axis=1)


def _expert_kernel(bexp_ref, bmap_ref, nused_ref, x_ref, wg_ref, wu_ref, wd_ref, y_ref, wgu_s, wd_s):
    del bmap_ref
    b = pl.program_id(0)
    e = bexp_ref[b]
    e_prev = bexp_ref[jnp.maximum(b - 1, 0)]

    @pl.when(jnp.logical_or(b == 0, e != e_prev))
    def _():
        wgu_s[:, :EXPERT_FF] = wg_ref[0].astype(_BF16)
        wgu_s[:, EXPERT_FF:] = wu_ref[0].astype(_BF16)
        wd_s[...] = wd_ref[0].astype(_BF16)

    @pl.when(b < nused_ref[0])
    def _():
        _store_packed(y_ref, _swiglu(_load_packed_bf16(x_ref, T_M), wgu_s[...], wd_s[...]))

    @pl.when(b >= nused_ref[0])
    def _():
        y_ref[...] = jnp.zeros_like(y_ref)


def _experts(bexp, bmap, nused, xs, wg, wu, wd):
    n_slots = xs.shape[0] // PACK_ROWS
    nb = n_slots // T_M
    d, ff = wg.shape[1], wg.shape[2]
    blk = T_M * PACK_ROWS
    return pl.pallas_call(
        _expert_kernel,
        out_shape=jax.ShapeDtypeStruct(xs.shape, _U32),
        grid_spec=pltpu.PrefetchScalarGridSpec(
            num_scalar_prefetch=3,
            grid=(nb,),
            in_specs=[pl.BlockSpec((blk, 128), lambda i, be, bm, nu: (bm[i], 0)),
                      pl.BlockSpec((1, d, ff), lambda i, be, bm, nu: (be[i], 0, 0)),
                      pl.BlockSpec((1, d, ff), lambda i, be, bm, nu: (be[i], 0, 0)),
                      pl.BlockSpec((1, ff, d), lambda i, be, bm, nu: (be[i], 0, 0))],
            out_specs=pl.BlockSpec((blk, 128), lambda i, be, bm, nu: (i, 0)),
            scratch_shapes=[pltpu.VMEM((d, 2 * ff), _BF16), pltpu.VMEM((ff, d), _BF16)]),
        compiler_params=pltpu.CompilerParams(
            dimension_semantics=("arbitrary",), vmem_limit_bytes=VMEM_LIMIT),
        name="experts",
    )(bexp, bmap, nused, xs, wg, wu, wd)


def _combine_kernel(dest_ref, ys_ref, hp_ref, w_ref, x1_ref, mod_ref, wgus_ref, wds_ref, gpost_ref,
                    o_ref, gbuf, sem):
    t_rows = hp_ref.shape[0] // PACK_ROWS

    def issue(t, _):
        for kk in range(TOP_K):
            _row_copy(ys_ref, dest_ref[0, 0, t * TOP_K + kk], gbuf, kk * t_rows + t, sem).start(priority=kk % 2)
        return 0

    lax.fori_loop(0, t_rows, issue, 0)

    shared = _swiglu(_load_packed_bf16(hp_ref, t_rows), wgus_ref[...], wds_ref[...])

    def drain(t, _):
        for kk in range(TOP_K):
            _row_copy(ys_ref, 0, gbuf, 0, sem).wait()
        return 0

    lax.fori_loop(0, t_rows, drain, 0)

    w = w_ref[...]
    cols = None
    for kk in range(TOP_K):
        wk = w[:, kk:kk + 1]
        parts = [wk * p for p in _load_packed(gbuf, kk * t_rows, t_rows)]
        cols = parts if cols is None else [a + b for a, b in zip(cols, parts)]
    y = jnp.concatenate(cols, axis=1) + shared
    gate_f = mod_ref[0, 5:6, :]
    o_ref[...] = x1_ref[...] + gate_f * (y * _rsqrt_mean_sq(y, -1) * gpost_ref[...])


def _combine(dest_tiles, ys, hp, w_tok, x1, mod, wgus, wds, gpost, tiles_per_batch):
    n, d = x1.shape
    t = T_ROW
    full = lambda a: pl.BlockSpec(a.shape, lambda i: (0,) * a.ndim)
    return pl.pallas_call(
        _combine_kernel,
        out_shape=jax.ShapeDtypeStruct((n, d), _F32),
        grid=(n // t,),
        in_specs=[pl.BlockSpec((1, 1, t * TOP_K), lambda i: (i, 0, 0), memory_space=pltpu.SMEM),
                  pl.BlockSpec(memory_space=pl.ANY),
                  pl.BlockSpec((t * PACK_ROWS, 128), lambda i: (i, 0)),
                  pl.BlockSpec((t, TOP_K), lambda i: (i, 0)),
                  pl.BlockSpec((t, d), lambda i: (i, 0)),
                  pl.BlockSpec((1, N_MOD, d), lambda i: (i // tiles_per_batch, 0, 0)),
                  full(wgus), full(wds), full(gpost)],
        out_specs=pl.BlockSpec((t, d), lambda i: (i, 0)),
        scratch_shapes=[pltpu.VMEM((TOP_K * t * PACK_ROWS, 128), _U32), pltpu.SemaphoreType.DMA(())],
        compiler_params=pltpu.CompilerParams(
            dimension_semantics=("arbitrary",), vmem_limit_bytes=VMEM_LIMIT),
        name="combine",
    )(dest_tiles, ys, hp, w_tok, x1, mod, wgus, wds, gpost)


def _deinterleave(n):
    return np.concatenate([np.arange(0, n, 2), np.arange(1, n, 2)])


def _rope_tables(s, dim):
    rows = s // GRID_W
    row = jnp.broadcast_to(jnp.arange(rows, dtype=_F32)[:, None], (rows, GRID_W)).reshape(s)
    col = jnp.broadcast_to(jnp.arange(GRID_W, dtype=_F32)[None, :], (rows, GRID_W)).reshape(s)
    n_freq = dim // 4
    inv_freq = ROPE_THETA ** (-jnp.arange(n_freq, dtype=_F32) / n_freq)
    ang = jnp.concatenate([row[:, None] * inv_freq, col[:, None] * inv_freq], axis=-1)
    return jnp.cos(ang).T, jnp.sin(ang).T


def _in_proj_perm():
    perm = list(range(OFF_KROPE))
    perm += list(OFF_KROPE + _deinterleave(MLA_ROPE))
    for hd in range(GQA_HEADS):
        perm += list(OFF_QG + hd * GQA_HEAD_DIM + _deinterleave(GQA_HEAD_DIM))
    for hd in range(GQA_KV_HEADS):
        perm += list(OFF_KG + hd * GQA_HEAD_DIM + _deinterleave(GQA_HEAD_DIM))
    perm += list(range(OFF_VG, IN_COLS))
    return np.asarray(perm)


def _q_b_perm():
    per = MLA_NOPE + MLA_ROPE
    nope = [hd * per + j for hd in range(MLA_HEADS) for j in range(MLA_NOPE)]
    even = [hd * per + MLA_NOPE + 2 * i for hd in range(MLA_HEADS) for i in range(MLA_ROPE // 2)]
    odd = [hd * per + MLA_NOPE + 2 * i + 1 for hd in range(MLA_HEADS) for i in range(MLA_ROPE // 2)]
    return np.asarray(nope + even + odd)


def _kv_b_perm():
    per = MLA_NOPE + MLA_V
    kk = [hd * per + j for hd in range(MLA_HEADS) for j in range(MLA_NOPE)]
    vv = [hd * per + MLA_NOPE + j for hd in range(MLA_HEADS) for j in range(MLA_V)]
    return np.asarray(kk + vv)


def _layer(x, c, w_ada, b_ada, g_pre_mix, g_post_mix, g_pre_ffn, g_post_ffn, w_in, g_q_a, w_q_b,
           g_kv_a, w_kv_b, g_q_head, g_k_head, g_mla_out, g_gqa_out, w_out, w_router, b_router,
           w_gate_e, w_up_e, w_down_e, w_gate_s, w_up_s, w_down_s):
    b, s, d = x.shape
    n = b * s
    col = lambda g: g.reshape(-1, 1).astype(_F32)
    row = lambda g: g.reshape(1, -1).astype(_F32)

    mod = _ada_mod(c, w_ada, b_ada)

    winT = w_in[:, _in_proj_perm()].T.astype(_BF16)
    wqbT = w_q_b[:, _q_b_perm()].T.astype(_BF16)
    wkvbT = w_kv_b[:, _kv_b_perm()].T.astype(_BF16)
    perm_h = _deinterleave(GQA_HEAD_DIM)
    cosm, sinm = _rope_tables(s, MLA_ROPE)
    cosg, sing = _rope_tables(s, GQA_HEAD_DIM)

    qT, k, vT = _project(x, mod, row(g_pre_mix), winT, col(g_q_a), wqbT, col(g_kv_a), wkvbT,
                         col(g_q_head[perm_h]), col(g_k_head[perm_h]), cosm, sinm, cosg, sing)
    oT = _attention(qT, k, vT)

    wr_t = w_router.T.astype(_F32)
    wrh = wr_t.astype(_BF16)
    wrl = (wr_t - wrh.astype(_F32)).astype(_BF16)
    x1, hp, idx_t, w_t, rank_t, cnt = _post_attention(
        x, oT, mod, col(g_mla_out), col(g_gqa_out), w_out.T.astype(_BF16), row(g_post_mix),
        row(g_pre_ffn), wrh, wrl, col(b_router))

    counts = cnt[:, 0].astype(jnp.int32)
    blocks_e = (counts + T_M - 1) // T_M
    blk_end = jnp.cumsum(blocks_e)
    start = (blk_end - blocks_e) * T_M
    n_blocks = n * TOP_K // T_M + N_EXPERTS
    nused = blk_end[-1:]
    bidx = jnp.minimum(jnp.arange(n_blocks, dtype=jnp.int32), nused[0] - 1)
    bexp = jnp.minimum(jnp.sum(blk_end[None, :] <= bidx[:, None], axis=1), N_EXPERTS - 1).astype(jnp.int32)
    dest = _slots(idx_t, rank_t, start)
    dest_tiles = dest.T.reshape(n // T_ROW, 1, T_ROW * TOP_K)

    xs = _dispatch(dest_tiles, hp, n_blocks * T_M)
    ys = _experts(bexp, bidx, nused.astype(jnp.int32), xs, w_gate_e, w_up_e, w_down_e)
    wgus = jnp.concatenate([w_gate_s, w_up_s], axis=1).astype(_BF16)
    out = _combine(dest_tiles, ys, hp, w_t.T, x1.reshape(n, d), mod, wgus, w_down_s.astype(_BF16),
                   row(g_post_ffn), s // T_ROW)
    return out.reshape(b, s, d)


def kernel(x, c, w_ada, b_ada, g_pre_mix, g_post_mix, g_pre_ffn, g_post_ffn, w_in, g_q_a, w_q_b, g_kv_a, w_kv_b, g_q_head, g_k_head, g_mla_out, g_gqa_out, w_out, w_router, b_router, w_gate_e, w_up_e, w_down_e, w_gate_s, w_up_s, w_down_s):
    depth = w_ada.shape[0]
    for l in range(depth):
        x = _layer(x, c, w_ada[l], b_ada[l], g_pre_mix[l], g_post_mix[l], g_pre_ffn[l], g_post_ffn[l],
                   w_in[l], g_q_a[l], w_q_b[l], g_kv_a[l], w_kv_b[l], g_q_head[l], g_k_head[l],
                   g_mla_out[l], g_gqa_out[l], w_out[l], w_router[l], b_router[l], w_gate_e[l],
                   w_up_e[l], w_down_e[l], w_gate_s[l], w_up_s[l], w_down_s[l])
    return x
```

```python
import functools

import jax
import jax.numpy as jnp
import numpy as np
from jax import lax
from jax.experimental import pallas as pl
from jax.experimental.pallas import tpu as pltpu

D_MODEL = 1024
GRID_W = 64
ROPE_THETA = 10000.0
NORM_EPS = 1e-6

MLA_HEADS = 8
MLA_Q_LORA = 256
MLA_KV_LORA = 128
MLA_NOPE = 64
MLA_ROPE = 32
MLA_V = 64

GQA_HEADS = 8
GQA_KV_HEADS = 2
GQA_HEAD_DIM = 64
GQA_GROUP = GQA_HEADS // GQA_KV_HEADS

N_HEADS = MLA_HEADS + GQA_HEADS
N_KV = MLA_HEADS + GQA_KV_HEADS
HEAD_PAD = 128
V_DIM = 64

N_EXPERTS = 256
TOP_K = 8
N_GROUPS = 8
GROUP_SIZE = N_EXPERTS // N_GROUPS
TOPK_GROUPS = 4
EXPERT_FF = 256
ROUTED_SCALE = 2.5
N_MOD = 6

OFF_QLAT = 0
OFF_KVLAT = OFF_QLAT + MLA_Q_LORA
OFF_KROPE = OFF_KVLAT + MLA_KV_LORA
OFF_QG = OFF_KROPE + MLA_ROPE
OFF_KG = OFF_QG + GQA_HEADS * GQA_HEAD_DIM
OFF_VG = OFF_KG + GQA_KV_HEADS * GQA_HEAD_DIM
IN_COLS = OFF_VG + GQA_KV_HEADS * GQA_HEAD_DIM

T_PROJ = 512
T_Q = 512
T_POST = 512
T_ROW = 256
T_M = 256
PACK_ROWS = 4
N_SCORE_SLOTS = 4
LOG2_E = 1.4426950408889634
VMEM_LIMIT = 56 * 1024 * 1024

_NT = (((1,), (1,)), ((), ()))
_F32 = jnp.float32
_BF16 = jnp.bfloat16
_U32 = jnp.uint32
_HI_MASK = np.uint32(0xFFFF0000)


def _rsqrt_mean_sq(v, axis):
    return lax.rsqrt(jnp.mean(v * v, axis=axis, keepdims=True) + NORM_EPS)


def _pack_pair(lo, hi):
    lo_b = pltpu.bitcast(lo.astype(_BF16).astype(_F32), _U32) >> 16
    hi_b = pltpu.bitcast(hi.astype(_BF16).astype(_F32), _U32) & _HI_MASK
    return lo_b | hi_b


def _unpack_pair(u):
    return pltpu.bitcast(u << 16, _F32), pltpu.bitcast(u & _HI_MASK, _F32)


def _token_rows(ref, first, count=1):
    return ref.at[pl.ds(pl.multiple_of(first * PACK_ROWS, PACK_ROWS), count * PACK_ROWS), :]


def _chunk_of_tokens(first, count, c):
    return pl.ds(first * PACK_ROWS + c, count, stride=PACK_ROWS)


def _store_packed(ref, val, first=0):
    rows = val.shape[0]
    for c in range(PACK_ROWS):
        ref[_chunk_of_tokens(first, rows, c), :] = _pack_pair(
            val[:, 256 * c:256 * c + 128], val[:, 256 * c + 128:256 * c + 256])


def _load_packed(ref, first, rows):
    parts = []
    for c in range(PACK_ROWS):
        parts += list(_unpack_pair(ref[_chunk_of_tokens(first, rows, c), :]))
    return parts


def _ada_kernel(c_ref, w_ref, b_ref, o_ref):
    c = c_ref[...]
    a = (c * jax.nn.sigmoid(c)).astype(_BF16)
    o_ref[...] = jnp.dot(a, w_ref[...].astype(_BF16), preferred_element_type=_F32) + b_ref[...]


def _ada_mod(c, w_ada, b_ada):
    b, d = c.shape
    cols = w_ada.shape[1]
    rows = 8
    tn = 1536
    c_pad = jnp.zeros((rows, d), _F32).at[:b].set(c)
    out = pl.pallas_call(
        _ada_kernel,
        out_shape=jax.ShapeDtypeStruct((rows, cols), _F32),
        grid=(cols // tn,),
        in_specs=[pl.BlockSpec((rows, d), lambda j: (0, 0)),
                  pl.BlockSpec((d, tn), lambda j: (0, j)),
                  pl.BlockSpec((1, tn), lambda j: (0, j))],
        out_specs=pl.BlockSpec((rows, tn), lambda j: (0, j)),
        compiler_params=pltpu.CompilerParams(vmem_limit_bytes=VMEM_LIMIT),
        name="ada_mod",
    )(c_pad, w_ada, b_ada.reshape(1, cols))
    return out[:b].reshape(b, N_MOD, d)


def _rope(x0, x1, cos, sin):
    return x0 * cos - x1 * sin, x0 * sin + x1 * cos


def _proj_kernel(x_ref, mod_ref, gpre_ref, win_ref, gqa_ref, wqb_ref, gkva_ref, wkvb_ref,
                 gqh_ref, gkh_ref, cosm_ref, sinm_ref, cosg_ref, sing_ref,
                 qT_ref, k_ref, vT_ref):
    x = x_ref[0]
    t = x.shape[0]
    shift = mod_ref[0, 0:1, :]
    scale = mod_ref[0, 1:2, :]
    h = x * _rsqrt_mean_sq(x, -1) * gpre_ref[...] * (1.0 + scale) + shift
    pT = lax.dot_general(win_ref[...], h.astype(_BF16), _NT, preferred_element_type=_F32)

    cm, sm = cosm_ref[...], sinm_ref[...]
    cg, sg = cosg_ref[...], sing_ref[...]
    half_m = MLA_ROPE // 2
    half_g = GQA_HEAD_DIM // 2

    ql = pT[OFF_QLAT:OFF_QLAT + MLA_Q_LORA]
    qn = (ql * _rsqrt_mean_sq(ql, 0) * gqa_ref[...]).astype(_BF16)
    qT = jnp.dot(wqb_ref[...], qn, preferred_element_type=_F32)
    sc_m = float((MLA_NOPE + MLA_ROPE) ** -0.5 * LOG2_E)
    n_nope = MLA_HEADS * MLA_NOPE
    n_half = MLA_HEADS * half_m
    q0, q1 = _rope(qT[n_nope:n_nope + n_half], qT[n_nope + n_half:],
                   jnp.tile(cm, (MLA_HEADS, 1)), jnp.tile(sm, (MLA_HEADS, 1)))
    zq_m = jnp.zeros((HEAD_PAD - MLA_NOPE - MLA_ROPE, t), _BF16)
    for hd in range(MLA_HEADS):
        qT_ref[0, hd, 0:MLA_NOPE, :] = (qT[hd * MLA_NOPE:(hd + 1) * MLA_NOPE] * sc_m).astype(_BF16)
        qT_ref[0, hd, MLA_NOPE:MLA_NOPE + half_m, :] = (q0[hd * half_m:(hd + 1) * half_m] * sc_m).astype(_BF16)
        qT_ref[0, hd, MLA_NOPE + half_m:MLA_NOPE + MLA_ROPE, :] = (
            q1[hd * half_m:(hd + 1) * half_m] * sc_m).astype(_BF16)
        qT_ref[0, hd, MLA_NOPE + MLA_ROPE:, :] = zq_m

    kl = pT[OFF_KVLAT:OFF_KVLAT + MLA_KV_LORA]
    kn = (kl * _rsqrt_mean_sq(kl, 0) * gkva_ref[...]).astype(_BF16)
    kvT = jnp.dot(wkvb_ref[...], kn, preferred_element_type=_F32)
    k0, k1 = _rope(pT[OFF_KROPE:OFF_KROPE + half_m], pT[OFF_KROPE + half_m:OFF_KROPE + MLA_ROPE], cm, sm)
    zk_m = jnp.zeros((HEAD_PAD - MLA_NOPE - MLA_ROPE, t), _F32)
    for hd in range(MLA_HEADS):
        kTh = jnp.concatenate([kvT[hd * MLA_NOPE:(hd + 1) * MLA_NOPE], k0, k1, zk_m], axis=0)
        k_ref[0, hd] = kTh.T.astype(_BF16)
        vT_ref[0, hd, 0] = kvT[n_nope + hd * MLA_V:n_nope + (hd + 1) * MLA_V].astype(_BF16)

    sc_g = float(GQA_HEAD_DIM ** -0.5 * LOG2_E)
    zq_g = jnp.zeros((HEAD_PAD - GQA_HEAD_DIM, t), _BF16)
    for hd in range(GQA_HEADS):
        blk = pT[OFF_QG + hd * GQA_HEAD_DIM:OFF_QG + (hd + 1) * GQA_HEAD_DIM]
        bn = blk * _rsqrt_mean_sq(blk, 0) * gqh_ref[...]
        g0, g1 = _rope(bn[:half_g], bn[half_g:], cg, sg)
        qT_ref[0, MLA_HEADS + hd, 0:half_g, :] = (g0 * sc_g).astype(_BF16)
        qT_ref[0, MLA_HEADS + hd, half_g:GQA_HEAD_DIM, :] = (g1 * sc_g).astype(_BF16)
        qT_ref[0, MLA_HEADS + hd, GQA_HEAD_DIM:, :] = zq_g

    zk_g = jnp.zeros((HEAD_PAD - GQA_HEAD_DIM, t), _F32)
    for hd in range(GQA_KV_HEADS):
        blk = pT[OFF_KG + hd * GQA_HEAD_DIM:OFF_KG + (hd + 1) * GQA_HEAD_DIM]
        bn = blk * _rsqrt_mean_sq(blk, 0) * gkh_ref[...]
        g0, g1 = _rope(bn[:half_g], bn[half_g:], cg, sg)
        kTh = jnp.concatenate([g0, g1, zk_g], axis=0)
        k_ref[0, MLA_HEADS + hd] = kTh.T.astype(_BF16)
        vT_ref[0, MLA_HEADS + hd, 0] = pT[OFF_VG + hd * V_DIM:OFF_VG + (hd + 1) * V_DIM].astype(_BF16)


def _project(x, mod, gpre, winT, gqa, wqbT, gkva, wkvbT, gqh, gkh, cosm, sinm, cosg, sing):
    b, s, d = x.shape
    t = T_PROJ
    nt = s // t
    full = lambda a: pl.BlockSpec(a.shape, lambda bi, i: (0,) * a.ndim)
    tab = lambda a: pl.BlockSpec((a.shape[0], t), lambda bi, i: (0, i))
    return pl.pallas_call(
        _proj_kernel,
        out_shape=(jax.ShapeDtypeStruct((b, N_HEADS, HEAD_PAD, s), _BF16),
                   jax.ShapeDtypeStruct((b, N_KV, s, HEAD_PAD), _BF16),
                   jax.ShapeDtypeStruct((b, N_KV, nt, V_DIM, t), _BF16)),
        grid=(b, nt),
        in_specs=[pl.BlockSpec((1, t, d), lambda bi, i: (bi, i, 0)),
                  pl.BlockSpec((1, N_MOD, d), lambda bi, i: (bi, 0, 0)),
                  full(gpre), full(winT), full(gqa), full(wqbT), full(gkva), full(wkvbT),
                  full(gqh), full(gkh), tab(cosm), tab(sinm), tab(cosg), tab(sing)],
        out_specs=(pl.BlockSpec((1, N_HEADS, HEAD_PAD, t), lambda bi, i: (bi, 0, 0, i)),
                   pl.BlockSpec((1, N_KV, t, HEAD_PAD), lambda bi, i: (bi, 0, i, 0)),
                   pl.BlockSpec((1, N_KV, 1, V_DIM, t), lambda bi, i: (bi, 0, i, 0, 0))),
        compiler_params=pltpu.CompilerParams(
            dimension_semantics=("arbitrary", "arbitrary"), vmem_limit_bytes=VMEM_LIMIT),
        name="in_proj",
    )(x, mod, gpre, winT, gqa, wqbT, gkva, wkvbT, gqh, gkh, cosm, sinm, cosg, sing)


def _attn_kernel(qT_ref, k_ref, vT_ref, o_ref, *slots, n_chunks, tk):
    qT = qT_ref[0, 0]
    tq = qT.shape[1]
    ns = len(slots)

    def scores(j, slot):
        kc = k_ref[0, 0, pl.ds(pl.multiple_of(j * tk, tk), tk), :]
        s = jnp.dot(kc, qT, preferred_element_type=_F32)
        slots[slot][...] = s
        return jnp.max(s, axis=0, keepdims=True)

    def accumulate(j, slot, mc, carry):
        m, l, acc = carry
        m_new = jnp.maximum(m, mc)
        alpha = jnp.exp2(m - m_new)
        p = jnp.exp2(slots[slot][...] - m_new)
        l = alpha * l + jnp.sum(p, axis=0, keepdims=True)
        acc = alpha * acc + jnp.dot(vT_ref[0, 0, j], p.astype(_BF16), preferred_element_type=_F32)
        return m_new, l, acc

    def group(jj, state):
        mcs, carry = state
        mcs = list(mcs)
        j = ns * jj
        for u in range(ns):
            nxt = scores(j + u + 2, (u + 2) % ns)
            carry = accumulate(j + u, u, mcs[u], carry)
            mcs[(u + 2) % ns] = nxt
        return tuple(mcs), carry

    carry = (jnp.full((1, tq), -1e30, _F32), jnp.zeros((1, tq), _F32), jnp.zeros((V_DIM, tq), _F32))
    zero = jnp.zeros((1, tq), _F32)
    mcs = (scores(0, 0), scores(1, 1)) + (zero,) * (ns - 2)
    n_loop = (n_chunks - 2) // ns
    mcs, carry = lax.fori_loop(0, n_loop, group, (mcs, carry))
    mcs = list(mcs)
    for j in range(n_loop * ns, n_chunks):
        if j + 2 < n_chunks:
            mcs[(j + 2) % ns] = scores(j + 2, (j + 2) % ns)
        carry = accumulate(j, j % ns, mcs[j % ns], carry)
    _, l, acc = carry
    o_ref[0, 0] = acc / l


def _kv_index(h):
    return jnp.where(h < MLA_HEADS, h, MLA_HEADS + (h - MLA_HEADS) // GQA_GROUP)


def _attention(qT, k, vT):
    b, nh, dp, s = qT.shape
    n_chunks, tk = vT.shape[2], vT.shape[4]
    assert n_chunks > N_SCORE_SLOTS
    return pl.pallas_call(
        functools.partial(_attn_kernel, n_chunks=n_chunks, tk=tk),
        out_shape=jax.ShapeDtypeStruct((b, nh, V_DIM, s), _F32),
        grid=(b, nh, s // T_Q),
        in_specs=[pl.BlockSpec((1, 1, dp, T_Q), lambda bi, h, i: (bi, h, 0, i)),
                  pl.BlockSpec((1, 1, s, dp), lambda bi, h, i: (bi, _kv_index(h), 0, 0)),
                  pl.BlockSpec((1, 1, n_chunks, V_DIM, tk), lambda bi, h, i: (bi, _kv_index(h), 0, 0, 0))],
        out_specs=pl.BlockSpec((1, 1, V_DIM, T_Q), lambda bi, h, i: (bi, h, 0, i)),
        scratch_shapes=[pltpu.VMEM((tk, T_Q), _F32) for _ in range(N_SCORE_SLOTS)],
        compiler_params=pltpu.CompilerParams(
            dimension_semantics=("arbitrary", "arbitrary", "arbitrary"), vmem_limit_bytes=VMEM_LIMIT),
        name="attention",
    )(qT, k, vT)


def _first_argmax(v, rows, n):
    m = jnp.max(v, axis=0, keepdims=True)
    idx = jnp.min(jnp.where(v == m, rows, n), axis=0, keepdims=True)
    return m, idx


def _post_kernel(x_ref, o_ref, mod_ref, gmla_ref, ggqa_ref, wout_ref, gpost_ref, gffn_ref,
                 wrh_ref, wrl_ref, br_ref,
                 x1_ref, hp_ref, idx_ref, w_ref, rank_ref, cnt_ref, carry_ref):
    first = jnp.logical_and(pl.program_id(0) == 0, pl.program_id(1) == 0)

    @pl.when(first)
    def _():
        carry_ref[...] = jnp.zeros_like(carry_ref)

    x = x_ref[0]
    t = x.shape[0]
    gate_m = mod_ref[0, 2:3, :]
    shift_f = mod_ref[0, 3:4, :]
    scale_f = mod_ref[0, 4:5, :]

    o = o_ref[0].reshape(N_HEADS * V_DIM, t)
    half = MLA_HEADS * V_DIM
    om, og = o[:half], o[half:]
    mixedT = jnp.concatenate([om * _rsqrt_mean_sq(om, 0) * gmla_ref[...],
                              og * _rsqrt_mean_sq(og, 0) * ggqa_ref[...]], axis=0).astype(_BF16)
    y = jnp.dot(wout_ref[...], mixedT, preferred_element_type=_F32).T
    x1 = x + gate_m * (y * _rsqrt_mean_sq(y, -1) * gpost_ref[...])
    x1_ref[0] = x1
    h2 = x1 * _rsqrt_mean_sq(x1, -1) * gffn_ref[...] * (1.0 + scale_f) + shift_f

    _store_packed(hp_ref, h2)

    hh = h2.astype(_BF16)
    hl = (h2 - hh.astype(_F32)).astype(_BF16)
    wrh = wrh_ref[...]
    logits = (lax.dot_general(wrh, hh, _NT, preferred_element_type=_F32)
              + lax.dot_general(wrh, hl, _NT, preferred_element_type=_F32)
              + lax.dot_general(wrl_ref[...], hh, _NT, preferred_element_type=_F32))
    scores = jax.nn.sigmoid(logits)
    biased = scores + br_ref[...]
    neg = -jnp.inf

    rows_g = lax.broadcasted_iota(jnp.int32, (GROUP_SIZE, t), 0)
    gscores = []
    for g in range(N_GROUPS):
        v = biased[g * GROUP_SIZE:(g + 1) * GROUP_SIZE]
        m1, i1 = _first_argmax(v, rows_g, GROUP_SIZE)
        m2 = jnp.max(jnp.where(rows_g == i1, neg, v), axis=0, keepdims=True)
        gscores.append(m1 + m2)
    cur = jnp.concatenate(gscores, axis=0)
    rows_8 = lax.broadcasted_iota(jnp.int32, (N_GROUPS, t), 0)
    gsel = jnp.zeros((N_GROUPS, t), _F32)
    for _ in range(TOPK_GROUPS):
        _, gi = _first_argmax(cur, rows_8, N_GROUPS)
        hit = rows_8 == gi
        gsel = jnp.where(hit, 1.0, gsel)
        cur = jnp.where(hit, neg, cur)
    emask = jnp.concatenate(
        [jnp.broadcast_to(gsel[g:g + 1], (GROUP_SIZE, t)) for g in range(N_GROUPS)], axis=0) > 0.5
    masked = jnp.where(emask, biased, neg)

    rows_e = lax.broadcasted_iota(jnp.int32, (N_EXPERTS, t), 0)
    idxs, ws = [], []
    for _ in range(TOP_K):
        _, ei = _first_argmax(masked, rows_e, N_EXPERTS)
        hit = rows_e == ei
        idxs.append(ei)
        ws.append(jnp.sum(jnp.where(hit, scores, 0.0), axis=0, keepdims=True))
        masked = jnp.where(hit, neg, masked)
    wsum = ws[0]
    for wk in ws[1:]:
        wsum = wsum + wk

    onehot = jnp.zeros((N_EXPERTS, t), _F32)
    for ei in idxs:
        onehot = onehot + jnp.where(rows_e == ei, 1.0, 0.0)
    upper = (lax.broadcasted_iota(jnp.int32, (t, t), 0) < lax.broadcasted_iota(jnp.int32, (t, t), 1))
    before = jnp.dot(onehot.astype(_BF16), jnp.where(upper, 1.0, 0.0).astype(_BF16),
                     preferred_element_type=_F32) + carry_ref[...]
    for kk in range(TOP_K):
        idx_ref[kk:kk + 1, :] = idxs[kk]
        w_ref[kk:kk + 1, :] = ws[kk] / wsum * ROUTED_SCALE
        rank_ref[kk:kk + 1, :] = jnp.sum(
            jnp.where(rows_e == idxs[kk], before, 0.0), axis=0, keepdims=True).astype(jnp.int32)
    total = carry_ref[...] + jnp.sum(onehot, axis=1, keepdims=True)
    carry_ref[...] = total
    cnt_ref[...] = jnp.broadcast_to(total, cnt_ref.shape)


def _post_attention(x, oT, mod, gmla, ggqa, woutT, gpost, gffn, wrh, wrl, br):
    b, s, d = x.shape
    n = b * s
    t = T_POST
    nt = s // t
    full = lambda a: pl.BlockSpec(a.shape, lambda bi, i: (0,) * a.ndim)
    tok = lambda bi, i: (0, bi * nt + i)
    return pl.pallas_call(
        _post_kernel,
        out_shape=(jax.ShapeDtypeStruct((b, s, d), _F32),
                   jax.ShapeDtypeStruct((n * PACK_ROWS, 128), _U32),
                   jax.ShapeDtypeStruct((TOP_K, n), jnp.int32),
                   jax.ShapeDtypeStruct((TOP_K, n), _F32),
                   jax.ShapeDtypeStruct((TOP_K, n), jnp.int32),
                   jax.ShapeDtypeStruct((N_EXPERTS, 128), _F32)),
        grid=(b, nt),
        in_specs=[pl.BlockSpec((1, t, d), lambda bi, i: (bi, i, 0)),
                  pl.BlockSpec((1, N_HEADS, V_DIM, t), lambda bi, i: (bi, 0, 0, i)),
                  pl.BlockSpec((1, N_MOD, d), lambda bi, i: (bi, 0, 0)),
                  full(gmla), full(ggqa), full(woutT), full(gpost), full(gffn),
                  full(wrh), full(wrl), full(br)],
        out_specs=(pl.BlockSpec((1, t, d), lambda bi, i: (bi, i, 0)),
                   pl.BlockSpec((t * PACK_ROWS, 128), lambda bi, i: (bi * nt + i, 0)),
                   pl.BlockSpec((TOP_K, t), tok),
                   pl.BlockSpec((TOP_K, t), tok),
                   pl.BlockSpec((TOP_K, t), tok),
                   pl.BlockSpec((N_EXPERTS, 128), lambda bi, i: (0, 0))),
        scratch_shapes=[pltpu.VMEM((N_EXPERTS, 1), _F32)],
        compiler_params=pltpu.CompilerParams(
            dimension_semantics=("arbitrary", "arbitrary"), vmem_limit_bytes=VMEM_LIMIT),
        name="out_proj_router",
    )(x, oT, mod, gmla, ggqa, woutT, gpost, gffn, wrh, wrl, br)


def _slot_kernel(idx_ref, rank_ref, start_ref, dest_ref):
    t = idx_ref.shape[1]
    rows_e = lax.broadcasted_iota(jnp.int32, (N_EXPERTS, t), 0)
    start = start_ref[...]
    for kk in range(TOP_K):
        base = jnp.sum(jnp.where(rows_e == idx_ref[kk:kk + 1, :], start, 0.0), axis=0, keepdims=True)
        dest_ref[kk:kk + 1, :] = rank_ref[kk:kk + 1, :] + base.astype(jnp.int32)


def _slots(idx_t, rank_t, start):
    n = idx_t.shape[1]
    t = T_POST
    return pl.pallas_call(
        _slot_kernel,
        out_shape=jax.ShapeDtypeStruct((TOP_K, n), jnp.int32),
        grid=(n // t,),
        in_specs=[pl.BlockSpec((TOP_K, t), lambda i: (0, i)),
                  pl.BlockSpec((TOP_K, t), lambda i: (0, i)),
                  pl.BlockSpec((N_EXPERTS, 1), lambda i: (0, 0))],
        out_specs=pl.BlockSpec((TOP_K, t), lambda i: (0, i)),
        compiler_params=pltpu.CompilerParams(dimension_semantics=("arbitrary",)),
        name="slot_index",
    )(idx_t, rank_t, start.astype(_F32).reshape(N_EXPERTS, 1))


def _row_copy(src, src_row, dst, dst_row, sem):
    return pltpu.make_async_copy(_token_rows(src, src_row), _token_rows(dst, dst_row), sem)


def _dispatch_kernel(dest_ref, h_ref, xs_in_ref, xs_ref, sem):
    del xs_in_ref
    t_rows = h_ref.shape[0] // PACK_ROWS

    def issue(t, _):
        for kk in range(TOP_K):
            _row_copy(h_ref, t, xs_ref, dest_ref[0, 0, t * TOP_K + kk], sem).start(priority=kk % 2)
        return 0

    lax.fori_loop(0, t_rows, issue, 0)

    def drain(t, _):
        for kk in range(TOP_K):
            _row_copy(h_ref, 0, xs_ref, 0, sem).wait()
        return 0

    lax.fori_loop(0, t_rows, drain, 0)


def _dispatch(dest_tiles, hp, n_slots):
    n = hp.shape[0] // PACK_ROWS
    t = T_ROW
    xs0 = jnp.zeros((n_slots * PACK_ROWS, 128), _U32)
    return pl.pallas_call(
        _dispatch_kernel,
        out_shape=jax.ShapeDtypeStruct((n_slots * PACK_ROWS, 128), _U32),
        grid=(n // t,),
        in_specs=[pl.BlockSpec((1, 1, t * TOP_K), lambda i: (i, 0, 0), memory_space=pltpu.SMEM),
                  pl.BlockSpec((t * PACK_ROWS, 128), lambda i: (i, 0)),
                  pl.BlockSpec(memory_space=pl.ANY)],
        out_specs=pl.BlockSpec(memory_space=pl.ANY),
        scratch_shapes=[pltpu.SemaphoreType.DMA(())],
        input_output_aliases={2: 0},
        compiler_params=pltpu.CompilerParams(
            dimension_semantics=("arbitrary",), vmem_limit_bytes=VMEM_LIMIT),
        name="dispatch",
    )(dest_tiles, hp, xs0)


def _swiglu(xb, wgu, wd):
    gu = jnp.dot(xb, wgu, preferred_element_type=_F32)
    g, u = gu[:, :EXPERT_FF], gu[:, EXPERT_FF:]
    mid = (g * jax.nn.sigmoid(g) * u).astype(_BF16)
    return jnp.dot(mid, wd, preferred_element_type=_F32)


def _load_packed_bf16(ref, rows):
    return jnp.concatenate([p.astype(_BF16) for p in _load_packed(ref, 0, rows)], axis=1)


def _expert_kernel(nblk_ref, bstart_ref, nused_ref, xs_ref, wg_ref, wu_ref, wd_ref, ys_ref,
                   xbuf, ybuf, xsem, ysem, wgu_s, wd_s, *, n_blocks):
    e = pl.program_id(0)
    nb = nblk_ref[e]
    b0 = bstart_ref[e]

    def x_copy(blk, slot):
        return pltpu.make_async_copy(_token_rows(xs_ref, blk * T_M, T_M), _token_rows(xbuf, slot * T_M, T_M),
                                     xsem.at[slot])

    def y_copy(blk, slot):
        return pltpu.make_async_copy(_token_rows(ybuf, slot * T_M, T_M), _token_rows(ys_ref, blk * T_M, T_M),
                                     ysem.at[slot])

    @pl.when(nb > 0)
    def _():
        x_copy(b0, 0).start()
        wgu_s[:, :EXPERT_FF] = wg_ref[0].astype(_BF16)
        wgu_s[:, EXPERT_FF:] = wu_ref[0].astype(_BF16)
        wd_s[...] = wd_ref[0].astype(_BF16)

        def block(j, _):
            slot = lax.rem(j, 2)
            x_copy(b0 + j, slot).wait()

            @pl.when(j + 1 < nb)
            def _():
                x_copy(b0 + j + 1, 1 - slot).start()

            @pl.when(j >= 2)
            def _():
                y_copy(b0 + j - 2, slot).wait()

            xb = jnp.concatenate([p.astype(_BF16) for p in _load_packed(xbuf, slot * T_M, T_M)], axis=1)
            _store_packed(ybuf, _swiglu(xb, wgu_s[...], wd_s[...]), slot * T_M)
            y_copy(b0 + j, slot).start()
            return 0

        lax.fori_loop(0, nb, block, 0)

        @pl.when(nb >= 2)
        def _():
            y_copy(b0 + nb - 2, lax.rem(nb, 2)).wait()

        y_copy(b0 + nb - 1, lax.rem(nb - 1, 2)).wait()

    @pl.when(e == pl.num_programs(0) - 1)
    def _():
        ybuf[pl.ds(0, T_M * PACK_ROWS), :] = jnp.zeros((T_M * PACK_ROWS, 128), _U32)

        def fill(blk, _):
            y_copy(blk, 0).start()
            return 0

        def drain(blk, _):
            y_copy(blk, 0).wait()
            return 0

        lax.fori_loop(nused_ref[0], n_blocks, fill, 0)
        lax.fori_loop(nused_ref[0], n_blocks, drain, 0)


def _experts(nblk, bstart, nused, xs, wg, wu, wd):
    n_slots = xs.shape[0] // PACK_ROWS
    n_exp, d, ff = wg.shape
    buf = pltpu.VMEM((2 * T_M * PACK_ROWS, 128), _U32)
    return pl.pallas_call(
        functools.partial(_expert_kernel, n_blocks=n_slots // T_M),
        out_shape=jax.ShapeDtypeStruct(xs.shape, _U32),
        grid_spec=pltpu.PrefetchScalarGridSpec(
            num_scalar_prefetch=3,
            grid=(n_exp,),
            in_specs=[pl.BlockSpec(memory_space=pl.ANY),
                      pl.BlockSpec((1, d, ff), lambda i, nbk, bst, nu: (i, 0, 0)),
                      pl.BlockSpec((1, d, ff), lambda i, nbk, bst, nu: (i, 0, 0)),
                      pl.BlockSpec((1, ff, d), lambda i, nbk, bst, nu: (i, 0, 0))],
            out_specs=pl.BlockSpec(memory_space=pl.ANY),
            scratch_shapes=[buf, buf, pltpu.SemaphoreType.DMA((2,)), pltpu.SemaphoreType.DMA((2,)),
                            pltpu.VMEM((d, 2 * ff), _BF16), pltpu.VMEM((ff, d), _BF16)]),
        compiler_params=pltpu.CompilerParams(
            dimension_semantics=("arbitrary",), vmem_limit_bytes=VMEM_LIMIT),
        name="experts",
    )(nblk, bstart, nused, xs, wg, wu, wd)


def _combine_kernel(dest_ref, ys_ref, hp_ref, w_ref, x1_ref, mod_ref, wgus_ref, wds_ref, gpost_ref,
                    o_ref, gbuf, sem):
    t_rows = hp_ref.shape[0] // PACK_ROWS

    def issue(t, _):
        for kk in range(TOP_K):
            _row_copy(ys_ref, dest_ref[0, 0, t * TOP_K + kk], gbuf, kk * t_rows + t, sem).start(priority=kk % 2)
        return 0

    lax.fori_loop(0, t_rows, issue, 0)

    shared = _swiglu(_load_packed_bf16(hp_ref, t_rows), wgus_ref[...], wds_ref[...])

    def drain(t, _):
        for kk in range(TOP_K):
            _row_copy(ys_ref, 0, gbuf, 0, sem).wait()
        return 0

    lax.fori_loop(0, t_rows, drain, 0)

    w = w_ref[...]
    cols = None
    for kk in range(TOP_K):
        wk = w[:, kk:kk + 1]
        parts = [wk * p for p in _load_packed(gbuf, kk * t_rows, t_rows)]
        cols = parts if cols is None else [a + b for a, b in zip(cols, parts)]
    y = jnp.concatenate(cols, axis=1) + shared
    gate_f = mod_ref[0, 5:6, :]
    o_ref[...] = x1_ref[...] + gate_f * (y * _rsqrt_mean_sq(y, -1) * gpost_ref[...])


def _combine(dest_tiles, ys, hp, w_tok, x1, mod, wgus, wds, gpost, tiles_per_batch):
    n, d = x1.shape
    t = T_ROW
    full = lambda a: pl.BlockSpec(a.shape, lambda i: (0,) * a.ndim)
    return pl.pallas_call(
        _combine_kernel,
        out_shape=jax.ShapeDtypeStruct((n, d), _F32),
        grid=(n // t,),
        in_specs=[pl.BlockSpec((1, 1, t * TOP_K), lambda i: (i, 0, 0), memory_space=pltpu.SMEM),
                  pl.BlockSpec(memory_space=pl.ANY),
                  pl.BlockSpec((t * PACK_ROWS, 128), lambda i: (i, 0)),
                  pl.BlockSpec((t, TOP_K), lambda i: (i, 0)),
                  pl.BlockSpec((t, d), lambda i: (i, 0)),
                  pl.BlockSpec((1, N_MOD, d), lambda i: (i // tiles_per_batch, 0, 0)),
                  full(wgus), full(wds), full(gpost)],
        out_specs=pl.BlockSpec((t, d), lambda i: (i, 0)),
        scratch_shapes=[pltpu.VMEM((TOP_K * t * PACK_ROWS, 128), _U32), pltpu.SemaphoreType.DMA(())],
        compiler_params=pltpu.CompilerParams(
            dimension_semantics=("arbitrary",), vmem_limit_bytes=VMEM_LIMIT),
        name="combine",
    )(dest_tiles, ys, hp, w_tok, x1, mod, wgus, wds, gpost)


def _deinterleave(n):
    return np.concatenate([np.arange(0, n, 2), np.arange(1, n, 2)])


def _rope_tables(s, dim):
    rows = s // GRID_W
    row = jnp.broadcast_to(jnp.arange(rows, dtype=_F32)[:, None], (rows, GRID_W)).reshape(s)
    col = jnp.broadcast_to(jnp.arange(GRID_W, dtype=_F32)[None, :], (rows, GRID_W)).reshape(s)
    n_freq = dim // 4
    inv_freq = ROPE_THETA ** (-jnp.arange(n_freq, dtype=_F32) / n_freq)
    ang = jnp.concatenate([row[:, None] * inv_freq, col[:, None] * inv_freq], axis=-1)
    return jnp.cos(ang).T, jnp.sin(ang).T


def _in_proj_perm():
    perm = list(range(OFF_KROPE))
    perm += list(OFF_KROPE + _deinterleave(MLA_ROPE))
    for hd in range(GQA_HEADS):
        perm += list(OFF_QG + hd * GQA_HEAD_DIM + _deinterleave(GQA_HEAD_DIM))
    for hd in range(GQA_KV_HEADS):
        perm += list(OFF_KG + hd * GQA_HEAD_DIM + _deinterleave(GQA_HEAD_DIM))
    perm += list(range(OFF_VG, IN_COLS))
    return np.asarray(perm)


def _q_b_perm():
    per = MLA_NOPE + MLA_ROPE
    nope = [hd * per + j for hd in range(MLA_HEADS) for j in range(MLA_NOPE)]
    even = [hd * per + MLA_NOPE + 2 * i for hd in range(MLA_HEADS) for i in range(MLA_ROPE // 2)]
    odd = [hd * per + MLA_NOPE + 2 * i + 1 for hd in range(MLA_HEADS) for i in range(MLA_ROPE // 2)]
    return np.asarray(nope + even + odd)


def _kv_b_perm():
    per = MLA_NOPE + MLA_V
    kk = [hd * per + j for hd in range(MLA_HEADS) for j in range(MLA_NOPE)]
    vv = [hd * per + MLA_NOPE + j for hd in range(MLA_HEADS) for j in range(MLA_V)]
    return np.asarray(kk + vv)


def _layer(x, c, w_ada, b_ada, g_pre_mix, g_post_mix, g_pre_ffn, g_post_ffn, w_in, g_q_a, w_q_b,
           g_kv_a, w_kv_b, g_q_head, g_k_head, g_mla_out, g_gqa_out, w_out, w_router, b_router,
           w_gate_e, w_up_e, w_down_e, w_gate_s, w_up_s, w_down_s):
    b, s, d = x.shape
    n = b * s
    col = lambda g: g.reshape(-1, 1).astype(_F32)
    row = lambda g: g.reshape(1, -1).astype(_F32)

    mod = _ada_mod(c, w_ada, b_ada)

    winT = w_in[:, _in_proj_perm()].T.astype(_BF16)
    wqbT = w_q_b[:, _q_b_perm()].T.astype(_BF16)
    wkvbT = w_kv_b[:, _kv_b_perm()].T.astype(_BF16)
    perm_h = _deinterleave(GQA_HEAD_DIM)
    cosm, sinm = _rope_tables(s, MLA_ROPE)
    cosg, sing = _rope_tables(s, GQA_HEAD_DIM)

    qT, k, vT = _project(x, mod, row(g_pre_mix), winT, col(g_q_a), wqbT, col(g_kv_a), wkvbT,
                         col(g_q_head[perm_h]), col(g_k_head[perm_h]), cosm, sinm, cosg, sing)
    oT = _attention(qT, k, vT)

    wr_t = w_router.T.astype(_F32)
    wrh = wr_t.astype(_BF16)
    wrl = (wr_t - wrh.astype(_F32)).astype(_BF16)
    x1, hp, idx_t, w_t, rank_t, cnt = _post_attention(
        x, oT, mod, col(g_mla_out), col(g_gqa_out), w_out.T.astype(_BF16), row(g_post_mix),
        row(g_pre_ffn), wrh, wrl, col(b_router))

    counts = cnt[:, 0].astype(jnp.int32)
    blocks_e = (counts + T_M - 1) // T_M
    blk_end = jnp.cumsum(blocks_e)
    start = (blk_end - blocks_e) * T_M
    n_blocks = n * TOP_K // T_M + N_EXPERTS
    nused = blk_end[-1:].astype(jnp.int32)
    bstart = (blk_end - blocks_e).astype(jnp.int32)
    dest = _slots(idx_t, rank_t, start)
    dest_tiles = dest.T.reshape(n // T_ROW, 1, T_ROW * TOP_K)

    xs = _dispatch(dest_tiles, hp, n_blocks * T_M)
    ys = _experts(blocks_e.astype(jnp.int32), bstart, nused, xs, w_gate_e, w_up_e, w_down_e)
    wgus = jnp.concatenate([w_gate_s, w_up_s], axis=1).astype(_BF16)
    out = _combine(dest_tiles, ys, hp, w_t.T, x1.reshape(n, d), mod, wgus, w_down_s.astype(_BF16),
                   row(g_post_ffn), s // T_ROW)
    return out.reshape(b, s, d)


def kernel(x, c, w_ada, b_ada, g_pre_mix, g_post_mix, g_pre_ffn, g_post_ffn, w_in, g_q_a, w_q_b, g_kv_a, w_kv_b, g_q_head, g_k_head, g_mla_out, g_gqa_out, w_out, w_router, b_router, w_gate_e, w_up_e, w_down_e, w_gate_s, w_up_s, w_down_s):
    depth = w_ada.shape[0]
    for l in range(depth):
        x = _layer(x, c, w_ada[l], b_ada[l], g_pre_mix[l], g_post_mix[l], g_pre_ffn[l], g_post_ffn[l],
                   w_in[l], g_q_a[l], w_q_b[l], g_kv_a[l], w_kv_b[l], g_q_head[l], g_k_head[l],
                   g_mla_out[l], g_gqa_out[l], w_out[l], w_router[l], b_router[l], w_gate_e[l],
                   w_up_e[l], w_down_e[l], w_gate_s[l], w_up_s[l], w_down_s[l])
    return x
```

```python
import functools

import jax
import jax.numpy as jnp
import numpy as np
from jax import lax
from jax.experimental import pallas as pl
from jax.experimental.pallas import tpu as pltpu

D_MODEL = 1024
GRID_W = 64
ROPE_THETA = 10000.0
NORM_EPS = 1e-6

MLA_HEADS = 8
MLA_Q_LORA = 256
MLA_KV_LORA = 128
MLA_NOPE = 64
MLA_ROPE = 32
MLA_V = 64

GQA_HEADS = 8
GQA_KV_HEADS = 2
GQA_HEAD_DIM = 64
GQA_GROUP = GQA_HEADS // GQA_KV_HEADS

N_HEADS = MLA_HEADS + GQA_HEADS
N_KV = MLA_HEADS + GQA_KV_HEADS
HEAD_PAD = 128
V_DIM = 64

N_EXPERTS = 256
TOP_K = 8
N_GROUPS = 8
GROUP_SIZE = N_EXPERTS // N_GROUPS
TOPK_GROUPS = 4
EXPERT_FF = 256
ROUTED_SCALE = 2.5
N_MOD = 6

OFF_QLAT = 0
OFF_KVLAT = OFF_QLAT + MLA_Q_LORA
OFF_KROPE = OFF_KVLAT + MLA_KV_LORA
OFF_QG = OFF_KROPE + MLA_ROPE
OFF_KG = OFF_QG + GQA_HEADS * GQA_HEAD_DIM
OFF_VG = OFF_KG + GQA_KV_HEADS * GQA_HEAD_DIM
IN_COLS = OFF_VG + GQA_KV_HEADS * GQA_HEAD_DIM

T_PROJ = 512
T_Q = 512
T_POST = 512
T_ROW = 256
T_M = 256
PACK_ROWS = 4
N_SCORE_SLOTS = 4
LOG2_E = 1.4426950408889634
VMEM_LIMIT = 56 * 1024 * 1024

_NT = (((1,), (1,)), ((), ()))
_F32 = jnp.float32
_BF16 = jnp.bfloat16
_U32 = jnp.uint32
_HI_MASK = np.uint32(0xFFFF0000)


def _rsqrt_mean_sq(v, axis):
    return lax.rsqrt(jnp.mean(v * v, axis=axis, keepdims=True) + NORM_EPS)


def _pack_pair(lo, hi):
    lo_b = pltpu.bitcast(lo.astype(_BF16).astype(_F32), _U32) >> 16
    hi_b = pltpu.bitcast(hi.astype(_BF16).astype(_F32), _U32) & _HI_MASK
    return lo_b | hi_b


def _unpack_pair(u):
    return pltpu.bitcast(u << 16, _F32), pltpu.bitcast(u & _HI_MASK, _F32)


def _token_rows(ref, first, count=1):
    return ref.at[pl.ds(pl.multiple_of(first * PACK_ROWS, PACK_ROWS), count * PACK_ROWS), :]


def _chunk_of_tokens(first, count, c):
    return pl.ds(first * PACK_ROWS + c, count, stride=PACK_ROWS)


def _store_packed(ref, val, first=0):
    rows = val.shape[0]
    for c in range(PACK_ROWS):
        ref[_chunk_of_tokens(first, rows, c), :] = _pack_pair(
            val[:, 256 * c:256 * c + 128], val[:, 256 * c + 128:256 * c + 256])


def _load_packed(ref, first, rows):
    parts = []
    for c in range(PACK_ROWS):
        parts += list(_unpack_pair(ref[_chunk_of_tokens(first, rows, c), :]))
    return parts


def _ada_kernel(c_ref, w_ref, b_ref, o_ref):
    c = c_ref[...]
    a = (c * jax.nn.sigmoid(c)).astype(_BF16)
    o_ref[...] = jnp.dot(a, w_ref[...].astype(_BF16), preferred_element_type=_F32) + b_ref[...]


def _ada_mod(c, w_ada, b_ada):
    b, d = c.shape
    cols = w_ada.shape[1]
    rows = 8
    tn = 1536
    c_pad = jnp.zeros((rows, d), _F32).at[:b].set(c)
    out = pl.pallas_call(
        _ada_kernel,
        out_shape=jax.ShapeDtypeStruct((rows, cols), _F32),
        grid=(cols // tn,),
        in_specs=[pl.BlockSpec((rows, d), lambda j: (0, 0)),
                  pl.BlockSpec((d, tn), lambda j: (0, j)),
                  pl.BlockSpec((1, tn), lambda j: (0, j))],
        out_specs=pl.BlockSpec((rows, tn), lambda j: (0, j)),
        compiler_params=pltpu.CompilerParams(vmem_limit_bytes=VMEM_LIMIT),
        name="ada_mod",
    )(c_pad, w_ada, b_ada.reshape(1, cols))
    return out[:b].reshape(b, N_MOD, d)


def _rope(x0, x1, cos, sin):
    return x0 * cos - x1 * sin, x0 * sin + x1 * cos


def _proj_kernel(x_ref, mod_ref, gpre_ref, win_ref, gqa_ref, wqb_ref, gkva_ref, wkvb_ref,
                 gqh_ref, gkh_ref, cosm_ref, sinm_ref, cosg_ref, sing_ref,
                 qT_ref, k_ref, vT_ref):
    x = x_ref[0]
    t = x.shape[0]
    shift = mod_ref[0, 0:1, :]
    scale = mod_ref[0, 1:2, :]
    h = x * _rsqrt_mean_sq(x, -1) * gpre_ref[...] * (1.0 + scale) + shift
    pT = lax.dot_general(win_ref[...], h.astype(_BF16), _NT, preferred_element_type=_F32)

    cm, sm = cosm_ref[...], sinm_ref[...]
    cg, sg = cosg_ref[...], sing_ref[...]
    half_m = MLA_ROPE // 2
    half_g = GQA_HEAD_DIM // 2

    ql = pT[OFF_QLAT:OFF_QLAT + MLA_Q_LORA]
    qn = (ql * _rsqrt_mean_sq(ql, 0) * gqa_ref[...]).astype(_BF16)
    qT = jnp.dot(wqb_ref[...], qn, preferred_element_type=_F32)
    sc_m = float((MLA_NOPE + MLA_ROPE) ** -0.5 * LOG2_E)
    n_nope = MLA_HEADS * MLA_NOPE
    n_half = MLA_HEADS * half_m
    q0, q1 = _rope(qT[n_nope:n_nope + n_half], qT[n_nope + n_half:],
                   jnp.tile(cm, (MLA_HEADS, 1)), jnp.tile(sm, (MLA_HEADS, 1)))
    zq_m = jnp.zeros((HEAD_PAD - MLA_NOPE - MLA_ROPE, t), _BF16)
    for hd in range(MLA_HEADS):
        qT_ref[0, hd, 0:MLA_NOPE, :] = (qT[hd * MLA_NOPE:(hd + 1) * MLA_NOPE] * sc_m).astype(_BF16)
        qT_ref[0, hd, MLA_NOPE:MLA_NOPE + half_m, :] = (q0[hd * half_m:(hd + 1) * half_m] * sc_m).astype(_BF16)
        qT_ref[0, hd, MLA_NOPE + half_m:MLA_NOPE + MLA_ROPE, :] = (
            q1[hd * half_m:(hd + 1) * half_m] * sc_m).astype(_BF16)
        qT_ref[0, hd, MLA_NOPE + MLA_ROPE:, :] = zq_m

    kl = pT[OFF_KVLAT:OFF_KVLAT + MLA_KV_LORA]
    kn = (kl * _rsqrt_mean_sq(kl, 0) * gkva_ref[...]).astype(_BF16)
    kvT = jnp.dot(wkvb_ref[...], kn, preferred_element_type=_F32)
    k0, k1 = _rope(pT[OFF_KROPE:OFF_KROPE + half_m], pT[OFF_KROPE + half_m:OFF_KROPE + MLA_ROPE], cm, sm)
    zk_m = jnp.zeros((HEAD_PAD - MLA_NOPE - MLA_ROPE, t), _F32)
    for hd in range(MLA_HEADS):
        kTh = jnp.concatenate([kvT[hd * MLA_NOPE:(hd + 1) * MLA_NOPE], k0, k1, zk_m], axis=0)
        k_ref[0, hd] = kTh.T.astype(_BF16)
        vT_ref[0, hd, 0] = kvT[n_nope + hd * MLA_V:n_nope + (hd + 1) * MLA_V].astype(_BF16)

    sc_g = float(GQA_HEAD_DIM ** -0.5 * LOG2_E)
    zq_g = jnp.zeros((HEAD_PAD - GQA_HEAD_DIM, t), _BF16)
    for hd in range(GQA_HEADS):
        blk = pT[OFF_QG + hd * GQA_HEAD_DIM:OFF_QG + (hd + 1) * GQA_HEAD_DIM]
        bn = blk * _rsqrt_mean_sq(blk, 0) * gqh_ref[...]
        g0, g1 = _rope(bn[:half_g], bn[half_g:], cg, sg)
        qT_ref[0, MLA_HEADS + hd, 0:half_g, :] = (g0 * sc_g).astype(_BF16)
        qT_ref[0, MLA_HEADS + hd, half_g:GQA_HEAD_DIM, :] = (g1 * sc_g).astype(_BF16)
        qT_ref[0, MLA_HEADS + hd, GQA_HEAD_DIM:, :] = zq_g

    zk_g = jnp.zeros((HEAD_PAD - GQA_HEAD_DIM, t), _F32)
    for hd in range(GQA_KV_HEADS):
        blk = pT[OFF_KG + hd * GQA_HEAD_DIM:OFF_KG + (hd + 1) * GQA_HEAD_DIM]
        bn = blk * _rsqrt_mean_sq(blk, 0) * gkh_ref[...]
        g0, g1 = _rope(bn[:half_g], bn[half_g:], cg, sg)
        kTh = jnp.concatenate([g0, g1, zk_g], axis=0)
        k_ref[0, MLA_HEADS + hd] = kTh.T.astype(_BF16)
        vT_ref[0, MLA_HEADS + hd, 0] = pT[OFF_VG + hd * V_DIM:OFF_VG + (hd + 1) * V_DIM].astype(_BF16)


def _project(x, mod, gpre, winT, gqa, wqbT, gkva, wkvbT, gqh, gkh, cosm, sinm, cosg, sing):
    b, s, d = x.shape
    t = T_PROJ
    nt = s // t
    full = lambda a: pl.BlockSpec(a.shape, lambda bi, i: (0,) * a.ndim)
    tab = lambda a: pl.BlockSpec((a.shape[0], t), lambda bi, i: (0, i))
    return pl.pallas_call(
        _proj_kernel,
        out_shape=(jax.ShapeDtypeStruct((b, N_HEADS, HEAD_PAD, s), _BF16),
                   jax.ShapeDtypeStruct((b, N_KV, s, HEAD_PAD), _BF16),
                   jax.ShapeDtypeStruct((b, N_KV, nt, V_DIM, t), _BF16)),
        grid=(b, nt),
        in_specs=[pl.BlockSpec((1, t, d), lambda bi, i: (bi, i, 0)),
                  pl.BlockSpec((1, N_MOD, d), lambda bi, i: (bi, 0, 0)),
                  full(gpre), full(winT), full(gqa), full(wqbT), full(gkva), full(wkvbT),
                  full(gqh), full(gkh), tab(cosm), tab(sinm), tab(cosg), tab(sing)],
        out_specs=(pl.BlockSpec((1, N_HEADS, HEAD_PAD, t), lambda bi, i: (bi, 0, 0, i)),
                   pl.BlockSpec((1, N_KV, t, HEAD_PAD), lambda bi, i: (bi, 0, i, 0)),
                   pl.BlockSpec((1, N_KV, 1, V_DIM, t), lambda bi, i: (bi, 0, i, 0, 0))),
        compiler_params=pltpu.CompilerParams(
            dimension_semantics=("arbitrary", "arbitrary"), vmem_limit_bytes=VMEM_LIMIT),
        name="in_proj",
    )(x, mod, gpre, winT, gqa, wqbT, gkva, wkvbT, gqh, gkh, cosm, sinm, cosg, sing)


def _attn_kernel(qT_ref, k_ref, vT_ref, o_ref, *slots, n_chunks, tk):
    qT = qT_ref[0, 0]
    tq = qT.shape[1]
    ns = len(slots)

    def scores(j, slot):
        kc = k_ref[0, 0, pl.ds(pl.multiple_of(j * tk, tk), tk), :]
        s = jnp.dot(kc, qT, preferred_element_type=_F32)
        slots[slot][...] = s
        return jnp.max(s, axis=0, keepdims=True)

    def accumulate(j, slot, mc, carry):
        m, l, acc = carry
        m_new = jnp.maximum(m, mc)
        alpha = jnp.exp2(m - m_new)
        p = jnp.exp2(slots[slot][...] - m_new)
        l = alpha * l + jnp.sum(p, axis=0, keepdims=True)
        acc = alpha * acc + jnp.dot(vT_ref[0, 0, j], p.astype(_BF16), preferred_element_type=_F32)
        return m_new, l, acc

    def group(jj, state):
        mcs, carry = state
        mcs = list(mcs)
        j = ns * jj
        for u in range(ns):
            nxt = scores(j + u + 2, (u + 2) % ns)
            carry = accumulate(j + u, u, mcs[u], carry)
            mcs[(u + 2) % ns] = nxt
        return tuple(mcs), carry

    carry = (jnp.full((1, tq), -1e30, _F32), jnp.zeros((1, tq), _F32), jnp.zeros((V_DIM, tq), _F32))
    zero = jnp.zeros((1, tq), _F32)
    mcs = (scores(0, 0), scores(1, 1)) + (zero,) * (ns - 2)
    n_loop = (n_chunks - 2) // ns
    mcs, carry = lax.fori_loop(0, n_loop, group, (mcs, carry))
    mcs = list(mcs)
    for j in range(n_loop * ns, n_chunks):
        if j + 2 < n_chunks:
            mcs[(j + 2) % ns] = scores(j + 2, (j + 2) % ns)
        carry = accumulate(j, j % ns, mcs[j % ns], carry)
    _, l, acc = carry
    o_ref[0, 0] = acc / l


def _kv_index(h):
    return jnp.where(h < MLA_HEADS, h, MLA_HEADS + (h - MLA_HEADS) // GQA_GROUP)


def _attention(qT, k, vT):
    b, nh, dp, s = qT.shape
    n_chunks, tk = vT.shape[2], vT.shape[4]
    assert n_chunks > N_SCORE_SLOTS
    return pl.pallas_call(
        functools.partial(_attn_kernel, n_chunks=n_chunks, tk=tk),
        out_shape=jax.ShapeDtypeStruct((b, nh, V_DIM, s), _F32),
        grid=(b, nh, s // T_Q),
        in_specs=[pl.BlockSpec((1, 1, dp, T_Q), lambda bi, h, i: (bi, h, 0, i)),
                  pl.BlockSpec((1, 1, s, dp), lambda bi, h, i: (bi, _kv_index(h), 0, 0)),
                  pl.BlockSpec((1, 1, n_chunks, V_DIM, tk), lambda bi, h, i: (bi, _kv_index(h), 0, 0, 0))],
        out_specs=pl.BlockSpec((1, 1, V_DIM, T_Q), lambda bi, h, i: (bi, h, 0, i)),
        scratch_shapes=[pltpu.VMEM((tk, T_Q), _F32) for _ in range(N_SCORE_SLOTS)],
        compiler_params=pltpu.CompilerParams(
            dimension_semantics=("arbitrary", "arbitrary", "arbitrary"), vmem_limit_bytes=VMEM_LIMIT),
        name="attention",
    )(qT, k, vT)


def _first_argmax(v, rows, n):
    m = jnp.max(v, axis=0, keepdims=True)
    idx = jnp.min(jnp.where(v == m, rows, n), axis=0, keepdims=True)
    return m, idx


def _post_kernel(x_ref, o_ref, mod_ref, gmla_ref, ggqa_ref, wout_ref, gpost_ref, gffn_ref,
                 wrh_ref, wrl_ref, br_ref,
                 x1_ref, hp_ref, idx_ref, w_ref, rank_ref, cnt_ref, carry_ref):
    first = jnp.logical_and(pl.program_id(0) == 0, pl.program_id(1) == 0)

    @pl.when(first)
    def _():
        carry_ref[...] = jnp.zeros_like(carry_ref)

    x = x_ref[0]
    t = x.shape[0]
    gate_m = mod_ref[0, 2:3, :]
    shift_f = mod_ref[0, 3:4, :]
    scale_f = mod_ref[0, 4:5, :]

    o = o_ref[0].reshape(N_HEADS * V_DIM, t)
    half = MLA_HEADS * V_DIM
    om, og = o[:half], o[half:]
    mixedT = jnp.concatenate([om * _rsqrt_mean_sq(om, 0) * gmla_ref[...],
                              og * _rsqrt_mean_sq(og, 0) * ggqa_ref[...]], axis=0).astype(_BF16)
    y = jnp.dot(wout_ref[...], mixedT, preferred_element_type=_F32).T
    x1 = x + gate_m * (y * _rsqrt_mean_sq(y, -1) * gpost_ref[...])
    x1_ref[0] = x1
    h2 = x1 * _rsqrt_mean_sq(x1, -1) * gffn_ref[...] * (1.0 + scale_f) + shift_f

    _store_packed(hp_ref, h2)

    hh = h2.astype(_BF16)
    hl = (h2 - hh.astype(_F32)).astype(_BF16)
    wrh = wrh_ref[...]
    logits = (lax.dot_general(wrh, hh, _NT, preferred_element_type=_F32)
              + lax.dot_general(wrh, hl, _NT, preferred_element_type=_F32)
              + lax.dot_general(wrl_ref[...], hh, _NT, preferred_element_type=_F32))
    scores = jax.nn.sigmoid(logits)
    biased = scores + br_ref[...]
    neg = -jnp.inf

    rows_g = lax.broadcasted_iota(jnp.int32, (GROUP_SIZE, t), 0)
    gscores = []
    for g in range(N_GROUPS):
        v = biased[g * GROUP_SIZE:(g + 1) * GROUP_SIZE]
        m1, i1 = _first_argmax(v, rows_g, GROUP_SIZE)
        m2 = jnp.max(jnp.where(rows_g == i1, neg, v), axis=0, keepdims=True)
        gscores.append(m1 + m2)
    cur = jnp.concatenate(gscores, axis=0)
    rows_8 = lax.broadcasted_iota(jnp.int32, (N_GROUPS, t), 0)
    gsel = jnp.zeros((N_GROUPS, t), _F32)
    for _ in range(TOPK_GROUPS):
        _, gi = _first_argmax(cur, rows_8, N_GROUPS)
        hit = rows_8 == gi
        gsel = jnp.where(hit, 1.0, gsel)
        cur = jnp.where(hit, neg, cur)
    emask = jnp.concatenate(
        [jnp.broadcast_to(gsel[g:g + 1], (GROUP_SIZE, t)) for g in range(N_GROUPS)], axis=0) > 0.5
    masked = jnp.where(emask, biased, neg)

    rows_e = lax.broadcasted_iota(jnp.int32, (N_EXPERTS, t), 0)
    idxs, ws = [], []
    for _ in range(TOP_K):
        _, ei = _first_argmax(masked, rows_e, N_EXPERTS)
        hit = rows_e == ei
        idxs.append(ei)
        ws.append(jnp.sum(jnp.where(hit, scores, 0.0), axis=0, keepdims=True))
        masked = jnp.where(hit, neg, masked)
    wsum = ws[0]
    for wk in ws[1:]:
        wsum = wsum + wk

    onehot = jnp.zeros((N_EXPERTS, t), _F32)
    for ei in idxs:
        onehot = onehot + jnp.where(rows_e == ei, 1.0, 0.0)
    upper = (lax.broadcasted_iota(jnp.int32, (t, t), 0) < lax.broadcasted_iota(jnp.int32, (t, t), 1))
    before = jnp.dot(onehot.astype(_BF16), jnp.where(upper, 1.0, 0.0).astype(_BF16),
                     preferred_element_type=_F32) + carry_ref[...]
    for kk in range(TOP_K):
        idx_ref[kk:kk + 1, :] = idxs[kk]
        w_ref[kk:kk + 1, :] = ws[kk] / wsum * ROUTED_SCALE
        rank_ref[kk:kk + 1, :] = jnp.sum(
            jnp.where(rows_e == idxs[kk], before, 0.0), axis=0, keepdims=True).astype(jnp.int32)
    total = carry_ref[...] + jnp.sum(onehot, axis=1, keepdims=True)
    carry_ref[...] = total
    cnt_ref[...] = jnp.broadcast_to(total, cnt_ref.shape)


def _post_attention(x, oT, mod, gmla, ggqa, woutT, gpost, gffn, wrh, wrl, br):
    b, s, d = x.shape
    n = b * s
    t = T_POST
    nt = s // t
    full = lambda a: pl.BlockSpec(a.shape, lambda bi, i: (0,) * a.ndim)
    tok = lambda bi, i: (0, bi * nt + i)
    return pl.pallas_call(
        _post_kernel,
        out_shape=(jax.ShapeDtypeStruct((b, s, d), _F32),
                   jax.ShapeDtypeStruct((n * PACK_ROWS, 128), _U32),
                   jax.ShapeDtypeStruct((TOP_K, n), jnp.int32),
                   jax.ShapeDtypeStruct((TOP_K, n), _F32),
                   jax.ShapeDtypeStruct((TOP_K, n), jnp.int32),
                   jax.ShapeDtypeStruct((N_EXPERTS, 128), _F32)),
        grid=(b, nt),
        in_specs=[pl.BlockSpec((1, t, d), lambda bi, i: (bi, i, 0)),
                  pl.BlockSpec((1, N_HEADS, V_DIM, t), lambda bi, i: (bi, 0, 0, i)),
                  pl.BlockSpec((1, N_MOD, d), lambda bi, i: (bi, 0, 0)),
                  full(gmla), full(ggqa), full(woutT), full(gpost), full(gffn),
                  full(wrh), full(wrl), full(br)],
        out_specs=(pl.BlockSpec((1, t, d), lambda bi, i: (bi, i, 0)),
                   pl.BlockSpec((t * PACK_ROWS, 128), lambda bi, i: (bi * nt + i, 0)),
                   pl.BlockSpec((TOP_K, t), tok),
                   pl.BlockSpec((TOP_K, t), tok),
                   pl.BlockSpec((TOP_K, t), tok),
                   pl.BlockSpec((N_EXPERTS, 128), lambda bi, i: (0, 0))),
        scratch_shapes=[pltpu.VMEM((N_EXPERTS, 1), _F32)],
        compiler_params=pltpu.CompilerParams(
            dimension_semantics=("arbitrary", "arbitrary"), vmem_limit_bytes=VMEM_LIMIT),
        name="out_proj_router",
    )(x, oT, mod, gmla, ggqa, woutT, gpost, gffn, wrh, wrl, br)


def _slot_kernel(idx_ref, rank_ref, start_ref, dest_ref):
    t = idx_ref.shape[1]
    rows_e = lax.broadcasted_iota(jnp.int32, (N_EXPERTS, t), 0)
    start = start_ref[...]
    for kk in range(TOP_K):
        base = jnp.sum(jnp.where(rows_e == idx_ref[kk:kk + 1, :], start, 0.0), axis=0, keepdims=True)
        dest_ref[kk:kk + 1, :] = rank_ref[kk:kk + 1, :] + base.astype(jnp.int32)


def _slots(idx_t, rank_t, start):
    n = idx_t.shape[1]
    t = T_POST
    return pl.pallas_call(
        _slot_kernel,
        out_shape=jax.ShapeDtypeStruct((TOP_K, n), jnp.int32),
        grid=(n // t,),
        in_specs=[pl.BlockSpec((TOP_K, t), lambda i: (0, i)),
                  pl.BlockSpec((TOP_K, t), lambda i: (0, i)),
                  pl.BlockSpec((N_EXPERTS, 1), lambda i: (0, 0))],
        out_specs=pl.BlockSpec((TOP_K, t), lambda i: (0, i)),
        compiler_params=pltpu.CompilerParams(dimension_semantics=("arbitrary",)),
        name="slot_index",
    )(idx_t, rank_t, start.astype(_F32).reshape(N_EXPERTS, 1))


def _row_copy(src, src_row, dst, dst_row, sem):
    return pltpu.make_async_copy(_token_rows(src, src_row), _token_rows(dst, dst_row), sem)


def _dispatch_kernel(dest_ref, h_ref, xs_in_ref, xs_ref, sem):
    del xs_in_ref
    t_rows = h_ref.shape[0] // PACK_ROWS

    def issue(t, _):
        for kk in range(TOP_K):
            _row_copy(h_ref, t, xs_ref, dest_ref[0, 0, t * TOP_K + kk], sem).start(priority=kk % 2)
        return 0

    lax.fori_loop(0, t_rows, issue, 0)

    def drain(t, _):
        for kk in range(TOP_K):
            _row_copy(h_ref, 0, xs_ref, 0, sem).wait()
        return 0

    lax.fori_loop(0, t_rows, drain, 0)


def _dispatch(dest_tiles, hp, n_slots):
    n = hp.shape[0] // PACK_ROWS
    t = T_ROW
    xs0 = jnp.zeros((n_slots * PACK_ROWS, 128), _U32)
    return pl.pallas_call(
        _dispatch_kernel,
        out_shape=jax.ShapeDtypeStruct((n_slots * PACK_ROWS, 128), _U32),
        grid=(n // t,),
        in_specs=[pl.BlockSpec((1, 1, t * TOP_K), lambda i: (i, 0, 0), memory_space=pltpu.SMEM),
                  pl.BlockSpec((t * PACK_ROWS, 128), lambda i: (i, 0)),
                  pl.BlockSpec(memory_space=pl.ANY)],
        out_specs=pl.BlockSpec(memory_space=pl.ANY),
        scratch_shapes=[pltpu.SemaphoreType.DMA(())],
        input_output_aliases={2: 0},
        compiler_params=pltpu.CompilerParams(
            dimension_semantics=("arbitrary",), vmem_limit_bytes=VMEM_LIMIT),
        name="dispatch",
    )(dest_tiles, hp, xs0)


def _swiglu(xb, wgu, wd):
    gu = jnp.dot(xb, wgu, preferred_element_type=_F32)
    g, u = gu[:, :EXPERT_FF], gu[:, EXPERT_FF:]
    mid = (g * jax.nn.sigmoid(g) * u).astype(_BF16)
    return jnp.dot(mid, wd, preferred_element_type=_F32)


def _load_packed_bf16(ref, rows):
    return jnp.concatenate([p.astype(_BF16) for p in _load_packed(ref, 0, rows)], axis=1)


def _expert_kernel(nblk_ref, bstart_ref, nused_ref, xs_ref, wg_ref, wu_ref, wd_ref, ys_ref,
                   xbuf, ybuf, xsem, ysem, wgu_s, wd_s, *, n_blocks):
    e = pl.program_id(0)
    nb = nblk_ref[e]
    b0 = bstart_ref[e]
    nused = nused_ref[0]

    def x_copy(blk):
        slot = lax.rem(blk, 2)
        return pltpu.make_async_copy(_token_rows(xs_ref, blk * T_M, T_M), _token_rows(xbuf, slot * T_M, T_M),
                                     xsem.at[slot])

    def y_copy(blk):
        slot = lax.rem(blk, 2)
        return pltpu.make_async_copy(_token_rows(ybuf, slot * T_M, T_M), _token_rows(ys_ref, blk * T_M, T_M),
                                     ysem.at[slot])

    @pl.when(e == 0)
    def _():
        x_copy(0).start()

    @pl.when(nb > 0)
    def _():
        wgu_s[:, :EXPERT_FF] = wg_ref[0].astype(_BF16)
        wgu_s[:, EXPERT_FF:] = wu_ref[0].astype(_BF16)
        wd_s[...] = wd_ref[0].astype(_BF16)

        def block(g, _):
            first = lax.rem(g, 2) * T_M
            x_copy(g).wait()

            @pl.when(g + 1 < nused)
            def _():
                x_copy(g + 1).start()

            @pl.when(g >= 2)
            def _():
                y_copy(g - 2).wait()

            xb = jnp.concatenate([p.astype(_BF16) for p in _load_packed(xbuf, first, T_M)], axis=1)
            _store_packed(ybuf, _swiglu(xb, wgu_s[...], wd_s[...]), first)
            y_copy(g).start()
            return 0

        lax.fori_loop(b0, b0 + nb, block, 0)

    @pl.when(e == pl.num_programs(0) - 1)
    def _():
        @pl.when(nused >= 2)
        def _():
            y_copy(nused - 2).wait()

        y_copy(nused - 1).wait()
        ybuf[pl.ds(0, T_M * PACK_ROWS), :] = jnp.zeros((T_M * PACK_ROWS, 128), _U32)

        def zero_copy(blk):
            return pltpu.make_async_copy(_token_rows(ybuf, 0, T_M), _token_rows(ys_ref, blk * T_M, T_M), ysem.at[0])

        def fill(blk, _):
            zero_copy(blk).start()
            return 0

        def drain(blk, _):
            zero_copy(blk).wait()
            return 0

        lax.fori_loop(nused, n_blocks, fill, 0)
        lax.fori_loop(nused, n_blocks, drain, 0)


def _experts(nblk, bstart, nused, xs, wg, wu, wd):
    n_slots = xs.shape[0] // PACK_ROWS
    n_exp, d, ff = wg.shape
    buf = pltpu.VMEM((2 * T_M * PACK_ROWS, 128), _U32)
    return pl.pallas_call(
        functools.partial(_expert_kernel, n_blocks=n_slots // T_M),
        out_shape=jax.ShapeDtypeStruct(xs.shape, _U32),
        grid_spec=pltpu.PrefetchScalarGridSpec(
            num_scalar_prefetch=3,
            grid=(n_exp,),
            in_specs=[pl.BlockSpec(memory_space=pl.ANY),
                      pl.BlockSpec((1, d, ff), lambda i, nbk, bst, nu: (i, 0, 0)),
                      pl.BlockSpec((1, d, ff), lambda i, nbk, bst, nu: (i, 0, 0)),
                      pl.BlockSpec((1, ff, d), lambda i, nbk, bst, nu: (i, 0, 0))],
            out_specs=pl.BlockSpec(memory_space=pl.ANY),
            scratch_shapes=[buf, buf, pltpu.SemaphoreType.DMA((2,)), pltpu.SemaphoreType.DMA((2,)),
                            pltpu.VMEM((d, 2 * ff), _BF16), pltpu.VMEM((ff, d), _BF16)]),
        compiler_params=pltpu.CompilerParams(
            dimension_semantics=("arbitrary",), vmem_limit_bytes=VMEM_LIMIT),
        name="experts",
    )(nblk, bstart, nused, xs, wg, wu, wd)


def _combine_kernel(dest_ref, ys_ref, hp_ref, w_ref, x1_ref, mod_ref, wgus_ref, wds_ref, gpost_ref,
                    o_ref, gbuf, sem):
    t_rows = hp_ref.shape[0] // PACK_ROWS

    def issue(t, _):
        for kk in range(TOP_K):
            _row_copy(ys_ref, dest_ref[0, 0, t * TOP_K + kk], gbuf, kk * t_rows + t, sem).start(priority=kk % 2)
        return 0

    lax.fori_loop(0, t_rows, issue, 0)

    shared = _swiglu(_load_packed_bf16(hp_ref, t_rows), wgus_ref[...], wds_ref[...])

    def drain(t, _):
        for kk in range(TOP_K):
            _row_copy(ys_ref, 0, gbuf, 0, sem).wait()
        return 0

    lax.fori_loop(0, t_rows, drain, 0)

    w = w_ref[...]
    cols = None
    for kk in range(TOP_K):
        wk = w[:, kk:kk + 1]
        parts = [wk * p for p in _load_packed(gbuf, kk * t_rows, t_rows)]
        cols = parts if cols is None else [a + b for a, b in zip(cols, parts)]
    y = jnp.concatenate(cols, axis=1) + shared
    gate_f = mod_ref[0, 5:6, :]
    o_ref[...] = x1_ref[...] + gate_f * (y * _rsqrt_mean_sq(y, -1) * gpost_ref[...])


def _combine(dest_tiles, ys, hp, w_tok, x1, mod, wgus, wds, gpost, tiles_per_batch):
    n, d = x1.shape
    t = T_ROW
    full = lambda a: pl.BlockSpec(a.shape, lambda i: (0,) * a.ndim)
    return pl.pallas_call(
        _combine_kernel,
        out_shape=jax.ShapeDtypeStruct((n, d), _F32),
        grid=(n // t,),
        in_specs=[pl.BlockSpec((1, 1, t * TOP_K), lambda i: (i, 0, 0), memory_space=pltpu.SMEM),
                  pl.BlockSpec(memory_space=pl.ANY),
                  pl.BlockSpec((t * PACK_ROWS, 128), lambda i: (i, 0)),
                  pl.BlockSpec((t, TOP_K), lambda i: (i, 0)),
                  pl.BlockSpec((t, d), lambda i: (i, 0)),
                  pl.BlockSpec((1, N_MOD, d), lambda i: (i // tiles_per_batch, 0, 0)),
                  full(wgus), full(wds), full(gpost)],
        out_specs=pl.BlockSpec((t, d), lambda i: (i, 0)),
        scratch_shapes=[pltpu.VMEM((TOP_K * t * PACK_ROWS, 128), _U32), pltpu.SemaphoreType.DMA(())],
        compiler_params=pltpu.CompilerParams(
            dimension_semantics=("arbitrary",), vmem_limit_bytes=VMEM_LIMIT),
        name="combine",
    )(dest_tiles, ys, hp, w_tok, x1, mod, wgus, wds, gpost)


def _deinterleave(n):
    return np.concatenate([np.arange(0, n, 2), np.arange(1, n, 2)])


def _rope_tables(s, dim):
    rows = s // GRID_W
    row = jnp.broadcast_to(jnp.arange(rows, dtype=_F32)[:, None], (rows, GRID_W)).reshape(s)
    col = jnp.broadcast_to(jnp.arange(GRID_W, dtype=_F32)[None, :], (rows, GRID_W)).reshape(s)
    n_freq = dim // 4
    inv_freq = ROPE_THETA ** (-jnp.arange(n_freq, dtype=_F32) / n_freq)
    ang = jnp.concatenate([row[:, None] * inv_freq, col[:, None] * inv_freq], axis=-1)
    return jnp.cos(ang).T, jnp.sin(ang).T


def _in_proj_perm():
    perm = list(range(OFF_KROPE))
    perm += list(OFF_KROPE + _deinterleave(MLA_ROPE))
    for hd in range(GQA_HEADS):
        perm += list(OFF_QG + hd * GQA_HEAD_DIM + _deinterleave(GQA_HEAD_DIM))
    for hd in range(GQA_KV_HEADS):
        perm += list(OFF_KG + hd * GQA_HEAD_DIM + _deinterleave(GQA_HEAD_DIM))
    perm += list(range(OFF_VG, IN_COLS))
    return np.asarray(perm)


def _q_b_perm():
    per = MLA_NOPE + MLA_ROPE
    nope = [hd * per + j for hd in range(MLA_HEADS) for j in range(MLA_NOPE)]
    even = [hd * per + MLA_NOPE + 2 * i for hd in range(MLA_HEADS) for i in range(MLA_ROPE // 2)]
    odd = [hd * per + MLA_NOPE + 2 * i + 1 for hd in range(MLA_HEADS) for i in range(MLA_ROPE // 2)]
    return np.asarray(nope + even + odd)


def _kv_b_perm():
    per = MLA_NOPE + MLA_V
    kk = [hd * per + j for hd in range(MLA_HEADS) for j in range(MLA_NOPE)]
    vv = [hd * per + MLA_NOPE + j for hd in range(MLA_HEADS) for j in range(MLA_V)]
    return np.asarray(kk + vv)


def _layer(x, c, w_ada, b_ada, g_pre_mix, g_post_mix, g_pre_ffn, g_post_ffn, w_in, g_q_a, w_q_b,
           g_kv_a, w_kv_b, g_q_head, g_k_head, g_mla_out, g_gqa_out, w_out, w_router, b_router,
           w_gate_e, w_up_e, w_down_e, w_gate_s, w_up_s, w_down_s):
    b, s, d = x.shape
    n = b * s
    col = lambda g: g.reshape(-1, 1).astype(_F32)
    row = lambda g: g.reshape(1, -1).astype(_F32)

    mod = _ada_mod(c, w_ada, b_ada)

    winT = w_in[:, _in_proj_perm()].T.astype(_BF16)
    wqbT = w_q_b[:, _q_b_perm()].T.astype(_BF16)
    wkvbT = w_kv_b[:, _kv_b_perm()].T.astype(_BF16)
    perm_h = _deinterleave(GQA_HEAD_DIM)
    cosm, sinm = _rope_tables(s, MLA_ROPE)
    cosg, sing = _rope_tables(s, GQA_HEAD_DIM)

    qT, k, vT = _project(x, mod, row(g_pre_mix), winT, col(g_q_a), wqbT, col(g_kv_a), wkvbT,
                         col(g_q_head[perm_h]), col(g_k_head[perm_h]), cosm, sinm, cosg, sing)
    oT = _attention(qT, k, vT)

    wr_t = w_router.T.astype(_F32)
    wrh = wr_t.astype(_BF16)
    wrl = (wr_t - wrh.astype(_F32)).astype(_BF16)
    x1, hp, idx_t, w_t, rank_t, cnt = _post_attention(
        x, oT, mod, col(g_mla_out), col(g_gqa_out), w_out.T.astype(_BF16), row(g_post_mix),
        row(g_pre_ffn), wrh, wrl, col(b_router))

    counts = cnt[:, 0].astype(jnp.int32)
    blocks_e = (counts + T_M - 1) // T_M
    blk_end = jnp.cumsum(blocks_e)
    start = (blk_end - blocks_e) * T_M
    n_blocks = n * TOP_K // T_M + N_EXPERTS
    nused = blk_end[-1:].astype(jnp.int32)
    bstart = (blk_end - blocks_e).astype(jnp.int32)
    dest = _slots(idx_t, rank_t, start)
    dest_tiles = dest.T.reshape(n // T_ROW, 1, T_ROW * TOP_K)

    xs = _dispatch(dest_tiles, hp, n_blocks * T_M)
    ys = _experts(blocks_e.astype(jnp.int32), bstart, nused, xs, w_gate_e, w_up_e, w_down_e)
    wgus = jnp.concatenate([w_gate_s, w_up_s], axis=1).astype(_BF16)
    out = _combine(dest_tiles, ys, hp, w_t.T, x1.reshape(n, d), mod, wgus, w_down_s.astype(_BF16),
                   row(g_post_ffn), s // T_ROW)
    return out.reshape(b, s, d)


def kernel(x, c, w_ada, b_ada, g_pre_mix, g_post_mix, g_pre_ffn, g_post_ffn, w_in, g_q_a, w_q_b, g_kv_a, w_kv_b, g_q_head, g_k_head, g_mla_out, g_gqa_out, w_out, w_router, b_router, w_gate_e, w_up_e, w_down_e, w_gate_s, w_up_s, w_down_s):
    depth = w_ada.shape[0]
    for l in range(depth):
        x = _layer(x, c, w_ada[l], b_ada[l], g_pre_mix[l], g_post_mix[l], g_pre_ffn[l], g_post_ffn[l],
                   w_in[l], g_q_a[l], w_q_b[l], g_kv_a[l], w_kv_b[l], g_q_head[l], g_k_head[l],
                   g_mla_out[l], g_gqa_out[l], w_out[l], w_router[l], b_router[l], w_gate_e[l],
                   w_up_e[l], w_down_e[l], w_gate_s[l], w_up_s[l], w_down_s[l])
    return x
```

```python
import functools

import jax
import jax.numpy as jnp
import numpy as np
from jax import lax
from jax.experimental import pallas as pl
from jax.experimental.pallas import tpu as pltpu

D_MODEL = 1024
GRID_W = 64
ROPE_THETA = 10000.0
NORM_EPS = 1e-6

MLA_HEADS = 8
MLA_Q_LORA = 256
MLA_KV_LORA = 128
MLA_NOPE = 64
MLA_ROPE = 32
MLA_V = 64

GQA_HEADS = 8
GQA_KV_HEADS = 2
GQA_HEAD_DIM = 64
GQA_GROUP = GQA_HEADS // GQA_KV_HEADS

N_HEADS = MLA_HEADS + GQA_HEADS
N_KV = MLA_HEADS + GQA_KV_HEADS
HEAD_PAD = 128
V_DIM = 64

N_EXPERTS = 256
TOP_K = 8
N_GROUPS = 8
GROUP_SIZE = N_EXPERTS // N_GROUPS
TOPK_GROUPS = 4
EXPERT_FF = 256
ROUTED_SCALE = 2.5
N_MOD = 6

OFF_QLAT = 0
OFF_KVLAT = OFF_QLAT + MLA_Q_LORA
OFF_KROPE = OFF_KVLAT + MLA_KV_LORA
OFF_QG = OFF_KROPE + MLA_ROPE
OFF_KG = OFF_QG + GQA_HEADS * GQA_HEAD_DIM
OFF_VG = OFF_KG + GQA_KV_HEADS * GQA_HEAD_DIM
IN_COLS = OFF_VG + GQA_KV_HEADS * GQA_HEAD_DIM

T_PROJ = 512
T_Q = 512
T_POST = 512
T_ROW = 256
T_M = 256
PACK_ROWS = 4
N_SCORE_SLOTS = 4
X_RING = 4
Y_RING = 3
LOG2_E = 1.4426950408889634
VMEM_LIMIT = 56 * 1024 * 1024

_NT = (((1,), (1,)), ((), ()))
_F32 = jnp.float32
_BF16 = jnp.bfloat16
_U32 = jnp.uint32
_HI_MASK = np.uint32(0xFFFF0000)


def _rsqrt_mean_sq(v, axis):
    return lax.rsqrt(jnp.mean(v * v, axis=axis, keepdims=True) + NORM_EPS)


def _pack_pair(lo, hi):
    lo_b = pltpu.bitcast(lo.astype(_BF16).astype(_F32), _U32) >> 16
    hi_b = pltpu.bitcast(hi.astype(_BF16).astype(_F32), _U32) & _HI_MASK
    return lo_b | hi_b


def _unpack_pair(u):
    return pltpu.bitcast(u << 16, _F32), pltpu.bitcast(u & _HI_MASK, _F32)


def _token_rows(ref, first, count=1):
    return ref.at[pl.ds(pl.multiple_of(first * PACK_ROWS, PACK_ROWS), count * PACK_ROWS), :]


def _chunk_of_tokens(first, count, c):
    return pl.ds(first * PACK_ROWS + c, count, stride=PACK_ROWS)


def _store_packed(ref, val, first=0):
    rows = val.shape[0]
    for c in range(PACK_ROWS):
        ref[_chunk_of_tokens(first, rows, c), :] = _pack_pair(
            val[:, 256 * c:256 * c + 128], val[:, 256 * c + 128:256 * c + 256])


def _load_packed(ref, first, rows):
    parts = []
    for c in range(PACK_ROWS):
        parts += list(_unpack_pair(ref[_chunk_of_tokens(first, rows, c), :]))
    return parts


def _ada_kernel(c_ref, w_ref, b_ref, o_ref):
    c = c_ref[...]
    a = (c * jax.nn.sigmoid(c)).astype(_BF16)
    o_ref[...] = jnp.dot(a, w_ref[...].astype(_BF16), preferred_element_type=_F32) + b_ref[...]


def _ada_mod(c, w_ada, b_ada):
    b, d = c.shape
    cols = w_ada.shape[1]
    rows = 8
    tn = 1536
    c_pad = jnp.zeros((rows, d), _F32).at[:b].set(c)
    out = pl.pallas_call(
        _ada_kernel,
        out_shape=jax.ShapeDtypeStruct((rows, cols), _F32),
        grid=(cols // tn,),
        in_specs=[pl.BlockSpec((rows, d), lambda j: (0, 0)),
                  pl.BlockSpec((d, tn), lambda j: (0, j)),
                  pl.BlockSpec((1, tn), lambda j: (0, j))],
        out_specs=pl.BlockSpec((rows, tn), lambda j: (0, j)),
        compiler_params=pltpu.CompilerParams(vmem_limit_bytes=VMEM_LIMIT),
        name="ada_mod",
    )(c_pad, w_ada, b_ada.reshape(1, cols))
    return out[:b].reshape(b, N_MOD, d)


def _rope(x0, x1, cos, sin):
    return x0 * cos - x1 * sin, x0 * sin + x1 * cos


def _proj_kernel(x_ref, mod_ref, gpre_ref, win_ref, gqa_ref, wqb_ref, gkva_ref, wkvb_ref,
                 gqh_ref, gkh_ref, cosm_ref, sinm_ref, cosg_ref, sing_ref,
                 qT_ref, k_ref, vT_ref):
    x = x_ref[0]
    t = x.shape[0]
    shift = mod_ref[0, 0:1, :]
    scale = mod_ref[0, 1:2, :]
    h = x * _rsqrt_mean_sq(x, -1) * gpre_ref[...] * (1.0 + scale) + shift
    pT = lax.dot_general(win_ref[...], h.astype(_BF16), _NT, preferred_element_type=_F32)

    cm, sm = cosm_ref[...], sinm_ref[...]
    cg, sg = cosg_ref[...], sing_ref[...]
    half_m = MLA_ROPE // 2
    half_g = GQA_HEAD_DIM // 2

    ql = pT[OFF_QLAT:OFF_QLAT + MLA_Q_LORA]
    qn = (ql * _rsqrt_mean_sq(ql, 0) * gqa_ref[...]).astype(_BF16)
    qT = jnp.dot(wqb_ref[...], qn, preferred_element_type=_F32)
    sc_m = float((MLA_NOPE + MLA_ROPE) ** -0.5 * LOG2_E)
    n_nope = MLA_HEADS * MLA_NOPE
    n_half = MLA_HEADS * half_m
    q0, q1 = _rope(qT[n_nope:n_nope + n_half], qT[n_nope + n_half:],
                   jnp.tile(cm, (MLA_HEADS, 1)), jnp.tile(sm, (MLA_HEADS, 1)))
    zq_m = jnp.zeros((HEAD_PAD - MLA_NOPE - MLA_ROPE, t), _BF16)
    for hd in range(MLA_HEADS):
        qT_ref[0, hd, 0:MLA_NOPE, :] = (qT[hd * MLA_NOPE:(hd + 1) * MLA_NOPE] * sc_m).astype(_BF16)
        qT_ref[0, hd, MLA_NOPE:MLA_NOPE + half_m, :] = (q0[hd * half_m:(hd + 1) * half_m] * sc_m).astype(_BF16)
        qT_ref[0, hd, MLA_NOPE + half_m:MLA_NOPE + MLA_ROPE, :] = (
            q1[hd * half_m:(hd + 1) * half_m] * sc_m).astype(_BF16)
        qT_ref[0, hd, MLA_NOPE + MLA_ROPE:, :] = zq_m

    kl = pT[OFF_KVLAT:OFF_KVLAT + MLA_KV_LORA]
    kn = (kl * _rsqrt_mean_sq(kl, 0) * gkva_ref[...]).astype(_BF16)
    kvT = jnp.dot(wkvb_ref[...], kn, preferred_element_type=_F32)
    k0, k1 = _rope(pT[OFF_KROPE:OFF_KROPE + half_m], pT[OFF_KROPE + half_m:OFF_KROPE + MLA_ROPE], cm, sm)
    zk_m = jnp.zeros((HEAD_PAD - MLA_NOPE - MLA_ROPE, t), _F32)
    for hd in range(MLA_HEADS):
        kTh = jnp.concatenate([kvT[hd * MLA_NOPE:(hd + 1) * MLA_NOPE], k0, k1, zk_m], axis=0)
        k_ref[0, hd] = kTh.T.astype(_BF16)
        vT_ref[0, hd, 0] = kvT[n_nope + hd * MLA_V:n_nope + (hd + 1) * MLA_V].astype(_BF16)

    sc_g = float(GQA_HEAD_DIM ** -0.5 * LOG2_E)
    zq_g = jnp.zeros((HEAD_PAD - GQA_HEAD_DIM, t), _BF16)
    for hd in range(GQA_HEADS):
        blk = pT[OFF_QG + hd * GQA_HEAD_DIM:OFF_QG + (hd + 1) * GQA_HEAD_DIM]
        bn = blk * _rsqrt_mean_sq(blk, 0) * gqh_ref[...]
        g0, g1 = _rope(bn[:half_g], bn[half_g:], cg, sg)
        qT_ref[0, MLA_HEADS + hd, 0:half_g, :] = (g0 * sc_g).astype(_BF16)
        qT_ref[0, MLA_HEADS + hd, half_g:GQA_HEAD_DIM, :] = (g1 * sc_g).astype(_BF16)
        qT_ref[0, MLA_HEADS + hd, GQA_HEAD_DIM:, :] = zq_g

    zk_g = jnp.zeros((HEAD_PAD - GQA_HEAD_DIM, t), _F32)
    for hd in range(GQA_KV_HEADS):
        blk = pT[OFF_KG + hd * GQA_HEAD_DIM:OFF_KG + (hd + 1) * GQA_HEAD_DIM]
        bn = blk * _rsqrt_mean_sq(blk, 0) * gkh_ref[...]
        g0, g1 = _rope(bn[:half_g], bn[half_g:], cg, sg)
        kTh = jnp.concatenate([g0, g1, zk_g], axis=0)
        k_ref[0, MLA_HEADS + hd] = kTh.T.astype(_BF16)
        vT_ref[0, MLA_HEADS + hd, 0] = pT[OFF_VG + hd * V_DIM:OFF_VG + (hd + 1) * V_DIM].astype(_BF16)


def _project(x, mod, gpre, winT, gqa, wqbT, gkva, wkvbT, gqh, gkh, cosm, sinm, cosg, sing):
    b, s, d = x.shape
    t = T_PROJ
    nt = s // t
    full = lambda a: pl.BlockSpec(a.shape, lambda bi, i: (0,) * a.ndim)
    tab = lambda a: pl.BlockSpec((a.shape[0], t), lambda bi, i: (0, i))
    return pl.pallas_call(
        _proj_kernel,
        out_shape=(jax.ShapeDtypeStruct((b, N_HEADS, HEAD_PAD, s), _BF16),
                   jax.ShapeDtypeStruct((b, N_KV, s, HEAD_PAD), _BF16),
                   jax.ShapeDtypeStruct((b, N_KV, nt, V_DIM, t), _BF16)),
        grid=(b, nt),
        in_specs=[pl.BlockSpec((1, t, d), lambda bi, i: (bi, i, 0)),
                  pl.BlockSpec((1, N_MOD, d), lambda bi, i: (bi, 0, 0)),
                  full(gpre), full(winT), full(gqa), full(wqbT), full(gkva), full(wkvbT),
                  full(gqh), full(gkh), tab(cosm), tab(sinm), tab(cosg), tab(sing)],
        out_specs=(pl.BlockSpec((1, N_HEADS, HEAD_PAD, t), lambda bi, i: (bi, 0, 0, i)),
                   pl.BlockSpec((1, N_KV, t, HEAD_PAD), lambda bi, i: (bi, 0, i, 0)),
                   pl.BlockSpec((1, N_KV, 1, V_DIM, t), lambda bi, i: (bi, 0, i, 0, 0))),
        compiler_params=pltpu.CompilerParams(
            dimension_semantics=("arbitrary", "arbitrary"), vmem_limit_bytes=VMEM_LIMIT),
        name="in_proj",
    )(x, mod, gpre, winT, gqa, wqbT, gkva, wkvbT, gqh, gkh, cosm, sinm, cosg, sing)


def _attn_kernel(qT_ref, k_ref, vT_ref, o_ref, *slots, n_chunks, tk):
    qT = qT_ref[0, 0]
    tq = qT.shape[1]
    ns = len(slots)

    def scores(j, slot):
        kc = k_ref[0, 0, pl.ds(pl.multiple_of(j * tk, tk), tk), :]
        s = jnp.dot(kc, qT, preferred_element_type=_F32)
        slots[slot][...] = s
        return jnp.max(s, axis=0, keepdims=True)

    def accumulate(j, slot, mc, carry):
        m, l, acc = carry
        m_new = jnp.maximum(m, mc)
        alpha = jnp.exp2(m - m_new)
        p = jnp.exp2(slots[slot][...] - m_new)
        l = alpha * l + jnp.sum(p, axis=0, keepdims=True)
        acc = alpha * acc + jnp.dot(vT_ref[0, 0, j], p.astype(_BF16), preferred_element_type=_F32)
        return m_new, l, acc

    def group(jj, state):
        mcs, carry = state
        mcs = list(mcs)
        j = ns * jj
        for u in range(ns):
            nxt = scores(j + u + 2, (u + 2) % ns)
            carry = accumulate(j + u, u, mcs[u], carry)
            mcs[(u + 2) % ns] = nxt
        return tuple(mcs), carry

    carry = (jnp.full((1, tq), -1e30, _F32), jnp.zeros((1, tq), _F32), jnp.zeros((V_DIM, tq), _F32))
    zero = jnp.zeros((1, tq), _F32)
    mcs = (scores(0, 0), scores(1, 1)) + (zero,) * (ns - 2)
    n_loop = (n_chunks - 2) // ns
    mcs, carry = lax.fori_loop(0, n_loop, group, (mcs, carry))
    mcs = list(mcs)
    for j in range(n_loop * ns, n_chunks):
        if j + 2 < n_chunks:
            mcs[(j + 2) % ns] = scores(j + 2, (j + 2) % ns)
        carry = accumulate(j, j % ns, mcs[j % ns], carry)
    _, l, acc = carry
    o_ref[0, 0] = acc / l


def _kv_index(h):
    return jnp.where(h < MLA_HEADS, h, MLA_HEADS + (h - MLA_HEADS) // GQA_GROUP)


def _attention(qT, k, vT):
    b, nh, dp, s = qT.shape
    n_chunks, tk = vT.shape[2], vT.shape[4]
    assert n_chunks > N_SCORE_SLOTS
    return pl.pallas_call(
        functools.partial(_attn_kernel, n_chunks=n_chunks, tk=tk),
        out_shape=jax.ShapeDtypeStruct((b, nh, V_DIM, s), _F32),
        grid=(b, nh, s // T_Q),
        in_specs=[pl.BlockSpec((1, 1, dp, T_Q), lambda bi, h, i: (bi, h, 0, i)),
                  pl.BlockSpec((1, 1, s, dp), lambda bi, h, i: (bi, _kv_index(h), 0, 0)),
                  pl.BlockSpec((1, 1, n_chunks, V_DIM, tk), lambda bi, h, i: (bi, _kv_index(h), 0, 0, 0))],
        out_specs=pl.BlockSpec((1, 1, V_DIM, T_Q), lambda bi, h, i: (bi, h, 0, i)),
        scratch_shapes=[pltpu.VMEM((tk, T_Q), _F32) for _ in range(N_SCORE_SLOTS)],
        compiler_params=pltpu.CompilerParams(
            dimension_semantics=("arbitrary", "arbitrary", "arbitrary"), vmem_limit_bytes=VMEM_LIMIT),
        name="attention",
    )(qT, k, vT)


def _first_argmax(v, rows, n):
    m = jnp.max(v, axis=0, keepdims=True)
    idx = jnp.min(jnp.where(v == m, rows, n), axis=0, keepdims=True)
    return m, idx


def _post_kernel(x_ref, o_ref, mod_ref, gmla_ref, ggqa_ref, wout_ref, gpost_ref, gffn_ref,
                 wrh_ref, wrl_ref, br_ref,
                 x1_ref, hp_ref, idx_ref, w_ref, rank_ref, cnt_ref, carry_ref):
    first = jnp.logical_and(pl.program_id(0) == 0, pl.program_id(1) == 0)

    @pl.when(first)
    def _():
        carry_ref[...] = jnp.zeros_like(carry_ref)

    x = x_ref[0]
    t = x.shape[0]
    gate_m = mod_ref[0, 2:3, :]
    shift_f = mod_ref[0, 3:4, :]
    scale_f = mod_ref[0, 4:5, :]

    o = o_ref[0].reshape(N_HEADS * V_DIM, t)
    half = MLA_HEADS * V_DIM
    om, og = o[:half], o[half:]
    mixedT = jnp.concatenate([om * _rsqrt_mean_sq(om, 0) * gmla_ref[...],
                              og * _rsqrt_mean_sq(og, 0) * ggqa_ref[...]], axis=0).astype(_BF16)
    y = jnp.dot(wout_ref[...], mixedT, preferred_element_type=_F32).T
    x1 = x + gate_m * (y * _rsqrt_mean_sq(y, -1) * gpost_ref[...])
    x1_ref[0] = x1
    h2 = x1 * _rsqrt_mean_sq(x1, -1) * gffn_ref[...] * (1.0 + scale_f) + shift_f

    _store_packed(hp_ref, h2)

    hh = h2.astype(_BF16)
    hl = (h2 - hh.astype(_F32)).astype(_BF16)
    wrh = wrh_ref[...]
    logits = (lax.dot_general(wrh, hh, _NT, preferred_element_type=_F32)
              + lax.dot_general(wrh, hl, _NT, preferred_element_type=_F32)
              + lax.dot_general(wrl_ref[...], hh, _NT, preferred_element_type=_F32))
    scores = jax.nn.sigmoid(logits)
    biased = scores + br_ref[...]
    neg = -jnp.inf

    rows_g = lax.broadcasted_iota(jnp.int32, (GROUP_SIZE, t), 0)
    gscores = []
    for g in range(N_GROUPS):
        v = biased[g * GROUP_SIZE:(g + 1) * GROUP_SIZE]
        m1, i1 = _first_argmax(v, rows_g, GROUP_SIZE)
        m2 = jnp.max(jnp.where(rows_g == i1, neg, v), axis=0, keepdims=True)
        gscores.append(m1 + m2)
    cur = jnp.concatenate(gscores, axis=0)
    rows_8 = lax.broadcasted_iota(jnp.int32, (N_GROUPS, t), 0)
    gsel = jnp.zeros((N_GROUPS, t), _F32)
    for _ in range(TOPK_GROUPS):
        _, gi = _first_argmax(cur, rows_8, N_GROUPS)
        hit = rows_8 == gi
        gsel = jnp.where(hit, 1.0, gsel)
        cur = jnp.where(hit, neg, cur)
    emask = jnp.concatenate(
        [jnp.broadcast_to(gsel[g:g + 1], (GROUP_SIZE, t)) for g in range(N_GROUPS)], axis=0) > 0.5
    masked = jnp.where(emask, biased, neg)

    rows_e = lax.broadcasted_iota(jnp.int32, (N_EXPERTS, t), 0)
    idxs, ws = [], []
    for _ in range(TOP_K):
        _, ei = _first_argmax(masked, rows_e, N_EXPERTS)
        hit = rows_e == ei
        idxs.append(ei)
        ws.append(jnp.sum(jnp.where(hit, scores, 0.0), axis=0, keepdims=True))
        masked = jnp.where(hit, neg, masked)
    wsum = ws[0]
    for wk in ws[1:]:
        wsum = wsum + wk

    onehot = jnp.zeros((N_EXPERTS, t), _F32)
    for ei in idxs:
        onehot = onehot + jnp.where(rows_e == ei, 1.0, 0.0)
    upper = (lax.broadcasted_iota(jnp.int32, (t, t), 0) < lax.broadcasted_iota(jnp.int32, (t, t), 1))
    before = jnp.dot(onehot.astype(_BF16), jnp.where(upper, 1.0, 0.0).astype(_BF16),
                     preferred_element_type=_F32) + carry_ref[...]
    for kk in range(TOP_K):
        idx_ref[kk:kk + 1, :] = idxs[kk]
        w_ref[kk:kk + 1, :] = ws[kk] / wsum * ROUTED_SCALE
        rank_ref[kk:kk + 1, :] = jnp.sum(
            jnp.where(rows_e == idxs[kk], before, 0.0), axis=0, keepdims=True).astype(jnp.int32)
    total = carry_ref[...] + jnp.sum(onehot, axis=1, keepdims=True)
    carry_ref[...] = total
    cnt_ref[...] = jnp.broadcast_to(total, cnt_ref.shape)


def _post_attention(x, oT, mod, gmla, ggqa, woutT, gpost, gffn, wrh, wrl, br):
    b, s, d = x.shape
    n = b * s
    t = T_POST
    nt = s // t
    full = lambda a: pl.BlockSpec(a.shape, lambda bi, i: (0,) * a.ndim)
    tok = lambda bi, i: (0, bi * nt + i)
    return pl.pallas_call(
        _post_kernel,
        out_shape=(jax.ShapeDtypeStruct((b, s, d), _F32),
                   jax.ShapeDtypeStruct((n * PACK_ROWS, 128), _U32),
                   jax.ShapeDtypeStruct((TOP_K, n), jnp.int32),
                   jax.ShapeDtypeStruct((TOP_K, n), _F32),
                   jax.ShapeDtypeStruct((TOP_K, n), jnp.int32),
                   jax.ShapeDtypeStruct((N_EXPERTS, 128), _F32)),
        grid=(b, nt),
        in_specs=[pl.BlockSpec((1, t, d), lambda bi, i: (bi, i, 0)),
                  pl.BlockSpec((1, N_HEADS, V_DIM, t), lambda bi, i: (bi, 0, 0, i)),
                  pl.BlockSpec((1, N_MOD, d), lambda bi, i: (bi, 0, 0)),
                  full(gmla), full(ggqa), full(woutT), full(gpost), full(gffn),
                  full(wrh), full(wrl), full(br)],
        out_specs=(pl.BlockSpec((1, t, d), lambda bi, i: (bi, i, 0)),
                   pl.BlockSpec((t * PACK_ROWS, 128), lambda bi, i: (bi * nt + i, 0)),
                   pl.BlockSpec((TOP_K, t), tok),
                   pl.BlockSpec((TOP_K, t), tok),
                   pl.BlockSpec((TOP_K, t), tok),
                   pl.BlockSpec((N_EXPERTS, 128), lambda bi, i: (0, 0))),
        scratch_shapes=[pltpu.VMEM((N_EXPERTS, 1), _F32)],
        compiler_params=pltpu.CompilerParams(
            dimension_semantics=("arbitrary", "arbitrary"), vmem_limit_bytes=VMEM_LIMIT),
        name="out_proj_router",
    )(x, oT, mod, gmla, ggqa, woutT, gpost, gffn, wrh, wrl, br)


def _slot_kernel(idx_ref, rank_ref, start_ref, dest_ref):
    t = idx_ref.shape[1]
    rows_e = lax.broadcasted_iota(jnp.int32, (N_EXPERTS, t), 0)
    start = start_ref[...]
    for kk in range(TOP_K):
        base = jnp.sum(jnp.where(rows_e == idx_ref[kk:kk + 1, :], start, 0.0), axis=0, keepdims=True)
        dest_ref[kk:kk + 1, :] = rank_ref[kk:kk + 1, :] + base.astype(jnp.int32)


def _slots(idx_t, rank_t, start):
    n = idx_t.shape[1]
    t = T_POST
    return pl.pallas_call(
        _slot_kernel,
        out_shape=jax.ShapeDtypeStruct((TOP_K, n), jnp.int32),
        grid=(n // t,),
        in_specs=[pl.BlockSpec((TOP_K, t), lambda i: (0, i)),
                  pl.BlockSpec((TOP_K, t), lambda i: (0, i)),
                  pl.BlockSpec((N_EXPERTS, 1), lambda i: (0, 0))],
        out_specs=pl.BlockSpec((TOP_K, t), lambda i: (0, i)),
        compiler_params=pltpu.CompilerParams(dimension_semantics=("arbitrary",)),
        name="slot_index",
    )(idx_t, rank_t, start.astype(_F32).reshape(N_EXPERTS, 1))


def _row_copy(src, src_row, dst, dst_row, sem):
    return pltpu.make_async_copy(_token_rows(src, src_row), _token_rows(dst, dst_row), sem)


def _dispatch_kernel(dest_ref, h_ref, xs_in_ref, xs_ref, sem):
    del xs_in_ref
    t_rows = h_ref.shape[0] // PACK_ROWS

    def issue(t, _):
        for kk in range(TOP_K):
            _row_copy(h_ref, t, xs_ref, dest_ref[0, 0, t * TOP_K + kk], sem).start(priority=kk % 2)
        return 0

    lax.fori_loop(0, t_rows, issue, 0)

    def drain(t, _):
        for kk in range(TOP_K):
            _row_copy(h_ref, 0, xs_ref, 0, sem).wait()
        return 0

    lax.fori_loop(0, t_rows, drain, 0)


def _dispatch(dest_tiles, hp, n_slots):
    n = hp.shape[0] // PACK_ROWS
    t = T_ROW
    xs0 = jnp.zeros((n_slots * PACK_ROWS, 128), _U32)
    return pl.pallas_call(
        _dispatch_kernel,
        out_shape=jax.ShapeDtypeStruct((n_slots * PACK_ROWS, 128), _U32),
        grid=(n // t,),
        in_specs=[pl.BlockSpec((1, 1, t * TOP_K), lambda i: (i, 0, 0), memory_space=pltpu.SMEM),
                  pl.BlockSpec((t * PACK_ROWS, 128), lambda i: (i, 0)),
                  pl.BlockSpec(memory_space=pl.ANY)],
        out_specs=pl.BlockSpec(memory_space=pl.ANY),
        scratch_shapes=[pltpu.SemaphoreType.DMA(())],
        input_output_aliases={2: 0},
        compiler_params=pltpu.CompilerParams(
            dimension_semantics=("arbitrary",), vmem_limit_bytes=VMEM_LIMIT),
        name="dispatch",
    )(dest_tiles, hp, xs0)


def _swiglu(xb, wgu, wd):
    gu = jnp.dot(xb, wgu, preferred_element_type=_F32)
    g, u = gu[:, :EXPERT_FF], gu[:, EXPERT_FF:]
    mid = (g * jax.nn.sigmoid(g) * u).astype(_BF16)
    return jnp.dot(mid, wd, preferred_element_type=_F32)


def _load_packed_bf16(ref, rows):
    return jnp.concatenate([p.astype(_BF16) for p in _load_packed(ref, 0, rows)], axis=1)


def _expert_kernel(nblk_ref, bstart_ref, nused_ref, xs_ref, wg_ref, wu_ref, wd_ref, ys_ref,
                   xbuf, ybuf, xsem, ysem, wgu_s, wd_s, *, n_blocks):
    e = pl.program_id(0)
    nb = nblk_ref[e]
    b0 = bstart_ref[e]
    nused = nused_ref[0]

    def x_copy(blk):
        slot = lax.rem(blk, X_RING)
        return pltpu.make_async_copy(_token_rows(xs_ref, blk * T_M, T_M), _token_rows(xbuf, slot * T_M, T_M),
                                     xsem.at[slot])

    def y_copy(blk):
        slot = lax.rem(blk, Y_RING)
        return pltpu.make_async_copy(_token_rows(ybuf, slot * T_M, T_M), _token_rows(ys_ref, blk * T_M, T_M),
                                     ysem.at[slot])

    @pl.when(e == 0)
    def _():
        for g in range(X_RING - 1):
            @pl.when(g < nused)
            def _():
                x_copy(g).start()

    @pl.when(nb > 0)
    def _():
        wgu_s[:, :EXPERT_FF] = wg_ref[0].astype(_BF16)
        wgu_s[:, EXPERT_FF:] = wu_ref[0].astype(_BF16)
        wd_s[...] = wd_ref[0].astype(_BF16)

        def block(g, _):
            x_copy(g).wait()

            @pl.when(g + X_RING - 1 < nused)
            def _():
                x_copy(g + X_RING - 1).start()

            @pl.when(g >= Y_RING)
            def _():
                y_copy(g - Y_RING).wait()

            xb = jnp.concatenate(
                [p.astype(_BF16) for p in _load_packed(xbuf, lax.rem(g, X_RING) * T_M, T_M)], axis=1)
            _store_packed(ybuf, _swiglu(xb, wgu_s[...], wd_s[...]), lax.rem(g, Y_RING) * T_M)
            y_copy(g).start()
            return 0

        lax.fori_loop(b0, b0 + nb, block, 0)

    @pl.when(e == pl.num_programs(0) - 1)
    def _():
        for back in range(Y_RING, 0, -1):
            @pl.when(nused >= back)
            def _():
                y_copy(nused - back).wait()

        ybuf[pl.ds(0, T_M * PACK_ROWS), :] = jnp.zeros((T_M * PACK_ROWS, 128), _U32)

        def zero_copy(blk):
            return pltpu.make_async_copy(_token_rows(ybuf, 0, T_M), _token_rows(ys_ref, blk * T_M, T_M), ysem.at[0])

        def fill(blk, _):
            zero_copy(blk).start()
            return 0

        def drain(blk, _):
            zero_copy(blk).wait()
            return 0

        lax.fori_loop(nused, n_blocks, fill, 0)
        lax.fori_loop(nused, n_blocks, drain, 0)


def _experts(nblk, bstart, nused, xs, wg, wu, wd):
    n_slots = xs.shape[0] // PACK_ROWS
    n_exp, d, ff = wg.shape
    ring = lambda depth: pltpu.VMEM((depth * T_M * PACK_ROWS, 128), _U32)
    return pl.pallas_call(
        functools.partial(_expert_kernel, n_blocks=n_slots // T_M),
        out_shape=jax.ShapeDtypeStruct(xs.shape, _U32),
        grid_spec=pltpu.PrefetchScalarGridSpec(
            num_scalar_prefetch=3,
            grid=(n_exp,),
            in_specs=[pl.BlockSpec(memory_space=pl.ANY),
                      pl.BlockSpec((1, d, ff), lambda i, nbk, bst, nu: (i, 0, 0)),
                      pl.BlockSpec((1, d, ff), lambda i, nbk, bst, nu: (i, 0, 0)),
                      pl.BlockSpec((1, ff, d), lambda i, nbk, bst, nu: (i, 0, 0))],
            out_specs=pl.BlockSpec(memory_space=pl.ANY),
            scratch_shapes=[ring(X_RING), ring(Y_RING),
                            pltpu.SemaphoreType.DMA((X_RING,)), pltpu.SemaphoreType.DMA((Y_RING,)),
                            pltpu.VMEM((d, 2 * ff), _BF16), pltpu.VMEM((ff, d), _BF16)]),
        compiler_params=pltpu.CompilerParams(
            dimension_semantics=("arbitrary",), vmem_limit_bytes=VMEM_LIMIT),
        name="experts",
    )(nblk, bstart, nused, xs, wg, wu, wd)


def _combine_kernel(dest_ref, ys_ref, hp_ref, w_ref, x1_ref, mod_ref, wgus_ref, wds_ref, gpost_ref,
                    o_ref, gbuf, sem):
    t_rows = hp_ref.shape[0] // PACK_ROWS

    def issue(t, _):
        for kk in range(TOP_K):
            _row_copy(ys_ref, dest_ref[0, 0, t * TOP_K + kk], gbuf, kk * t_rows + t, sem).start(priority=kk % 2)
        return 0

    lax.fori_loop(0, t_rows, issue, 0)

    shared = _swiglu(_load_packed_bf16(hp_ref, t_rows), wgus_ref[...], wds_ref[...])

    def drain(t, _):
        for kk in range(TOP_K):
            _row_copy(ys_ref, 0, gbuf, 0, sem).wait()
        return 0

    lax.fori_loop(0, t_rows, drain, 0)

    w = w_ref[...]
    cols = None
    for kk in range(TOP_K):
        wk = w[:, kk:kk + 1]
        parts = [wk * p for p in _load_packed(gbuf, kk * t_rows, t_rows)]
        cols = parts if cols is None else [a + b for a, b in zip(cols, parts)]
    y = jnp.concatenate(cols, axis=1) + shared
    gate_f = mod_ref[0, 5:6, :]
    o_ref[...] = x1_ref[...] + gate_f * (y * _rsqrt_mean_sq(y, -1) * gpost_ref[...])


def _combine(dest_tiles, ys, hp, w_tok, x1, mod, wgus, wds, gpost, tiles_per_batch):
    n, d = x1.shape
    t = T_ROW
    full = lambda a: pl.BlockSpec(a.shape, lambda i: (0,) * a.ndim)
    return pl.pallas_call(
        _combine_kernel,
        out_shape=jax.ShapeDtypeStruct((n, d), _F32),
        grid=(n // t,),
        in_specs=[pl.BlockSpec((1, 1, t * TOP_K), lambda i: (i, 0, 0), memory_space=pltpu.SMEM),
                  pl.BlockSpec(memory_space=pl.ANY),
                  pl.BlockSpec((t * PACK_ROWS, 128), lambda i: (i, 0)),
                  pl.BlockSpec((t, TOP_K), lambda i: (i, 0)),
                  pl.BlockSpec((t, d), lambda i: (i, 0)),
                  pl.BlockSpec((1, N_MOD, d), lambda i: (i // tiles_per_batch, 0, 0)),
                  full(wgus), full(wds), full(gpost)],
        out_specs=pl.BlockSpec((t, d), lambda i: (i, 0)),
        scratch_shapes=[pltpu.VMEM((TOP_K * t * PACK_ROWS, 128), _U32), pltpu.SemaphoreType.DMA(())],
        compiler_params=pltpu.CompilerParams(
            dimension_semantics=("arbitrary",), vmem_limit_bytes=VMEM_LIMIT),
        name="combine",
    )(dest_tiles, ys, hp, w_tok, x1, mod, wgus, wds, gpost)


def _deinterleave(n):
    return np.concatenate([np.arange(0, n, 2), np.arange(1, n, 2)])


def _rope_tables(s, dim):
    rows = s // GRID_W
    row = jnp.broadcast_to(jnp.arange(rows, dtype=_F32)[:, None], (rows, GRID_W)).reshape(s)
    col = jnp.broadcast_to(jnp.arange(GRID_W, dtype=_F32)[None, :], (rows, GRID_W)).reshape(s)
    n_freq = dim // 4
    inv_freq = ROPE_THETA ** (-jnp.arange(n_freq, dtype=_F32) / n_freq)
    ang = jnp.concatenate([row[:, None] * inv_freq, col[:, None] * inv_freq], axis=-1)
    return jnp.cos(ang).T, jnp.sin(ang).T


def _in_proj_perm():
    perm = list(range(OFF_KROPE))
    perm += list(OFF_KROPE + _deinterleave(MLA_ROPE))
    for hd in range(GQA_HEADS):
        perm += list(OFF_QG + hd * GQA_HEAD_DIM + _deinterleave(GQA_HEAD_DIM))
    for hd in range(GQA_KV_HEADS):
        perm += list(OFF_KG + hd * GQA_HEAD_DIM + _deinterleave(GQA_HEAD_DIM))
    perm += list(range(OFF_VG, IN_COLS))
    return np.asarray(perm)


def _q_b_perm():
    per = MLA_NOPE + MLA_ROPE
    nope = [hd * per + j for hd in range(MLA_HEADS) for j in range(MLA_NOPE)]
    even = [hd * per + MLA_NOPE + 2 * i for hd in range(MLA_HEADS) for i in range(MLA_ROPE // 2)]
    odd = [hd * per + MLA_NOPE + 2 * i + 1 for hd in range(MLA_HEADS) for i in range(MLA_ROPE // 2)]
    return np.asarray(nope + even + odd)


def _kv_b_perm():
    per = MLA_NOPE + MLA_V
    kk = [hd * per + j for hd in range(MLA_HEADS) for j in range(MLA_NOPE)]
    vv = [hd * per + MLA_NOPE + j for hd in range(MLA_HEADS) for j in range(MLA_V)]
    return np.asarray(kk + vv)


def _layer(x, c, w_ada, b_ada, g_pre_mix, g_post_mix, g_pre_ffn, g_post_ffn, w_in, g_q_a, w_q_b,
           g_kv_a, w_kv_b, g_q_head, g_k_head, g_mla_out, g_gqa_out, w_out, w_router, b_router,
           w_gate_e, w_up_e, w_down_e, w_gate_s, w_up_s, w_down_s):
    b, s, d = x.shape
    n = b * s
    col = lambda g: g.reshape(-1, 1).astype(_F32)
    row = lambda g: g.reshape(1, -1).astype(_F32)

    mod = _ada_mod(c, w_ada, b_ada)

    winT = w_in[:, _in_proj_perm()].T.astype(_BF16)
    wqbT = w_q_b[:, _q_b_perm()].T.astype(_BF16)
    wkvbT = w_kv_b[:, _kv_b_perm()].T.astype(_BF16)
    perm_h = _deinterleave(GQA_HEAD_DIM)
    cosm, sinm = _rope_tables(s, MLA_ROPE)
    cosg, sing = _rope_tables(s, GQA_HEAD_DIM)

    qT, k, vT = _project(x, mod, row(g_pre_mix), winT, col(g_q_a), wqbT, col(g_kv_a), wkvbT,
                         col(g_q_head[perm_h]), col(g_k_head[perm_h]), cosm, sinm, cosg, sing)
    oT = _attention(qT, k, vT)

    wr_t = w_router.T.astype(_F32)
    wrh = wr_t.astype(_BF16)
    wrl = (wr_t - wrh.astype(_F32)).astype(_BF16)
    x1, hp, idx_t, w_t, rank_t, cnt = _post_attention(
        x, oT, mod, col(g_mla_out), col(g_gqa_out), w_out.T.astype(_BF16), row(g_post_mix),
        row(g_pre_ffn), wrh, wrl, col(b_router))

    counts = cnt[:, 0].astype(jnp.int32)
    blocks_e = (counts + T_M - 1) // T_M
    blk_end = jnp.cumsum(blocks_e)
    start = (blk_end - blocks_e) * T_M
    n_blocks = n * TOP_K // T_M + N_EXPERTS
    nused = blk_end[-1:].astype(jnp.int32)
    bstart = (blk_end - blocks_e).astype(jnp.int32)
    dest = _slots(idx_t, rank_t, start)
    dest_tiles = dest.T.reshape(n // T_ROW, 1, T_ROW * TOP_K)

    xs = _dispatch(dest_tiles, hp, n_blocks * T_M)
    ys = _experts(blocks_e.astype(jnp.int32), bstart, nused, xs, w_gate_e, w_up_e, w_down_e)
    wgus = jnp.concatenate([w_gate_s, w_up_s], axis=1).astype(_BF16)
    out = _combine(dest_tiles, ys, hp, w_t.T, x1.reshape(n, d), mod, wgus, w_down_s.astype(_BF16),
                   row(g_post_ffn), s // T_ROW)
    return out.reshape(b, s, d)


def kernel(x, c, w_ada, b_ada, g_pre_mix, g_post_mix, g_pre_ffn, g_post_ffn, w_in, g_q_a, w_q_b, g_kv_a, w_kv_b, g_q_head, g_k_head, g_mla_out, g_gqa_out, w_out, w_router, b_router, w_gate_e, w_up_e, w_down_e, w_gate_s, w_up_s, w_down_s):
    depth = w_ada.shape[0]
    for l in range(depth):
        x = _layer(x, c, w_ada[l], b_ada[l], g_pre_mix[l], g_post_mix[l], g_pre_ffn[l], g_post_ffn[l],
                   w_in[l], g_q_a[l], w_q_b[l], g_kv_a[l], w_kv_b[l], g_q_head[l], g_k_head[l],
                   g_mla_out[l], g_gqa_out[l], w_out[l], w_router[l], b_router[l], w_gate_e[l],
                   w_up_e[l], w_down_e[l], w_gate_s[l], w_up_s[l], w_down_s[l])
    return x
```

```python
import functools

import jax
import jax.numpy as jnp
import numpy as np
from jax import lax
from jax.experimental import pallas as pl
from jax.experimental.pallas import tpu as pltpu

D_MODEL = 1024
GRID_W = 64
ROPE_THETA = 10000.0
NORM_EPS = 1e-6

MLA_HEADS = 8
MLA_Q_LORA = 256
MLA_KV_LORA = 128
MLA_NOPE = 64
MLA_ROPE = 32
MLA_V = 64

GQA_HEADS = 8
GQA_KV_HEADS = 2
GQA_HEAD_DIM = 64
GQA_GROUP = GQA_HEADS // GQA_KV_HEADS

N_HEADS = MLA_HEADS + GQA_HEADS
N_KV = MLA_HEADS + GQA_KV_HEADS
HEAD_PAD = 128
V_DIM = 64

N_EXPERTS = 256
TOP_K = 8
N_GROUPS = 8
GROUP_SIZE = N_EXPERTS // N_GROUPS
TOPK_GROUPS = 4
EXPERT_FF = 256
ROUTED_SCALE = 2.5
N_MOD = 6

OFF_QLAT = 0
OFF_KVLAT = OFF_QLAT + MLA_Q_LORA
OFF_KROPE = OFF_KVLAT + MLA_KV_LORA
OFF_QG = OFF_KROPE + MLA_ROPE
OFF_KG = OFF_QG + GQA_HEADS * GQA_HEAD_DIM
OFF_VG = OFF_KG + GQA_KV_HEADS * GQA_HEAD_DIM
IN_COLS = OFF_VG + GQA_KV_HEADS * GQA_HEAD_DIM

T_PROJ = 512
T_Q = 512
T_POST = 512
T_ROW = 256
T_M = 256
PACK_ROWS = 4
N_SCORE_SLOTS = 4
X_RING = 4
Y_RING = 3
LOG2_E = 1.4426950408889634
VMEM_LIMIT = 56 * 1024 * 1024

_NT = (((1,), (1,)), ((), ()))
_F32 = jnp.float32
_BF16 = jnp.bfloat16
_U32 = jnp.uint32
_HI_MASK = np.uint32(0xFFFF0000)


def _rsqrt_mean_sq(v, axis):
    return lax.rsqrt(jnp.mean(v * v, axis=axis, keepdims=True) + NORM_EPS)


def _pack_pair(lo, hi):
    lo_b = pltpu.bitcast(lo.astype(_BF16).astype(_F32), _U32) >> 16
    hi_b = pltpu.bitcast(hi.astype(_BF16).astype(_F32), _U32) & _HI_MASK
    return lo_b | hi_b


def _unpack_pair(u):
    return pltpu.bitcast(u << 16, _F32), pltpu.bitcast(u & _HI_MASK, _F32)


def _token_rows(ref, first, count=1):
    return ref.at[pl.ds(pl.multiple_of(first * PACK_ROWS, PACK_ROWS), count * PACK_ROWS), :]


def _chunk_of_tokens(first, count, c):
    return pl.ds(first * PACK_ROWS + c, count, stride=PACK_ROWS)


def _store_packed(ref, val, first=0):
    rows = val.shape[0]
    for c in range(PACK_ROWS):
        ref[_chunk_of_tokens(first, rows, c), :] = _pack_pair(
            val[:, 256 * c:256 * c + 128], val[:, 256 * c + 128:256 * c + 256])


def _load_packed(ref, first, rows):
    parts = []
    for c in range(PACK_ROWS):
        parts += list(_unpack_pair(ref[_chunk_of_tokens(first, rows, c), :]))
    return parts


def _ada_kernel(c_ref, w_ref, b_ref, o_ref):
    c = c_ref[...]
    a = (c * jax.nn.sigmoid(c)).astype(_BF16)
    o_ref[...] = jnp.dot(a, w_ref[...].astype(_BF16), preferred_element_type=_F32) + b_ref[...]


def _ada_mod(c, w_ada, b_ada):
    b, d = c.shape
    cols = w_ada.shape[1]
    rows = 8
    tn = 1536
    c_pad = jnp.zeros((rows, d), _F32).at[:b].set(c)
    out = pl.pallas_call(
        _ada_kernel,
        out_shape=jax.ShapeDtypeStruct((rows, cols), _F32),
        grid=(cols // tn,),
        in_specs=[pl.BlockSpec((rows, d), lambda j: (0, 0)),
                  pl.BlockSpec((d, tn), lambda j: (0, j)),
                  pl.BlockSpec((1, tn), lambda j: (0, j))],
        out_specs=pl.BlockSpec((rows, tn), lambda j: (0, j)),
        compiler_params=pltpu.CompilerParams(vmem_limit_bytes=VMEM_LIMIT),
        name="ada_mod",
    )(c_pad, w_ada, b_ada.reshape(1, cols))
    return out[:b].reshape(b, N_MOD, d)


def _rope(x0, x1, cos, sin):
    return x0 * cos - x1 * sin, x0 * sin + x1 * cos


def _proj_kernel(x_ref, mod_ref, gpre_ref, win_ref, gqa_ref, wqb_ref, gkva_ref, wkvb_ref,
                 gqh_ref, gkh_ref, cosm_ref, sinm_ref, cosg_ref, sing_ref,
                 qT_ref, k_ref, vT_ref):
    x = x_ref[0]
    t = x.shape[0]
    shift = mod_ref[0, 0:1, :]
    scale = mod_ref[0, 1:2, :]
    h = x * _rsqrt_mean_sq(x, -1) * gpre_ref[...] * (1.0 + scale) + shift
    pT = lax.dot_general(win_ref[...], h.astype(_BF16), _NT, preferred_element_type=_F32)

    cm, sm = cosm_ref[...], sinm_ref[...]
    cg, sg = cosg_ref[...], sing_ref[...]
    half_m = MLA_ROPE // 2
    half_g = GQA_HEAD_DIM // 2

    ql = pT[OFF_QLAT:OFF_QLAT + MLA_Q_LORA]
    qn = (ql * _rsqrt_mean_sq(ql, 0) * gqa_ref[...]).astype(_BF16)
    qT = jnp.dot(wqb_ref[...], qn, preferred_element_type=_F32)
    sc_m = float((MLA_NOPE + MLA_ROPE) ** -0.5 * LOG2_E)
    n_nope = MLA_HEADS * MLA_NOPE
    n_half = MLA_HEADS * half_m
    q0, q1 = _rope(qT[n_nope:n_nope + n_half], qT[n_nope + n_half:],
                   jnp.tile(cm, (MLA_HEADS, 1)), jnp.tile(sm, (MLA_HEADS, 1)))
    zq_m = jnp.zeros((HEAD_PAD - MLA_NOPE - MLA_ROPE, t), _BF16)
    for hd in range(MLA_HEADS):
        qT_ref[0, hd, 0:MLA_NOPE, :] = (qT[hd * MLA_NOPE:(hd + 1) * MLA_NOPE] * sc_m).astype(_BF16)
        qT_ref[0, hd, MLA_NOPE:MLA_NOPE + half_m, :] = (q0[hd * half_m:(hd + 1) * half_m] * sc_m).astype(_BF16)
        qT_ref[0, hd, MLA_NOPE + half_m:MLA_NOPE + MLA_ROPE, :] = (
            q1[hd * half_m:(hd + 1) * half_m] * sc_m).astype(_BF16)
        qT_ref[0, hd, MLA_NOPE + MLA_ROPE:, :] = zq_m

    kl = pT[OFF_KVLAT:OFF_KVLAT + MLA_KV_LORA]
    kn = (kl * _rsqrt_mean_sq(kl, 0) * gkva_ref[...]).astype(_BF16)
    kvT = jnp.dot(wkvb_ref[...], kn, preferred_element_type=_F32)
    k0, k1 = _rope(pT[OFF_KROPE:OFF_KROPE + half_m], pT[OFF_KROPE + half_m:OFF_KROPE + MLA_ROPE], cm, sm)
    zk_m = jnp.zeros((HEAD_PAD - MLA_NOPE - MLA_ROPE, t), _F32)
    for hd in range(MLA_HEADS):
        kTh = jnp.concatenate([kvT[hd * MLA_NOPE:(hd + 1) * MLA_NOPE], k0, k1, zk_m], axis=0)
        k_ref[0, hd] = kTh.T.astype(_BF16)
        vT_ref[0, hd, 0] = kvT[n_nope + hd * MLA_V:n_nope + (hd + 1) * MLA_V].astype(_BF16)

    sc_g = float(GQA_HEAD_DIM ** -0.5 * LOG2_E)
    zq_g = jnp.zeros((HEAD_PAD - GQA_HEAD_DIM, t), _BF16)
    for hd in range(GQA_HEADS):
        blk = pT[OFF_QG + hd * GQA_HEAD_DIM:OFF_QG + (hd + 1) * GQA_HEAD_DIM]
        bn = blk * _rsqrt_mean_sq(blk, 0) * gqh_ref[...]
        g0, g1 = _rope(bn[:half_g], bn[half_g:], cg, sg)
        qT_ref[0, MLA_HEADS + hd, 0:half_g, :] = (g0 * sc_g).astype(_BF16)
        qT_ref[0, MLA_HEADS + hd, half_g:GQA_HEAD_DIM, :] = (g1 * sc_g).astype(_BF16)
        qT_ref[0, MLA_HEADS + hd, GQA_HEAD_DIM:, :] = zq_g

    zk_g = jnp.zeros((HEAD_PAD - GQA_HEAD_DIM, t), _F32)
    for hd in range(GQA_KV_HEADS):
        blk = pT[OFF_KG + hd * GQA_HEAD_DIM:OFF_KG + (hd + 1) * GQA_HEAD_DIM]
        bn = blk * _rsqrt_mean_sq(blk, 0) * gkh_ref[...]
        g0, g1 = _rope(bn[:half_g], bn[half_g:], cg, sg)
        kTh = jnp.concatenate([g0, g1, zk_g], axis=0)
        k_ref[0, MLA_HEADS + hd] = kTh.T.astype(_BF16)
        vT_ref[0, MLA_HEADS + hd, 0] = pT[OFF_VG + hd * V_DIM:OFF_VG + (hd + 1) * V_DIM].astype(_BF16)


def _project(x, mod, gpre, winT, gqa, wqbT, gkva, wkvbT, gqh, gkh, cosm, sinm, cosg, sing):
    b, s, d = x.shape
    t = T_PROJ
    nt = s // t
    full = lambda a: pl.BlockSpec(a.shape, lambda bi, i: (0,) * a.ndim)
    tab = lambda a: pl.BlockSpec((a.shape[0], t), lambda bi, i: (0, i))
    return pl.pallas_call(
        _proj_kernel,
        out_shape=(jax.ShapeDtypeStruct((b, N_HEADS, HEAD_PAD, s), _BF16),
                   jax.ShapeDtypeStruct((b, N_KV, s, HEAD_PAD), _BF16),
                   jax.ShapeDtypeStruct((b, N_KV, nt, V_DIM, t), _BF16)),
        grid=(b, nt),
        in_specs=[pl.BlockSpec((1, t, d), lambda bi, i: (bi, i, 0)),
                  pl.BlockSpec((1, N_MOD, d), lambda bi, i: (bi, 0, 0)),
                  full(gpre), full(winT), full(gqa), full(wqbT), full(gkva), full(wkvbT),
                  full(gqh), full(gkh), tab(cosm), tab(sinm), tab(cosg), tab(sing)],
        out_specs=(pl.BlockSpec((1, N_HEADS, HEAD_PAD, t), lambda bi, i: (bi, 0, 0, i)),
                   pl.BlockSpec((1, N_KV, t, HEAD_PAD), lambda bi, i: (bi, 0, i, 0)),
                   pl.BlockSpec((1, N_KV, 1, V_DIM, t), lambda bi, i: (bi, 0, i, 0, 0))),
        compiler_params=pltpu.CompilerParams(
            dimension_semantics=("arbitrary", "arbitrary"), vmem_limit_bytes=VMEM_LIMIT),
        name="in_proj",
    )(x, mod, gpre, winT, gqa, wqbT, gkva, wkvbT, gqh, gkh, cosm, sinm, cosg, sing)


def _attn_kernel(qT_ref, k_ref, vT_ref, o_ref, *slots, n_chunks, tk):
    qT = qT_ref[0, 0]
    tq = qT.shape[1]
    ns = len(slots)

    def scores(j, slot):
        kc = k_ref[0, 0, pl.ds(pl.multiple_of(j * tk, tk), tk), :]
        s = jnp.dot(kc, qT, preferred_element_type=_F32)
        slots[slot][...] = s
        return jnp.max(s, axis=0, keepdims=True)

    def accumulate(j, slot, mc, carry):
        m, l, acc = carry
        m_new = jnp.maximum(m, mc)
        alpha = jnp.exp2(m - m_new)
        p = jnp.exp2(slots[slot][...] - m_new)
        l = alpha * l + jnp.sum(p, axis=0, keepdims=True)
        acc = alpha * acc + jnp.dot(vT_ref[0, 0, j], p.astype(_BF16), preferred_element_type=_F32)
        return m_new, l, acc

    def group(jj, state):
        mcs, carry = state
        mcs = list(mcs)
        j = ns * jj
        for u in range(ns):
            nxt = scores(j + u + 2, (u + 2) % ns)
            carry = accumulate(j + u, u, mcs[u], carry)
            mcs[(u + 2) % ns] = nxt
        return tuple(mcs), carry

    carry = (jnp.full((1, tq), -1e30, _F32), jnp.zeros((1, tq), _F32), jnp.zeros((V_DIM, tq), _F32))
    zero = jnp.zeros((1, tq), _F32)
    mcs = (scores(0, 0), scores(1, 1)) + (zero,) * (ns - 2)
    n_loop = (n_chunks - 2) // ns
    mcs, carry = lax.fori_loop(0, n_loop, group, (mcs, carry))
    mcs = list(mcs)
    for j in range(n_loop * ns, n_chunks):
        if j + 2 < n_chunks:
            mcs[(j + 2) % ns] = scores(j + 2, (j + 2) % ns)
        carry = accumulate(j, j % ns, mcs[j % ns], carry)
    _, l, acc = carry
    o_ref[0, 0] = acc / l


def _kv_index(h):
    return jnp.where(h < MLA_HEADS, h, MLA_HEADS + (h - MLA_HEADS) // GQA_GROUP)


def _attention(qT, k, vT):
    b, nh, dp, s = qT.shape
    n_chunks, tk = vT.shape[2], vT.shape[4]
    assert n_chunks > N_SCORE_SLOTS
    return pl.pallas_call(
        functools.partial(_attn_kernel, n_chunks=n_chunks, tk=tk),
        out_shape=jax.ShapeDtypeStruct((b, nh, V_DIM, s), _F32),
        grid=(b, nh, s // T_Q),
        in_specs=[pl.BlockSpec((1, 1, dp, T_Q), lambda bi, h, i: (bi, h, 0, i)),
                  pl.BlockSpec((1, 1, s, dp), lambda bi, h, i: (bi, _kv_index(h), 0, 0)),
                  pl.BlockSpec((1, 1, n_chunks, V_DIM, tk), lambda bi, h, i: (bi, _kv_index(h), 0, 0, 0))],
        out_specs=pl.BlockSpec((1, 1, V_DIM, T_Q), lambda bi, h, i: (bi, h, 0, i)),
        scratch_shapes=[pltpu.VMEM((tk, T_Q), _F32) for _ in range(N_SCORE_SLOTS)],
        compiler_params=pltpu.CompilerParams(
            dimension_semantics=("arbitrary", "arbitrary", "arbitrary"), vmem_limit_bytes=VMEM_LIMIT),
        name="attention",
    )(qT, k, vT)


def _first_argmax(v, rows, n):
    m = jnp.max(v, axis=0, keepdims=True)
    idx = jnp.min(jnp.where(v == m, rows, n), axis=0, keepdims=True)
    return m, idx


def _post_kernel(x_ref, o_ref, mod_ref, gmla_ref, ggqa_ref, wout_ref, gpost_ref, gffn_ref,
                 wrh_ref, wrl_ref, br_ref,
                 x1_ref, hp_ref, idx_ref, w_ref, rank_ref, cnt_ref, carry_ref):
    first = jnp.logical_and(pl.program_id(0) == 0, pl.program_id(1) == 0)

    @pl.when(first)
    def _():
        carry_ref[...] = jnp.zeros_like(carry_ref)

    x = x_ref[0]
    t = x.shape[0]
    gate_m = mod_ref[0, 2:3, :]
    shift_f = mod_ref[0, 3:4, :]
    scale_f = mod_ref[0, 4:5, :]

    o = o_ref[0].reshape(N_HEADS * V_DIM, t)
    half = MLA_HEADS * V_DIM
    om, og = o[:half], o[half:]
    mixedT = jnp.concatenate([om * _rsqrt_mean_sq(om, 0) * gmla_ref[...],
                              og * _rsqrt_mean_sq(og, 0) * ggqa_ref[...]], axis=0).astype(_BF16)
    y = jnp.dot(wout_ref[...], mixedT, preferred_element_type=_F32).T
    x1 = x + gate_m * (y * _rsqrt_mean_sq(y, -1) * gpost_ref[...])
    x1_ref[0] = x1
    h2 = x1 * _rsqrt_mean_sq(x1, -1) * gffn_ref[...] * (1.0 + scale_f) + shift_f

    _store_packed(hp_ref, h2)

    hh = h2.astype(_BF16)
    hl = (h2 - hh.astype(_F32)).astype(_BF16)
    wrh = wrh_ref[...]
    logits = (lax.dot_general(wrh, hh, _NT, preferred_element_type=_F32)
              + lax.dot_general(wrh, hl, _NT, preferred_element_type=_F32)
              + lax.dot_general(wrl_ref[...], hh, _NT, preferred_element_type=_F32))
    scores = jax.nn.sigmoid(logits)
    biased = scores + br_ref[...]
    neg = -jnp.inf

    rows_g = lax.broadcasted_iota(jnp.int32, (GROUP_SIZE, t), 0)
    gscores = []
    for g in range(N_GROUPS):
        v = biased[g * GROUP_SIZE:(g + 1) * GROUP_SIZE]
        m1, i1 = _first_argmax(v, rows_g, GROUP_SIZE)
        m2 = jnp.max(jnp.where(rows_g == i1, neg, v), axis=0, keepdims=True)
        gscores.append(m1 + m2)
    cur = jnp.concatenate(gscores, axis=0)
    rows_8 = lax.broadcasted_iota(jnp.int32, (N_GROUPS, t), 0)
    gsel = jnp.zeros((N_GROUPS, t), _F32)
    for _ in range(TOPK_GROUPS):
        _, gi = _first_argmax(cur, rows_8, N_GROUPS)
        hit = rows_8 == gi
        gsel = jnp.where(hit, 1.0, gsel)
        cur = jnp.where(hit, neg, cur)
    emask = jnp.concatenate(
        [jnp.broadcast_to(gsel[g:g + 1], (GROUP_SIZE, t)) for g in range(N_GROUPS)], axis=0) > 0.5
    masked = jnp.where(emask, biased, neg)

    rows_e = lax.broadcasted_iota(jnp.int32, (N_EXPERTS, t), 0)
    idxs, ws = [], []
    for _ in range(TOP_K):
        _, ei = _first_argmax(masked, rows_e, N_EXPERTS)
        hit = rows_e == ei
        idxs.append(ei)
        ws.append(jnp.sum(jnp.where(hit, scores, 0.0), axis=0, keepdims=True))
        masked = jnp.where(hit, neg, masked)
    wsum = ws[0]
    for wk in ws[1:]:
        wsum = wsum + wk

    onehot = jnp.zeros((N_EXPERTS, t), _F32)
    for ei in idxs:
        onehot = onehot + jnp.where(rows_e == ei, 1.0, 0.0)
    upper = (lax.broadcasted_iota(jnp.int32, (t, t), 0) < lax.broadcasted_iota(jnp.int32, (t, t), 1))
    before = jnp.dot(onehot.astype(_BF16), jnp.where(upper, 1.0, 0.0).astype(_BF16),
                     preferred_element_type=_F32) + carry_ref[...]
    for kk in range(TOP_K):
        idx_ref[kk:kk + 1, :] = idxs[kk]
        w_ref[kk:kk + 1, :] = ws[kk] / wsum * ROUTED_SCALE
        rank_ref[kk:kk + 1, :] = jnp.sum(
            jnp.where(rows_e == idxs[kk], before, 0.0), axis=0, keepdims=True).astype(jnp.int32)
    total = carry_ref[...] + jnp.sum(onehot, axis=1, keepdims=True)
    carry_ref[...] = total
    cnt_ref[...] = jnp.broadcast_to(total, cnt_ref.shape)


def _post_attention(x, oT, mod, gmla, ggqa, woutT, gpost, gffn, wrh, wrl, br):
    b, s, d = x.shape
    n = b * s
    t = T_POST
    nt = s // t
    full = lambda a: pl.BlockSpec(a.shape, lambda bi, i: (0,) * a.ndim)
    tok = lambda bi, i: (0, bi * nt + i)
    return pl.pallas_call(
        _post_kernel,
        out_shape=(jax.ShapeDtypeStruct((b, s, d), _F32),
                   jax.ShapeDtypeStruct((n * PACK_ROWS, 128), _U32),
                   jax.ShapeDtypeStruct((TOP_K, n), jnp.int32),
                   jax.ShapeDtypeStruct((TOP_K, n), _F32),
                   jax.ShapeDtypeStruct((TOP_K, n), jnp.int32),
                   jax.ShapeDtypeStruct((N_EXPERTS, 128), _F32)),
        grid=(b, nt),
        in_specs=[pl.BlockSpec((1, t, d), lambda bi, i: (bi, i, 0)),
                  pl.BlockSpec((1, N_HEADS, V_DIM, t), lambda bi, i: (bi, 0, 0, i)),
                  pl.BlockSpec((1, N_MOD, d), lambda bi, i: (bi, 0, 0)),
                  full(gmla), full(ggqa), full(woutT), full(gpost), full(gffn),
                  full(wrh), full(wrl), full(br)],
        out_specs=(pl.BlockSpec((1, t, d), lambda bi, i: (bi, i, 0)),
                   pl.BlockSpec((t * PACK_ROWS, 128), lambda bi, i: (bi * nt + i, 0)),
                   pl.BlockSpec((TOP_K, t), tok),
                   pl.BlockSpec((TOP_K, t), tok),
                   pl.BlockSpec((TOP_K, t), tok),
                   pl.BlockSpec((N_EXPERTS, 128), lambda bi, i: (0, 0))),
        scratch_shapes=[pltpu.VMEM((N_EXPERTS, 1), _F32)],
        compiler_params=pltpu.CompilerParams(
            dimension_semantics=("arbitrary", "arbitrary"), vmem_limit_bytes=VMEM_LIMIT),
        name="out_proj_router",
    )(x, oT, mod, gmla, ggqa, woutT, gpost, gffn, wrh, wrl, br)


def _slot_kernel(idx_ref, rank_ref, start_ref, dest_ref):
    t = idx_ref.shape[1]
    rows_e = lax.broadcasted_iota(jnp.int32, (N_EXPERTS, t), 0)
    start = start_ref[...]
    for kk in range(TOP_K):
        base = jnp.sum(jnp.where(rows_e == idx_ref[kk:kk + 1, :], start, 0.0), axis=0, keepdims=True)
        dest_ref[kk:kk + 1, :] = rank_ref[kk:kk + 1, :] + base.astype(jnp.int32)


def _slots(idx_t, rank_t, start):
    n = idx_t.shape[1]
    t = T_POST
    return pl.pallas_call(
        _slot_kernel,
        out_shape=jax.ShapeDtypeStruct((TOP_K, n), jnp.int32),
        grid=(n // t,),
        in_specs=[pl.BlockSpec((TOP_K, t), lambda i: (0, i)),
                  pl.BlockSpec((TOP_K, t), lambda i: (0, i)),
                  pl.BlockSpec((N_EXPERTS, 1), lambda i: (0, 0))],
        out_specs=pl.BlockSpec((TOP_K, t), lambda i: (0, i)),
        compiler_params=pltpu.CompilerParams(dimension_semantics=("arbitrary",)),
        name="slot_index",
    )(idx_t, rank_t, start.astype(_F32).reshape(N_EXPERTS, 1))


def _row_copy(src, src_row, dst, dst_row, sem):
    return pltpu.make_async_copy(_token_rows(src, src_row), _token_rows(dst, dst_row), sem)


def _dispatch_kernel(pad_start_ref, pad_len_ref, nused_ref, dest_ref, h_ref, xs_ref, zbuf, sem, zsem, *,
                     n_blocks):
    t_rows = h_ref.shape[0] // PACK_ROWS

    def issue(t, _):
        for kk in range(TOP_K):
            _row_copy(h_ref, t, xs_ref, dest_ref[0, 0, t * TOP_K + kk], sem).start(priority=kk % 2)
        return 0

    lax.fori_loop(0, t_rows, issue, 0)

    def drain(t, _):
        for kk in range(TOP_K):
            _row_copy(h_ref, 0, xs_ref, 0, sem).wait()
        return 0

    lax.fori_loop(0, t_rows, drain, 0)

    @pl.when(pl.program_id(0) == pl.num_programs(0) - 1)
    def _():
        zbuf[...] = jnp.zeros_like(zbuf)

        def zero_copy(first, count):
            return pltpu.make_async_copy(_token_rows(zbuf, 0, count), _token_rows(xs_ref, first, count), zsem)

        def sweep(act):
            def expert_pad(e, _):
                first = pad_start_ref[e]
                for piece in [T_M >> sh for sh in range(1, T_M.bit_length())]:
                    has = pad_len_ref[e] & piece

                    @pl.when(has != 0)
                    def _():
                        act(zero_copy(first, piece))

                    first = first + has
                return 0

            lax.fori_loop(0, N_EXPERTS, expert_pad, 0)

            def spare_block(blk, _):
                act(zero_copy(blk * T_M, T_M))
                return 0

            lax.fori_loop(nused_ref[0], n_blocks, spare_block, 0)

        sweep(lambda c: c.start())
        sweep(lambda c: c.wait())


def _dispatch(pad_start, pad_len, nused, dest_tiles, hp, n_slots):
    n = hp.shape[0] // PACK_ROWS
    t = T_ROW
    return pl.pallas_call(
        functools.partial(_dispatch_kernel, n_blocks=n_slots // T_M),
        out_shape=jax.ShapeDtypeStruct((n_slots * PACK_ROWS, 128), _U32),
        grid_spec=pltpu.PrefetchScalarGridSpec(
            num_scalar_prefetch=3,
            grid=(n // t,),
            in_specs=[pl.BlockSpec((1, 1, t * TOP_K), lambda i, ps, pn, nu: (i, 0, 0), memory_space=pltpu.SMEM),
                      pl.BlockSpec((t * PACK_ROWS, 128), lambda i, ps, pn, nu: (i, 0))],
            out_specs=pl.BlockSpec(memory_space=pl.ANY),
            scratch_shapes=[pltpu.VMEM((T_M * PACK_ROWS, 128), _U32),
                            pltpu.SemaphoreType.DMA(()), pltpu.SemaphoreType.DMA(())]),
        compiler_params=pltpu.CompilerParams(
            dimension_semantics=("arbitrary",), vmem_limit_bytes=VMEM_LIMIT),
        name="dispatch",
    )(pad_start, pad_len, nused, dest_tiles, hp)


def _swiglu(xb, wgu, wd):
    gu = jnp.dot(xb, wgu, preferred_element_type=_F32)
    g, u = gu[:, :EXPERT_FF], gu[:, EXPERT_FF:]
    mid = (g * jax.nn.sigmoid(g) * u).astype(_BF16)
    return jnp.dot(mid, wd, preferred_element_type=_F32)


def _load_packed_bf16(ref, rows):
    return jnp.concatenate([p.astype(_BF16) for p in _load_packed(ref, 0, rows)], axis=1)


def _expert_kernel(nblk_ref, bstart_ref, nused_ref, xs_ref, wg_ref, wu_ref, wd_ref, ys_ref,
                   xbuf, ybuf, xsem, ysem, wgu_s, wd_s, *, n_blocks):
    e = pl.program_id(0)
    nb = nblk_ref[e]
    b0 = bstart_ref[e]
    nused = nused_ref[0]

    def x_copy(blk):
        slot = lax.rem(blk, X_RING)
        return pltpu.make_async_copy(_token_rows(xs_ref, blk * T_M, T_M), _token_rows(xbuf, slot * T_M, T_M),
                                     xsem.at[slot])

    def y_copy(blk):
        slot = lax.rem(blk, Y_RING)
        return pltpu.make_async_copy(_token_rows(ybuf, slot * T_M, T_M), _token_rows(ys_ref, blk * T_M, T_M),
                                     ysem.at[slot])

    @pl.when(e == 0)
    def _():
        for g in range(X_RING - 1):
            @pl.when(g < nused)
            def _():
                x_copy(g).start()

    @pl.when(nb > 0)
    def _():
        wgu_s[:, :EXPERT_FF] = wg_ref[0].astype(_BF16)
        wgu_s[:, EXPERT_FF:] = wu_ref[0].astype(_BF16)
        wd_s[...] = wd_ref[0].astype(_BF16)

        def block(g, _):
            x_copy(g).wait()

            @pl.when(g + X_RING - 1 < nused)
            def _():
                x_copy(g + X_RING - 1).start()

            @pl.when(g >= Y_RING)
            def _():
                y_copy(g - Y_RING).wait()

            xb = jnp.concatenate(
                [p.astype(_BF16) for p in _load_packed(xbuf, lax.rem(g, X_RING) * T_M, T_M)], axis=1)
            _store_packed(ybuf, _swiglu(xb, wgu_s[...], wd_s[...]), lax.rem(g, Y_RING) * T_M)
            y_copy(g).start()
            return 0

        lax.fori_loop(b0, b0 + nb, block, 0)

    @pl.when(e == pl.num_programs(0) - 1)
    def _():
        for back in range(Y_RING, 0, -1):
            @pl.when(nused >= back)
            def _():
                y_copy(nused - back).wait()

        ybuf[pl.ds(0, T_M * PACK_ROWS), :] = jnp.zeros((T_M * PACK_ROWS, 128), _U32)

        def zero_copy(blk):
            return pltpu.make_async_copy(_token_rows(ybuf, 0, T_M), _token_rows(ys_ref, blk * T_M, T_M), ysem.at[0])

        def fill(blk, _):
            zero_copy(blk).start()
            return 0

        def drain(blk, _):
            zero_copy(blk).wait()
            return 0

        lax.fori_loop(nused, n_blocks, fill, 0)
        lax.fori_loop(nused, n_blocks, drain, 0)


def _experts(nblk, bstart, nused, xs, wg, wu, wd):
    n_slots = xs.shape[0] // PACK_ROWS
    n_exp, d, ff = wg.shape
    ring = lambda depth: pltpu.VMEM((depth * T_M * PACK_ROWS, 128), _U32)
    return pl.pallas_call(
        functools.partial(_expert_kernel, n_blocks=n_slots // T_M),
        out_shape=jax.ShapeDtypeStruct(xs.shape, _U32),
        grid_spec=pltpu.PrefetchScalarGridSpec(
            num_scalar_prefetch=3,
            grid=(n_exp,),
            in_specs=[pl.BlockSpec(memory_space=pl.ANY),
                      pl.BlockSpec((1, d, ff), lambda i, nbk, bst, nu: (i, 0, 0)),
                      pl.BlockSpec((1, d, ff), lambda i, nbk, bst, nu: (i, 0, 0)),
                      pl.BlockSpec((1, ff, d), lambda i, nbk, bst, nu: (i, 0, 0))],
            out_specs=pl.BlockSpec(memory_space=pl.ANY),
            scratch_shapes=[ring(X_RING), ring(Y_RING),
                            pltpu.SemaphoreType.DMA((X_RING,)), pltpu.SemaphoreType.DMA((Y_RING,)),
                            pltpu.VMEM((d, 2 * ff), _BF16), pltpu.VMEM((ff, d), _BF16)]),
        compiler_params=pltpu.CompilerParams(
            dimension_semantics=("arbitrary",), vmem_limit_bytes=VMEM_LIMIT),
        name="experts",
    )(nblk, bstart, nused, xs, wg, wu, wd)


def _combine_kernel(dest_ref, dest_next_ref, ys_ref, hp_ref, w_ref, x1_ref, mod_ref, wgus_ref, wds_ref,
                    gpost_ref, o_ref, gbuf, sem):
    i = pl.program_id(0)
    t_rows = hp_ref.shape[0] // PACK_ROWS
    tile_rows = TOP_K * t_rows

    def gather(d_ref, slot):
        def issue(t, _):
            for kk in range(TOP_K):
                _row_copy(ys_ref, d_ref[0, 0, t * TOP_K + kk], gbuf, slot * tile_rows + kk * t_rows + t,
                          sem.at[slot]).start(priority=kk % 2)
            return 0

        lax.fori_loop(0, t_rows, issue, 0)

    slot = lax.rem(i, 2)

    @pl.when(i == 0)
    def _():
        gather(dest_ref, 0)

    @pl.when(i + 1 < pl.num_programs(0))
    def _():
        gather(dest_next_ref, 1 - slot)

    shared = _swiglu(_load_packed_bf16(hp_ref, t_rows), wgus_ref[...], wds_ref[...])

    def drain(t, _):
        for kk in range(TOP_K):
            _row_copy(ys_ref, 0, gbuf, 0, sem.at[slot]).wait()
        return 0

    lax.fori_loop(0, t_rows, drain, 0)

    w = w_ref[...]
    cols = None
    for kk in range(TOP_K):
        wk = w[:, kk:kk + 1]
        parts = [wk * p for p in _load_packed(gbuf, slot * tile_rows + kk * t_rows, t_rows)]
        cols = parts if cols is None else [a + b for a, b in zip(cols, parts)]
    y = jnp.concatenate(cols, axis=1) + shared
    gate_f = mod_ref[0, 5:6, :]
    o_ref[...] = x1_ref[...] + gate_f * (y * _rsqrt_mean_sq(y, -1) * gpost_ref[...])


def _combine(dest_tiles, ys, hp, w_tok, x1, mod, wgus, wds, gpost, tiles_per_batch):
    n, d = x1.shape
    t = T_ROW
    nt = n // t
    full = lambda a: pl.BlockSpec(a.shape, lambda i: (0,) * a.ndim)
    dest_spec = lambda shift: pl.BlockSpec(
        (1, 1, t * TOP_K), lambda i: (jnp.minimum(i + shift, nt - 1), 0, 0), memory_space=pltpu.SMEM)
    return pl.pallas_call(
        _combine_kernel,
        out_shape=jax.ShapeDtypeStruct((n, d), _F32),
        grid=(nt,),
        in_specs=[dest_spec(0), dest_spec(1),
                  pl.BlockSpec(memory_space=pl.ANY),
                  pl.BlockSpec((t * PACK_ROWS, 128), lambda i: (i, 0)),
                  pl.BlockSpec((t, TOP_K), lambda i: (i, 0)),
                  pl.BlockSpec((t, d), lambda i: (i, 0)),
                  pl.BlockSpec((1, N_MOD, d), lambda i: (i // tiles_per_batch, 0, 0)),
                  full(wgus), full(wds), full(gpost)],
        out_specs=pl.BlockSpec((t, d), lambda i: (i, 0)),
        scratch_shapes=[pltpu.VMEM((2 * TOP_K * t * PACK_ROWS, 128), _U32), pltpu.SemaphoreType.DMA((2,))],
        compiler_params=pltpu.CompilerParams(
            dimension_semantics=("arbitrary",), vmem_limit_bytes=VMEM_LIMIT),
        name="combine",
    )(dest_tiles, dest_tiles, ys, hp, w_tok, x1, mod, wgus, wds, gpost)


def _deinterleave(n):
    return np.concatenate([np.arange(0, n, 2), np.arange(1, n, 2)])


def _rope_tables(s, dim):
    rows = s // GRID_W
    row = jnp.broadcast_to(jnp.arange(rows, dtype=_F32)[:, None], (rows, GRID_W)).reshape(s)
    col = jnp.broadcast_to(jnp.arange(GRID_W, dtype=_F32)[None, :], (rows, GRID_W)).reshape(s)
    n_freq = dim // 4
    inv_freq = ROPE_THETA ** (-jnp.arange(n_freq, dtype=_F32) / n_freq)
    ang = jnp.concatenate([row[:, None] * inv_freq, col[:, None] * inv_freq], axis=-1)
    return jnp.cos(ang).T, jnp.sin(ang).T


def _in_proj_perm():
    perm = list(range(OFF_KROPE))
    perm += list(OFF_KROPE + _deinterleave(MLA_ROPE))
    for hd in range(GQA_HEADS):
        perm += list(OFF_QG + hd * GQA_HEAD_DIM + _deinterleave(GQA_HEAD_DIM))
    for hd in range(GQA_KV_HEADS):
        perm += list(OFF_KG + hd * GQA_HEAD_DIM + _deinterleave(GQA_HEAD_DIM))
    perm += list(range(OFF_VG, IN_COLS))
    return np.asarray(perm)


def _q_b_perm():
    per = MLA_NOPE + MLA_ROPE
    nope = [hd * per + j for hd in range(MLA_HEADS) for j in range(MLA_NOPE)]
    even = [hd * per + MLA_NOPE + 2 * i for hd in range(MLA_HEADS) for i in range(MLA_ROPE // 2)]
    odd = [hd * per + MLA_NOPE + 2 * i + 1 for hd in range(MLA_HEADS) for i in range(MLA_ROPE // 2)]
    return np.asarray(nope + even + odd)


def _kv_b_perm():
    per = MLA_NOPE + MLA_V
    kk = [hd * per + j for hd in range(MLA_HEADS) for j in range(MLA_NOPE)]
    vv = [hd * per + MLA_NOPE + j for hd in range(MLA_HEADS) for j in range(MLA_V)]
    return np.asarray(kk + vv)


def _layer(x, c, w_ada, b_ada, g_pre_mix, g_post_mix, g_pre_ffn, g_post_ffn, w_in, g_q_a, w_q_b,
           g_kv_a, w_kv_b, g_q_head, g_k_head, g_mla_out, g_gqa_out, w_out, w_router, b_router,
           w_gate_e, w_up_e, w_down_e, w_gate_s, w_up_s, w_down_s):
    b, s, d = x.shape
    n = b * s
    col = lambda g: g.reshape(-1, 1).astype(_F32)
    row = lambda g: g.reshape(1, -1).astype(_F32)

    mod = _ada_mod(c, w_ada, b_ada)

    winT = w_in[:, _in_proj_perm()].T.astype(_BF16)
    wqbT = w_q_b[:, _q_b_perm()].T.astype(_BF16)
    wkvbT = w_kv_b[:, _kv_b_perm()].T.astype(_BF16)
    perm_h = _deinterleave(GQA_HEAD_DIM)
    cosm, sinm = _rope_tables(s, MLA_ROPE)
    cosg, sing = _rope_tables(s, GQA_HEAD_DIM)

    qT, k, vT = _project(x, mod, row(g_pre_mix), winT, col(g_q_a), wqbT, col(g_kv_a), wkvbT,
                         col(g_q_head[perm_h]), col(g_k_head[perm_h]), cosm, sinm, cosg, sing)
    oT = _attention(qT, k, vT)

    wr_t = w_router.T.astype(_F32)
    wrh = wr_t.astype(_BF16)
    wrl = (wr_t - wrh.astype(_F32)).astype(_BF16)
    x1, hp, idx_t, w_t, rank_t, cnt = _post_attention(
        x, oT, mod, col(g_mla_out), col(g_gqa_out), w_out.T.astype(_BF16), row(g_post_mix),
        row(g_pre_ffn), wrh, wrl, col(b_router))

    counts = cnt[:, 0].astype(jnp.int32)
    blocks_e = (counts + T_M - 1) // T_M
    blk_end = jnp.cumsum(blocks_e)
    start = (blk_end - blocks_e) * T_M
    n_blocks = n * TOP_K // T_M + N_EXPERTS
    nused = blk_end[-1:].astype(jnp.int32)
    bstart = (blk_end - blocks_e).astype(jnp.int32)
    dest = _slots(idx_t, rank_t, start)
    dest_tiles = dest.T.reshape(n // T_ROW, 1, T_ROW * TOP_K)

    xs = _dispatch((start + counts).astype(jnp.int32), (blocks_e * T_M - counts).astype(jnp.int32), nused,
                   dest_tiles, hp, n_blocks * T_M)
    ys = _experts(blocks_e.astype(jnp.int32), bstart, nused, xs, w_gate_e, w_up_e, w_down_e)
    wgus = jnp.concatenate([w_gate_s, w_up_s], axis=1).astype(_BF16)
    out = _combine(dest_tiles, ys, hp, w_t.T, x1.reshape(n, d), mod, wgus, w_down_s.astype(_BF16),
                   row(g_post_ffn), s // T_ROW)
    return out.reshape(b, s, d)


def kernel(x, c, w_ada, b_ada, g_pre_mix, g_post_mix, g_pre_ffn, g_post_ffn, w_in, g_q_a, w_q_b, g_kv_a, w_kv_b, g_q_head, g_k_head, g_mla_out, g_gqa_out, w_out, w_router, b_router, w_gate_e, w_up_e, w_down_e, w_gate_s, w_up_s, w_down_s):
    depth = w_ada.shape[0]
    for l in range(depth):
        x = _layer(x, c, w_ada[l], b_ada[l], g_pre_mix[l], g_post_mix[l], g_pre_ffn[l], g_post_ffn[l],
                   w_in[l], g_q_a[l], w_q_b[l], g_kv_a[l], w_kv_b[l], g_q_head[l], g_k_head[l],
                   g_mla_out[l], g_gqa_out[l], w_out[l], w_router[l], b_router[l], w_gate_e[l],
                   w_up_e[l], w_down_e[l], w_gate_s[l], w_up_s[l], w_down_s[l])
    return x
```

```python
import functools

import jax
import jax.numpy as jnp
import numpy as np
from jax import lax
from jax.experimental import pallas as pl
from jax.experimental.pallas import tpu as pltpu

D_MODEL = 1024
GRID_W = 64
ROPE_THETA = 10000.0
NORM_EPS = 1e-6

MLA_HEADS = 8
MLA_Q_LORA = 256
MLA_KV_LORA = 128
MLA_NOPE = 64
MLA_ROPE = 32
MLA_V = 64

GQA_HEADS = 8
GQA_KV_HEADS = 2
GQA_HEAD_DIM = 64
GQA_GROUP = GQA_HEADS // GQA_KV_HEADS

N_HEADS = MLA_HEADS + GQA_HEADS
N_KV = MLA_HEADS + GQA_KV_HEADS
HEAD_PAD = 128
V_DIM = 64

N_EXPERTS = 256
TOP_K = 8
N_GROUPS = 8
GROUP_SIZE = N_EXPERTS // N_GROUPS
TOPK_GROUPS = 4
EXPERT_FF = 256
ROUTED_SCALE = 2.5
N_MOD = 6

OFF_QLAT = 0
OFF_KVLAT = OFF_QLAT + MLA_Q_LORA
OFF_KROPE = OFF_KVLAT + MLA_KV_LORA
OFF_QG = OFF_KROPE + MLA_ROPE
OFF_KG = OFF_QG + GQA_HEADS * GQA_HEAD_DIM
OFF_VG = OFF_KG + GQA_KV_HEADS * GQA_HEAD_DIM
IN_COLS = OFF_VG + GQA_KV_HEADS * GQA_HEAD_DIM

T_PROJ = 512
T_POST = 512
T_ROW = 256
T_M = 256
PACK_ROWS = 4
N_SCORE_SLOTS = 4
X_RING = 4
Y_RING = 3
LOG2_E = 1.4426950408889634
VMEM_LIMIT = 56 * 1024 * 1024

_NT = (((1,), (1,)), ((), ()))
_F32 = jnp.float32
_BF16 = jnp.bfloat16
_U32 = jnp.uint32
_HI_MASK = np.uint32(0xFFFF0000)


def _rsqrt_mean_sq(v, axis):
    return lax.rsqrt(jnp.mean(v * v, axis=axis, keepdims=True) + NORM_EPS)


def _pack_pair(lo, hi):
    lo_b = pltpu.bitcast(lo.astype(_BF16).astype(_F32), _U32) >> 16
    hi_b = pltpu.bitcast(hi.astype(_BF16).astype(_F32), _U32) & _HI_MASK
    return lo_b | hi_b


def _unpack_pair(u):
    return pltpu.bitcast(u << 16, _F32), pltpu.bitcast(u & _HI_MASK, _F32)


def _token_rows(ref, first, count=1):
    return ref.at[pl.ds(pl.multiple_of(first * PACK_ROWS, PACK_ROWS), count * PACK_ROWS), :]


def _chunk_of_tokens(first, count, c):
    return pl.ds(first * PACK_ROWS + c, count, stride=PACK_ROWS)


def _store_packed(ref, val, first=0):
    rows = val.shape[0]
    for c in range(PACK_ROWS):
        ref[_chunk_of_tokens(first, rows, c), :] = _pack_pair(
            val[:, 256 * c:256 * c + 128], val[:, 256 * c + 128:256 * c + 256])


def _load_packed(ref, first, rows):
    parts = []
    for c in range(PACK_ROWS):
        parts += list(_unpack_pair(ref[_chunk_of_tokens(first, rows, c), :]))
    return parts


def _ada_kernel(c_ref, w_ref, b_ref, o_ref):
    c = c_ref[...]
    a = (c * jax.nn.sigmoid(c)).astype(_BF16)
    o_ref[...] = jnp.dot(a, w_ref[...].astype(_BF16), preferred_element_type=_F32) + b_ref[...]


def _ada_mod(c, w_ada, b_ada):
    b, d = c.shape
    cols = w_ada.shape[1]
    rows = 8
    tn = 1536
    c_pad = jnp.zeros((rows, d), _F32).at[:b].set(c)
    out = pl.pallas_call(
        _ada_kernel,
        out_shape=jax.ShapeDtypeStruct((rows, cols), _F32),
        grid=(cols // tn,),
        in_specs=[pl.BlockSpec((rows, d), lambda j: (0, 0)),
                  pl.BlockSpec((d, tn), lambda j: (0, j)),
                  pl.BlockSpec((1, tn), lambda j: (0, j))],
        out_specs=pl.BlockSpec((rows, tn), lambda j: (0, j)),
        compiler_params=pltpu.CompilerParams(vmem_limit_bytes=VMEM_LIMIT),
        name="ada_mod",
    )(c_pad, w_ada, b_ada.reshape(1, cols))
    return out[:b].reshape(b, N_MOD, d)


def _rope(x0, x1, cos, sin):
    return x0 * cos - x1 * sin, x0 * sin + x1 * cos


def _proj_kernel(x_ref, mod_ref, gpre_ref, win_ref, gqa_ref, wqb_ref, gkva_ref, wkvb_ref,
                 gqh_ref, gkh_ref, cosm_ref, sinm_ref, cosg_ref, sing_ref,
                 qT_ref, k_ref, vT_ref):
    x = x_ref[0]
    t = x.shape[0]
    shift = mod_ref[0, 0:1, :]
    scale = mod_ref[0, 1:2, :]
    h = x * _rsqrt_mean_sq(x, -1) * gpre_ref[...] * (1.0 + scale) + shift
    pT = lax.dot_general(win_ref[...], h.astype(_BF16), _NT, preferred_element_type=_F32)

    cm, sm = cosm_ref[...], sinm_ref[...]
    cg, sg = cosg_ref[...], sing_ref[...]
    half_m = MLA_ROPE // 2
    half_g = GQA_HEAD_DIM // 2

    ql = pT[OFF_QLAT:OFF_QLAT + MLA_Q_LORA]
    qn = (ql * _rsqrt_mean_sq(ql, 0) * gqa_ref[...]).astype(_BF16)
    qT = jnp.dot(wqb_ref[...], qn, preferred_element_type=_F32)
    sc_m = float((MLA_NOPE + MLA_ROPE) ** -0.5 * LOG2_E)
    n_nope = MLA_HEADS * MLA_NOPE
    n_half = MLA_HEADS * half_m
    q0, q1 = _rope(qT[n_nope:n_nope + n_half], qT[n_nope + n_half:],
                   jnp.tile(cm, (MLA_HEADS, 1)), jnp.tile(sm, (MLA_HEADS, 1)))
    zq_m = jnp.zeros((HEAD_PAD - MLA_NOPE - MLA_ROPE, t), _BF16)
    for hd in range(MLA_HEADS):
        qT_ref[0, hd, 0, 0:MLA_NOPE, :] = (qT[hd * MLA_NOPE:(hd + 1) * MLA_NOPE] * sc_m).astype(_BF16)
        qT_ref[0, hd, 0, MLA_NOPE:MLA_NOPE + half_m, :] = (q0[hd * half_m:(hd + 1) * half_m] * sc_m).astype(_BF16)
        qT_ref[0, hd, 0, MLA_NOPE + half_m:MLA_NOPE + MLA_ROPE, :] = (
            q1[hd * half_m:(hd + 1) * half_m] * sc_m).astype(_BF16)
        qT_ref[0, hd, 0, MLA_NOPE + MLA_ROPE:, :] = zq_m

    kl = pT[OFF_KVLAT:OFF_KVLAT + MLA_KV_LORA]
    kn = (kl * _rsqrt_mean_sq(kl, 0) * gkva_ref[...]).astype(_BF16)
    kvT = jnp.dot(wkvb_ref[...], kn, preferred_element_type=_F32)
    k0, k1 = _rope(pT[OFF_KROPE:OFF_KROPE + half_m], pT[OFF_KROPE + half_m:OFF_KROPE + MLA_ROPE], cm, sm)
    zk_m = jnp.zeros((HEAD_PAD - MLA_NOPE - MLA_ROPE, t), _F32)
    for hd in range(MLA_HEADS):
        kTh = jnp.concatenate([kvT[hd * MLA_NOPE:(hd + 1) * MLA_NOPE], k0, k1, zk_m], axis=0)
        k_ref[0, hd] = kTh.T.astype(_BF16)
        vT_ref[0, hd, 0] = kvT[n_nope + hd * MLA_V:n_nope + (hd + 1) * MLA_V].astype(_BF16)

    sc_g = float(GQA_HEAD_DIM ** -0.5 * LOG2_E)
    zq_g = jnp.zeros((HEAD_PAD - GQA_HEAD_DIM, t), _BF16)
    for hd in range(GQA_HEADS):
        blk = pT[OFF_QG + hd * GQA_HEAD_DIM:OFF_QG + (hd + 1) * GQA_HEAD_DIM]
        bn = blk * _rsqrt_mean_sq(blk, 0) * gqh_ref[...]
        g0, g1 = _rope(bn[:half_g], bn[half_g:], cg, sg)
        qT_ref[0, MLA_HEADS + hd, 0, 0:half_g, :] = (g0 * sc_g).astype(_BF16)
        qT_ref[0, MLA_HEADS + hd, 0, half_g:GQA_HEAD_DIM, :] = (g1 * sc_g).astype(_BF16)
        qT_ref[0, MLA_HEADS + hd, 0, GQA_HEAD_DIM:, :] = zq_g

    zk_g = jnp.zeros((HEAD_PAD - GQA_HEAD_DIM, t), _F32)
    for hd in range(GQA_KV_HEADS):
        blk = pT[OFF_KG + hd * GQA_HEAD_DIM:OFF_KG + (hd + 1) * GQA_HEAD_DIM]
        bn = blk * _rsqrt_mean_sq(blk, 0) * gkh_ref[...]
        g0, g1 = _rope(bn[:half_g], bn[half_g:], cg, sg)
        kTh = jnp.concatenate([g0, g1, zk_g], axis=0)
        k_ref[0, MLA_HEADS + hd] = kTh.T.astype(_BF16)
        vT_ref[0, MLA_HEADS + hd, 0] = pT[OFF_VG + hd * V_DIM:OFF_VG + (hd + 1) * V_DIM].astype(_BF16)


def _project(x, mod, gpre, winT, gqa, wqbT, gkva, wkvbT, gqh, gkh, cosm, sinm, cosg, sing):
    b, s, d = x.shape
    t = T_PROJ
    nt = s // t
    full = lambda a: pl.BlockSpec(a.shape, lambda bi, i: (0,) * a.ndim)
    tab = lambda a: pl.BlockSpec((a.shape[0], t), lambda bi, i: (0, i))
    return pl.pallas_call(
        _proj_kernel,
        out_shape=(jax.ShapeDtypeStruct((b, N_HEADS, nt, HEAD_PAD, t), _BF16),
                   jax.ShapeDtypeStruct((b, N_KV, s, HEAD_PAD), _BF16),
                   jax.ShapeDtypeStruct((b, N_KV, nt, V_DIM, t), _BF16)),
        grid=(b, nt),
        in_specs=[pl.BlockSpec((1, t, d), lambda bi, i: (bi, i, 0)),
                  pl.BlockSpec((1, N_MOD, d), lambda bi, i: (bi, 0, 0)),
                  full(gpre), full(winT), full(gqa), full(wqbT), full(gkva), full(wkvbT),
                  full(gqh), full(gkh), tab(cosm), tab(sinm), tab(cosg), tab(sing)],
        out_specs=(pl.BlockSpec((1, N_HEADS, 1, HEAD_PAD, t), lambda bi, i: (bi, 0, i, 0, 0)),
                   pl.BlockSpec((1, N_KV, t, HEAD_PAD), lambda bi, i: (bi, 0, i, 0)),
                   pl.BlockSpec((1, N_KV, 1, V_DIM, t), lambda bi, i: (bi, 0, i, 0, 0))),
        compiler_params=pltpu.CompilerParams(
            dimension_semantics=("arbitrary", "arbitrary"), vmem_limit_bytes=VMEM_LIMIT),
        name="in_proj",
    )(x, mod, gpre, winT, gqa, wqbT, gkva, wkvbT, gqh, gkh, cosm, sinm, cosg, sing)


def _attn_kernel(qT_ref, k_ref, vT_ref, o_ref, *slots, n_tiles, n_chunks, tk):
    ns = len(slots)
    total = n_tiles * n_chunks
    tq = qT_ref.shape[-1]

    def scores(g, slot):
        tile, j = g // n_chunks, g % n_chunks
        kc = k_ref[0, 0, pl.ds(pl.multiple_of(j * tk, tk), tk), :]
        s = jnp.dot(kc, qT_ref[0, 0, tile], preferred_element_type=_F32)
        slots[slot][...] = s
        return jnp.max(s, axis=0, keepdims=True)

    def accumulate(g, slot, mc, carry, first=False, last=False):
        m, l, acc = carry
        if first is not False:
            m = jnp.where(first, -1e30, m)
            l = jnp.where(first, 0.0, l)
            acc = jnp.where(first, 0.0, acc)
        m_new = jnp.maximum(m, mc)
        alpha = jnp.exp2(m - m_new)
        p = jnp.exp2(slots[slot][...] - m_new)
        l = alpha * l + jnp.sum(p, axis=0, keepdims=True)
        acc = alpha * acc + jnp.dot(vT_ref[0, 0, g % n_chunks], p.astype(_BF16), preferred_element_type=_F32)
        if last is not False:
            o_ref[0, 0, g // n_chunks] = acc / l
        return m_new, l, acc

    def group(jj, state):
        mcs, carry = state
        mcs = list(mcs)
        g0 = ns * jj
        opens = (g0 % n_chunks) == 0
        for u in range(ns):
            nxt = scores(g0 + u + 2, (u + 2) % ns)
            carry = accumulate(g0 + u, u, mcs[u], carry, first=opens if u == 0 else False,
                               last=(u == ns - 1))
            mcs[(u + 2) % ns] = nxt
        return tuple(mcs), carry

    carry = (jnp.full((1, tq), -1e30, _F32), jnp.zeros((1, tq), _F32), jnp.zeros((V_DIM, tq), _F32))
    zero = jnp.zeros((1, tq), _F32)
    mcs = (scores(0, 0), scores(1, 1)) + (zero,) * (ns - 2)
    n_loop = (total - 2) // ns
    mcs, carry = lax.fori_loop(0, n_loop, group, (mcs, carry))
    mcs = list(mcs)
    for g in range(n_loop * ns, total):
        if g + 2 < total:
            mcs[(g + 2) % ns] = scores(g + 2, (g + 2) % ns)
        carry = accumulate(g, g % ns, mcs[g % ns], carry, first=(g % n_chunks == 0), last=(g == total - 1))


def _kv_index(h):
    return jnp.where(h < MLA_HEADS, h, MLA_HEADS + (h - MLA_HEADS) // GQA_GROUP)


def _attention(qT, k, vT):
    b, nh, n_tiles, dp, tq = qT.shape
    s = k.shape[2]
    n_chunks, tk = vT.shape[2], vT.shape[4]
    assert n_chunks % N_SCORE_SLOTS == 0 and n_tiles * n_chunks > N_SCORE_SLOTS
    return pl.pallas_call(
        functools.partial(_attn_kernel, n_tiles=n_tiles, n_chunks=n_chunks, tk=tk),
        out_shape=jax.ShapeDtypeStruct((b, nh, n_tiles, V_DIM, tq), _F32),
        grid=(b, nh),
        in_specs=[pl.BlockSpec((1, 1, n_tiles, dp, tq), lambda bi, h: (bi, h, 0, 0, 0)),
                  pl.BlockSpec((1, 1, s, dp), lambda bi, h: (bi, _kv_index(h), 0, 0)),
                  pl.BlockSpec((1, 1, n_chunks, V_DIM, tk), lambda bi, h: (bi, _kv_index(h), 0, 0, 0))],
        out_specs=pl.BlockSpec((1, 1, n_tiles, V_DIM, tq), lambda bi, h: (bi, h, 0, 0, 0)),
        scratch_shapes=[pltpu.VMEM((tk, tq), _F32) for _ in range(N_SCORE_SLOTS)],
        compiler_params=pltpu.CompilerParams(
            dimension_semantics=("arbitrary", "arbitrary"), vmem_limit_bytes=VMEM_LIMIT),
        name="attention",
    )(qT, k, vT)


def _first_argmax(v, rows, n):
    m = jnp.max(v, axis=0, keepdims=True)
    idx = jnp.min(jnp.where(v == m, rows, n), axis=0, keepdims=True)
    return m, idx


def _post_kernel(x_ref, o_ref, mod_ref, gmla_ref, ggqa_ref, wout_ref, gpost_ref, gffn_ref,
                 wrh_ref, wrl_ref, br_ref,
                 x1_ref, hp_ref, idx_ref, w_ref, rank_ref, cnt_ref, carry_ref):
    first = jnp.logical_and(pl.program_id(0) == 0, pl.program_id(1) == 0)

    @pl.when(first)
    def _():
        carry_ref[...] = jnp.zeros_like(carry_ref)

    x = x_ref[0]
    t = x.shape[0]
    gate_m = mod_ref[0, 2:3, :]
    shift_f = mod_ref[0, 3:4, :]
    scale_f = mod_ref[0, 4:5, :]

    o = o_ref[0, :, 0].reshape(N_HEADS * V_DIM, t)
    half = MLA_HEADS * V_DIM
    om, og = o[:half], o[half:]
    mixedT = jnp.concatenate([om * _rsqrt_mean_sq(om, 0) * gmla_ref[...],
                              og * _rsqrt_mean_sq(og, 0) * ggqa_ref[...]], axis=0).astype(_BF16)
    y = jnp.dot(wout_ref[...], mixedT, preferred_element_type=_F32).T
    x1 = x + gate_m * (y * _rsqrt_mean_sq(y, -1) * gpost_ref[...])
    x1_ref[0] = x1
    h2 = x1 * _rsqrt_mean_sq(x1, -1) * gffn_ref[...] * (1.0 + scale_f) + shift_f

    _store_packed(hp_ref, h2)

    hh = h2.astype(_BF16)
    hl = (h2 - hh.astype(_F32)).astype(_BF16)
    wrh = wrh_ref[...]
    logits = (lax.dot_general(wrh, hh, _NT, preferred_element_type=_F32)
              + lax.dot_general(wrh, hl, _NT, preferred_element_type=_F32)
              + lax.dot_general(wrl_ref[...], hh, _NT, preferred_element_type=_F32))
    scores = jax.nn.sigmoid(logits)
    biased = scores + br_ref[...]
    neg = -jnp.inf

    rows_g = lax.broadcasted_iota(jnp.int32, (GROUP_SIZE, t), 0)
    gscores = []
    for g in range(N_GROUPS):
        v = biased[g * GROUP_SIZE:(g + 1) * GROUP_SIZE]
        m1, i1 = _first_argmax(v, rows_g, GROUP_SIZE)
        m2 = jnp.max(jnp.where(rows_g == i1, neg, v), axis=0, keepdims=True)
        gscores.append(m1 + m2)
    cur = jnp.concatenate(gscores, axis=0)
    rows_8 = lax.broadcasted_iota(jnp.int32, (N_GROUPS, t), 0)
    gsel = jnp.zeros((N_GROUPS, t), _F32)
    for _ in range(TOPK_GROUPS):
        _, gi = _first_argmax(cur, rows_8, N_GROUPS)
        hit = rows_8 == gi
        gsel = jnp.where(hit, 1.0, gsel)
        cur = jnp.where(hit, neg, cur)
    emask = jnp.concatenate(
        [jnp.broadcast_to(gsel[g:g + 1], (GROUP_SIZE, t)) for g in range(N_GROUPS)], axis=0) > 0.5
    masked = jnp.where(emask, biased, neg)

    rows_e = lax.broadcasted_iota(jnp.int32, (N_EXPERTS, t), 0)
    idxs, ws = [], []
    for _ in range(TOP_K):
        _, ei = _first_argmax(masked, rows_e, N_EXPERTS)
        hit = rows_e == ei
        idxs.append(ei)
        ws.append(jnp.sum(jnp.where(hit, scores, 0.0), axis=0, keepdims=True))
        masked = jnp.where(hit, neg, masked)
    wsum = ws[0]
    for wk in ws[1:]:
        wsum = wsum + wk

    onehot = jnp.zeros((N_EXPERTS, t), _F32)
    for ei in idxs:
        onehot = onehot + jnp.where(rows_e == ei, 1.0, 0.0)
    upper = (lax.broadcasted_iota(jnp.int32, (t, t), 0) < lax.broadcasted_iota(jnp.int32, (t, t), 1))
    before = jnp.dot(onehot.astype(_BF16), jnp.where(upper, 1.0, 0.0).astype(_BF16),
                     preferred_element_type=_F32) + carry_ref[...]
    for kk in range(TOP_K):
        idx_ref[kk:kk + 1, :] = idxs[kk]
        w_ref[kk:kk + 1, :] = ws[kk] / wsum * ROUTED_SCALE
        rank_ref[kk:kk + 1, :] = jnp.sum(
            jnp.where(rows_e == idxs[kk], before, 0.0), axis=0, keepdims=True).astype(jnp.int32)
    total = carry_ref[...] + jnp.sum(onehot, axis=1, keepdims=True)
    carry_ref[...] = total
    cnt_ref[...] = jnp.broadcast_to(total, cnt_ref.shape)


def _post_attention(x, oT, mod, gmla, ggqa, woutT, gpost, gffn, wrh, wrl, br):
    b, s, d = x.shape
    n = b * s
    t = T_POST
    nt = s // t
    assert oT.shape[2:] == (nt, V_DIM, t)
    full = lambda a: pl.BlockSpec(a.shape, lambda bi, i: (0,) * a.ndim)
    tok = lambda bi, i: (0, bi * nt + i)
    return pl.pallas_call(
        _post_kernel,
        out_shape=(jax.ShapeDtypeStruct((b, s, d), _F32),
                   jax.ShapeDtypeStruct((n * PACK_ROWS, 128), _U32),
                   jax.ShapeDtypeStruct((TOP_K, n), jnp.int32),
                   jax.ShapeDtypeStruct((TOP_K, n), _F32),
                   jax.ShapeDtypeStruct((TOP_K, n), jnp.int32),
                   jax.ShapeDtypeStruct((N_EXPERTS, 128), _F32)),
        grid=(b, nt),
        in_specs=[pl.BlockSpec((1, t, d), lambda bi, i: (bi, i, 0)),
                  pl.BlockSpec((1, N_HEADS, 1, V_DIM, t), lambda bi, i: (bi, 0, i, 0, 0)),
                  pl.BlockSpec((1, N_MOD, d), lambda bi, i: (bi, 0, 0)),
                  full(gmla), full(ggqa), full(woutT), full(gpost), full(gffn),
                  full(wrh), full(wrl), full(br)],
        out_specs=(pl.BlockSpec((1, t, d), lambda bi, i: (bi, i, 0)),
                   pl.BlockSpec((t * PACK_ROWS, 128), lambda bi, i: (bi * nt + i, 0)),
                   pl.BlockSpec((TOP_K, t), tok),
                   pl.BlockSpec((TOP_K, t), tok),
                   pl.BlockSpec((TOP_K, t), tok),
                   pl.BlockSpec((N_EXPERTS, 128), lambda bi, i: (0, 0))),
        scratch_shapes=[pltpu.VMEM((N_EXPERTS, 1), _F32)],
        compiler_params=pltpu.CompilerParams(
            dimension_semantics=("arbitrary", "arbitrary"), vmem_limit_bytes=VMEM_LIMIT),
        name="out_proj_router",
    )(x, oT, mod, gmla, ggqa, woutT, gpost, gffn, wrh, wrl, br)


def _slot_kernel(idx_ref, rank_ref, start_ref, dest_ref):
    t = idx_ref.shape[1]
    rows_e = lax.broadcasted_iota(jnp.int32, (N_EXPERTS, t), 0)
    start = start_ref[...]
    for kk in range(TOP_K):
        base = jnp.sum(jnp.where(rows_e == idx_ref[kk:kk + 1, :], start, 0.0), axis=0, keepdims=True)
        dest_ref[kk:kk + 1, :] = rank_ref[kk:kk + 1, :] + base.astype(jnp.int32)


def _slots(idx_t, rank_t, start):
    n = idx_t.shape[1]
    t = T_POST
    return pl.pallas_call(
        _slot_kernel,
        out_shape=jax.ShapeDtypeStruct((TOP_K, n), jnp.int32),
        grid=(n // t,),
        in_specs=[pl.BlockSpec((TOP_K, t), lambda i: (0, i)),
                  pl.BlockSpec((TOP_K, t), lambda i: (0, i)),
                  pl.BlockSpec((N_EXPERTS, 1), lambda i: (0, 0))],
        out_specs=pl.BlockSpec((TOP_K, t), lambda i: (0, i)),
        compiler_params=pltpu.CompilerParams(dimension_semantics=("arbitrary",)),
        name="slot_index",
    )(idx_t, rank_t, start.astype(_F32).reshape(N_EXPERTS, 1))


def _row_copy(src, src_row, dst, dst_row, sem):
    return pltpu.make_async_copy(_token_rows(src, src_row), _token_rows(dst, dst_row), sem)


def _dispatch_kernel(pad_start_ref, pad_len_ref, nused_ref, dest_ref, h_ref, xs_ref, zbuf, sem, zsem, *,
                     n_blocks):
    t_rows = h_ref.shape[0] // PACK_ROWS

    def issue(t, _):
        for kk in range(TOP_K):
            _row_copy(h_ref, t, xs_ref, dest_ref[0, 0, t * TOP_K + kk], sem).start(priority=kk % 2)
        return 0

    lax.fori_loop(0, t_rows, issue, 0)

    def drain(t, _):
        for kk in range(TOP_K):
            _row_copy(h_ref, 0, xs_ref, 0, sem).wait()
        return 0

    lax.fori_loop(0, t_rows, drain, 0)

    @pl.when(pl.program_id(0) == pl.num_programs(0) - 1)
    def _():
        zbuf[...] = jnp.zeros_like(zbuf)

        def zero_copy(first, count):
            return pltpu.make_async_copy(_token_rows(zbuf, 0, count), _token_rows(xs_ref, first, count), zsem)

        def sweep(act):
            def expert_pad(e, _):
                first = pad_start_ref[e]
                for piece in [T_M >> sh for sh in range(1, T_M.bit_length())]:
                    has = pad_len_ref[e] & piece

                    @pl.when(has != 0)
                    def _():
                        act(zero_copy(first, piece))

                    first = first + has
                return 0

            lax.fori_loop(0, N_EXPERTS, expert_pad, 0)

            def spare_block(blk, _):
                act(zero_copy(blk * T_M, T_M))
                return 0

            lax.fori_loop(nused_ref[0], n_blocks, spare_block, 0)

        sweep(lambda c: c.start())
        sweep(lambda c: c.wait())


def _dispatch(pad_start, pad_len, nused, dest_tiles, hp, n_slots):
    n = hp.shape[0] // PACK_ROWS
    t = T_ROW
    return pl.pallas_call(
        functools.partial(_dispatch_kernel, n_blocks=n_slots // T_M),
        out_shape=jax.ShapeDtypeStruct((n_slots * PACK_ROWS, 128), _U32),
        grid_spec=pltpu.PrefetchScalarGridSpec(
            num_scalar_prefetch=3,
            grid=(n // t,),
            in_specs=[pl.BlockSpec((1, 1, t * TOP_K), lambda i, ps, pn, nu: (i, 0, 0), memory_space=pltpu.SMEM),
                      pl.BlockSpec((t * PACK_ROWS, 128), lambda i, ps, pn, nu: (i, 0))],
            out_specs=pl.BlockSpec(memory_space=pl.ANY),
            scratch_shapes=[pltpu.VMEM((T_M * PACK_ROWS, 128), _U32),
                            pltpu.SemaphoreType.DMA(()), pltpu.SemaphoreType.DMA(())]),
        compiler_params=pltpu.CompilerParams(
            dimension_semantics=("arbitrary",), vmem_limit_bytes=VMEM_LIMIT),
        name="dispatch",
    )(pad_start, pad_len, nused, dest_tiles, hp)


def _swiglu(xb, wgu, wd):
    gu = jnp.dot(xb, wgu, preferred_element_type=_F32)
    g, u = gu[:, :EXPERT_FF], gu[:, EXPERT_FF:]
    mid = (g * jax.nn.sigmoid(g) * u).astype(_BF16)
    return jnp.dot(mid, wd, preferred_element_type=_F32)


def _load_packed_bf16(ref, rows):
    return jnp.concatenate([p.astype(_BF16) for p in _load_packed(ref, 0, rows)], axis=1)


def _expert_kernel(nblk_ref, bstart_ref, nused_ref, xs_ref, wg_ref, wu_ref, wd_ref, ys_ref,
                   xbuf, ybuf, xsem, ysem, wgu_s, wd_s, *, n_blocks):
    e = pl.program_id(0)
    nb = nblk_ref[e]
    b0 = bstart_ref[e]
    nused = nused_ref[0]

    def x_copy(blk):
        slot = lax.rem(blk, X_RING)
        return pltpu.make_async_copy(_token_rows(xs_ref, blk * T_M, T_M), _token_rows(xbuf, slot * T_M, T_M),
                                     xsem.at[slot])

    def y_copy(blk):
        slot = lax.rem(blk, Y_RING)
        return pltpu.make_async_copy(_token_rows(ybuf, slot * T_M, T_M), _token_rows(ys_ref, blk * T_M, T_M),
                                     ysem.at[slot])

    @pl.when(e == 0)
    def _():
        for g in range(X_RING - 1):
            @pl.when(g < nused)
            def _():
                x_copy(g).start()

    @pl.when(nb > 0)
    def _():
        wgu_s[:, :EXPERT_FF] = wg_ref[0].astype(_BF16)
        wgu_s[:, EXPERT_FF:] = wu_ref[0].astype(_BF16)
        wd_s[...] = wd_ref[0].astype(_BF16)

        def block(g, _):
            x_copy(g).wait()

            @pl.when(g + X_RING - 1 < nused)
            def _():
                x_copy(g + X_RING - 1).start()

            @pl.when(g >= Y_RING)
            def _():
                y_copy(g - Y_RING).wait()

            xb = jnp.concatenate(
                [p.astype(_BF16) for p in _load_packed(xbuf, lax.rem(g, X_RING) * T_M, T_M)], axis=1)
            _store_packed(ybuf, _swiglu(xb, wgu_s[...], wd_s[...]), lax.rem(g, Y_RING) * T_M)
            y_copy(g).start()
            return 0

        lax.fori_loop(b0, b0 + nb, block, 0)

    @pl.when(e == pl.num_programs(0) - 1)
    def _():
        for back in range(Y_RING, 0, -1):
            @pl.when(nused >= back)
            def _():
                y_copy(nused - back).wait()

        ybuf[pl.ds(0, T_M * PACK_ROWS), :] = jnp.zeros((T_M * PACK_ROWS, 128), _U32)

        def zero_copy(blk):
            return pltpu.make_async_copy(_token_rows(ybuf, 0, T_M), _token_rows(ys_ref, blk * T_M, T_M), ysem.at[0])

        def fill(blk, _):
            zero_copy(blk).start()
            return 0

        def drain(blk, _):
            zero_copy(blk).wait()
            return 0

        lax.fori_loop(nused, n_blocks, fill, 0)
        lax.fori_loop(nused, n_blocks, drain, 0)


def _experts(nblk, bstart, nused, xs, wg, wu, wd):
    n_slots = xs.shape[0] // PACK_ROWS
    n_exp, d, ff = wg.shape
    ring = lambda depth: pltpu.VMEM((depth * T_M * PACK_ROWS, 128), _U32)
    return pl.pallas_call(
        functools.partial(_expert_kernel, n_blocks=n_slots // T_M),
        out_shape=jax.ShapeDtypeStruct(xs.shape, _U32),
        grid_spec=pltpu.PrefetchScalarGridSpec(
            num_scalar_prefetch=3,
            grid=(n_exp,),
            in_specs=[pl.BlockSpec(memory_space=pl.ANY),
                      pl.BlockSpec((1, d, ff), lambda i, nbk, bst, nu: (i, 0, 0)),
                      pl.BlockSpec((1, d, ff), lambda i, nbk, bst, nu: (i, 0, 0)),
                      pl.BlockSpec((1, ff, d), lambda i, nbk, bst, nu: (i, 0, 0))],
            out_specs=pl.BlockSpec(memory_space=pl.ANY),
            scratch_shapes=[ring(X_RING), ring(Y_RING),
                            pltpu.SemaphoreType.DMA((X_RING,)), pltpu.SemaphoreType.DMA((Y_RING,)),
                            pltpu.VMEM((d, 2 * ff), _BF16), pltpu.VMEM((ff, d), _BF16)]),
        compiler_params=pltpu.CompilerParams(
            dimension_semantics=("arbitrary",), vmem_limit_bytes=VMEM_LIMIT),
        name="experts",
    )(nblk, bstart, nused, xs, wg, wu, wd)


def _combine_kernel(dest_ref, dest_next_ref, ys_ref, hp_ref, w_ref, x1_ref, mod_ref, wgus_ref, wds_ref,
                    gpost_ref, o_ref, gbuf, sem):
    i = pl.program_id(0)
    t_rows = hp_ref.shape[0] // PACK_ROWS
    tile_rows = TOP_K * t_rows

    def gather(d_ref, slot):
        def issue(t, _):
            for kk in range(TOP_K):
                _row_copy(ys_ref, d_ref[0, 0, t * TOP_K + kk], gbuf, slot * tile_rows + kk * t_rows + t,
                          sem.at[slot]).start(priority=kk % 2)
            return 0

        lax.fori_loop(0, t_rows, issue, 0)

    slot = lax.rem(i, 2)

    @pl.when(i == 0)
    def _():
        gather(dest_ref, 0)

    @pl.when(i + 1 < pl.num_programs(0))
    def _():
        gather(dest_next_ref, 1 - slot)

    shared = _swiglu(_load_packed_bf16(hp_ref, t_rows), wgus_ref[...], wds_ref[...])

    def drain(t, _):
        for kk in range(TOP_K):
            _row_copy(ys_ref, 0, gbuf, 0, sem.at[slot]).wait()
        return 0

    lax.fori_loop(0, t_rows, drain, 0)

    w = w_ref[...]
    cols = None
    for kk in range(TOP_K):
        wk = w[:, kk:kk + 1]
        parts = [wk * p for p in _load_packed(gbuf, slot * tile_rows + kk * t_rows, t_rows)]
        cols = parts if cols is None else [a + b for a, b in zip(cols, parts)]
    y = jnp.concatenate(cols, axis=1) + shared
    gate_f = mod_ref[0, 5:6, :]
    o_ref[...] = x1_ref[...] + gate_f * (y * _rsqrt_mean_sq(y, -1) * gpost_ref[...])


def _combine(dest_tiles, ys, hp, w_tok, x1, mod, wgus, wds, gpost, tiles_per_batch):
    n, d = x1.shape
    t = T_ROW
    nt = n // t
    full = lambda a: pl.BlockSpec(a.shape, lambda i: (0,) * a.ndim)
    dest_spec = lambda shift: pl.BlockSpec(
        (1, 1, t * TOP_K), lambda i: (jnp.minimum(i + shift, nt - 1), 0, 0), memory_space=pltpu.SMEM)
    return pl.pallas_call(
        _combine_kernel,
        out_shape=jax.ShapeDtypeStruct((n, d), _F32),
        grid=(nt,),
        in_specs=[dest_spec(0), dest_spec(1),
                  pl.BlockSpec(memory_space=pl.ANY),
                  pl.BlockSpec((t * PACK_ROWS, 128), lambda i: (i, 0)),
                  pl.BlockSpec((t, TOP_K), lambda i: (i, 0)),
                  pl.BlockSpec((t, d), lambda i: (i, 0)),
                  pl.BlockSpec((1, N_MOD, d), lambda i: (i // tiles_per_batch, 0, 0)),
                  full(wgus), full(wds), full(gpost)],
        out_specs=pl.BlockSpec((t, d), lambda i: (i, 0)),
        scratch_shapes=[pltpu.VMEM((2 * TOP_K * t * PACK_ROWS, 128), _U32), pltpu.SemaphoreType.DMA((2,))],
        compiler_params=pltpu.CompilerParams(
            dimension_semantics=("arbitrary",), vmem_limit_bytes=VMEM_LIMIT),
        name="combine",
    )(dest_tiles, dest_tiles, ys, hp, w_tok, x1, mod, wgus, wds, gpost)


def _deinterleave(n):
    return np.concatenate([np.arange(0, n, 2), np.arange(1, n, 2)])


def _rope_tables(s, dim):
    rows = s // GRID_W
    row = jnp.broadcast_to(jnp.arange(rows, dtype=_F32)[:, None], (rows, GRID_W)).reshape(s)
    col = jnp.broadcast_to(jnp.arange(GRID_W, dtype=_F32)[None, :], (rows, GRID_W)).reshape(s)
    n_freq = dim // 4
    inv_freq = ROPE_THETA ** (-jnp.arange(n_freq, dtype=_F32) / n_freq)
    ang = jnp.concatenate([row[:, None] * inv_freq, col[:, None] * inv_freq], axis=-1)
    return jnp.cos(ang).T, jnp.sin(ang).T


def _in_proj_perm():
    perm = list(range(OFF_KROPE))
    perm += list(OFF_KROPE + _deinterleave(MLA_ROPE))
    for hd in range(GQA_HEADS):
        perm += list(OFF_QG + hd * GQA_HEAD_DIM + _deinterleave(GQA_HEAD_DIM))
    for hd in range(GQA_KV_HEADS):
        perm += list(OFF_KG + hd * GQA_HEAD_DIM + _deinterleave(GQA_HEAD_DIM))
    perm += list(range(OFF_VG, IN_COLS))
    return np.asarray(perm)


def _q_b_perm():
    per = MLA_NOPE + MLA_ROPE
    nope = [hd * per + j for hd in range(MLA_HEADS) for j in range(MLA_NOPE)]
    even = [hd * per + MLA_NOPE + 2 * i for hd in range(MLA_HEADS) for i in range(MLA_ROPE // 2)]
    odd = [hd * per + MLA_NOPE + 2 * i + 1 for hd in range(MLA_HEADS) for i in range(MLA_ROPE // 2)]
    return np.asarray(nope + even + odd)


def _kv_b_perm():
    per = MLA_NOPE + MLA_V
    kk = [hd * per + j for hd in range(MLA_HEADS) for j in range(MLA_NOPE)]
    vv = [hd * per + MLA_NOPE + j for hd in range(MLA_HEADS) for j in range(MLA_V)]
    return np.asarray(kk + vv)


def _layer(x, c, w_ada, b_ada, g_pre_mix, g_post_mix, g_pre_ffn, g_post_ffn, w_in, g_q_a, w_q_b,
           g_kv_a, w_kv_b, g_q_head, g_k_head, g_mla_out, g_gqa_out, w_out, w_router, b_router,
           w_gate_e, w_up_e, w_down_e, w_gate_s, w_up_s, w_down_s):
    b, s, d = x.shape
    n = b * s
    col = lambda g: g.reshape(-1, 1).astype(_F32)
    row = lambda g: g.reshape(1, -1).astype(_F32)

    mod = _ada_mod(c, w_ada, b_ada)

    winT = w_in[:, _in_proj_perm()].T.astype(_BF16)
    wqbT = w_q_b[:, _q_b_perm()].T.astype(_BF16)
    wkvbT = w_kv_b[:, _kv_b_perm()].T.astype(_BF16)
    perm_h = _deinterleave(GQA_HEAD_DIM)
    cosm, sinm = _rope_tables(s, MLA_ROPE)
    cosg, sing = _rope_tables(s, GQA_HEAD_DIM)

    qT, k, vT = _project(x, mod, row(g_pre_mix), winT, col(g_q_a), wqbT, col(g_kv_a), wkvbT,
                         col(g_q_head[perm_h]), col(g_k_head[perm_h]), cosm, sinm, cosg, sing)
    oT = _attention(qT, k, vT)

    wr_t = w_router.T.astype(_F32)
    wrh = wr_t.astype(_BF16)
    wrl = (wr_t - wrh.astype(_F32)).astype(_BF16)
    x1, hp, idx_t, w_t, rank_t, cnt = _post_attention(
        x, oT, mod, col(g_mla_out), col(g_gqa_out), w_out.T.astype(_BF16), row(g_post_mix),
        row(g_pre_ffn), wrh, wrl, col(b_router))

    counts = cnt[:, 0].astype(jnp.int32)
    blocks_e = (counts + T_M - 1) // T_M
    blk_end = jnp.cumsum(blocks_e)
    start = (blk_end - blocks_e) * T_M
    n_blocks = n * TOP_K // T_M + N_EXPERTS
    nused = blk_end[-1:].astype(jnp.int32)
    bstart = (blk_end - blocks_e).astype(jnp.int32)
    dest = _slots(idx_t, rank_t, start)
    dest_tiles = dest.T.reshape(n // T_ROW, 1, T_ROW * TOP_K)

    xs = _dispatch((start + counts).astype(jnp.int32), (blocks_e * T_M - counts).astype(jnp.int32), nused,
                   dest_tiles, hp, n_blocks * T_M)
    ys = _experts(blocks_e.astype(jnp.int32), bstart, nused, xs, w_gate_e, w_up_e, w_down_e)
    wgus = jnp.concatenate([w_gate_s, w_up_s], axis=1).astype(_BF16)
    out = _combine(dest_tiles, ys, hp, w_t.T, x1.reshape(n, d), mod, wgus, w_down_s.astype(_BF16),
                   row(g_post_ffn), s // T_ROW)
    return out.reshape(b, s, d)


def kernel(x, c, w_ada, b_ada, g_pre_mix, g_post_mix, g_pre_ffn, g_post_ffn, w_in, g_q_a, w_q_b, g_kv_a, w_kv_b, g_q_head, g_k_head, g_mla_out, g_gqa_out, w_out, w_router, b_router, w_gate_e, w_up_e, w_down_e, w_gate_s, w_up_s, w_down_s):
    depth = w_ada.shape[0]
    for l in range(depth):
        x = _layer(x, c, w_ada[l], b_ada[l], g_pre_mix[l], g_post_mix[l], g_pre_ffn[l], g_post_ffn[l],
                   w_in[l], g_q_a[l], w_q_b[l], g_kv_a[l], w_kv_b[l], g_q_head[l], g_k_head[l],
                   g_mla_out[l], g_gqa_out[l], w_out[l], w_router[l], b_router[l], w_gate_e[l],
                   w_up_e[l], w_down_e[l], w_gate_s[l], w_up_s[l], w_down_s[l])
    return x
```

```python
import functools

import jax
import jax.numpy as jnp
import numpy as np
from jax import lax
from jax.experimental import pallas as pl
from jax.experimental.pallas import tpu as pltpu

D_MODEL = 1024
GRID_W = 64
ROPE_THETA = 10000.0
NORM_EPS = 1e-6

MLA_HEADS = 8
MLA_Q_LORA = 256
MLA_KV_LORA = 128
MLA_NOPE = 64
MLA_ROPE = 32
MLA_V = 64

GQA_HEADS = 8
GQA_KV_HEADS = 2
GQA_HEAD_DIM = 64
GQA_GROUP = GQA_HEADS // GQA_KV_HEADS

N_HEADS = MLA_HEADS + GQA_HEADS
N_KV = MLA_HEADS + GQA_KV_HEADS
HEAD_PAD = 128
V_DIM = 64
V_ROWS = 80

N_EXPERTS = 256
TOP_K = 8
N_GROUPS = 8
GROUP_SIZE = N_EXPERTS // N_GROUPS
TOPK_GROUPS = 4
EXPERT_FF = 256
ROUTED_SCALE = 2.5
N_MOD = 6

OFF_QLAT = 0
OFF_KVLAT = OFF_QLAT + MLA_Q_LORA
OFF_KROPE = OFF_KVLAT + MLA_KV_LORA
OFF_QG = OFF_KROPE + MLA_ROPE
OFF_KG = OFF_QG + GQA_HEADS * GQA_HEAD_DIM
OFF_VG = OFF_KG + GQA_KV_HEADS * GQA_HEAD_DIM
IN_COLS = OFF_VG + GQA_KV_HEADS * GQA_HEAD_DIM

T_PROJ = 512
T_POST = 512
T_ROW = 256
T_M = 256
PACK_ROWS = 4
N_SCORE_SLOTS = 4
ATTN_GROUP = 16
X_RING = 4
Y_RING = 3
LOG2_E = 1.4426950408889634
VMEM_LIMIT = 56 * 1024 * 1024

_NT = (((1,), (1,)), ((), ()))
_F32 = jnp.float32
_BF16 = jnp.bfloat16
_U32 = jnp.uint32
_HI_MASK = np.uint32(0xFFFF0000)


def _rsqrt_mean_sq(v, axis):
    return lax.rsqrt(jnp.mean(v * v, axis=axis, keepdims=True) + NORM_EPS)


def _pack_pair(lo, hi):
    lo_b = pltpu.bitcast(lo.astype(_BF16).astype(_F32), _U32) >> 16
    hi_b = pltpu.bitcast(hi.astype(_BF16).astype(_F32), _U32) & _HI_MASK
    return lo_b | hi_b


def _unpack_pair(u):
    return pltpu.bitcast(u << 16, _F32), pltpu.bitcast(u & _HI_MASK, _F32)


def _token_rows(ref, first, count=1):
    return ref.at[pl.ds(pl.multiple_of(first * PACK_ROWS, PACK_ROWS), count * PACK_ROWS), :]


def _chunk_of_tokens(first, count, c):
    return pl.ds(first * PACK_ROWS + c, count, stride=PACK_ROWS)


def _store_packed(ref, val, first=0):
    rows = val.shape[0]
    for c in range(PACK_ROWS):
        ref[_chunk_of_tokens(first, rows, c), :] = _pack_pair(
            val[:, 256 * c:256 * c + 128], val[:, 256 * c + 128:256 * c + 256])


def _load_packed(ref, first, rows):
    parts = []
    for c in range(PACK_ROWS):
        parts += list(_unpack_pair(ref[_chunk_of_tokens(first, rows, c), :]))
    return parts


def _ada_kernel(c_ref, w_ref, b_ref, o_ref):
    c = c_ref[...]
    a = (c * jax.nn.sigmoid(c)).astype(_BF16)
    o_ref[...] = jnp.dot(a, w_ref[...].astype(_BF16), preferred_element_type=_F32) + b_ref[...]


def _ada_mod(c, w_ada, b_ada):
    b, d = c.shape
    cols = w_ada.shape[1]
    rows = 8
    tn = 1536
    c_pad = jnp.zeros((rows, d), _F32).at[:b].set(c)
    out = pl.pallas_call(
        _ada_kernel,
        out_shape=jax.ShapeDtypeStruct((rows, cols), _F32),
        grid=(cols // tn,),
        in_specs=[pl.BlockSpec((rows, d), lambda j: (0, 0)),
                  pl.BlockSpec((d, tn), lambda j: (0, j)),
                  pl.BlockSpec((1, tn), lambda j: (0, j))],
        out_specs=pl.BlockSpec((rows, tn), lambda j: (0, j)),
        compiler_params=pltpu.CompilerParams(vmem_limit_bytes=VMEM_LIMIT),
        name="ada_mod",
    )(c_pad, w_ada, b_ada.reshape(1, cols))
    return out[:b].reshape(b, N_MOD, d)


def _rope(x0, x1, cos, sin):
    return x0 * cos - x1 * sin, x0 * sin + x1 * cos


def _proj_kernel(x_ref, mod_ref, gpre_ref, win_ref, gqa_ref, wqb_ref, gkva_ref, wkvb_ref,
                 gqh_ref, gkh_ref, cosm_ref, sinm_ref, cosg_ref, sing_ref,
                 qT_ref, k_ref, vT_ref):
    x = x_ref[0]
    t = x.shape[0]
    shift = mod_ref[0, 0:1, :]
    scale = mod_ref[0, 1:2, :]
    h = x * _rsqrt_mean_sq(x, -1) * gpre_ref[...] * (1.0 + scale) + shift
    pT = lax.dot_general(win_ref[...], h.astype(_BF16), _NT, preferred_element_type=_F32)

    cm, sm = cosm_ref[...], sinm_ref[...]
    cg, sg = cosg_ref[...], sing_ref[...]
    half_m = MLA_ROPE // 2
    half_g = GQA_HEAD_DIM // 2
    pad_rows = lax.broadcasted_iota(jnp.int32, (V_ROWS - V_DIM, t), 0)
    ones_row = jnp.where(pad_rows == 0, 1.0, 0.0).astype(_BF16)

    ql = pT[OFF_QLAT:OFF_QLAT + MLA_Q_LORA]
    qn = (ql * _rsqrt_mean_sq(ql, 0) * gqa_ref[...]).astype(_BF16)
    qT = jnp.dot(wqb_ref[...], qn, preferred_element_type=_F32)
    sc_m = float((MLA_NOPE + MLA_ROPE) ** -0.5 * LOG2_E)
    n_nope = MLA_HEADS * MLA_NOPE
    n_half = MLA_HEADS * half_m
    q0, q1 = _rope(qT[n_nope:n_nope + n_half], qT[n_nope + n_half:],
                   jnp.tile(cm, (MLA_HEADS, 1)), jnp.tile(sm, (MLA_HEADS, 1)))
    zq_m = jnp.zeros((HEAD_PAD - MLA_NOPE - MLA_ROPE, t), _BF16)
    for hd in range(MLA_HEADS):
        qT_ref[0, hd, 0, 0:MLA_NOPE, :] = (qT[hd * MLA_NOPE:(hd + 1) * MLA_NOPE] * sc_m).astype(_BF16)
        qT_ref[0, hd, 0, MLA_NOPE:MLA_NOPE + half_m, :] = (q0[hd * half_m:(hd + 1) * half_m] * sc_m).astype(_BF16)
        qT_ref[0, hd, 0, MLA_NOPE + half_m:MLA_NOPE + MLA_ROPE, :] = (
            q1[hd * half_m:(hd + 1) * half_m] * sc_m).astype(_BF16)
        qT_ref[0, hd, 0, MLA_NOPE + MLA_ROPE:, :] = zq_m

    kl = pT[OFF_KVLAT:OFF_KVLAT + MLA_KV_LORA]
    kn = (kl * _rsqrt_mean_sq(kl, 0) * gkva_ref[...]).astype(_BF16)
    kvT = jnp.dot(wkvb_ref[...], kn, preferred_element_type=_F32)
    k0, k1 = _rope(pT[OFF_KROPE:OFF_KROPE + half_m], pT[OFF_KROPE + half_m:OFF_KROPE + MLA_ROPE], cm, sm)
    zk_m = jnp.zeros((HEAD_PAD - MLA_NOPE - MLA_ROPE, t), _F32)
    for hd in range(MLA_HEADS):
        kTh = jnp.concatenate([kvT[hd * MLA_NOPE:(hd + 1) * MLA_NOPE], k0, k1, zk_m], axis=0)
        k_ref[0, hd] = kTh.T.astype(_BF16)
        vT_ref[0, hd, 0, :V_DIM] = kvT[n_nope + hd * MLA_V:n_nope + (hd + 1) * MLA_V].astype(_BF16)
        vT_ref[0, hd, 0, V_DIM:] = ones_row

    sc_g = float(GQA_HEAD_DIM ** -0.5 * LOG2_E)
    zq_g = jnp.zeros((HEAD_PAD - GQA_HEAD_DIM, t), _BF16)
    for hd in range(GQA_HEADS):
        blk = pT[OFF_QG + hd * GQA_HEAD_DIM:OFF_QG + (hd + 1) * GQA_HEAD_DIM]
        bn = blk * _rsqrt_mean_sq(blk, 0) * gqh_ref[...]
        g0, g1 = _rope(bn[:half_g], bn[half_g:], cg, sg)
        qT_ref[0, MLA_HEADS + hd, 0, 0:half_g, :] = (g0 * sc_g).astype(_BF16)
        qT_ref[0, MLA_HEADS + hd, 0, half_g:GQA_HEAD_DIM, :] = (g1 * sc_g).astype(_BF16)
        qT_ref[0, MLA_HEADS + hd, 0, GQA_HEAD_DIM:, :] = zq_g

    zk_g = jnp.zeros((HEAD_PAD - GQA_HEAD_DIM, t), _F32)
    for hd in range(GQA_KV_HEADS):
        blk = pT[OFF_KG + hd * GQA_HEAD_DIM:OFF_KG + (hd + 1) * GQA_HEAD_DIM]
        bn = blk * _rsqrt_mean_sq(blk, 0) * gkh_ref[...]
        g0, g1 = _rope(bn[:half_g], bn[half_g:], cg, sg)
        kTh = jnp.concatenate([g0, g1, zk_g], axis=0)
        k_ref[0, MLA_HEADS + hd] = kTh.T.astype(_BF16)
        vT_ref[0, MLA_HEADS + hd, 0, :V_DIM] = pT[OFF_VG + hd * V_DIM:OFF_VG + (hd + 1) * V_DIM].astype(_BF16)
        vT_ref[0, MLA_HEADS + hd, 0, V_DIM:] = ones_row


def _project(x, mod, gpre, winT, gqa, wqbT, gkva, wkvbT, gqh, gkh, cosm, sinm, cosg, sing):
    b, s, d = x.shape
    t = T_PROJ
    nt = s // t
    full = lambda a: pl.BlockSpec(a.shape, lambda bi, i: (0,) * a.ndim)
    tab = lambda a: pl.BlockSpec((a.shape[0], t), lambda bi, i: (0, i))
    return pl.pallas_call(
        _proj_kernel,
        out_shape=(jax.ShapeDtypeStruct((b, N_HEADS, nt, HEAD_PAD, t), _BF16),
                   jax.ShapeDtypeStruct((b, N_KV, s, HEAD_PAD), _BF16),
                   jax.ShapeDtypeStruct((b, N_KV, nt, V_ROWS, t), _BF16)),
        grid=(b, nt),
        in_specs=[pl.BlockSpec((1, t, d), lambda bi, i: (bi, i, 0)),
                  pl.BlockSpec((1, N_MOD, d), lambda bi, i: (bi, 0, 0)),
                  full(gpre), full(winT), full(gqa), full(wqbT), full(gkva), full(wkvbT),
                  full(gqh), full(gkh), tab(cosm), tab(sinm), tab(cosg), tab(sing)],
        out_specs=(pl.BlockSpec((1, N_HEADS, 1, HEAD_PAD, t), lambda bi, i: (bi, 0, i, 0, 0)),
                   pl.BlockSpec((1, N_KV, t, HEAD_PAD), lambda bi, i: (bi, 0, i, 0)),
                   pl.BlockSpec((1, N_KV, 1, V_ROWS, t), lambda bi, i: (bi, 0, i, 0, 0))),
        compiler_params=pltpu.CompilerParams(
            dimension_semantics=("arbitrary", "arbitrary"), vmem_limit_bytes=VMEM_LIMIT),
        name="in_proj",
    )(x, mod, gpre, winT, gqa, wqbT, gkva, wkvbT, gqh, gkh, cosm, sinm, cosg, sing)


def _attn_kernel(qT_ref, k_ref, vT_ref, o_ref, *slots, n_tiles, n_chunks, tk, group_size):
    ns = len(slots)
    total = n_tiles * n_chunks
    tq = qT_ref.shape[-1]

    def scores(g, slot):
        tile, j = g // n_chunks, g % n_chunks
        kc = k_ref[0, 0, pl.ds(pl.multiple_of(j * tk, tk), tk), :]
        s = jnp.dot(kc, qT_ref[0, 0, tile], preferred_element_type=_F32)
        slots[slot][...] = s
        return jnp.max(s, axis=0, keepdims=True)

    def accumulate(g, slot, mc, carry, first=False, last=False):
        m, acc = carry
        if first is not False:
            m = jnp.where(first, -1e30, m)
            acc = jnp.where(first, 0.0, acc)
        m_new = jnp.maximum(m, mc)
        alpha = jnp.exp2(m - m_new)
        p = jnp.exp2(slots[slot][...] - m_new).astype(_BF16)
        acc = alpha * acc + jnp.dot(vT_ref[0, 0, g % n_chunks], p, preferred_element_type=_F32)
        if last is not False:
            o_ref[0, 0, g // n_chunks] = acc[:V_DIM] / acc[V_DIM:V_DIM + 1]
        return m_new, acc

    def group(jj, state):
        mcs, carry = state
        mcs = list(mcs)
        g0 = group_size * jj
        opens = (g0 % n_chunks) == 0
        for u in range(group_size):
            nxt = scores(g0 + u + 2, (u + 2) % ns)
            carry = accumulate(g0 + u, u % ns, mcs[u % ns], carry, first=opens if u == 0 else False,
                               last=(u == group_size - 1))
            mcs[(u + 2) % ns] = nxt
        return tuple(mcs), carry

    carry = (jnp.full((1, tq), -1e30, _F32), jnp.zeros((V_ROWS, tq), _F32))
    zero = jnp.zeros((1, tq), _F32)
    mcs = (scores(0, 0), scores(1, 1)) + (zero,) * (ns - 2)
    n_loop = (total - 2) // group_size
    mcs, carry = lax.fori_loop(0, n_loop, group, (mcs, carry))
    mcs = list(mcs)
    for g in range(n_loop * group_size, total):
        if g + 2 < total:
            mcs[(g + 2) % ns] = scores(g + 2, (g + 2) % ns)
        carry = accumulate(g, g % ns, mcs[g % ns], carry, first=(g % n_chunks == 0), last=(g == total - 1))


def _kv_index(h):
    return jnp.where(h < MLA_HEADS, h, MLA_HEADS + (h - MLA_HEADS) // GQA_GROUP)


def _attention(qT, k, vT):
    b, nh, n_tiles, dp, tq = qT.shape
    s = k.shape[2]
    n_chunks, tk = vT.shape[2], vT.shape[4]
    group_size = min(ATTN_GROUP, n_chunks)
    assert group_size % N_SCORE_SLOTS == 0 and n_chunks % group_size == 0 and n_tiles * n_chunks > group_size
    return pl.pallas_call(
        functools.partial(_attn_kernel, n_tiles=n_tiles, n_chunks=n_chunks, tk=tk, group_size=group_size),
        out_shape=jax.ShapeDtypeStruct((b, nh, n_tiles, V_DIM, tq), _F32),
        grid=(b, nh),
        in_specs=[pl.BlockSpec((1, 1, n_tiles, dp, tq), lambda bi, h: (bi, h, 0, 0, 0)),
                  pl.BlockSpec((1, 1, s, dp), lambda bi, h: (bi, _kv_index(h), 0, 0)),
                  pl.BlockSpec((1, 1, n_chunks, V_ROWS, tk), lambda bi, h: (bi, _kv_index(h), 0, 0, 0))],
        out_specs=pl.BlockSpec((1, 1, n_tiles, V_DIM, tq), lambda bi, h: (bi, h, 0, 0, 0)),
        scratch_shapes=[pltpu.VMEM((tk, tq), _F32) for _ in range(N_SCORE_SLOTS)],
        compiler_params=pltpu.CompilerParams(
            dimension_semantics=("arbitrary", "arbitrary"), vmem_limit_bytes=VMEM_LIMIT),
        name="attention",
    )(qT, k, vT)


def _first_argmax(v, rows, n):
    m = jnp.max(v, axis=0, keepdims=True)
    idx = jnp.min(jnp.where(v == m, rows, n), axis=0, keepdims=True)
    return m, idx


def _post_kernel(x_ref, o_ref, mod_ref, gmla_ref, ggqa_ref, wout_ref, gpost_ref, gffn_ref,
                 wrh_ref, wrl_ref, br_ref,
                 x1_ref, hp_ref, idx_ref, w_ref, rank_ref, cnt_ref, carry_ref):
    first = jnp.logical_and(pl.program_id(0) == 0, pl.program_id(1) == 0)

    @pl.when(first)
    def _():
        carry_ref[...] = jnp.zeros_like(carry_ref)

    x = x_ref[0]
    t = x.shape[0]
    gate_m = mod_ref[0, 2:3, :]
    shift_f = mod_ref[0, 3:4, :]
    scale_f = mod_ref[0, 4:5, :]

    o = o_ref[0, :, 0].reshape(N_HEADS * V_DIM, t)
    half = MLA_HEADS * V_DIM
    om, og = o[:half], o[half:]
    mixedT = jnp.concatenate([om * _rsqrt_mean_sq(om, 0) * gmla_ref[...],
                              og * _rsqrt_mean_sq(og, 0) * ggqa_ref[...]], axis=0).astype(_BF16)
    y = jnp.dot(wout_ref[...], mixedT, preferred_element_type=_F32).T
    x1 = x + gate_m * (y * _rsqrt_mean_sq(y, -1) * gpost_ref[...])
    x1_ref[0] = x1
    h2 = x1 * _rsqrt_mean_sq(x1, -1) * gffn_ref[...] * (1.0 + scale_f) + shift_f

    _store_packed(hp_ref, h2)

    hh = h2.astype(_BF16)
    hl = (h2 - hh.astype(_F32)).astype(_BF16)
    wrh = wrh_ref[...]
    logits = (lax.dot_general(wrh, hh, _NT, preferred_element_type=_F32)
              + lax.dot_general(wrh, hl, _NT, preferred_element_type=_F32)
              + lax.dot_general(wrl_ref[...], hh, _NT, preferred_element_type=_F32))
    scores = jax.nn.sigmoid(logits)
    biased = scores + br_ref[...]
    neg = -jnp.inf

    rows_g = lax.broadcasted_iota(jnp.int32, (GROUP_SIZE, t), 0)
    gscores = []
    for g in range(N_GROUPS):
        v = biased[g * GROUP_SIZE:(g + 1) * GROUP_SIZE]
        m1, i1 = _first_argmax(v, rows_g, GROUP_SIZE)
        m2 = jnp.max(jnp.where(rows_g == i1, neg, v), axis=0, keepdims=True)
        gscores.append(m1 + m2)
    cur = jnp.concatenate(gscores, axis=0)
    rows_8 = lax.broadcasted_iota(jnp.int32, (N_GROUPS, t), 0)
    gsel = jnp.zeros((N_GROUPS, t), _F32)
    for _ in range(TOPK_GROUPS):
        _, gi = _first_argmax(cur, rows_8, N_GROUPS)
        hit = rows_8 == gi
        gsel = jnp.where(hit, 1.0, gsel)
        cur = jnp.where(hit, neg, cur)
    emask = jnp.concatenate(
        [jnp.broadcast_to(gsel[g:g + 1], (GROUP_SIZE, t)) for g in range(N_GROUPS)], axis=0) > 0.5
    masked = jnp.where(emask, biased, neg)

    rows_e = lax.broadcasted_iota(jnp.int32, (N_EXPERTS, t), 0)
    idxs, ws = [], []
    for _ in range(TOP_K):
        _, ei = _first_argmax(masked, rows_e, N_EXPERTS)
        hit = rows_e == ei
        idxs.append(ei)
        ws.append(jnp.sum(jnp.where(hit, scores, 0.0), axis=0, keepdims=True))
        masked = jnp.where(hit, neg, masked)
    wsum = ws[0]
    for wk in ws[1:]:
        wsum = wsum + wk

    onehot = jnp.zeros((N_EXPERTS, t), _F32)
    for ei in idxs:
        onehot = onehot + jnp.where(rows_e == ei, 1.0, 0.0)
    upper = (lax.broadcasted_iota(jnp.int32, (t, t), 0) < lax.broadcasted_iota(jnp.int32, (t, t), 1))
    before = jnp.dot(onehot.astype(_BF16), jnp.where(upper, 1.0, 0.0).astype(_BF16),
                     preferred_element_type=_F32) + carry_ref[...]
    for kk in range(TOP_K):
        idx_ref[kk:kk + 1, :] = idxs[kk]
        w_ref[kk:kk + 1, :] = ws[kk] / wsum * ROUTED_SCALE
        rank_ref[kk:kk + 1, :] = jnp.sum(
            jnp.where(rows_e == idxs[kk], before, 0.0), axis=0, keepdims=True).astype(jnp.int32)
    total = carry_ref[...] + jnp.sum(onehot, axis=1, keepdims=True)
    carry_ref[...] = total
    cnt_ref[...] = jnp.broadcast_to(total, cnt_ref.shape)


def _post_attention(x, oT, mod, gmla, ggqa, woutT, gpost, gffn, wrh, wrl, br):
    b, s, d = x.shape
    n = b * s
    t = T_POST
    nt = s // t
    assert oT.shape[2:] == (nt, V_DIM, t)
    full = lambda a: pl.BlockSpec(a.shape, lambda bi, i: (0,) * a.ndim)
    tok = lambda bi, i: (0, bi * nt + i)
    return pl.pallas_call(
        _post_kernel,
        out_shape=(jax.ShapeDtypeStruct((b, s, d), _F32),
                   jax.ShapeDtypeStruct((n * PACK_ROWS, 128), _U32),
                   jax.ShapeDtypeStruct((TOP_K, n), jnp.int32),
                   jax.ShapeDtypeStruct((TOP_K, n), _F32),
                   jax.ShapeDtypeStruct((TOP_K, n), jnp.int32),
                   jax.ShapeDtypeStruct((N_EXPERTS, 128), _F32)),
        grid=(b, nt),
        in_specs=[pl.BlockSpec((1, t, d), lambda bi, i: (bi, i, 0)),
                  pl.BlockSpec((1, N_HEADS, 1, V_DIM, t), lambda bi, i: (bi, 0, i, 0, 0)),
                  pl.BlockSpec((1, N_MOD, d), lambda bi, i: (bi, 0, 0)),
                  full(gmla), full(ggqa), full(woutT), full(gpost), full(gffn),
                  full(wrh), full(wrl), full(br)],
        out_specs=(pl.BlockSpec((1, t, d), lambda bi, i: (bi, i, 0)),
                   pl.BlockSpec((t * PACK_ROWS, 128), lambda bi, i: (bi * nt + i, 0)),
                   pl.BlockSpec((TOP_K, t), tok),
                   pl.BlockSpec((TOP_K, t), tok),
                   pl.BlockSpec((TOP_K, t), tok),
                   pl.BlockSpec((N_EXPERTS, 128), lambda bi, i: (0, 0))),
        scratch_shapes=[pltpu.VMEM((N_EXPERTS, 1), _F32)],
        compiler_params=pltpu.CompilerParams(
            dimension_semantics=("arbitrary", "arbitrary"), vmem_limit_bytes=VMEM_LIMIT),
        name="out_proj_router",
    )(x, oT, mod, gmla, ggqa, woutT, gpost, gffn, wrh, wrl, br)


def _slot_kernel(idx_ref, rank_ref, start_ref, dest_ref):
    t = idx_ref.shape[1]
    rows_e = lax.broadcasted_iota(jnp.int32, (N_EXPERTS, t), 0)
    start = start_ref[...]
    for kk in range(TOP_K):
        base = jnp.sum(jnp.where(rows_e == idx_ref[kk:kk + 1, :], start, 0.0), axis=0, keepdims=True)
        dest_ref[kk:kk + 1, :] = rank_ref[kk:kk + 1, :] + base.astype(jnp.int32)


def _slots(idx_t, rank_t, start):
    n = idx_t.shape[1]
    t = T_POST
    return pl.pallas_call(
        _slot_kernel,
        out_shape=jax.ShapeDtypeStruct((TOP_K, n), jnp.int32),
        grid=(n // t,),
        in_specs=[pl.BlockSpec((TOP_K, t), lambda i: (0, i)),
                  pl.BlockSpec((TOP_K, t), lambda i: (0, i)),
                  pl.BlockSpec((N_EXPERTS, 1), lambda i: (0, 0))],
        out_specs=pl.BlockSpec((TOP_K, t), lambda i: (0, i)),
        compiler_params=pltpu.CompilerParams(dimension_semantics=("arbitrary",)),
        name="slot_index",
    )(idx_t, rank_t, start.astype(_F32).reshape(N_EXPERTS, 1))


def _row_copy(src, src_row, dst, dst_row, sem):
    return pltpu.make_async_copy(_token_rows(src, src_row), _token_rows(dst, dst_row), sem)


def _dispatch_kernel(pad_start_ref, pad_len_ref, nused_ref, dest_ref, h_ref, xs_ref, zbuf, sem, zsem, *,
                     n_blocks):
    t_rows = h_ref.shape[0] // PACK_ROWS

    def issue(t, _):
        for kk in range(TOP_K):
            _row_copy(h_ref, t, xs_ref, dest_ref[0, 0, t * TOP_K + kk], sem).start(priority=kk % 2)
        return 0

    lax.fori_loop(0, t_rows, issue, 0)

    def drain(t, _):
        for kk in range(TOP_K):
            _row_copy(h_ref, 0, xs_ref, 0, sem).wait()
        return 0

    lax.fori_loop(0, t_rows, drain, 0)

    @pl.when(pl.program_id(0) == pl.num_programs(0) - 1)
    def _():
        zbuf[...] = jnp.zeros_like(zbuf)

        def zero_copy(first, count):
            return pltpu.make_async_copy(_token_rows(zbuf, 0, count), _token_rows(xs_ref, first, count), zsem)

        def sweep(act):
            def expert_pad(e, _):
                first = pad_start_ref[e]
                for piece in [T_M >> sh for sh in range(1, T_M.bit_length())]:
                    has = pad_len_ref[e] & piece

                    @pl.when(has != 0)
                    def _():
                        act(zero_copy(first, piece))

                    first = first + has
                return 0

            lax.fori_loop(0, N_EXPERTS, expert_pad, 0)

            def spare_block(blk, _):
                act(zero_copy(blk * T_M, T_M))
                return 0

            lax.fori_loop(nused_ref[0], n_blocks, spare_block, 0)

        sweep(lambda c: c.start())
        sweep(lambda c: c.wait())


def _dispatch(pad_start, pad_len, nused, dest_tiles, hp, n_slots):
    n = hp.shape[0] // PACK_ROWS
    t = T_ROW
    return pl.pallas_call(
        functools.partial(_dispatch_kernel, n_blocks=n_slots // T_M),
        out_shape=jax.ShapeDtypeStruct((n_slots * PACK_ROWS, 128), _U32),
        grid_spec=pltpu.PrefetchScalarGridSpec(
            num_scalar_prefetch=3,
            grid=(n // t,),
            in_specs=[pl.BlockSpec((1, 1, t * TOP_K), lambda i, ps, pn, nu: (i, 0, 0), memory_space=pltpu.SMEM),
                      pl.BlockSpec((t * PACK_ROWS, 128), lambda i, ps, pn, nu: (i, 0))],
            out_specs=pl.BlockSpec(memory_space=pl.ANY),
            scratch_shapes=[pltpu.VMEM((T_M * PACK_ROWS, 128), _U32),
                            pltpu.SemaphoreType.DMA(()), pltpu.SemaphoreType.DMA(())]),
        compiler_params=pltpu.CompilerParams(
            dimension_semantics=("arbitrary",), vmem_limit_bytes=VMEM_LIMIT),
        name="dispatch",
    )(pad_start, pad_len, nused, dest_tiles, hp)


def _swiglu(xb, wgu, wd):
    gu = jnp.dot(xb, wgu, preferred_element_type=_F32)
    g, u = gu[:, :EXPERT_FF], gu[:, EXPERT_FF:]
    mid = (g * jax.nn.sigmoid(g) * u).astype(_BF16)
    return jnp.dot(mid, wd, preferred_element_type=_F32)


def _load_packed_bf16(ref, rows):
    return jnp.concatenate([p.astype(_BF16) for p in _load_packed(ref, 0, rows)], axis=1)


def _expert_kernel(nblk_ref, bstart_ref, nused_ref, xs_ref, wg_ref, wu_ref, wd_ref, ys_ref,
                   xbuf, ybuf, xsem, ysem, wgu_s, wd_s, *, n_blocks):
    e = pl.program_id(0)
    nb = nblk_ref[e]
    b0 = bstart_ref[e]
    nused = nused_ref[0]

    def x_copy(blk):
        slot = lax.rem(blk, X_RING)
        return pltpu.make_async_copy(_token_rows(xs_ref, blk * T_M, T_M), _token_rows(xbuf, slot * T_M, T_M),
                                     xsem.at[slot])

    def y_copy(blk):
        slot = lax.rem(blk, Y_RING)
        return pltpu.make_async_copy(_token_rows(ybuf, slot * T_M, T_M), _token_rows(ys_ref, blk * T_M, T_M),
                                     ysem.at[slot])

    @pl.when(e == 0)
    def _():
        for g in range(X_RING - 1):
            @pl.when(g < nused)
            def _():
                x_copy(g).start()

    @pl.when(nb > 0)
    def _():
        wgu_s[:, :EXPERT_FF] = wg_ref[0].astype(_BF16)
        wgu_s[:, EXPERT_FF:] = wu_ref[0].astype(_BF16)
        wd_s[...] = wd_ref[0].astype(_BF16)

        def block(g, _):
            x_copy(g).wait()

            @pl.when(g + X_RING - 1 < nused)
            def _():
                x_copy(g + X_RING - 1).start()

            @pl.when(g >= Y_RING)
            def _():
                y_copy(g - Y_RING).wait()

            xb = jnp.concatenate(
                [p.astype(_BF16) for p in _load_packed(xbuf, lax.rem(g, X_RING) * T_M, T_M)], axis=1)
            _store_packed(ybuf, _swiglu(xb, wgu_s[...], wd_s[...]), lax.rem(g, Y_RING) * T_M)
            y_copy(g).start()
            return 0

        lax.fori_loop(b0, b0 + nb, block, 0)

    @pl.when(e == pl.num_programs(0) - 1)
    def _():
        for back in range(Y_RING, 0, -1):
            @pl.when(nused >= back)
            def _():
                y_copy(nused - back).wait()

        ybuf[pl.ds(0, T_M * PACK_ROWS), :] = jnp.zeros((T_M * PACK_ROWS, 128), _U32)

        def zero_copy(blk):
            return pltpu.make_async_copy(_token_rows(ybuf, 0, T_M), _token_rows(ys_ref, blk * T_M, T_M), ysem.at[0])

        def fill(blk, _):
            zero_copy(blk).start()
            return 0

        def drain(blk, _):
            zero_copy(blk).wait()
            return 0

        lax.fori_loop(nused, n_blocks, fill, 0)
        lax.fori_loop(nused, n_blocks, drain, 0)


def _experts(nblk, bstart, nused, xs, wg, wu, wd):
    n_slots = xs.shape[0] // PACK_ROWS
    n_exp, d, ff = wg.shape
    ring = lambda depth: pltpu.VMEM((depth * T_M * PACK_ROWS, 128), _U32)
    return pl.pallas_call(
        functools.partial(_expert_kernel, n_blocks=n_slots // T_M),
        out_shape=jax.ShapeDtypeStruct(xs.shape, _U32),
        grid_spec=pltpu.PrefetchScalarGridSpec(
            num_scalar_prefetch=3,
            grid=(n_exp,),
            in_specs=[pl.BlockSpec(memory_space=pl.ANY),
                      pl.BlockSpec((1, d, ff), lambda i, nbk, bst, nu: (i, 0, 0)),
                      pl.BlockSpec((1, d, ff), lambda i, nbk, bst, nu: (i, 0, 0)),
                      pl.BlockSpec((1, ff, d), lambda i, nbk, bst, nu: (i, 0, 0))],
            out_specs=pl.BlockSpec(memory_space=pl.ANY),
            scratch_shapes=[ring(X_RING), ring(Y_RING),
                            pltpu.SemaphoreType.DMA((X_RING,)), pltpu.SemaphoreType.DMA((Y_RING,)),
                            pltpu.VMEM((d, 2 * ff), _BF16), pltpu.VMEM((ff, d), _BF16)]),
        compiler_params=pltpu.CompilerParams(
            dimension_semantics=("arbitrary",), vmem_limit_bytes=VMEM_LIMIT),
        name="experts",
    )(nblk, bstart, nused, xs, wg, wu, wd)


def _combine_kernel(dest_ref, dest_next_ref, ys_ref, hp_ref, w_ref, x1_ref, mod_ref, wgus_ref, wds_ref,
                    gpost_ref, o_ref, gbuf, sem):
    i = pl.program_id(0)
    t_rows = hp_ref.shape[0] // PACK_ROWS
    tile_rows = TOP_K * t_rows

    def gather(d_ref, slot):
        def issue(t, _):
            for kk in range(TOP_K):
                _row_copy(ys_ref, d_ref[0, 0, t * TOP_K + kk], gbuf, slot * tile_rows + kk * t_rows + t,
                          sem.at[slot]).start(priority=kk % 2)
            return 0

        lax.fori_loop(0, t_rows, issue, 0)

    slot = lax.rem(i, 2)

    @pl.when(i == 0)
    def _():
        gather(dest_ref, 0)

    @pl.when(i + 1 < pl.num_programs(0))
    def _():
        gather(dest_next_ref, 1 - slot)

    shared = _swiglu(_load_packed_bf16(hp_ref, t_rows), wgus_ref[...], wds_ref[...])

    def drain(t, _):
        for kk in range(TOP_K):
            _row_copy(ys_ref, 0, gbuf, 0, sem.at[slot]).wait()
        return 0

    lax.fori_loop(0, t_rows, drain, 0)

    w = w_ref[...]
    cols = None
    for kk in range(TOP_K):
        wk = w[:, kk:kk + 1]
        parts = [wk * p for p in _load_packed(gbuf, slot * tile_rows + kk * t_rows, t_rows)]
        cols = parts if cols is None else [a + b for a, b in zip(cols, parts)]
    y = jnp.concatenate(cols, axis=1) + shared
    gate_f = mod_ref[0, 5:6, :]
    o_ref[...] = x1_ref[...] + gate_f * (y * _rsqrt_mean_sq(y, -1) * gpost_ref[...])


def _combine(dest_tiles, ys, hp, w_tok, x1, mod, wgus, wds, gpost, tiles_per_batch):
    n, d = x1.shape
    t = T_ROW
    nt = n // t
    full = lambda a: pl.BlockSpec(a.shape, lambda i: (0,) * a.ndim)
    dest_spec = lambda shift: pl.BlockSpec(
        (1, 1, t * TOP_K), lambda i: (jnp.minimum(i + shift, nt - 1), 0, 0), memory_space=pltpu.SMEM)
    return pl.pallas_call(
        _combine_kernel,
        out_shape=jax.ShapeDtypeStruct((n, d), _F32),
        grid=(nt,),
        in_specs=[dest_spec(0), dest_spec(1),
                  pl.BlockSpec(memory_space=pl.ANY),
                  pl.BlockSpec((t * PACK_ROWS, 128), lambda i: (i, 0)),
                  pl.BlockSpec((t, TOP_K), lambda i: (i, 0)),
                  pl.BlockSpec((t, d), lambda i: (i, 0)),
                  pl.BlockSpec((1, N_MOD, d), lambda i: (i // tiles_per_batch, 0, 0)),
                  full(wgus), full(wds), full(gpost)],
        out_specs=pl.BlockSpec((t, d), lambda i: (i, 0)),
        scratch_shapes=[pltpu.VMEM((2 * TOP_K * t * PACK_ROWS, 128), _U32), pltpu.SemaphoreType.DMA((2,))],
        compiler_params=pltpu.CompilerParams(
            dimension_semantics=("arbitrary",), vmem_limit_bytes=VMEM_LIMIT),
        name="combine",
    )(dest_tiles, dest_tiles, ys, hp, w_tok, x1, mod, wgus, wds, gpost)


def _deinterleave(n):
    return np.concatenate([np.arange(0, n, 2), np.arange(1, n, 2)])


def _rope_tables(s, dim):
    rows = s // GRID_W
    row = jnp.broadcast_to(jnp.arange(rows, dtype=_F32)[:, None], (rows, GRID_W)).reshape(s)
    col = jnp.broadcast_to(jnp.arange(GRID_W, dtype=_F32)[None, :], (rows, GRID_W)).reshape(s)
    n_freq = dim // 4
    inv_freq = ROPE_THETA ** (-jnp.arange(n_freq, dtype=_F32) / n_freq)
    ang = jnp.concatenate([row[:, None] * inv_freq, col[:, None] * inv_freq], axis=-1)
    return jnp.cos(ang).T, jnp.sin(ang).T


def _in_proj_perm():
    perm = list(range(OFF_KROPE))
    perm += list(OFF_KROPE + _deinterleave(MLA_ROPE))
    for hd in range(GQA_HEADS):
        perm += list(OFF_QG + hd * GQA_HEAD_DIM + _deinterleave(GQA_HEAD_DIM))
    for hd in range(GQA_KV_HEADS):
        perm += list(OFF_KG + hd * GQA_HEAD_DIM + _deinterleave(GQA_HEAD_DIM))
    perm += list(range(OFF_VG, IN_COLS))
    return np.asarray(perm)


def _q_b_perm():
    per = MLA_NOPE + MLA_ROPE
    nope = [hd * per + j for hd in range(MLA_HEADS) for j in range(MLA_NOPE)]
    even = [hd * per + MLA_NOPE + 2 * i for hd in range(MLA_HEADS) for i in range(MLA_ROPE // 2)]
    odd = [hd * per + MLA_NOPE + 2 * i + 1 for hd in range(MLA_HEADS) for i in range(MLA_ROPE // 2)]
    return np.asarray(nope + even + odd)


def _kv_b_perm():
    per = MLA_NOPE + MLA_V
    kk = [hd * per + j for hd in range(MLA_HEADS) for j in range(MLA_NOPE)]
    vv = [hd * per + MLA_NOPE + j for hd in range(MLA_HEADS) for j in range(MLA_V)]
    return np.asarray(kk + vv)


def _layer(x, c, w_ada, b_ada, g_pre_mix, g_post_mix, g_pre_ffn, g_post_ffn, w_in, g_q_a, w_q_b,
           g_kv_a, w_kv_b, g_q_head, g_k_head, g_mla_out, g_gqa_out, w_out, w_router, b_router,
           w_gate_e, w_up_e, w_down_e, w_gate_s, w_up_s, w_down_s):
    b, s, d = x.shape
    n = b * s
    col = lambda g: g.reshape(-1, 1).astype(_F32)
    row = lambda g: g.reshape(1, -1).astype(_F32)

    mod = _ada_mod(c, w_ada, b_ada)

    winT = w_in[:, _in_proj_perm()].T.astype(_BF16)
    wqbT = w_q_b[:, _q_b_perm()].T.astype(_BF16)
    wkvbT = w_kv_b[:, _kv_b_perm()].T.astype(_BF16)
    perm_h = _deinterleave(GQA_HEAD_DIM)
    cosm, sinm = _rope_tables(s, MLA_ROPE)
    cosg, sing = _rope_tables(s, GQA_HEAD_DIM)

    qT, k, vT = _project(x, mod, row(g_pre_mix), winT, col(g_q_a), wqbT, col(g_kv_a), wkvbT,
                         col(g_q_head[perm_h]), col(g_k_head[perm_h]), cosm, sinm, cosg, sing)
    oT = _attention(qT, k, vT)

    wr_t = w_router.T.astype(_F32)
    wrh = wr_t.astype(_BF16)
    wrl = (wr_t - wrh.astype(_F32)).astype(_BF16)
    x1, hp, idx_t, w_t, rank_t, cnt = _post_attention(
        x, oT, mod, col(g_mla_out), col(g_gqa_out), w_out.T.astype(_BF16), row(g_post_mix),
        row(g_pre_ffn), wrh, wrl, col(b_router))

    counts = cnt[:, 0].astype(jnp.int32)
    blocks_e = (counts + T_M - 1) // T_M
    blk_end = jnp.cumsum(blocks_e)
    start = (blk_end - blocks_e) * T_M
    n_blocks = n * TOP_K // T_M + N_EXPERTS
    nused = blk_end[-1:].astype(jnp.int32)
    bstart = (blk_end - blocks_e).astype(jnp.int32)
    dest = _slots(idx_t, rank_t, start)
    dest_tiles = dest.T.reshape(n // T_ROW, 1, T_ROW * TOP_K)

    xs = _dispatch((start + counts).astype(jnp.int32), (blocks_e * T_M - counts).astype(jnp.int32), nused,
                   dest_tiles, hp, n_blocks * T_M)
    ys = _experts(blocks_e.astype(jnp.int32), bstart, nused, xs, w_gate_e, w_up_e, w_down_e)
    wgus = jnp.concatenate([w_gate_s, w_up_s], axis=1).astype(_BF16)
    out = _combine(dest_tiles, ys, hp, w_t.T, x1.reshape(n, d), mod, wgus, w_down_s.astype(_BF16),
                   row(g_post_ffn), s // T_ROW)
    return out.reshape(b, s, d)


def kernel(x, c, w_ada, b_ada, g_pre_mix, g_post_mix, g_pre_ffn, g_post_ffn, w_in, g_q_a, w_q_b, g_kv_a, w_kv_b, g_q_head, g_k_head, g_mla_out, g_gqa_out, w_out, w_router, b_router, w_gate_e, w_up_e, w_down_e, w_gate_s, w_up_s, w_down_s):
    depth = w_ada.shape[0]
    for l in range(depth):
        x = _layer(x, c, w_ada[l], b_ada[l], g_pre_mix[l], g_post_mix[l], g_pre_ffn[l], g_post_ffn[l],
                   w_in[l], g_q_a[l], w_q_b[l], g_kv_a[l], w_kv_b[l], g_q_head[l], g_k_head[l],
                   g_mla_out[l], g_gqa_out[l], w_out[l], w_router[l], b_router[l], w_gate_e[l],
                   w_up_e[l], w_down_e[l], w_gate_s[l], w_up_s[l], w_down_s[l])
    return x
```

```python
import functools

import jax
import jax.numpy as jnp
import numpy as np
from jax import lax
from jax.experimental import pallas as pl
from jax.experimental.pallas import tpu as pltpu

D_MODEL = 1024
GRID_W = 64
ROPE_THETA = 10000.0
NORM_EPS = 1e-6

MLA_HEADS = 8
MLA_Q_LORA = 256
MLA_KV_LORA = 128
MLA_NOPE = 64
MLA_ROPE = 32
MLA_V = 64

GQA_HEADS = 8
GQA_KV_HEADS = 2
GQA_HEAD_DIM = 64
GQA_GROUP = GQA_HEADS // GQA_KV_HEADS

N_HEADS = MLA_HEADS + GQA_HEADS
N_KV = MLA_HEADS + GQA_KV_HEADS
HEAD_PAD = 128
V_DIM = 64
V_ROWS = 80

N_EXPERTS = 256
TOP_K = 8
N_GROUPS = 8
GROUP_SIZE = N_EXPERTS // N_GROUPS
TOPK_GROUPS = 4
EXPERT_FF = 256
ROUTED_SCALE = 2.5
N_MOD = 6

OFF_QLAT = 0
OFF_KVLAT = OFF_QLAT + MLA_Q_LORA
OFF_KROPE = OFF_KVLAT + MLA_KV_LORA
OFF_QG = OFF_KROPE + MLA_ROPE
OFF_KG = OFF_QG + GQA_HEADS * GQA_HEAD_DIM
OFF_VG = OFF_KG + GQA_KV_HEADS * GQA_HEAD_DIM
IN_COLS = OFF_VG + GQA_KV_HEADS * GQA_HEAD_DIM

T_PROJ = 512
T_POST = 512
T_ROW = 512
T_M = 256
PACK_ROWS = 4
N_SCORE_SLOTS = 4
ATTN_GROUP = 16
X_RING = 6
Y_RING = 4
LOG2_E = 1.4426950408889634
VMEM_LIMIT = 56 * 1024 * 1024

_NT = (((1,), (1,)), ((), ()))
_F32 = jnp.float32
_BF16 = jnp.bfloat16
_U32 = jnp.uint32
_HI_MASK = np.uint32(0xFFFF0000)


def _rsqrt_mean_sq(v, axis):
    return lax.rsqrt(jnp.mean(v * v, axis=axis, keepdims=True) + NORM_EPS)


def _pack_pair(lo, hi):
    lo_b = pltpu.bitcast(lo.astype(_BF16).astype(_F32), _U32) >> 16
    hi_b = pltpu.bitcast(hi.astype(_BF16).astype(_F32), _U32) & _HI_MASK
    return lo_b | hi_b


def _unpack_pair(u):
    return pltpu.bitcast(u << 16, _F32), pltpu.bitcast(u & _HI_MASK, _F32)


def _token_rows(ref, first, count=1):
    return ref.at[pl.ds(pl.multiple_of(first * PACK_ROWS, PACK_ROWS), count * PACK_ROWS), :]


def _chunk_of_tokens(first, count, c):
    return pl.ds(first * PACK_ROWS + c, count, stride=PACK_ROWS)


def _store_packed(ref, val, first=0):
    rows = val.shape[0]
    for c in range(PACK_ROWS):
        ref[_chunk_of_tokens(first, rows, c), :] = _pack_pair(
            val[:, 256 * c:256 * c + 128], val[:, 256 * c + 128:256 * c + 256])


def _load_packed(ref, first, rows):
    parts = []
    for c in range(PACK_ROWS):
        parts += list(_unpack_pair(ref[_chunk_of_tokens(first, rows, c), :]))
    return parts


def _ada_kernel(c_ref, w_ref, b_ref, o_ref):
    c = c_ref[...]
    a = (c * jax.nn.sigmoid(c)).astype(_BF16)
    o_ref[...] = jnp.dot(a, w_ref[...].astype(_BF16), preferred_element_type=_F32) + b_ref[...]


def _ada_mod(c, w_ada, b_ada):
    b, d = c.shape
    cols = w_ada.shape[1]
    rows = 8
    tn = 1536
    c_pad = jnp.zeros((rows, d), _F32).at[:b].set(c)
    out = pl.pallas_call(
        _ada_kernel,
        out_shape=jax.ShapeDtypeStruct((rows, cols), _F32),
        grid=(cols // tn,),
        in_specs=[pl.BlockSpec((rows, d), lambda j: (0, 0)),
                  pl.BlockSpec((d, tn), lambda j: (0, j)),
                  pl.BlockSpec((1, tn), lambda j: (0, j))],
        out_specs=pl.BlockSpec((rows, tn), lambda j: (0, j)),
        compiler_params=pltpu.CompilerParams(vmem_limit_bytes=VMEM_LIMIT),
        name="ada_mod",
    )(c_pad, w_ada, b_ada.reshape(1, cols))
    return out[:b].reshape(b, N_MOD, d)


def _rope(x0, x1, cos, sin):
    return x0 * cos - x1 * sin, x0 * sin + x1 * cos


def _proj_kernel(x_ref, mod_ref, gpre_ref, win_ref, gqa_ref, wqb_ref, gkva_ref, wkvb_ref,
                 gqh_ref, gkh_ref, cosm_ref, sinm_ref, cosg_ref, sing_ref,
                 qT_ref, k_ref, vT_ref):
    x = x_ref[0]
    t = x.shape[0]
    shift = mod_ref[0, 0:1, :]
    scale = mod_ref[0, 1:2, :]
    h = x * _rsqrt_mean_sq(x, -1) * gpre_ref[...] * (1.0 + scale) + shift
    pT = lax.dot_general(win_ref[...], h.astype(_BF16), _NT, preferred_element_type=_F32)

    cm, sm = cosm_ref[...], sinm_ref[...]
    cg, sg = cosg_ref[...], sing_ref[...]
    half_m = MLA_ROPE // 2
    half_g = GQA_HEAD_DIM // 2
    pad_rows = lax.broadcasted_iota(jnp.int32, (V_ROWS - V_DIM, t), 0)
    ones_row = jnp.where(pad_rows == 0, 1.0, 0.0).astype(_BF16)

    ql = pT[OFF_QLAT:OFF_QLAT + MLA_Q_LORA]
    qn = (ql * _rsqrt_mean_sq(ql, 0) * gqa_ref[...]).astype(_BF16)
    qT = jnp.dot(wqb_ref[...], qn, preferred_element_type=_F32)
    sc_m = float((MLA_NOPE + MLA_ROPE) ** -0.5 * LOG2_E)
    n_nope = MLA_HEADS * MLA_NOPE
    n_half = MLA_HEADS * half_m
    q0, q1 = _rope(qT[n_nope:n_nope + n_half], qT[n_nope + n_half:],
                   jnp.tile(cm, (MLA_HEADS, 1)), jnp.tile(sm, (MLA_HEADS, 1)))
    zq_m = jnp.zeros((HEAD_PAD - MLA_NOPE - MLA_ROPE, t), _BF16)
    for hd in range(MLA_HEADS):
        qT_ref[0, hd, 0, 0:MLA_NOPE, :] = (qT[hd * MLA_NOPE:(hd + 1) * MLA_NOPE] * sc_m).astype(_BF16)
        qT_ref[0, hd, 0, MLA_NOPE:MLA_NOPE + half_m, :] = (q0[hd * half_m:(hd + 1) * half_m] * sc_m).astype(_BF16)
        qT_ref[0, hd, 0, MLA_NOPE + half_m:MLA_NOPE + MLA_ROPE, :] = (
            q1[hd * half_m:(hd + 1) * half_m] * sc_m).astype(_BF16)
        qT_ref[0, hd, 0, MLA_NOPE + MLA_ROPE:, :] = zq_m

    kl = pT[OFF_KVLAT:OFF_KVLAT + MLA_KV_LORA]
    kn = (kl * _rsqrt_mean_sq(kl, 0) * gkva_ref[...]).astype(_BF16)
    kvT = jnp.dot(wkvb_ref[...], kn, preferred_element_type=_F32)
    k0, k1 = _rope(pT[OFF_KROPE:OFF_KROPE + half_m], pT[OFF_KROPE + half_m:OFF_KROPE + MLA_ROPE], cm, sm)
    zk_m = jnp.zeros((HEAD_PAD - MLA_NOPE - MLA_ROPE, t), _F32)
    for hd in range(MLA_HEADS):
        kTh = jnp.concatenate([kvT[hd * MLA_NOPE:(hd + 1) * MLA_NOPE], k0, k1, zk_m], axis=0)
        k_ref[0, hd] = kTh.T.astype(_BF16)
        vT_ref[0, hd, 0, :V_DIM] = kvT[n_nope + hd * MLA_V:n_nope + (hd + 1) * MLA_V].astype(_BF16)
        vT_ref[0, hd, 0, V_DIM:] = ones_row

    sc_g = float(GQA_HEAD_DIM ** -0.5 * LOG2_E)
    zq_g = jnp.zeros((HEAD_PAD - GQA_HEAD_DIM, t), _BF16)
    for hd in range(GQA_HEADS):
        blk = pT[OFF_QG + hd * GQA_HEAD_DIM:OFF_QG + (hd + 1) * GQA_HEAD_DIM]
        bn = blk * _rsqrt_mean_sq(blk, 0) * gqh_ref[...]
        g0, g1 = _rope(bn[:half_g], bn[half_g:], cg, sg)
        qT_ref[0, MLA_HEADS + hd, 0, 0:half_g, :] = (g0 * sc_g).astype(_BF16)
        qT_ref[0, MLA_HEADS + hd, 0, half_g:GQA_HEAD_DIM, :] = (g1 * sc_g).astype(_BF16)
        qT_ref[0, MLA_HEADS + hd, 0, GQA_HEAD_DIM:, :] = zq_g

    zk_g = jnp.zeros((HEAD_PAD - GQA_HEAD_DIM, t), _F32)
    for hd in range(GQA_KV_HEADS):
        blk = pT[OFF_KG + hd * GQA_HEAD_DIM:OFF_KG + (hd + 1) * GQA_HEAD_DIM]
        bn = blk * _rsqrt_mean_sq(blk, 0) * gkh_ref[...]
        g0, g1 = _rope(bn[:half_g], bn[half_g:], cg, sg)
        kTh = jnp.concatenate([g0, g1, zk_g], axis=0)
        k_ref[0, MLA_HEADS + hd] = kTh.T.astype(_BF16)
        vT_ref[0, MLA_HEADS + hd, 0, :V_DIM] = pT[OFF_VG + hd * V_DIM:OFF_VG + (hd + 1) * V_DIM].astype(_BF16)
        vT_ref[0, MLA_HEADS + hd, 0, V_DIM:] = ones_row


def _project(x, mod, gpre, winT, gqa, wqbT, gkva, wkvbT, gqh, gkh, cosm, sinm, cosg, sing):
    b, s, d = x.shape
    t = T_PROJ
    nt = s // t
    full = lambda a: pl.BlockSpec(a.shape, lambda bi, i: (0,) * a.ndim)
    tab = lambda a: pl.BlockSpec((a.shape[0], t), lambda bi, i: (0, i))
    return pl.pallas_call(
        _proj_kernel,
        out_shape=(jax.ShapeDtypeStruct((b, N_HEADS, nt, HEAD_PAD, t), _BF16),
                   jax.ShapeDtypeStruct((b, N_KV, s, HEAD_PAD), _BF16),
                   jax.ShapeDtypeStruct((b, N_KV, nt, V_ROWS, t), _BF16)),
        grid=(b, nt),
        in_specs=[pl.BlockSpec((1, t, d), lambda bi, i: (bi, i, 0)),
                  pl.BlockSpec((1, N_MOD, d), lambda bi, i: (bi, 0, 0)),
                  full(gpre), full(winT), full(gqa), full(wqbT), full(gkva), full(wkvbT),
                  full(gqh), full(gkh), tab(cosm), tab(sinm), tab(cosg), tab(sing)],
        out_specs=(pl.BlockSpec((1, N_HEADS, 1, HEAD_PAD, t), lambda bi, i: (bi, 0, i, 0, 0)),
                   pl.BlockSpec((1, N_KV, t, HEAD_PAD), lambda bi, i: (bi, 0, i, 0)),
                   pl.BlockSpec((1, N_KV, 1, V_ROWS, t), lambda bi, i: (bi, 0, i, 0, 0))),
        compiler_params=pltpu.CompilerParams(
            dimension_semantics=("arbitrary", "arbitrary"), vmem_limit_bytes=VMEM_LIMIT),
        name="in_proj",
    )(x, mod, gpre, winT, gqa, wqbT, gkva, wkvbT, gqh, gkh, cosm, sinm, cosg, sing)


def _attn_kernel(qT_ref, k_ref, vT_ref, o_ref, *slots, n_tiles, n_chunks, tk, group_size):
    ns = len(slots)
    total = n_tiles * n_chunks
    tq = qT_ref.shape[-1]

    def scores(g, slot):
        tile, j = g // n_chunks, g % n_chunks
        kc = k_ref[0, 0, pl.ds(pl.multiple_of(j * tk, tk), tk), :]
        s = jnp.dot(kc, qT_ref[0, 0, tile], preferred_element_type=_F32)
        slots[slot][...] = s
        return jnp.max(s, axis=0, keepdims=True)

    def accumulate(g, slot, mc, carry, first=False, last=False):
        m, acc = carry
        if first is not False:
            m = jnp.where(first, -1e30, m)
            acc = jnp.where(first, 0.0, acc)
        m_new = jnp.maximum(m, mc)
        alpha = jnp.exp2(m - m_new)
        p = jnp.exp2(slots[slot][...] - m_new).astype(_BF16)
        acc = alpha * acc + jnp.dot(vT_ref[0, 0, g % n_chunks], p, preferred_element_type=_F32)
        if last is not False:
            o_ref[0, 0, g // n_chunks] = acc[:V_DIM] / acc[V_DIM:V_DIM + 1]
        return m_new, acc

    def group(jj, state):
        mcs, carry = state
        mcs = list(mcs)
        g0 = group_size * jj
        opens = (g0 % n_chunks) == 0
        for u in range(group_size):
            nxt = scores(g0 + u + 2, (u + 2) % ns)
            carry = accumulate(g0 + u, u % ns, mcs[u % ns], carry, first=opens if u == 0 else False,
                               last=(u == group_size - 1))
            mcs[(u + 2) % ns] = nxt
        return tuple(mcs), carry

    carry = (jnp.full((1, tq), -1e30, _F32), jnp.zeros((V_ROWS, tq), _F32))
    zero = jnp.zeros((1, tq), _F32)
    mcs = (scores(0, 0), scores(1, 1)) + (zero,) * (ns - 2)
    n_loop = (total - 2) // group_size
    mcs, carry = lax.fori_loop(0, n_loop, group, (mcs, carry))
    mcs = list(mcs)
    for g in range(n_loop * group_size, total):
        if g + 2 < total:
            mcs[(g + 2) % ns] = scores(g + 2, (g + 2) % ns)
        carry = accumulate(g, g % ns, mcs[g % ns], carry, first=(g % n_chunks == 0), last=(g == total - 1))


def _kv_index(h):
    return jnp.where(h < MLA_HEADS, h, MLA_HEADS + (h - MLA_HEADS) // GQA_GROUP)


def _attention(qT, k, vT):
    b, nh, n_tiles, dp, tq = qT.shape
    s = k.shape[2]
    n_chunks, tk = vT.shape[2], vT.shape[4]
    group_size = min(ATTN_GROUP, n_chunks)
    assert group_size % N_SCORE_SLOTS == 0 and n_chunks % group_size == 0 and n_tiles * n_chunks > group_size
    return pl.pallas_call(
        functools.partial(_attn_kernel, n_tiles=n_tiles, n_chunks=n_chunks, tk=tk, group_size=group_size),
        out_shape=jax.ShapeDtypeStruct((b, nh, n_tiles, V_DIM, tq), _F32),
        grid=(b, nh),
        in_specs=[pl.BlockSpec((1, 1, n_tiles, dp, tq), lambda bi, h: (bi, h, 0, 0, 0)),
                  pl.BlockSpec((1, 1, s, dp), lambda bi, h: (bi, _kv_index(h), 0, 0)),
                  pl.BlockSpec((1, 1, n_chunks, V_ROWS, tk), lambda bi, h: (bi, _kv_index(h), 0, 0, 0))],
        out_specs=pl.BlockSpec((1, 1, n_tiles, V_DIM, tq), lambda bi, h: (bi, h, 0, 0, 0)),
        scratch_shapes=[pltpu.VMEM((tk, tq), _F32) for _ in range(N_SCORE_SLOTS)],
        compiler_params=pltpu.CompilerParams(
            dimension_semantics=("arbitrary", "arbitrary"), vmem_limit_bytes=VMEM_LIMIT),
        name="attention",
    )(qT, k, vT)


def _first_argmax(v, rows, n):
    m = jnp.max(v, axis=0, keepdims=True)
    idx = jnp.min(jnp.where(v == m, rows, n), axis=0, keepdims=True)
    return m, idx


def _post_kernel(x_ref, o_ref, mod_ref, gmla_ref, ggqa_ref, wout_ref, gpost_ref, gffn_ref,
                 wrh_ref, wrl_ref, br_ref,
                 x1_ref, hp_ref, idx_ref, w_ref, rank_ref, cnt_ref, carry_ref):
    first = jnp.logical_and(pl.program_id(0) == 0, pl.program_id(1) == 0)

    @pl.when(first)
    def _():
        carry_ref[...] = jnp.zeros_like(carry_ref)

    x = x_ref[0]
    t = x.shape[0]
    gate_m = mod_ref[0, 2:3, :]
    shift_f = mod_ref[0, 3:4, :]
    scale_f = mod_ref[0, 4:5, :]

    o = o_ref[0, :, 0].reshape(N_HEADS * V_DIM, t)
    half = MLA_HEADS * V_DIM
    om, og = o[:half], o[half:]
    mixedT = jnp.concatenate([om * _rsqrt_mean_sq(om, 0) * gmla_ref[...],
                              og * _rsqrt_mean_sq(og, 0) * ggqa_ref[...]], axis=0).astype(_BF16)
    y = jnp.dot(wout_ref[...], mixedT, preferred_element_type=_F32).T
    x1 = x + gate_m * (y * _rsqrt_mean_sq(y, -1) * gpost_ref[...])
    x1_ref[0] = x1
    h2 = x1 * _rsqrt_mean_sq(x1, -1) * gffn_ref[...] * (1.0 + scale_f) + shift_f

    _store_packed(hp_ref, h2)

    hh = h2.astype(_BF16)
    hl = (h2 - hh.astype(_F32)).astype(_BF16)
    wrh = wrh_ref[...]
    logits = (lax.dot_general(wrh, hh, _NT, preferred_element_type=_F32)
              + lax.dot_general(wrh, hl, _NT, preferred_element_type=_F32)
              + lax.dot_general(wrl_ref[...], hh, _NT, preferred_element_type=_F32))
    scores = jax.nn.sigmoid(logits)
    biased = scores + br_ref[...]
    neg = -jnp.inf

    rows_g = lax.broadcasted_iota(jnp.int32, (GROUP_SIZE, t), 0)
    gscores = []
    for g in range(N_GROUPS):
        v = biased[g * GROUP_SIZE:(g + 1) * GROUP_SIZE]
        m1, i1 = _first_argmax(v, rows_g, GROUP_SIZE)
        m2 = jnp.max(jnp.where(rows_g == i1, neg, v), axis=0, keepdims=True)
        gscores.append(m1 + m2)
    cur = jnp.concatenate(gscores, axis=0)
    rows_8 = lax.broadcasted_iota(jnp.int32, (N_GROUPS, t), 0)
    gsel = jnp.zeros((N_GROUPS, t), _F32)
    for _ in range(TOPK_GROUPS):
        _, gi = _first_argmax(cur, rows_8, N_GROUPS)
        hit = rows_8 == gi
        gsel = jnp.where(hit, 1.0, gsel)
        cur = jnp.where(hit, neg, cur)
    emask = jnp.concatenate(
        [jnp.broadcast_to(gsel[g:g + 1], (GROUP_SIZE, t)) for g in range(N_GROUPS)], axis=0) > 0.5
    masked = jnp.where(emask, biased, neg)

    rows_e = lax.broadcasted_iota(jnp.int32, (N_EXPERTS, t), 0)
    idxs, ws = [], []
    for _ in range(TOP_K):
        _, ei = _first_argmax(masked, rows_e, N_EXPERTS)
        hit = rows_e == ei
        idxs.append(ei)
        ws.append(jnp.sum(jnp.where(hit, scores, 0.0), axis=0, keepdims=True))
        masked = jnp.where(hit, neg, masked)
    wsum = ws[0]
    for wk in ws[1:]:
        wsum = wsum + wk

    onehot = jnp.zeros((N_EXPERTS, t), _F32)
    for ei in idxs:
        onehot = onehot + jnp.where(rows_e == ei, 1.0, 0.0)
    upper = (lax.broadcasted_iota(jnp.int32, (t, t), 0) < lax.broadcasted_iota(jnp.int32, (t, t), 1))
    before = jnp.dot(onehot.astype(_BF16), jnp.where(upper, 1.0, 0.0).astype(_BF16),
                     preferred_element_type=_F32) + carry_ref[...]
    for kk in range(TOP_K):
        idx_ref[kk:kk + 1, :] = idxs[kk]
        w_ref[kk:kk + 1, :] = ws[kk] / wsum * ROUTED_SCALE
        rank_ref[kk:kk + 1, :] = jnp.sum(
            jnp.where(rows_e == idxs[kk], before, 0.0), axis=0, keepdims=True).astype(jnp.int32)
    total = carry_ref[...] + jnp.sum(onehot, axis=1, keepdims=True)
    carry_ref[...] = total
    cnt_ref[...] = jnp.broadcast_to(total, cnt_ref.shape)


def _post_attention(x, oT, mod, gmla, ggqa, woutT, gpost, gffn, wrh, wrl, br):
    b, s, d = x.shape
    n = b * s
    t = T_POST
    nt = s // t
    assert oT.shape[2:] == (nt, V_DIM, t)
    full = lambda a: pl.BlockSpec(a.shape, lambda bi, i: (0,) * a.ndim)
    tok = lambda bi, i: (0, bi * nt + i)
    return pl.pallas_call(
        _post_kernel,
        out_shape=(jax.ShapeDtypeStruct((b, s, d), _F32),
                   jax.ShapeDtypeStruct((n * PACK_ROWS, 128), _U32),
                   jax.ShapeDtypeStruct((TOP_K, n), jnp.int32),
                   jax.ShapeDtypeStruct((TOP_K, n), _F32),
                   jax.ShapeDtypeStruct((TOP_K, n), jnp.int32),
                   jax.ShapeDtypeStruct((N_EXPERTS, 128), _F32)),
        grid=(b, nt),
        in_specs=[pl.BlockSpec((1, t, d), lambda bi, i: (bi, i, 0)),
                  pl.BlockSpec((1, N_HEADS, 1, V_DIM, t), lambda bi, i: (bi, 0, i, 0, 0)),
                  pl.BlockSpec((1, N_MOD, d), lambda bi, i: (bi, 0, 0)),
                  full(gmla), full(ggqa), full(woutT), full(gpost), full(gffn),
                  full(wrh), full(wrl), full(br)],
        out_specs=(pl.BlockSpec((1, t, d), lambda bi, i: (bi, i, 0)),
                   pl.BlockSpec((t * PACK_ROWS, 128), lambda bi, i: (bi * nt + i, 0)),
                   pl.BlockSpec((TOP_K, t), tok),
                   pl.BlockSpec((TOP_K, t), tok),
                   pl.BlockSpec((TOP_K, t), tok),
                   pl.BlockSpec((N_EXPERTS, 128), lambda bi, i: (0, 0))),
        scratch_shapes=[pltpu.VMEM((N_EXPERTS, 1), _F32)],
        compiler_params=pltpu.CompilerParams(
            dimension_semantics=("arbitrary", "arbitrary"), vmem_limit_bytes=VMEM_LIMIT),
        name="out_proj_router",
    )(x, oT, mod, gmla, ggqa, woutT, gpost, gffn, wrh, wrl, br)


def _slot_kernel(idx_ref, rank_ref, start_ref, dest_ref):
    t = idx_ref.shape[1]
    rows_e = lax.broadcasted_iota(jnp.int32, (N_EXPERTS, t), 0)
    start = start_ref[...]
    for kk in range(TOP_K):
        base = jnp.sum(jnp.where(rows_e == idx_ref[kk:kk + 1, :], start, 0.0), axis=0, keepdims=True)
        dest_ref[kk:kk + 1, :] = rank_ref[kk:kk + 1, :] + base.astype(jnp.int32)


def _slots(idx_t, rank_t, start):
    n = idx_t.shape[1]
    t = T_POST
    return pl.pallas_call(
        _slot_kernel,
        out_shape=jax.ShapeDtypeStruct((TOP_K, n), jnp.int32),
        grid=(n // t,),
        in_specs=[pl.BlockSpec((TOP_K, t), lambda i: (0, i)),
                  pl.BlockSpec((TOP_K, t), lambda i: (0, i)),
                  pl.BlockSpec((N_EXPERTS, 1), lambda i: (0, 0))],
        out_specs=pl.BlockSpec((TOP_K, t), lambda i: (0, i)),
        compiler_params=pltpu.CompilerParams(dimension_semantics=("arbitrary",)),
        name="slot_index",
    )(idx_t, rank_t, start.astype(_F32).reshape(N_EXPERTS, 1))


def _row_copy(src, src_row, dst, dst_row, sem):
    return pltpu.make_async_copy(_token_rows(src, src_row), _token_rows(dst, dst_row), sem)


def _dispatch_kernel(pad_start_ref, pad_len_ref, nused_ref, dest_ref, h_ref, xs_ref, zbuf, sem, zsem, *,
                     n_blocks):
    t_rows = h_ref.shape[0] // PACK_ROWS

    def issue(t, _):
        for kk in range(TOP_K):
            _row_copy(h_ref, t, xs_ref, dest_ref[0, 0, t * TOP_K + kk], sem).start(priority=kk % 2)
        return 0

    lax.fori_loop(0, t_rows, issue, 0)

    def drain(t, _):
        for kk in range(TOP_K):
            _row_copy(h_ref, 0, xs_ref, 0, sem).wait()
        return 0

    lax.fori_loop(0, t_rows, drain, 0)

    @pl.when(pl.program_id(0) == pl.num_programs(0) - 1)
    def _():
        zbuf[...] = jnp.zeros_like(zbuf)

        def zero_copy(first, count):
            return pltpu.make_async_copy(_token_rows(zbuf, 0, count), _token_rows(xs_ref, first, count), zsem)

        def sweep(act):
            def expert_pad(e, _):
                first = pad_start_ref[e]
                for piece in [T_M >> sh for sh in range(1, T_M.bit_length())]:
                    has = pad_len_ref[e] & piece

                    @pl.when(has != 0)
                    def _():
                        act(zero_copy(first, piece))

                    first = first + has
                return 0

            lax.fori_loop(0, N_EXPERTS, expert_pad, 0)

            def spare_block(blk, _):
                act(zero_copy(blk * T_M, T_M))
                return 0

            lax.fori_loop(nused_ref[0], n_blocks, spare_block, 0)

        sweep(lambda c: c.start())
        sweep(lambda c: c.wait())


def _dispatch(pad_start, pad_len, nused, dest_tiles, hp, n_slots):
    n = hp.shape[0] // PACK_ROWS
    t = T_ROW
    return pl.pallas_call(
        functools.partial(_dispatch_kernel, n_blocks=n_slots // T_M),
        out_shape=jax.ShapeDtypeStruct((n_slots * PACK_ROWS, 128), _U32),
        grid_spec=pltpu.PrefetchScalarGridSpec(
            num_scalar_prefetch=3,
            grid=(n // t,),
            in_specs=[pl.BlockSpec((1, 1, t * TOP_K), lambda i, ps, pn, nu: (i, 0, 0), memory_space=pltpu.SMEM),
                      pl.BlockSpec((t * PACK_ROWS, 128), lambda i, ps, pn, nu: (i, 0))],
            out_specs=pl.BlockSpec(memory_space=pl.ANY),
            scratch_shapes=[pltpu.VMEM((T_M * PACK_ROWS, 128), _U32),
                            pltpu.SemaphoreType.DMA(()), pltpu.SemaphoreType.DMA(())]),
        compiler_params=pltpu.CompilerParams(
            dimension_semantics=("arbitrary",), vmem_limit_bytes=VMEM_LIMIT),
        name="dispatch",
    )(pad_start, pad_len, nused, dest_tiles, hp)


def _swiglu(xb, wgu, wd):
    gu = jnp.dot(xb, wgu, preferred_element_type=_F32)
    g, u = gu[:, :EXPERT_FF], gu[:, EXPERT_FF:]
    mid = (g * jax.nn.sigmoid(g) * u).astype(_BF16)
    return jnp.dot(mid, wd, preferred_element_type=_F32)


def _load_packed_bf16(ref, rows):
    return jnp.concatenate([p.astype(_BF16) for p in _load_packed(ref, 0, rows)], axis=1)


def _expert_kernel(nblk_ref, bstart_ref, nused_ref, xs_ref, wg_ref, wu_ref, wd_ref, ys_ref,
                   xbuf, ybuf, xsem, ysem, wgu_s, wd_s, *, n_blocks):
    e = pl.program_id(0)
    nb = nblk_ref[e]
    b0 = bstart_ref[e]
    nused = nused_ref[0]

    def x_copy(blk):
        slot = lax.rem(blk, X_RING)
        return pltpu.make_async_copy(_token_rows(xs_ref, blk * T_M, T_M), _token_rows(xbuf, slot * T_M, T_M),
                                     xsem.at[slot])

    def y_copy(blk):
        slot = lax.rem(blk, Y_RING)
        return pltpu.make_async_copy(_token_rows(ybuf, slot * T_M, T_M), _token_rows(ys_ref, blk * T_M, T_M),
                                     ysem.at[slot])

    @pl.when(e == 0)
    def _():
        for g in range(X_RING - 1):
            @pl.when(g < nused)
            def _():
                x_copy(g).start()

    @pl.when(nb > 0)
    def _():
        wgu_s[:, :EXPERT_FF] = wg_ref[0].astype(_BF16)
        wgu_s[:, EXPERT_FF:] = wu_ref[0].astype(_BF16)
        wd_s[...] = wd_ref[0].astype(_BF16)

        def block(g, _):
            x_copy(g).wait()

            @pl.when(g + X_RING - 1 < nused)
            def _():
                x_copy(g + X_RING - 1).start()

            @pl.when(g >= Y_RING)
            def _():
                y_copy(g - Y_RING).wait()

            xb = jnp.concatenate(
                [p.astype(_BF16) for p in _load_packed(xbuf, lax.rem(g, X_RING) * T_M, T_M)], axis=1)
            _store_packed(ybuf, _swiglu(xb, wgu_s[...], wd_s[...]), lax.rem(g, Y_RING) * T_M)
            y_copy(g).start()
            return 0

        lax.fori_loop(b0, b0 + nb, block, 0)

    @pl.when(e == pl.num_programs(0) - 1)
    def _():
        for back in range(Y_RING, 0, -1):
            @pl.when(nused >= back)
            def _():
                y_copy(nused - back).wait()

        ybuf[pl.ds(0, T_M * PACK_ROWS), :] = jnp.zeros((T_M * PACK_ROWS, 128), _U32)

        def zero_copy(blk):
            return pltpu.make_async_copy(_token_rows(ybuf, 0, T_M), _token_rows(ys_ref, blk * T_M, T_M), ysem.at[0])

        def fill(blk, _):
            zero_copy(blk).start()
            return 0

        def drain(blk, _):
            zero_copy(blk).wait()
            return 0

        lax.fori_loop(nused, n_blocks, fill, 0)
        lax.fori_loop(nused, n_blocks, drain, 0)


def _experts(nblk, bstart, nused, xs, wg, wu, wd):
    n_slots = xs.shape[0] // PACK_ROWS
    n_exp, d, ff = wg.shape
    ring = lambda depth: pltpu.VMEM((depth * T_M * PACK_ROWS, 128), _U32)
    return pl.pallas_call(
        functools.partial(_expert_kernel, n_blocks=n_slots // T_M),
        out_shape=jax.ShapeDtypeStruct(xs.shape, _U32),
        grid_spec=pltpu.PrefetchScalarGridSpec(
            num_scalar_prefetch=3,
            grid=(n_exp,),
            in_specs=[pl.BlockSpec(memory_space=pl.ANY),
                      pl.BlockSpec((1, d, ff), lambda i, nbk, bst, nu: (i, 0, 0)),
                      pl.BlockSpec((1, d, ff), lambda i, nbk, bst, nu: (i, 0, 0)),
                      pl.BlockSpec((1, ff, d), lambda i, nbk, bst, nu: (i, 0, 0))],
            out_specs=pl.BlockSpec(memory_space=pl.ANY),
            scratch_shapes=[ring(X_RING), ring(Y_RING),
                            pltpu.SemaphoreType.DMA((X_RING,)), pltpu.SemaphoreType.DMA((Y_RING,)),
                            pltpu.VMEM((d, 2 * ff), _BF16), pltpu.VMEM((ff, d), _BF16)]),
        compiler_params=pltpu.CompilerParams(
            dimension_semantics=("arbitrary",), vmem_limit_bytes=VMEM_LIMIT),
        name="experts",
    )(nblk, bstart, nused, xs, wg, wu, wd)


def _combine_kernel(dest_ref, dest_next_ref, ys_ref, hp_ref, w_ref, x1_ref, mod_ref, wgus_ref, wds_ref,
                    gpost_ref, o_ref, gbuf, sem):
    i = pl.program_id(0)
    t_rows = hp_ref.shape[0] // PACK_ROWS
    tile_rows = TOP_K * t_rows

    def gather(d_ref, slot):
        def issue(t, _):
            for kk in range(TOP_K):
                _row_copy(ys_ref, d_ref[0, 0, t * TOP_K + kk], gbuf, slot * tile_rows + kk * t_rows + t,
                          sem.at[slot]).start(priority=kk % 2)
            return 0

        lax.fori_loop(0, t_rows, issue, 0)

    slot = lax.rem(i, 2)

    @pl.when(i == 0)
    def _():
        gather(dest_ref, 0)

    @pl.when(i + 1 < pl.num_programs(0))
    def _():
        gather(dest_next_ref, 1 - slot)

    shared = _swiglu(_load_packed_bf16(hp_ref, t_rows), wgus_ref[...], wds_ref[...])

    def drain(t, _):
        for kk in range(TOP_K):
            _row_copy(ys_ref, 0, gbuf, 0, sem.at[slot]).wait()
        return 0

    lax.fori_loop(0, t_rows, drain, 0)

    w = w_ref[...]
    cols = None
    for kk in range(TOP_K):
        wk = w[:, kk:kk + 1]
        parts = [wk * p for p in _load_packed(gbuf, slot * tile_rows + kk * t_rows, t_rows)]
        cols = parts if cols is None else [a + b for a, b in zip(cols, parts)]
    y = jnp.concatenate(cols, axis=1) + shared
    gate_f = mod_ref[0, 5:6, :]
    o_ref[...] = x1_ref[...] + gate_f * (y * _rsqrt_mean_sq(y, -1) * gpost_ref[...])


def _combine(dest_tiles, ys, hp, w_tok, x1, mod, wgus, wds, gpost, tiles_per_batch):
    n, d = x1.shape
    t = T_ROW
    nt = n // t
    full = lambda a: pl.BlockSpec(a.shape, lambda i: (0,) * a.ndim)
    dest_spec = lambda shift: pl.BlockSpec(
        (1, 1, t * TOP_K), lambda i: (jnp.minimum(i + shift, nt - 1), 0, 0), memory_space=pltpu.SMEM)
    return pl.pallas_call(
        _combine_kernel,
        out_shape=jax.ShapeDtypeStruct((n, d), _F32),
        grid=(nt,),
        in_specs=[dest_spec(0), dest_spec(1),
                  pl.BlockSpec(memory_space=pl.ANY),
                  pl.BlockSpec((t * PACK_ROWS, 128), lambda i: (i, 0)),
                  pl.BlockSpec((t, TOP_K), lambda i: (i, 0)),
                  pl.BlockSpec((t, d), lambda i: (i, 0)),
                  pl.BlockSpec((1, N_MOD, d), lambda i: (i // tiles_per_batch, 0, 0)),
                  full(wgus), full(wds), full(gpost)],
        out_specs=pl.BlockSpec((t, d), lambda i: (i, 0)),
        scratch_shapes=[pltpu.VMEM((2 * TOP_K * t * PACK_ROWS, 128), _U32), pltpu.SemaphoreType.DMA((2,))],
        compiler_params=pltpu.CompilerParams(
            dimension_semantics=("arbitrary",), vmem_limit_bytes=VMEM_LIMIT),
        name="combine",
    )(dest_tiles, dest_tiles, ys, hp, w_tok, x1, mod, wgus, wds, gpost)


def _deinterleave(n):
    return np.concatenate([np.arange(0, n, 2), np.arange(1, n, 2)])


def _rope_tables(s, dim):
    rows = s // GRID_W
    row = jnp.broadcast_to(jnp.arange(rows, dtype=_F32)[:, None], (rows, GRID_W)).reshape(s)
    col = jnp.broadcast_to(jnp.arange(GRID_W, dtype=_F32)[None, :], (rows, GRID_W)).reshape(s)
    n_freq = dim // 4
    inv_freq = ROPE_THETA ** (-jnp.arange(n_freq, dtype=_F32) / n_freq)
    ang = jnp.concatenate([row[:, None] * inv_freq, col[:, None] * inv_freq], axis=-1)
    return jnp.cos(ang).T, jnp.sin(ang).T


def _in_proj_perm():
    perm = list(range(OFF_KROPE))
    perm += list(OFF_KROPE + _deinterleave(MLA_ROPE))
    for hd in range(GQA_HEADS):
        perm += list(OFF_QG + hd * GQA_HEAD_DIM + _deinterleave(GQA_HEAD_DIM))
    for hd in range(GQA_KV_HEADS):
        perm += list(OFF_KG + hd * GQA_HEAD_DIM + _deinterleave(GQA_HEAD_DIM))
    perm += list(range(OFF_VG, IN_COLS))
    return np.asarray(perm)


def _q_b_perm():
    per = MLA_NOPE + MLA_ROPE
    nope = [hd * per + j for hd in range(MLA_HEADS) for j in range(MLA_NOPE)]
    even = [hd * per + MLA_NOPE + 2 * i for hd in range(MLA_HEADS) for i in range(MLA_ROPE // 2)]
    odd = [hd * per + MLA_NOPE + 2 * i + 1 for hd in range(MLA_HEADS) for i in range(MLA_ROPE // 2)]
    return np.asarray(nope + even + odd)


def _kv_b_perm():
    per = MLA_NOPE + MLA_V
    kk = [hd * per + j for hd in range(MLA_HEADS) for j in range(MLA_NOPE)]
    vv = [hd * per + MLA_NOPE + j for hd in range(MLA_HEADS) for j in range(MLA_V)]
    return np.asarray(kk + vv)


def _layer(x, c, w_ada, b_ada, g_pre_mix, g_post_mix, g_pre_ffn, g_post_ffn, w_in, g_q_a, w_q_b,
           g_kv_a, w_kv_b, g_q_head, g_k_head, g_mla_out, g_gqa_out, w_out, w_router, b_router,
           w_gate_e, w_up_e, w_down_e, w_gate_s, w_up_s, w_down_s):
    b, s, d = x.shape
    n = b * s
    col = lambda g: g.reshape(-1, 1).astype(_F32)
    row = lambda g: g.reshape(1, -1).astype(_F32)

    mod = _ada_mod(c, w_ada, b_ada)

    winT = w_in[:, _in_proj_perm()].T.astype(_BF16)
    wqbT = w_q_b[:, _q_b_perm()].T.astype(_BF16)
    wkvbT = w_kv_b[:, _kv_b_perm()].T.astype(_BF16)
    perm_h = _deinterleave(GQA_HEAD_DIM)
    cosm, sinm = _rope_tables(s, MLA_ROPE)
    cosg, sing = _rope_tables(s, GQA_HEAD_DIM)

    qT, k, vT = _project(x, mod, row(g_pre_mix), winT, col(g_q_a), wqbT, col(g_kv_a), wkvbT,
                         col(g_q_head[perm_h]), col(g_k_head[perm_h]), cosm, sinm, cosg, sing)
    oT = _attention(qT, k, vT)

    wr_t = w_router.T.astype(_F32)
    wrh = wr_t.astype(_BF16)
    wrl = (wr_t - wrh.astype(_F32)).astype(_BF16)
    x1, hp, idx_t, w_t, rank_t, cnt = _post_attention(
        x, oT, mod, col(g_mla_out), col(g_gqa_out), w_out.T.astype(_BF16), row(g_post_mix),
        row(g_pre_ffn), wrh, wrl, col(b_router))

    counts = cnt[:, 0].astype(jnp.int32)
    blocks_e = (counts + T_M - 1) // T_M
    blk_end = jnp.cumsum(blocks_e)
    start = (blk_end - blocks_e) * T_M
    n_blocks = n * TOP_K // T_M + N_EXPERTS
    nused = blk_end[-1:].astype(jnp.int32)
    bstart = (blk_end - blocks_e).astype(jnp.int32)
    dest = _slots(idx_t, rank_t, start)
    dest_tiles = dest.T.reshape(n // T_ROW, 1, T_ROW * TOP_K)

    xs = _dispatch((start + counts).astype(jnp.int32), (blocks_e * T_M - counts).astype(jnp.int32), nused,
                   dest_tiles, hp, n_blocks * T_M)
    ys = _experts(blocks_e.astype(jnp.int32), bstart, nused, xs, w_gate_e, w_up_e, w_down_e)
    wgus = jnp.concatenate([w_gate_s, w_up_s], axis=1).astype(_BF16)
    out = _combine(dest_tiles, ys, hp, w_t.T, x1.reshape(n, d), mod, wgus, w_down_s.astype(_BF16),
                   row(g_post_ffn), s // T_ROW)
    return out.reshape(b, s, d)


def kernel(x, c, w_ada, b_ada, g_pre_mix, g_post_mix, g_pre_ffn, g_post_ffn, w_in, g_q_a, w_q_b, g_kv_a, w_kv_b, g_q_head, g_k_head, g_mla_out, g_gqa_out, w_out, w_router, b_router, w_gate_e, w_up_e, w_down_e, w_gate_s, w_up_s, w_down_s):
    depth = w_ada.shape[0]
    for l in range(depth):
        x = _layer(x, c, w_ada[l], b_ada[l], g_pre_mix[l], g_post_mix[l], g_pre_ffn[l], g_post_ffn[l],
                   w_in[l], g_q_a[l], w_q_b[l], g_kv_a[l], w_kv_b[l], g_q_head[l], g_k_head[l],
                   g_mla_out[l], g_gqa_out[l], w_out[l], w_router[l], b_router[l], w_gate_e[l],
                   w_up_e[l], w_down_e[l], w_gate_s[l], w_up_s[l], w_down_s[l])
    return x
```

```python
import functools

import jax
import jax.numpy as jnp
import numpy as np
from jax import lax
from jax.experimental import pallas as pl
from jax.experimental.pallas import tpu as pltpu

D_MODEL = 1024
GRID_W = 64
ROPE_THETA = 10000.0
NORM_EPS = 1e-6

MLA_HEADS = 8
MLA_Q_LORA = 256
MLA_KV_LORA = 128
MLA_NOPE = 64
MLA_ROPE = 32
MLA_V = 64

GQA_HEADS = 8
GQA_KV_HEADS = 2
GQA_HEAD_DIM = 64
GQA_GROUP = GQA_HEADS // GQA_KV_HEADS

N_HEADS = MLA_HEADS + GQA_HEADS
N_KV = MLA_HEADS + GQA_KV_HEADS
HEAD_PAD = 128
V_DIM = 64
V_ROWS = 80

N_EXPERTS = 256
TOP_K = 8
N_GROUPS = 8
GROUP_SIZE = N_EXPERTS // N_GROUPS
TOPK_GROUPS = 4
EXPERT_FF = 256
ROUTED_SCALE = 2.5
N_MOD = 6

OFF_QLAT = 0
OFF_KVLAT = OFF_QLAT + MLA_Q_LORA
OFF_KROPE = OFF_KVLAT + MLA_KV_LORA
OFF_QG = OFF_KROPE + MLA_ROPE
OFF_KG = OFF_QG + GQA_HEADS * GQA_HEAD_DIM
OFF_VG = OFF_KG + GQA_KV_HEADS * GQA_HEAD_DIM
IN_COLS = OFF_VG + GQA_KV_HEADS * GQA_HEAD_DIM

T_PROJ = 512
T_POST = 512
T_ROW = 256
T_M = 256
PACK_ROWS = 4
N_SCORE_SLOTS = 4
ATTN_GROUP = 16
X_RING = 4
Y_RING = 3
LOG2_E = 1.4426950408889634
VMEM_LIMIT = 56 * 1024 * 1024

_NT = (((1,), (1,)), ((), ()))
_F32 = jnp.float32
_BF16 = jnp.bfloat16
_U32 = jnp.uint32
_HI_MASK = np.uint32(0xFFFF0000)


def _rsqrt_mean_sq(v, axis):
    return lax.rsqrt(jnp.mean(v * v, axis=axis, keepdims=True) + NORM_EPS)


def _pack_pair(lo, hi):
    lo_b = pltpu.bitcast(lo.astype(_BF16).astype(_F32), _U32) >> 16
    hi_b = pltpu.bitcast(hi.astype(_BF16).astype(_F32), _U32) & _HI_MASK
    return lo_b | hi_b


def _unpack_pair(u):
    return pltpu.bitcast(u << 16, _F32), pltpu.bitcast(u & _HI_MASK, _F32)


def _token_rows(ref, first, count=1):
    return ref.at[pl.ds(pl.multiple_of(first * PACK_ROWS, PACK_ROWS), count * PACK_ROWS), :]


def _chunk_of_tokens(first, count, c):
    return pl.ds(first * PACK_ROWS + c, count, stride=PACK_ROWS)


def _store_packed(ref, val, first=0):
    rows = val.shape[0]
    for c in range(PACK_ROWS):
        ref[_chunk_of_tokens(first, rows, c), :] = _pack_pair(
            val[:, 256 * c:256 * c + 128], val[:, 256 * c + 128:256 * c + 256])


def _load_packed(ref, first, rows):
    parts = []
    for c in range(PACK_ROWS):
        parts += list(_unpack_pair(ref[_chunk_of_tokens(first, rows, c), :]))
    return parts


def _ada_kernel(c_ref, w_ref, b_ref, o_ref):
    c = c_ref[...]
    a = (c * jax.nn.sigmoid(c)).astype(_BF16)
    o_ref[...] = jnp.dot(a, w_ref[...].astype(_BF16), preferred_element_type=_F32) + b_ref[...]


def _ada_mod(c, w_ada, b_ada):
    b, d = c.shape
    cols = w_ada.shape[1]
    rows = 8
    tn = 1536
    c_pad = jnp.zeros((rows, d), _F32).at[:b].set(c)
    out = pl.pallas_call(
        _ada_kernel,
        out_shape=jax.ShapeDtypeStruct((rows, cols), _F32),
        grid=(cols // tn,),
        in_specs=[pl.BlockSpec((rows, d), lambda j: (0, 0)),
                  pl.BlockSpec((d, tn), lambda j: (0, j)),
                  pl.BlockSpec((1, tn), lambda j: (0, j))],
        out_specs=pl.BlockSpec((rows, tn), lambda j: (0, j)),
        compiler_params=pltpu.CompilerParams(vmem_limit_bytes=VMEM_LIMIT),
        name="ada_mod",
    )(c_pad, w_ada, b_ada.reshape(1, cols))
    return out[:b].reshape(b, N_MOD, d)


def _rope(x0, x1, cos, sin):
    return x0 * cos - x1 * sin, x0 * sin + x1 * cos


def _proj_kernel(x_ref, mod_ref, gpre_ref, win_ref, gqa_ref, wqb_ref, gkva_ref, wkvb_ref,
                 gqh_ref, gkh_ref, cosm_ref, sinm_ref, cosg_ref, sing_ref,
                 qT_ref, k_ref, vT_ref):
    x = x_ref[0]
    t = x.shape[0]
    shift = mod_ref[0, 0:1, :]
    scale = mod_ref[0, 1:2, :]
    h = x * _rsqrt_mean_sq(x, -1) * gpre_ref[...] * (1.0 + scale) + shift
    pT = lax.dot_general(win_ref[...], h.astype(_BF16), _NT, preferred_element_type=_F32)

    cm, sm = cosm_ref[...], sinm_ref[...]
    cg, sg = cosg_ref[...], sing_ref[...]
    half_m = MLA_ROPE // 2
    half_g = GQA_HEAD_DIM // 2
    pad_rows = lax.broadcasted_iota(jnp.int32, (V_ROWS - V_DIM, t), 0)
    ones_row = jnp.where(pad_rows == 0, 1.0, 0.0).astype(_BF16)

    ql = pT[OFF_QLAT:OFF_QLAT + MLA_Q_LORA]
    qn = (ql * _rsqrt_mean_sq(ql, 0) * gqa_ref[...]).astype(_BF16)
    qT = jnp.dot(wqb_ref[...], qn, preferred_element_type=_F32)
    sc_m = float((MLA_NOPE + MLA_ROPE) ** -0.5 * LOG2_E)
    n_nope = MLA_HEADS * MLA_NOPE
    n_half = MLA_HEADS * half_m
    q0, q1 = _rope(qT[n_nope:n_nope + n_half], qT[n_nope + n_half:],
                   jnp.tile(cm, (MLA_HEADS, 1)), jnp.tile(sm, (MLA_HEADS, 1)))
    zq_m = jnp.zeros((HEAD_PAD - MLA_NOPE - MLA_ROPE, t), _BF16)
    for hd in range(MLA_HEADS):
        qT_ref[0, hd, 0, 0:MLA_NOPE, :] = (qT[hd * MLA_NOPE:(hd + 1) * MLA_NOPE] * sc_m).astype(_BF16)
        qT_ref[0, hd, 0, MLA_NOPE:MLA_NOPE + half_m, :] = (q0[hd * half_m:(hd + 1) * half_m] * sc_m).astype(_BF16)
        qT_ref[0, hd, 0, MLA_NOPE + half_m:MLA_NOPE + MLA_ROPE, :] = (
            q1[hd * half_m:(hd + 1) * half_m] * sc_m).astype(_BF16)
        qT_ref[0, hd, 0, MLA_NOPE + MLA_ROPE:, :] = zq_m

    kl = pT[OFF_KVLAT:OFF_KVLAT + MLA_KV_LORA]
    kn = (kl * _rsqrt_mean_sq(kl, 0) * gkva_ref[...]).astype(_BF16)
    kvT = jnp.dot(wkvb_ref[...], kn, preferred_element_type=_F32)
    k0, k1 = _rope(pT[OFF_KROPE:OFF_KROPE + half_m], pT[OFF_KROPE + half_m:OFF_KROPE + MLA_ROPE], cm, sm)
    zk_m = jnp.zeros((HEAD_PAD - MLA_NOPE - MLA_ROPE, t), _F32)
    for hd in range(MLA_HEADS):
        kTh = jnp.concatenate([kvT[hd * MLA_NOPE:(hd + 1) * MLA_NOPE], k0, k1, zk_m], axis=0)
        k_ref[0, hd] = kTh.T.astype(_BF16)
        vT_ref[0, hd, 0, :V_DIM] = kvT[n_nope + hd * MLA_V:n_nope + (hd + 1) * MLA_V].astype(_BF16)
        vT_ref[0, hd, 0, V_DIM:] = ones_row

    sc_g = float(GQA_HEAD_DIM ** -0.5 * LOG2_E)
    zq_g = jnp.zeros((HEAD_PAD - GQA_HEAD_DIM, t), _BF16)
    for hd in range(GQA_HEADS):
        blk = pT[OFF_QG + hd * GQA_HEAD_DIM:OFF_QG + (hd + 1) * GQA_HEAD_DIM]
        bn = blk * _rsqrt_mean_sq(blk, 0) * gqh_ref[...]
        g0, g1 = _rope(bn[:half_g], bn[half_g:], cg, sg)
        qT_ref[0, MLA_HEADS + hd, 0, 0:half_g, :] = (g0 * sc_g).astype(_BF16)
        qT_ref[0, MLA_HEADS + hd, 0, half_g:GQA_HEAD_DIM, :] = (g1 * sc_g).astype(_BF16)
        qT_ref[0, MLA_HEADS + hd, 0, GQA_HEAD_DIM:, :] = zq_g

    zk_g = jnp.zeros((HEAD_PAD - GQA_HEAD_DIM, t), _F32)
    for hd in range(GQA_KV_HEADS):
        blk = pT[OFF_KG + hd * GQA_HEAD_DIM:OFF_KG + (hd + 1) * GQA_HEAD_DIM]
        bn = blk * _rsqrt_mean_sq(blk, 0) * gkh_ref[...]
        g0, g1 = _rope(bn[:half_g], bn[half_g:], cg, sg)
        kTh = jnp.concatenate([g0, g1, zk_g], axis=0)
        k_ref[0, MLA_HEADS + hd] = kTh.T.astype(_BF16)
        vT_ref[0, MLA_HEADS + hd, 0, :V_DIM] = pT[OFF_VG + hd * V_DIM:OFF_VG + (hd + 1) * V_DIM].astype(_BF16)
        vT_ref[0, MLA_HEADS + hd, 0, V_DIM:] = ones_row


def _project(x, mod, gpre, winT, gqa, wqbT, gkva, wkvbT, gqh, gkh, cosm, sinm, cosg, sing):
    b, s, d = x.shape
    t = T_PROJ
    nt = s // t
    full = lambda a: pl.BlockSpec(a.shape, lambda bi, i: (0,) * a.ndim)
    tab = lambda a: pl.BlockSpec((a.shape[0], t), lambda bi, i: (0, i))
    return pl.pallas_call(
        _proj_kernel,
        out_shape=(jax.ShapeDtypeStruct((b, N_HEADS, nt, HEAD_PAD, t), _BF16),
                   jax.ShapeDtypeStruct((b, N_KV, s, HEAD_PAD), _BF16),
                   jax.ShapeDtypeStruct((b, N_KV, nt, V_ROWS, t), _BF16)),
        grid=(b, nt),
        in_specs=[pl.BlockSpec((1, t, d), lambda bi, i: (bi, i, 0)),
                  pl.BlockSpec((1, N_MOD, d), lambda bi, i: (bi, 0, 0)),
                  full(gpre), full(winT), full(gqa), full(wqbT), full(gkva), full(wkvbT),
                  full(gqh), full(gkh), tab(cosm), tab(sinm), tab(cosg), tab(sing)],
        out_specs=(pl.BlockSpec((1, N_HEADS, 1, HEAD_PAD, t), lambda bi, i: (bi, 0, i, 0, 0)),
                   pl.BlockSpec((1, N_KV, t, HEAD_PAD), lambda bi, i: (bi, 0, i, 0)),
                   pl.BlockSpec((1, N_KV, 1, V_ROWS, t), lambda bi, i: (bi, 0, i, 0, 0))),
        compiler_params=pltpu.CompilerParams(
            dimension_semantics=("arbitrary", "arbitrary"), vmem_limit_bytes=VMEM_LIMIT),
        name="in_proj",
    )(x, mod, gpre, winT, gqa, wqbT, gkva, wkvbT, gqh, gkh, cosm, sinm, cosg, sing)


def _attn_kernel(qT_ref, k_ref, vT_ref, o_ref, *slots, n_tiles, n_chunks, tk, group_size):
    ns = len(slots)
    total = n_tiles * n_chunks
    tq = qT_ref.shape[-1]

    def scores(g, slot):
        tile, j = g // n_chunks, g % n_chunks
        kc = k_ref[0, 0, pl.ds(pl.multiple_of(j * tk, tk), tk), :]
        s = jnp.dot(kc, qT_ref[0, 0, tile], preferred_element_type=_F32)
        slots[slot][...] = s
        return jnp.max(s, axis=0, keepdims=True)

    def accumulate(g, slot, mc, carry, first=False, last=False):
        m, acc = carry
        if first is not False:
            m = jnp.where(first, -1e30, m)
            acc = jnp.where(first, 0.0, acc)
        m_new = jnp.maximum(m, mc)
        alpha = jnp.exp2(m - m_new)
        p = jnp.exp2(slots[slot][...] - m_new).astype(_BF16)
        acc = alpha * acc + jnp.dot(vT_ref[0, 0, g % n_chunks], p, preferred_element_type=_F32)
        if last is not False:
            o_ref[0, 0, g // n_chunks] = acc[:V_DIM] / acc[V_DIM:V_DIM + 1]
        return m_new, acc

    def group(jj, state):
        mcs, carry = state
        mcs = list(mcs)
        g0 = group_size * jj
        opens = (g0 % n_chunks) == 0
        for u in range(group_size):
            nxt = scores(g0 + u + 2, (u + 2) % ns)
            carry = accumulate(g0 + u, u % ns, mcs[u % ns], carry, first=opens if u == 0 else False,
                               last=(u == group_size - 1))
            mcs[(u + 2) % ns] = nxt
        return tuple(mcs), carry

    carry = (jnp.full((1, tq), -1e30, _F32), jnp.zeros((V_ROWS, tq), _F32))
    zero = jnp.zeros((1, tq), _F32)
    mcs = (scores(0, 0), scores(1, 1)) + (zero,) * (ns - 2)
    n_loop = (total - 2) // group_size
    mcs, carry = lax.fori_loop(0, n_loop, group, (mcs, carry))
    mcs = list(mcs)
    for g in range(n_loop * group_size, total):
        if g + 2 < total:
            mcs[(g + 2) % ns] = scores(g + 2, (g + 2) % ns)
        carry = accumulate(g, g % ns, mcs[g % ns], carry, first=(g % n_chunks == 0), last=(g == total - 1))


def _kv_index(h):
    return jnp.where(h < MLA_HEADS, h, MLA_HEADS + (h - MLA_HEADS) // GQA_GROUP)


def _attention(qT, k, vT):
    b, nh, n_tiles, dp, tq = qT.shape
    s = k.shape[2]
    n_chunks, tk = vT.shape[2], vT.shape[4]
    group_size = min(ATTN_GROUP, n_chunks)
    assert group_size % N_SCORE_SLOTS == 0 and n_chunks % group_size == 0 and n_tiles * n_chunks > group_size
    return pl.pallas_call(
        functools.partial(_attn_kernel, n_tiles=n_tiles, n_chunks=n_chunks, tk=tk, group_size=group_size),
        out_shape=jax.ShapeDtypeStruct((b, nh, n_tiles, V_DIM, tq), _F32),
        grid=(b, nh),
        in_specs=[pl.BlockSpec((1, 1, n_tiles, dp, tq), lambda bi, h: (bi, h, 0, 0, 0)),
                  pl.BlockSpec((1, 1, s, dp), lambda bi, h: (bi, _kv_index(h), 0, 0)),
                  pl.BlockSpec((1, 1, n_chunks, V_ROWS, tk), lambda bi, h: (bi, _kv_index(h), 0, 0, 0))],
        out_specs=pl.BlockSpec((1, 1, n_tiles, V_DIM, tq), lambda bi, h: (bi, h, 0, 0, 0)),
        scratch_shapes=[pltpu.VMEM((tk, tq), _F32) for _ in range(N_SCORE_SLOTS)],
        compiler_params=pltpu.CompilerParams(
            dimension_semantics=("arbitrary", "arbitrary"), vmem_limit_bytes=VMEM_LIMIT),
        name="attention",
    )(qT, k, vT)


def _first_argmax(v, rows, n):
    m = jnp.max(v, axis=0, keepdims=True)
    idx = jnp.min(jnp.where(v == m, rows, n), axis=0, keepdims=True)
    return m, idx


def _post_kernel(x_ref, o_ref, mod_ref, gmla_ref, ggqa_ref, wout_ref, gpost_ref, gffn_ref,
                 wrh_ref, wrl_ref, br_ref,
                 x1_ref, hp_ref, idx_ref, w_ref, rank_ref, cnt_ref, carry_ref):
    first = jnp.logical_and(pl.program_id(0) == 0, pl.program_id(1) == 0)

    @pl.when(first)
    def _():
        carry_ref[...] = jnp.zeros_like(carry_ref)

    x = x_ref[0]
    t = x.shape[0]
    gate_m = mod_ref[0, 2:3, :]
    shift_f = mod_ref[0, 3:4, :]
    scale_f = mod_ref[0, 4:5, :]

    o = o_ref[0, :, 0].reshape(N_HEADS * V_DIM, t)
    half = MLA_HEADS * V_DIM
    om, og = o[:half], o[half:]
    mixedT = jnp.concatenate([om * _rsqrt_mean_sq(om, 0) * gmla_ref[...],
                              og * _rsqrt_mean_sq(og, 0) * ggqa_ref[...]], axis=0).astype(_BF16)
    y = jnp.dot(wout_ref[...], mixedT, preferred_element_type=_F32).T
    x1 = x + gate_m * (y * _rsqrt_mean_sq(y, -1) * gpost_ref[...])
    x1_ref[0] = x1
    h2 = x1 * _rsqrt_mean_sq(x1, -1) * gffn_ref[...] * (1.0 + scale_f) + shift_f

    _store_packed(hp_ref, h2)

    hh = h2.astype(_BF16)
    hl = (h2 - hh.astype(_F32)).astype(_BF16)
    wrh = wrh_ref[...]
    logits = (lax.dot_general(wrh, hh, _NT, preferred_element_type=_F32)
              + lax.dot_general(wrh, hl, _NT, preferred_element_type=_F32)
              + lax.dot_general(wrl_ref[...], hh, _NT, preferred_element_type=_F32))
    scores = jax.nn.sigmoid(logits)
    biased = scores + br_ref[...]
    neg = -jnp.inf

    rows_g = lax.broadcasted_iota(jnp.int32, (GROUP_SIZE, t), 0)
    gscores = []
    for g in range(N_GROUPS):
        v = biased[g * GROUP_SIZE:(g + 1) * GROUP_SIZE]
        m1, i1 = _first_argmax(v, rows_g, GROUP_SIZE)
        m2 = jnp.max(jnp.where(rows_g == i1, neg, v), axis=0, keepdims=True)
        gscores.append(m1 + m2)
    cur = jnp.concatenate(gscores, axis=0)
    rows_8 = lax.broadcasted_iota(jnp.int32, (N_GROUPS, t), 0)
    gsel = jnp.zeros((N_GROUPS, t), _F32)
    for _ in range(TOPK_GROUPS):
        _, gi = _first_argmax(cur, rows_8, N_GROUPS)
        hit = rows_8 == gi
        gsel = jnp.where(hit, 1.0, gsel)
        cur = jnp.where(hit, neg, cur)
    emask = jnp.concatenate(
        [jnp.broadcast_to(gsel[g:g + 1], (GROUP_SIZE, t)) for g in range(N_GROUPS)], axis=0) > 0.5
    masked = jnp.where(emask, biased, neg)

    rows_e = lax.broadcasted_iota(jnp.int32, (N_EXPERTS, t), 0)
    idxs, ws = [], []
    for _ in range(TOP_K):
        _, ei = _first_argmax(masked, rows_e, N_EXPERTS)
        hit = rows_e == ei
        idxs.append(ei)
        ws.append(jnp.sum(jnp.where(hit, scores, 0.0), axis=0, keepdims=True))
        masked = jnp.where(hit, neg, masked)
    wsum = ws[0]
    for wk in ws[1:]:
        wsum = wsum + wk

    onehot = jnp.zeros((N_EXPERTS, t), _F32)
    for ei in idxs:
        onehot = onehot + jnp.where(rows_e == ei, 1.0, 0.0)
    upper = (lax.broadcasted_iota(jnp.int32, (t, t), 0) < lax.broadcasted_iota(jnp.int32, (t, t), 1))
    before = jnp.dot(onehot.astype(_BF16), jnp.where(upper, 1.0, 0.0).astype(_BF16),
                     preferred_element_type=_F32) + carry_ref[...]
    for kk in range(TOP_K):
        idx_ref[kk:kk + 1, :] = idxs[kk]
        w_ref[kk:kk + 1, :] = ws[kk] / wsum * ROUTED_SCALE
        rank_ref[kk:kk + 1, :] = jnp.sum(
            jnp.where(rows_e == idxs[kk], before, 0.0), axis=0, keepdims=True).astype(jnp.int32)
    total = carry_ref[...] + jnp.sum(onehot, axis=1, keepdims=True)
    carry_ref[...] = total
    cnt_ref[...] = jnp.broadcast_to(total, cnt_ref.shape)


def _post_attention(x, oT, mod, gmla, ggqa, woutT, gpost, gffn, wrh, wrl, br):
    b, s, d = x.shape
    n = b * s
    t = T_POST
    nt = s // t
    assert oT.shape[2:] == (nt, V_DIM, t)
    full = lambda a: pl.BlockSpec(a.shape, lambda bi, i: (0,) * a.ndim)
    tok = lambda bi, i: (0, bi * nt + i)
    return pl.pallas_call(
        _post_kernel,
        out_shape=(jax.ShapeDtypeStruct((b, s, d), _F32),
                   jax.ShapeDtypeStruct((n * PACK_ROWS, 128), _U32),
                   jax.ShapeDtypeStruct((TOP_K, n), jnp.int32),
                   jax.ShapeDtypeStruct((TOP_K, n), _F32),
                   jax.ShapeDtypeStruct((TOP_K, n), jnp.int32),
                   jax.ShapeDtypeStruct((N_EXPERTS, 128), _F32)),
        grid=(b, nt),
        in_specs=[pl.BlockSpec((1, t, d), lambda bi, i: (bi, i, 0)),
                  pl.BlockSpec((1, N_HEADS, 1, V_DIM, t), lambda bi, i: (bi, 0, i, 0, 0)),
                  pl.BlockSpec((1, N_MOD, d), lambda bi, i: (bi, 0, 0)),
                  full(gmla), full(ggqa), full(woutT), full(gpost), full(gffn),
                  full(wrh), full(wrl), full(br)],
        out_specs=(pl.BlockSpec((1, t, d), lambda bi, i: (bi, i, 0)),
                   pl.BlockSpec((t * PACK_ROWS, 128), lambda bi, i: (bi * nt + i, 0)),
                   pl.BlockSpec((TOP_K, t), tok),
                   pl.BlockSpec((TOP_K, t), tok),
                   pl.BlockSpec((TOP_K, t), tok),
                   pl.BlockSpec((N_EXPERTS, 128), lambda bi, i: (0, 0))),
        scratch_shapes=[pltpu.VMEM((N_EXPERTS, 1), _F32)],
        compiler_params=pltpu.CompilerParams(
            dimension_semantics=("arbitrary", "arbitrary"), vmem_limit_bytes=VMEM_LIMIT),
        name="out_proj_router",
    )(x, oT, mod, gmla, ggqa, woutT, gpost, gffn, wrh, wrl, br)


def _slot_kernel(idx_ref, rank_ref, start_ref, dest_ref):
    t = idx_ref.shape[1]
    rows_e = lax.broadcasted_iota(jnp.int32, (N_EXPERTS, t), 0)
    start = start_ref[...]
    for kk in range(TOP_K):
        base = jnp.sum(jnp.where(rows_e == idx_ref[kk:kk + 1, :], start, 0.0), axis=0, keepdims=True)
        dest_ref[kk:kk + 1, :] = rank_ref[kk:kk + 1, :] + base.astype(jnp.int32)


def _slots(idx_t, rank_t, start):
    n = idx_t.shape[1]
    t = T_POST
    return pl.pallas_call(
        _slot_kernel,
        out_shape=jax.ShapeDtypeStruct((TOP_K, n), jnp.int32),
        grid=(n // t,),
        in_specs=[pl.BlockSpec((TOP_K, t), lambda i: (0, i)),
                  pl.BlockSpec((TOP_K, t), lambda i: (0, i)),
                  pl.BlockSpec((N_EXPERTS, 1), lambda i: (0, 0))],
        out_specs=pl.BlockSpec((TOP_K, t), lambda i: (0, i)),
        compiler_params=pltpu.CompilerParams(dimension_semantics=("arbitrary",)),
        name="slot_index",
    )(idx_t, rank_t, start.astype(_F32).reshape(N_EXPERTS, 1))


def _row_copy(src, src_row, dst, dst_row, sem):
    return pltpu.make_async_copy(_token_rows(src, src_row), _token_rows(dst, dst_row), sem)


def _dispatch_kernel(pad_start_ref, pad_len_ref, nused_ref, dest_ref, h_ref, xs_ref, zbuf, sem, zsem, *,
                     n_blocks):
    t_rows = h_ref.shape[0] // PACK_ROWS

    def issue(t, _):
        for kk in range(TOP_K):
            _row_copy(h_ref, t, xs_ref, dest_ref[0, 0, t * TOP_K + kk], sem).start(priority=kk % 2)
        return 0

    lax.fori_loop(0, t_rows, issue, 0)

    def drain(t, _):
        for kk in range(TOP_K):
            _row_copy(h_ref, 0, xs_ref, 0, sem).wait()
        return 0

    lax.fori_loop(0, t_rows, drain, 0)

    @pl.when(pl.program_id(0) == pl.num_programs(0) - 1)
    def _():
        zbuf[...] = jnp.zeros_like(zbuf)

        def zero_copy(first, count):
            return pltpu.make_async_copy(_token_rows(zbuf, 0, count), _token_rows(xs_ref, first, count), zsem)

        def sweep(act):
            def expert_pad(e, _):
                first = pad_start_ref[e]
                for piece in [T_M >> sh for sh in range(1, T_M.bit_length())]:
                    has = pad_len_ref[e] & piece

                    @pl.when(has != 0)
                    def _():
                        act(zero_copy(first, piece))

                    first = first + has
                return 0

            lax.fori_loop(0, N_EXPERTS, expert_pad, 0)

            def spare_block(blk, _):
                act(zero_copy(blk * T_M, T_M))
                return 0

            lax.fori_loop(nused_ref[0], n_blocks, spare_block, 0)

        sweep(lambda c: c.start())
        sweep(lambda c: c.wait())


def _dispatch(pad_start, pad_len, nused, dest_tiles, hp, n_slots):
    n = hp.shape[0] // PACK_ROWS
    t = T_ROW
    return pl.pallas_call(
        functools.partial(_dispatch_kernel, n_blocks=n_slots // T_M),
        out_shape=jax.ShapeDtypeStruct((n_slots * PACK_ROWS, 128), _U32),
        grid_spec=pltpu.PrefetchScalarGridSpec(
            num_scalar_prefetch=3,
            grid=(n // t,),
            in_specs=[pl.BlockSpec((1, 1, t * TOP_K), lambda i, ps, pn, nu: (i, 0, 0), memory_space=pltpu.SMEM),
                      pl.BlockSpec((t * PACK_ROWS, 128), lambda i, ps, pn, nu: (i, 0))],
            out_specs=pl.BlockSpec(memory_space=pl.ANY),
            scratch_shapes=[pltpu.VMEM((T_M * PACK_ROWS, 128), _U32),
                            pltpu.SemaphoreType.DMA(()), pltpu.SemaphoreType.DMA(())]),
        compiler_params=pltpu.CompilerParams(
            dimension_semantics=("arbitrary",), vmem_limit_bytes=VMEM_LIMIT),
        name="dispatch",
    )(pad_start, pad_len, nused, dest_tiles, hp)


def _swiglu(xb, wgu, wd):
    gu = jnp.dot(xb, wgu, preferred_element_type=_F32)
    g, u = gu[:, :EXPERT_FF], gu[:, EXPERT_FF:]
    mid = (g * jax.nn.sigmoid(g) * u).astype(_BF16)
    return jnp.dot(mid, wd, preferred_element_type=_F32)


def _load_packed_bf16(ref, rows):
    return jnp.concatenate([p.astype(_BF16) for p in _load_packed(ref, 0, rows)], axis=1)


def _expert_kernel(nblk_ref, bstart_ref, nused_ref, xs_ref, wg_ref, wu_ref, wd_ref, ys_ref,
                   xbuf, ybuf, xsem, ysem, wgu_s, wd_s, *, n_blocks):
    e = pl.program_id(0)
    nb = nblk_ref[e]
    b0 = bstart_ref[e]
    nused = nused_ref[0]

    def x_copy(blk):
        slot = lax.rem(blk, X_RING)
        return pltpu.make_async_copy(_token_rows(xs_ref, blk * T_M, T_M), _token_rows(xbuf, slot * T_M, T_M),
                                     xsem.at[slot])

    def y_copy(blk):
        slot = lax.rem(blk, Y_RING)
        return pltpu.make_async_copy(_token_rows(ybuf, slot * T_M, T_M), _token_rows(ys_ref, blk * T_M, T_M),
                                     ysem.at[slot])

    @pl.when(e == 0)
    def _():
        for g in range(X_RING - 1):
            @pl.when(g < nused)
            def _():
                x_copy(g).start()

    @pl.when(nb > 0)
    def _():
        wgu_s[:, :EXPERT_FF] = wg_ref[0].astype(_BF16)
        wgu_s[:, EXPERT_FF:] = wu_ref[0].astype(_BF16)
        wd_s[...] = wd_ref[0].astype(_BF16)

        def block(g, _):
            x_copy(g).wait()

            @pl.when(g + X_RING - 1 < nused)
            def _():
                x_copy(g + X_RING - 1).start()

            @pl.when(g >= Y_RING)
            def _():
                y_copy(g - Y_RING).wait()

            xb = jnp.concatenate(
                [p.astype(_BF16) for p in _load_packed(xbuf, lax.rem(g, X_RING) * T_M, T_M)], axis=1)
            _store_packed(ybuf, _swiglu(xb, wgu_s[...], wd_s[...]), lax.rem(g, Y_RING) * T_M)
            y_copy(g).start()
            return 0

        lax.fori_loop(b0, b0 + nb, block, 0)

    @pl.when(e == pl.num_programs(0) - 1)
    def _():
        for back in range(Y_RING, 0, -1):
            @pl.when(nused >= back)
            def _():
                y_copy(nused - back).wait()

        ybuf[pl.ds(0, T_M * PACK_ROWS), :] = jnp.zeros((T_M * PACK_ROWS, 128), _U32)

        def zero_copy(blk):
            return pltpu.make_async_copy(_token_rows(ybuf, 0, T_M), _token_rows(ys_ref, blk * T_M, T_M), ysem.at[0])

        def fill(blk, _):
            zero_copy(blk).start()
            return 0

        def drain(blk, _):
            zero_copy(blk).wait()
            return 0

        lax.fori_loop(nused, n_blocks, fill, 0)
        lax.fori_loop(nused, n_blocks, drain, 0)


def _experts(nblk, bstart, nused, xs, wg, wu, wd):
    n_slots = xs.shape[0] // PACK_ROWS
    n_exp, d, ff = wg.shape
    ring = lambda depth: pltpu.VMEM((depth * T_M * PACK_ROWS, 128), _U32)
    return pl.pallas_call(
        functools.partial(_expert_kernel, n_blocks=n_slots // T_M),
        out_shape=jax.ShapeDtypeStruct(xs.shape, _U32),
        grid_spec=pltpu.PrefetchScalarGridSpec(
            num_scalar_prefetch=3,
            grid=(n_exp,),
            in_specs=[pl.BlockSpec(memory_space=pl.ANY),
                      pl.BlockSpec((1, d, ff), lambda i, nbk, bst, nu: (i, 0, 0)),
                      pl.BlockSpec((1, d, ff), lambda i, nbk, bst, nu: (i, 0, 0)),
                      pl.BlockSpec((1, ff, d), lambda i, nbk, bst, nu: (i, 0, 0))],
            out_specs=pl.BlockSpec(memory_space=pl.ANY),
            scratch_shapes=[ring(X_RING), ring(Y_RING),
                            pltpu.SemaphoreType.DMA((X_RING,)), pltpu.SemaphoreType.DMA((Y_RING,)),
                            pltpu.VMEM((d, 2 * ff), _BF16), pltpu.VMEM((ff, d), _BF16)]),
        compiler_params=pltpu.CompilerParams(
            dimension_semantics=("arbitrary",), vmem_limit_bytes=VMEM_LIMIT),
        name="experts",
    )(nblk, bstart, nused, xs, wg, wu, wd)


def _regroup_kernel(dest_ref, ys_ref, g_ref, sem):
    i = pl.program_id(0)
    t_rows = dest_ref.shape[2] // TOP_K
    base = i * TOP_K * t_rows

    def issue(t, _):
        for kk in range(TOP_K):
            _row_copy(ys_ref, dest_ref[0, 0, t * TOP_K + kk], g_ref, base + kk * t_rows + t, sem).start(
                priority=kk % 2)
        return 0

    lax.fori_loop(0, t_rows, issue, 0)

    def drain(t, _):
        for kk in range(TOP_K):
            _row_copy(ys_ref, 0, g_ref, 0, sem).wait()
        return 0

    lax.fori_loop(0, t_rows, drain, 0)


def _regroup(dest_tiles, ys):
    nt, _, per_tile = dest_tiles.shape
    return pl.pallas_call(
        _regroup_kernel,
        out_shape=jax.ShapeDtypeStruct((nt * per_tile * PACK_ROWS, 128), _U32),
        grid=(nt,),
        in_specs=[pl.BlockSpec((1, 1, per_tile), lambda i: (i, 0, 0), memory_space=pltpu.SMEM),
                  pl.BlockSpec(memory_space=pl.ANY)],
        out_specs=pl.BlockSpec(memory_space=pl.ANY),
        scratch_shapes=[pltpu.SemaphoreType.DMA(())],
        compiler_params=pltpu.CompilerParams(dimension_semantics=("arbitrary",)),
        name="regroup",
    )(dest_tiles, ys)


def _combine_kernel(g_ref, hp_ref, w_ref, x1_ref, mod_ref, wgus_ref, wds_ref, gpost_ref, o_ref):
    t_rows = hp_ref.shape[0] // PACK_ROWS
    shared = _swiglu(_load_packed_bf16(hp_ref, t_rows), wgus_ref[...], wds_ref[...])
    w = w_ref[...]
    cols = None
    for kk in range(TOP_K):
        wk = w[:, kk:kk + 1]
        parts = [wk * p for p in _load_packed(g_ref, kk * t_rows, t_rows)]
        cols = parts if cols is None else [a + b for a, b in zip(cols, parts)]
    y = jnp.concatenate(cols, axis=1) + shared
    gate_f = mod_ref[0, 5:6, :]
    o_ref[...] = x1_ref[...] + gate_f * (y * _rsqrt_mean_sq(y, -1) * gpost_ref[...])


def _combine(g, hp, w_tok, x1, mod, wgus, wds, gpost, tiles_per_batch):
    n, d = x1.shape
    t = T_ROW
    full = lambda a: pl.BlockSpec(a.shape, lambda i: (0,) * a.ndim)
    return pl.pallas_call(
        _combine_kernel,
        out_shape=jax.ShapeDtypeStruct((n, d), _F32),
        grid=(n // t,),
        in_specs=[pl.BlockSpec((TOP_K * t * PACK_ROWS, 128), lambda i: (i, 0)),
                  pl.BlockSpec((t * PACK_ROWS, 128), lambda i: (i, 0)),
                  pl.BlockSpec((t, TOP_K), lambda i: (i, 0)),
                  pl.BlockSpec((t, d), lambda i: (i, 0)),
                  pl.BlockSpec((1, N_MOD, d), lambda i: (i // tiles_per_batch, 0, 0)),
                  full(wgus), full(wds), full(gpost)],
        out_specs=pl.BlockSpec((t, d), lambda i: (i, 0)),
        compiler_params=pltpu.CompilerParams(
            dimension_semantics=("arbitrary",), vmem_limit_bytes=VMEM_LIMIT),
        name="combine",
    )(g, hp, w_tok, x1, mod, wgus, wds, gpost)


def _deinterleave(n):
    return np.concatenate([np.arange(0, n, 2), np.arange(1, n, 2)])


def _rope_tables(s, dim):
    rows = s // GRID_W
    row = jnp.broadcast_to(jnp.arange(rows, dtype=_F32)[:, None], (rows, GRID_W)).reshape(s)
    col = jnp.broadcast_to(jnp.arange(GRID_W, dtype=_F32)[None, :], (rows, GRID_W)).reshape(s)
    n_freq = dim // 4
    inv_freq = ROPE_THETA ** (-jnp.arange(n_freq, dtype=_F32) / n_freq)
    ang = jnp.concatenate([row[:, None] * inv_freq, col[:, None] * inv_freq], axis=-1)
    return jnp.cos(ang).T, jnp.sin(ang).T


def _in_proj_perm():
    perm = list(range(OFF_KROPE))
    perm += list(OFF_KROPE + _deinterleave(MLA_ROPE))
    for hd in range(GQA_HEADS):
        perm += list(OFF_QG + hd * GQA_HEAD_DIM + _deinterleave(GQA_HEAD_DIM))
    for hd in range(GQA_KV_HEADS):
        perm += list(OFF_KG + hd * GQA_HEAD_DIM + _deinterleave(GQA_HEAD_DIM))
    perm += list(range(OFF_VG, IN_COLS))
    return np.asarray(perm)


def _q_b_perm():
    per = MLA_NOPE + MLA_ROPE
    nope = [hd * per + j for hd in range(MLA_HEADS) for j in range(MLA_NOPE)]
    even = [hd * per + MLA_NOPE + 2 * i for hd in range(MLA_HEADS) for i in range(MLA_ROPE // 2)]
    odd = [hd * per + MLA_NOPE + 2 * i + 1 for hd in range(MLA_HEADS) for i in range(MLA_ROPE // 2)]
    return np.asarray(nope + even + odd)


def _kv_b_perm():
    per = MLA_NOPE + MLA_V
    kk = [hd * per + j for hd in range(MLA_HEADS) for j in range(MLA_NOPE)]
    vv = [hd * per + MLA_NOPE + j for hd in range(MLA_HEADS) for j in range(MLA_V)]
    return np.asarray(kk + vv)


def _layer(x, c, w_ada, b_ada, g_pre_mix, g_post_mix, g_pre_ffn, g_post_ffn, w_in, g_q_a, w_q_b,
           g_kv_a, w_kv_b, g_q_head, g_k_head, g_mla_out, g_gqa_out, w_out, w_router, b_router,
           w_gate_e, w_up_e, w_down_e, w_gate_s, w_up_s, w_down_s):
    b, s, d = x.shape
    n = b * s
    col = lambda g: g.reshape(-1, 1).astype(_F32)
    row = lambda g: g.reshape(1, -1).astype(_F32)

    mod = _ada_mod(c, w_ada, b_ada)

    winT = w_in[:, _in_proj_perm()].T.astype(_BF16)
    wqbT = w_q_b[:, _q_b_perm()].T.astype(_BF16)
    wkvbT = w_kv_b[:, _kv_b_perm()].T.astype(_BF16)
    perm_h = _deinterleave(GQA_HEAD_DIM)
    cosm, sinm = _rope_tables(s, MLA_ROPE)
    cosg, sing = _rope_tables(s, GQA_HEAD_DIM)

    qT, k, vT = _project(x, mod, row(g_pre_mix), winT, col(g_q_a), wqbT, col(g_kv_a), wkvbT,
                         col(g_q_head[perm_h]), col(g_k_head[perm_h]), cosm, sinm, cosg, sing)
    oT = _attention(qT, k, vT)

    wr_t = w_router.T.astype(_F32)
    wrh = wr_t.astype(_BF16)
    wrl = (wr_t - wrh.astype(_F32)).astype(_BF16)
    x1, hp, idx_t, w_t, rank_t, cnt = _post_attention(
        x, oT, mod, col(g_mla_out), col(g_gqa_out), w_out.T.astype(_BF16), row(g_post_mix),
        row(g_pre_ffn), wrh, wrl, col(b_router))

    counts = cnt[:, 0].astype(jnp.int32)
    blocks_e = (counts + T_M - 1) // T_M
    blk_end = jnp.cumsum(blocks_e)
    start = (blk_end - blocks_e) * T_M
    n_blocks = n * TOP_K // T_M + N_EXPERTS
    nused = blk_end[-1:].astype(jnp.int32)
    bstart = (blk_end - blocks_e).astype(jnp.int32)
    dest = _slots(idx_t, rank_t, start)
    dest_tiles = dest.T.reshape(n // T_ROW, 1, T_ROW * TOP_K)

    xs = _dispatch((start + counts).astype(jnp.int32), (blocks_e * T_M - counts).astype(jnp.int32), nused,
                   dest_tiles, hp, n_blocks * T_M)
    ys = _experts(blocks_e.astype(jnp.int32), bstart, nused, xs, w_gate_e, w_up_e, w_down_e)
    wgus = jnp.concatenate([w_gate_s, w_up_s], axis=1).astype(_BF16)
    out = _combine(_regroup(dest_tiles, ys), hp, w_t.T, x1.reshape(n, d), mod, wgus, w_down_s.astype(_BF16),
                   row(g_post_ffn), s // T_ROW)
    return out.reshape(b, s, d)


def kernel(x, c, w_ada, b_ada, g_pre_mix, g_post_mix, g_pre_ffn, g_post_ffn, w_in, g_q_a, w_q_b, g_kv_a, w_kv_b, g_q_head, g_k_head, g_mla_out, g_gqa_out, w_out, w_router, b_router, w_gate_e, w_up_e, w_down_e, w_gate_s, w_up_s, w_down_s):
    depth = w_ada.shape[0]
    for l in range(depth):
        x = _layer(x, c, w_ada[l], b_ada[l], g_pre_mix[l], g_post_mix[l], g_pre_ffn[l], g_post_ffn[l],
                   w_in[l], g_q_a[l], w_q_b[l], g_kv_a[l], w_kv_b[l], g_q_head[l], g_k_head[l],
                   g_mla_out[l], g_gqa_out[l], w_out[l], w_router[l], b_router[l], w_gate_e[l],
                   w_up_e[l], w_down_e[l], w_gate_s[l], w_up_s[l], w_down_s[l])
    return x
```

```python
import functools

import jax
import jax.numpy as jnp
import numpy as np
from jax import lax
from jax.experimental import pallas as pl
from jax.experimental.pallas import tpu as pltpu

D_MODEL = 1024
GRID_W = 64
ROPE_THETA = 10000.0
NORM_EPS = 1e-6

MLA_HEADS = 8
MLA_Q_LORA = 256
MLA_KV_LORA = 128
MLA_NOPE = 64
MLA_ROPE = 32
MLA_V = 64

GQA_HEADS = 8
GQA_KV_HEADS = 2
GQA_HEAD_DIM = 64
GQA_GROUP = GQA_HEADS // GQA_KV_HEADS

LANES = 128
SUBLANES = 8
VMEM_BYTES = 64 * 1024 * 1024

N_HEADS = MLA_HEADS + GQA_HEADS
N_KV = MLA_HEADS + GQA_KV_HEADS
HEAD_PAD = LANES
V_DIM = 64
V_ROWS = 80

N_EXPERTS = 256
TOP_K = 8
N_GROUPS = 8
GROUP_SIZE = N_EXPERTS // N_GROUPS
TOPK_GROUPS = 4
EXPERT_FF = 256
ROUTED_SCALE = 2.5
N_MOD = 6

OFF_QLAT = 0
OFF_KVLAT = OFF_QLAT + MLA_Q_LORA
OFF_KROPE = OFF_KVLAT + MLA_KV_LORA
OFF_QG = OFF_KROPE + MLA_ROPE
OFF_KG = OFF_QG + GQA_HEADS * GQA_HEAD_DIM
OFF_VG = OFF_KG + GQA_KV_HEADS * GQA_HEAD_DIM
IN_COLS = OFF_VG + GQA_KV_HEADS * GQA_HEAD_DIM

T_PROJ = 512
T_POST = 512
T_ROW = 256
T_M = 256
T_ADA = 1536
PACK_ROWS = D_MODEL // (2 * LANES)
N_SCORE_SLOTS = 4
ATTN_GROUP = 16
X_RING = 4
Y_RING = 3
GATHER_STRIDE = TOP_K + 1
LOG2_E = 1.4426950408889634
VMEM_LIMIT = VMEM_BYTES * 7 // 8

_NT = (((1,), (1,)), ((), ()))
_F32 = jnp.float32
_BF16 = jnp.bfloat16
_U32 = jnp.uint32
_HI_MASK = np.uint32(0xFFFF0000)


def _rsqrt_mean_sq(v, axis):
    return lax.rsqrt(jnp.mean(v * v, axis=axis, keepdims=True) + NORM_EPS)


def _pack_pair(lo, hi):
    lo_b = pltpu.bitcast(lo.astype(_BF16).astype(_F32), _U32) >> 16
    hi_b = pltpu.bitcast(hi.astype(_BF16).astype(_F32), _U32) & _HI_MASK
    return lo_b | hi_b


def _unpack_pair(u):
    return pltpu.bitcast(u << 16, _F32), pltpu.bitcast(u & _HI_MASK, _F32)


def _token_rows(ref, first, count=1):
    return ref.at[pl.ds(pl.multiple_of(first * PACK_ROWS, PACK_ROWS), count * PACK_ROWS), :]


def _chunk_of_tokens(first, count, c, token_stride=1):
    return pl.ds(first * PACK_ROWS + c, count, stride=token_stride * PACK_ROWS)


def _store_packed(ref, val, first=0):
    rows = val.shape[0]
    for c in range(PACK_ROWS):
        lo = 2 * c * LANES
        ref[_chunk_of_tokens(first, rows, c), :] = _pack_pair(val[:, lo:lo + LANES], val[:, lo + LANES:lo + 2 * LANES])


def _load_packed(ref, first, rows, token_stride=1):
    parts = []
    for c in range(PACK_ROWS):
        parts += list(_unpack_pair(ref[_chunk_of_tokens(first, rows, c, token_stride), :]))
    return parts


def _ada_kernel(c_ref, w_ref, b_ref, o_ref):
    c = c_ref[...]
    a = (c * jax.nn.sigmoid(c)).astype(_BF16)
    o_ref[...] = jnp.dot(a, w_ref[...].astype(_BF16), preferred_element_type=_F32) + b_ref[...]


def _ada_mod(c, w_ada, b_ada):
    b, d = c.shape
    cols = w_ada.shape[1]
    rows = SUBLANES
    tn = T_ADA
    c_pad = jnp.zeros((rows, d), _F32).at[:b].set(c)
    out = pl.pallas_call(
        _ada_kernel,
        out_shape=jax.ShapeDtypeStruct((rows, cols), _F32),
        grid=(cols // tn,),
        in_specs=[pl.BlockSpec((rows, d), lambda j: (0, 0)),
                  pl.BlockSpec((d, tn), lambda j: (0, j)),
                  pl.BlockSpec((1, tn), lambda j: (0, j))],
        out_specs=pl.BlockSpec((rows, tn), lambda j: (0, j)),
        compiler_params=pltpu.CompilerParams(vmem_limit_bytes=VMEM_LIMIT),
        name="ada_mod",
    )(c_pad, w_ada, b_ada.reshape(1, cols))
    return out[:b].reshape(b, N_MOD, d)


def _rope(x0, x1, cos, sin):
    return x0 * cos - x1 * sin, x0 * sin + x1 * cos


def _proj_kernel(x_ref, mod_ref, gpre_ref, win_ref, gqa_ref, wqb_ref, gkva_ref, wkvb_ref,
                 gqh_ref, gkh_ref, cosm_ref, sinm_ref, cosg_ref, sing_ref,
                 qT_ref, k_ref, vT_ref):
    x = x_ref[0]
    t = x.shape[0]
    shift = mod_ref[0, 0:1, :]
    scale = mod_ref[0, 1:2, :]
    h = x * _rsqrt_mean_sq(x, -1) * gpre_ref[...] * (1.0 + scale) + shift
    pT = lax.dot_general(win_ref[...], h.astype(_BF16), _NT, preferred_element_type=_F32)

    cm, sm = cosm_ref[...], sinm_ref[...]
    cg, sg = cosg_ref[...], sing_ref[...]
    half_m = MLA_ROPE // 2
    half_g = GQA_HEAD_DIM // 2
    pad_rows = lax.broadcasted_iota(jnp.int32, (V_ROWS - V_DIM, t), 0)
    ones_row = jnp.where(pad_rows == 0, 1.0, 0.0).astype(_BF16)

    ql = pT[OFF_QLAT:OFF_QLAT + MLA_Q_LORA]
    qn = (ql * _rsqrt_mean_sq(ql, 0) * gqa_ref[...]).astype(_BF16)
    qT = jnp.dot(wqb_ref[...], qn, preferred_element_type=_F32)
    sc_m = float((MLA_NOPE + MLA_ROPE) ** -0.5 * LOG2_E)
    n_nope = MLA_HEADS * MLA_NOPE
    n_half = MLA_HEADS * half_m
    q0, q1 = _rope(qT[n_nope:n_nope + n_half], qT[n_nope + n_half:],
                   jnp.tile(cm, (MLA_HEADS, 1)), jnp.tile(sm, (MLA_HEADS, 1)))
    zq_m = jnp.zeros((HEAD_PAD - MLA_NOPE - MLA_ROPE, t), _BF16)
    for hd in range(MLA_HEADS):
        qT_ref[0, hd, 0, 0:MLA_NOPE, :] = (qT[hd * MLA_NOPE:(hd + 1) * MLA_NOPE] * sc_m).astype(_BF16)
        qT_ref[0, hd, 0, MLA_NOPE:MLA_NOPE + half_m, :] = (q0[hd * half_m:(hd + 1) * half_m] * sc_m).astype(_BF16)
        qT_ref[0, hd, 0, MLA_NOPE + half_m:MLA_NOPE + MLA_ROPE, :] = (
            q1[hd * half_m:(hd + 1) * half_m] * sc_m).astype(_BF16)
        qT_ref[0, hd, 0, MLA_NOPE + MLA_ROPE:, :] = zq_m

    kl = pT[OFF_KVLAT:OFF_KVLAT + MLA_KV_LORA]
    kn = (kl * _rsqrt_mean_sq(kl, 0) * gkva_ref[...]).astype(_BF16)
    kvT = jnp.dot(wkvb_ref[...], kn, preferred_element_type=_F32)
    k0, k1 = _rope(pT[OFF_KROPE:OFF_KROPE + half_m], pT[OFF_KROPE + half_m:OFF_KROPE + MLA_ROPE], cm, sm)
    zk_m = jnp.zeros((HEAD_PAD - MLA_NOPE - MLA_ROPE, t), _F32)
    for hd in range(MLA_HEADS):
        kTh = jnp.concatenate([kvT[hd * MLA_NOPE:(hd + 1) * MLA_NOPE], k0, k1, zk_m], axis=0)
        k_ref[0, hd] = kTh.T.astype(_BF16)
        vT_ref[0, hd, 0, :V_DIM] = kvT[n_nope + hd * MLA_V:n_nope + (hd + 1) * MLA_V].astype(_BF16)
        vT_ref[0, hd, 0, V_DIM:] = ones_row

    sc_g = float(GQA_HEAD_DIM ** -0.5 * LOG2_E)
    zq_g = jnp.zeros((HEAD_PAD - GQA_HEAD_DIM, t), _BF16)
    for hd in range(GQA_HEADS):
        blk = pT[OFF_QG + hd * GQA_HEAD_DIM:OFF_QG + (hd + 1) * GQA_HEAD_DIM]
        bn = blk * _rsqrt_mean_sq(blk, 0) * gqh_ref[...]
        g0, g1 = _rope(bn[:half_g], bn[half_g:], cg, sg)
        qT_ref[0, MLA_HEADS + hd, 0, 0:half_g, :] = (g0 * sc_g).astype(_BF16)
        qT_ref[0, MLA_HEADS + hd, 0, half_g:GQA_HEAD_DIM, :] = (g1 * sc_g).astype(_BF16)
        qT_ref[0, MLA_HEADS + hd, 0, GQA_HEAD_DIM:, :] = zq_g

    zk_g = jnp.zeros((HEAD_PAD - GQA_HEAD_DIM, t), _F32)
    for hd in range(GQA_KV_HEADS):
        blk = pT[OFF_KG + hd * GQA_HEAD_DIM:OFF_KG + (hd + 1) * GQA_HEAD_DIM]
        bn = blk * _rsqrt_mean_sq(blk, 0) * gkh_ref[...]
        g0, g1 = _rope(bn[:half_g], bn[half_g:], cg, sg)
        kTh = jnp.concatenate([g0, g1, zk_g], axis=0)
        k_ref[0, MLA_HEADS + hd] = kTh.T.astype(_BF16)
        vT_ref[0, MLA_HEADS + hd, 0, :V_DIM] = pT[OFF_VG + hd * V_DIM:OFF_VG + (hd + 1) * V_DIM].astype(_BF16)
        vT_ref[0, MLA_HEADS + hd, 0, V_DIM:] = ones_row


def _project(x, mod, gpre, winT, gqa, wqbT, gkva, wkvbT, gqh, gkh, cosm, sinm, cosg, sing):
    b, s, d = x.shape
    t = T_PROJ
    nt = s // t
    full = lambda a: pl.BlockSpec(a.shape, lambda bi, i: (0,) * a.ndim)
    tab = lambda a: pl.BlockSpec((a.shape[0], t), lambda bi, i: (0, i))
    return pl.pallas_call(
        _proj_kernel,
        out_shape=(jax.ShapeDtypeStruct((b, N_HEADS, nt, HEAD_PAD, t), _BF16),
                   jax.ShapeDtypeStruct((b, N_KV, s, HEAD_PAD), _BF16),
                   jax.ShapeDtypeStruct((b, N_KV, nt, V_ROWS, t), _BF16)),
        grid=(b, nt),
        in_specs=[pl.BlockSpec((1, t, d), lambda bi, i: (bi, i, 0)),
                  pl.BlockSpec((1, N_MOD, d), lambda bi, i: (bi, 0, 0)),
                  full(gpre), full(winT), full(gqa), full(wqbT), full(gkva), full(wkvbT),
                  full(gqh), full(gkh), tab(cosm), tab(sinm), tab(cosg), tab(sing)],
        out_specs=(pl.BlockSpec((1, N_HEADS, 1, HEAD_PAD, t), lambda bi, i: (bi, 0, i, 0, 0)),
                   pl.BlockSpec((1, N_KV, t, HEAD_PAD), lambda bi, i: (bi, 0, i, 0)),
                   pl.BlockSpec((1, N_KV, 1, V_ROWS, t), lambda bi, i: (bi, 0, i, 0, 0))),
        compiler_params=pltpu.CompilerParams(
            dimension_semantics=("arbitrary", "arbitrary"), vmem_limit_bytes=VMEM_LIMIT),
        name="in_proj",
    )(x, mod, gpre, winT, gqa, wqbT, gkva, wkvbT, gqh, gkh, cosm, sinm, cosg, sing)


def _attn_kernel(qT_ref, k_ref, vT_ref, o_ref, *slots, n_tiles, n_chunks, tk, group_size):
    ns = len(slots)
    total = n_tiles * n_chunks
    tq = qT_ref.shape[-1]

    def scores(g, slot):
        tile, j = g // n_chunks, g % n_chunks
        kc = k_ref[0, 0, pl.ds(pl.multiple_of(j * tk, tk), tk), :]
        s = jnp.dot(kc, qT_ref[0, 0, tile], preferred_element_type=_F32)
        slots[slot][...] = s
        return jnp.max(s, axis=0, keepdims=True)

    def accumulate(g, slot, mc, carry, first=False, last=False):
        m, acc = carry
        if first is not False:
            m = jnp.where(first, -1e30, m)
            acc = jnp.where(first, 0.0, acc)
        m_new = jnp.maximum(m, mc)
        alpha = jnp.exp2(m - m_new)
        p = jnp.exp2(slots[slot][...] - m_new).astype(_BF16)
        acc = alpha * acc + jnp.dot(vT_ref[0, 0, g % n_chunks], p, preferred_element_type=_F32)
        if last is not False:
            o_ref[0, 0, g // n_chunks] = acc[:V_DIM] / acc[V_DIM:V_DIM + 1]
        return m_new, acc

    def group(jj, state):
        mcs, carry = state
        mcs = list(mcs)
        g0 = group_size * jj
        opens = (g0 % n_chunks) == 0
        for u in range(group_size):
            nxt = scores(g0 + u + 2, (u + 2) % ns)
            carry = accumulate(g0 + u, u % ns, mcs[u % ns], carry, first=opens if u == 0 else False,
                               last=(u == group_size - 1))
            mcs[(u + 2) % ns] = nxt
        return tuple(mcs), carry

    carry = (jnp.full((1, tq), -1e30, _F32), jnp.zeros((V_ROWS, tq), _F32))
    zero = jnp.zeros((1, tq), _F32)
    mcs = (scores(0, 0), scores(1, 1)) + (zero,) * (ns - 2)
    n_loop = (total - 2) // group_size
    mcs, carry = lax.fori_loop(0, n_loop, group, (mcs, carry))
    mcs = list(mcs)
    for g in range(n_loop * group_size, total):
        if g + 2 < total:
            mcs[(g + 2) % ns] = scores(g + 2, (g + 2) % ns)
        carry = accumulate(g, g % ns, mcs[g % ns], carry, first=(g % n_chunks == 0), last=(g == total - 1))


def _kv_index(h):
    return jnp.where(h < MLA_HEADS, h, MLA_HEADS + (h - MLA_HEADS) // GQA_GROUP)


def _attention(qT, k, vT):
    b, nh, n_tiles, dp, tq = qT.shape
    s = k.shape[2]
    n_chunks, tk = vT.shape[2], vT.shape[4]
    group_size = min(ATTN_GROUP, n_chunks)
    assert group_size % N_SCORE_SLOTS == 0 and n_chunks % group_size == 0 and n_tiles * n_chunks > group_size
    return pl.pallas_call(
        functools.partial(_attn_kernel, n_tiles=n_tiles, n_chunks=n_chunks, tk=tk, group_size=group_size),
        out_shape=jax.ShapeDtypeStruct((b, nh, n_tiles, V_DIM, tq), _F32),
        grid=(b, nh),
        in_specs=[pl.BlockSpec((1, 1, n_tiles, dp, tq), lambda bi, h: (bi, h, 0, 0, 0)),
                  pl.BlockSpec((1, 1, s, dp), lambda bi, h: (bi, _kv_index(h), 0, 0)),
                  pl.BlockSpec((1, 1, n_chunks, V_ROWS, tk), lambda bi, h: (bi, _kv_index(h), 0, 0, 0))],
        out_specs=pl.BlockSpec((1, 1, n_tiles, V_DIM, tq), lambda bi, h: (bi, h, 0, 0, 0)),
        scratch_shapes=[pltpu.VMEM((tk, tq), _F32) for _ in range(N_SCORE_SLOTS)],
        compiler_params=pltpu.CompilerParams(
            dimension_semantics=("arbitrary", "arbitrary"), vmem_limit_bytes=VMEM_LIMIT),
        name="attention",
    )(qT, k, vT)


def _first_argmax(v, rows, n):
    m = jnp.max(v, axis=0, keepdims=True)
    idx = jnp.min(jnp.where(v == m, rows, n), axis=0, keepdims=True)
    return m, idx


def _post_kernel(x_ref, o_ref, mod_ref, gmla_ref, ggqa_ref, wout_ref, gpost_ref, gffn_ref,
                 wrh_ref, wrl_ref, br_ref,
                 x1_ref, hp_ref, idx_ref, w_ref, rank_ref, cnt_ref, carry_ref):
    first = jnp.logical_and(pl.program_id(0) == 0, pl.program_id(1) == 0)

    @pl.when(first)
    def _():
        carry_ref[...] = jnp.zeros_like(carry_ref)

    x = x_ref[0]
    t = x.shape[0]
    gate_m = mod_ref[0, 2:3, :]
    shift_f = mod_ref[0, 3:4, :]
    scale_f = mod_ref[0, 4:5, :]

    o = o_ref[0, :, 0].reshape(N_HEADS * V_DIM, t)
    half = MLA_HEADS * V_DIM
    om, og = o[:half], o[half:]
    mixedT = jnp.concatenate([om * _rsqrt_mean_sq(om, 0) * gmla_ref[...],
                              og * _rsqrt_mean_sq(og, 0) * ggqa_ref[...]], axis=0).astype(_BF16)
    y = jnp.dot(wout_ref[...], mixedT, preferred_element_type=_F32).T
    x1 = x + gate_m * (y * _rsqrt_mean_sq(y, -1) * gpost_ref[...])
    x1_ref[0] = x1
    h2 = x1 * _rsqrt_mean_sq(x1, -1) * gffn_ref[...] * (1.0 + scale_f) + shift_f

    _store_packed(hp_ref, h2)

    hh = h2.astype(_BF16)
    hl = (h2 - hh.astype(_F32)).astype(_BF16)
    wrh = wrh_ref[...]
    logits = (lax.dot_general(wrh, hh, _NT, preferred_element_type=_F32)
              + lax.dot_general(wrh, hl, _NT, preferred_element_type=_F32)
              + lax.dot_general(wrl_ref[...], hh, _NT, preferred_element_type=_F32))
    scores = jax.nn.sigmoid(logits)
    biased = scores + br_ref[...]
    neg = -jnp.inf

    rows_g = lax.broadcasted_iota(jnp.int32, (GROUP_SIZE, t), 0)
    gscores = []
    for g in range(N_GROUPS):
        v = biased[g * GROUP_SIZE:(g + 1) * GROUP_SIZE]
        m1, i1 = _first_argmax(v, rows_g, GROUP_SIZE)
        m2 = jnp.max(jnp.where(rows_g == i1, neg, v), axis=0, keepdims=True)
        gscores.append(m1 + m2)
    cur = jnp.concatenate(gscores, axis=0)
    rows_8 = lax.broadcasted_iota(jnp.int32, (N_GROUPS, t), 0)
    gsel = jnp.zeros((N_GROUPS, t), _F32)
    for _ in range(TOPK_GROUPS):
        _, gi = _first_argmax(cur, rows_8, N_GROUPS)
        hit = rows_8 == gi
        gsel = jnp.where(hit, 1.0, gsel)
        cur = jnp.where(hit, neg, cur)
    emask = jnp.concatenate(
        [jnp.broadcast_to(gsel[g:g + 1], (GROUP_SIZE, t)) for g in range(N_GROUPS)], axis=0) > 0.5
    masked = jnp.where(emask, biased, neg)

    rows_e = lax.broadcasted_iota(jnp.int32, (N_EXPERTS, t), 0)
    idxs, ws = [], []
    for _ in range(TOP_K):
        _, ei = _first_argmax(masked, rows_e, N_EXPERTS)
        hit = rows_e == ei
        idxs.append(ei)
        ws.append(jnp.sum(jnp.where(hit, scores, 0.0), axis=0, keepdims=True))
        masked = jnp.where(hit, neg, masked)
    wsum = ws[0]
    for wk in ws[1:]:
        wsum = wsum + wk

    onehot = jnp.zeros((N_EXPERTS, t), _F32)
    for ei in idxs:
        onehot = onehot + jnp.where(rows_e == ei, 1.0, 0.0)
    upper = (lax.broadcasted_iota(jnp.int32, (t, t), 0) < lax.broadcasted_iota(jnp.int32, (t, t), 1))
    before = jnp.dot(onehot.astype(_BF16), jnp.where(upper, 1.0, 0.0).astype(_BF16),
                     preferred_element_type=_F32) + carry_ref[...]
    for kk in range(TOP_K):
        idx_ref[kk:kk + 1, :] = idxs[kk]
        w_ref[kk:kk + 1, :] = ws[kk] / wsum * ROUTED_SCALE
        rank_ref[kk:kk + 1, :] = jnp.sum(
            jnp.where(rows_e == idxs[kk], before, 0.0), axis=0, keepdims=True).astype(jnp.int32)
    total = carry_ref[...] + jnp.sum(onehot, axis=1, keepdims=True)
    carry_ref[...] = total
    cnt_ref[...] = jnp.broadcast_to(total, cnt_ref.shape)


def _post_attention(x, oT, mod, gmla, ggqa, woutT, gpost, gffn, wrh, wrl, br):
    b, s, d = x.shape
    n = b * s
    t = T_POST
    nt = s // t
    assert oT.shape[2:] == (nt, V_DIM, t)
    full = lambda a: pl.BlockSpec(a.shape, lambda bi, i: (0,) * a.ndim)
    tok = lambda bi, i: (0, bi * nt + i)
    return pl.pallas_call(
        _post_kernel,
        out_shape=(jax.ShapeDtypeStruct((b, s, d), _F32),
                   jax.ShapeDtypeStruct((n * PACK_ROWS, LANES), _U32),
                   jax.ShapeDtypeStruct((TOP_K, n), jnp.int32),
                   jax.ShapeDtypeStruct((TOP_K, n), _F32),
                   jax.ShapeDtypeStruct((TOP_K, n), jnp.int32),
                   jax.ShapeDtypeStruct((N_EXPERTS, LANES), _F32)),
        grid=(b, nt),
        in_specs=[pl.BlockSpec((1, t, d), lambda bi, i: (bi, i, 0)),
                  pl.BlockSpec((1, N_HEADS, 1, V_DIM, t), lambda bi, i: (bi, 0, i, 0, 0)),
                  pl.BlockSpec((1, N_MOD, d), lambda bi, i: (bi, 0, 0)),
                  full(gmla), full(ggqa), full(woutT), full(gpost), full(gffn),
                  full(wrh), full(wrl), full(br)],
        out_specs=(pl.BlockSpec((1, t, d), lambda bi, i: (bi, i, 0)),
                   pl.BlockSpec((t * PACK_ROWS, LANES), lambda bi, i: (bi * nt + i, 0)),
                   pl.BlockSpec((TOP_K, t), tok),
                   pl.BlockSpec((TOP_K, t), tok),
                   pl.BlockSpec((TOP_K, t), tok),
                   pl.BlockSpec((N_EXPERTS, LANES), lambda bi, i: (0, 0))),
        scratch_shapes=[pltpu.VMEM((N_EXPERTS, 1), _F32)],
        compiler_params=pltpu.CompilerParams(
            dimension_semantics=("arbitrary", "arbitrary"), vmem_limit_bytes=VMEM_LIMIT),
        name="out_proj_router",
    )(x, oT, mod, gmla, ggqa, woutT, gpost, gffn, wrh, wrl, br)


def _slot_kernel(idx_ref, rank_ref, start_ref, dest_ref):
    t = idx_ref.shape[1]
    rows_e = lax.broadcasted_iota(jnp.int32, (N_EXPERTS, t), 0)
    start = start_ref[...]
    for kk in range(TOP_K):
        base = jnp.sum(jnp.where(rows_e == idx_ref[kk:kk + 1, :], start, 0.0), axis=0, keepdims=True)
        dest_ref[kk:kk + 1, :] = rank_ref[kk:kk + 1, :] + base.astype(jnp.int32)


def _slots(idx_t, rank_t, start):
    n = idx_t.shape[1]
    t = T_POST
    return pl.pallas_call(
        _slot_kernel,
        out_shape=jax.ShapeDtypeStruct((TOP_K, n), jnp.int32),
        grid=(n // t,),
        in_specs=[pl.BlockSpec((TOP_K, t), lambda i: (0, i)),
                  pl.BlockSpec((TOP_K, t), lambda i: (0, i)),
                  pl.BlockSpec((N_EXPERTS, 1), lambda i: (0, 0))],
        out_specs=pl.BlockSpec((TOP_K, t), lambda i: (0, i)),
        compiler_params=pltpu.CompilerParams(dimension_semantics=("arbitrary",)),
        name="slot_index",
    )(idx_t, rank_t, start.astype(_F32).reshape(N_EXPERTS, 1))


def _row_copy(src, src_row, dst, dst_row, sem):
    return pltpu.make_async_copy(_token_rows(src, src_row), _token_rows(dst, dst_row), sem)


def _dispatch_kernel(pad_start_ref, pad_len_ref, nused_ref, dest_ref, h_ref, xs_ref, zbuf, sem, zsem, *,
                     n_blocks):
    t_rows = h_ref.shape[0] // PACK_ROWS

    def issue(t, _):
        for kk in range(TOP_K):
            _row_copy(h_ref, t, xs_ref, dest_ref[0, 0, t * TOP_K + kk], sem).start(priority=kk % 2)
        return 0

    lax.fori_loop(0, t_rows, issue, 0)

    def drain(t, _):
        for kk in range(TOP_K):
            _row_copy(h_ref, 0, xs_ref, 0, sem).wait()
        return 0

    lax.fori_loop(0, t_rows, drain, 0)

    @pl.when(pl.program_id(0) == pl.num_programs(0) - 1)
    def _():
        zbuf[...] = jnp.zeros_like(zbuf)

        def zero_copy(first, count):
            return pltpu.make_async_copy(_token_rows(zbuf, 0, count), _token_rows(xs_ref, first, count), zsem)

        def sweep(act):
            def expert_pad(e, _):
                first = pad_start_ref[e]
                for piece in [T_M >> sh for sh in range(1, T_M.bit_length())]:
                    has = pad_len_ref[e] & piece

                    @pl.when(has != 0)
                    def _():
                        act(zero_copy(first, piece))

                    first = first + has
                return 0

            lax.fori_loop(0, N_EXPERTS, expert_pad, 0)

            def spare_block(blk, _):
                act(zero_copy(blk * T_M, T_M))
                return 0

            lax.fori_loop(nused_ref[0], n_blocks, spare_block, 0)

        sweep(lambda c: c.start())
        sweep(lambda c: c.wait())


def _dispatch(pad_start, pad_len, nused, dest_tiles, hp, n_slots):
    n = hp.shape[0] // PACK_ROWS
    t = T_ROW
    return pl.pallas_call(
        functools.partial(_dispatch_kernel, n_blocks=n_slots // T_M),
        out_shape=jax.ShapeDtypeStruct((n_slots * PACK_ROWS, LANES), _U32),
        grid_spec=pltpu.PrefetchScalarGridSpec(
            num_scalar_prefetch=3,
            grid=(n // t,),
            in_specs=[pl.BlockSpec((1, 1, t * TOP_K), lambda i, ps, pn, nu: (i, 0, 0), memory_space=pltpu.SMEM),
                      pl.BlockSpec((t * PACK_ROWS, LANES), lambda i, ps, pn, nu: (i, 0))],
            out_specs=pl.BlockSpec(memory_space=pl.ANY),
            scratch_shapes=[pltpu.VMEM((T_M * PACK_ROWS, LANES), _U32),
                            pltpu.SemaphoreType.DMA(()), pltpu.SemaphoreType.DMA(())]),
        compiler_params=pltpu.CompilerParams(
            dimension_semantics=("arbitrary",), vmem_limit_bytes=VMEM_LIMIT),
        name="dispatch",
    )(pad_start, pad_len, nused, dest_tiles, hp)


def _swiglu(xb, wgu, wd):
    gu = jnp.dot(xb, wgu, preferred_element_type=_F32)
    g, u = gu[:, :EXPERT_FF], gu[:, EXPERT_FF:]
    mid = (g * jax.nn.sigmoid(g) * u).astype(_BF16)
    return jnp.dot(mid, wd, preferred_element_type=_F32)


def _load_packed_bf16(ref, rows):
    return jnp.concatenate([p.astype(_BF16) for p in _load_packed(ref, 0, rows)], axis=1)


def _expert_kernel(nblk_ref, bstart_ref, nused_ref, xs_ref, wg_ref, wu_ref, wd_ref, ys_ref,
                   xbuf, ybuf, xsem, ysem, wgu_s, wd_s, *, n_blocks):
    e = pl.program_id(0)
    nb = nblk_ref[e]
    b0 = bstart_ref[e]
    nused = nused_ref[0]

    def x_copy(blk):
        slot = lax.rem(blk, X_RING)
        return pltpu.make_async_copy(_token_rows(xs_ref, blk * T_M, T_M), _token_rows(xbuf, slot * T_M, T_M),
                                     xsem.at[slot])

    def y_copy(blk):
        slot = lax.rem(blk, Y_RING)
        return pltpu.make_async_copy(_token_rows(ybuf, slot * T_M, T_M), _token_rows(ys_ref, blk * T_M, T_M),
                                     ysem.at[slot])

    @pl.when(e == 0)
    def _():
        for g in range(X_RING - 1):
            @pl.when(g < nused)
            def _():
                x_copy(g).start()

    @pl.when(nb > 0)
    def _():
        wgu_s[:, :EXPERT_FF] = wg_ref[0].astype(_BF16)
        wgu_s[:, EXPERT_FF:] = wu_ref[0].astype(_BF16)
        wd_s[...] = wd_ref[0].astype(_BF16)

        def block(g, _):
            x_copy(g).wait()

            @pl.when(g + X_RING - 1 < nused)
            def _():
                x_copy(g + X_RING - 1).start(priority=1)

            @pl.when(g >= Y_RING)
            def _():
                y_copy(g - Y_RING).wait()

            xb = jnp.concatenate(
                [p.astype(_BF16) for p in _load_packed(xbuf, lax.rem(g, X_RING) * T_M, T_M)], axis=1)
            _store_packed(ybuf, _swiglu(xb, wgu_s[...], wd_s[...]), lax.rem(g, Y_RING) * T_M)
            y_copy(g).start(priority=1)
            return 0

        lax.fori_loop(b0, b0 + nb, block, 0)

    @pl.when(e == pl.num_programs(0) - 1)
    def _():
        for back in range(Y_RING, 0, -1):
            @pl.when(nused >= back)
            def _():
                y_copy(nused - back).wait()

        ybuf[pl.ds(0, T_M * PACK_ROWS), :] = jnp.zeros((T_M * PACK_ROWS, LANES), _U32)

        def zero_copy(blk):
            return pltpu.make_async_copy(_token_rows(ybuf, 0, T_M), _token_rows(ys_ref, blk * T_M, T_M), ysem.at[0])

        def fill(blk, _):
            zero_copy(blk).start()
            return 0

        def drain(blk, _):
            zero_copy(blk).wait()
            return 0

        lax.fori_loop(nused, n_blocks, fill, 0)
        lax.fori_loop(nused, n_blocks, drain, 0)


def _experts(nblk, bstart, nused, xs, wg, wu, wd):
    n_slots = xs.shape[0] // PACK_ROWS
    n_exp, d, ff = wg.shape
    ring = lambda depth: pltpu.VMEM((depth * T_M * PACK_ROWS, LANES), _U32)
    return pl.pallas_call(
        functools.partial(_expert_kernel, n_blocks=n_slots // T_M),
        out_shape=jax.ShapeDtypeStruct(xs.shape, _U32),
        grid_spec=pltpu.PrefetchScalarGridSpec(
            num_scalar_prefetch=3,
            grid=(n_exp,),
            in_specs=[pl.BlockSpec(memory_space=pl.ANY),
                      pl.BlockSpec((1, d, ff), lambda i, nbk, bst, nu: (i, 0, 0)),
                      pl.BlockSpec((1, d, ff), lambda i, nbk, bst, nu: (i, 0, 0)),
                      pl.BlockSpec((1, ff, d), lambda i, nbk, bst, nu: (i, 0, 0))],
            out_specs=pl.BlockSpec(memory_space=pl.ANY),
            scratch_shapes=[ring(X_RING), ring(Y_RING),
                            pltpu.SemaphoreType.DMA((X_RING,)), pltpu.SemaphoreType.DMA((Y_RING,)),
                            pltpu.VMEM((d, 2 * ff), _BF16), pltpu.VMEM((ff, d), _BF16)]),
        compiler_params=pltpu.CompilerParams(
            dimension_semantics=("arbitrary",), vmem_limit_bytes=VMEM_LIMIT),
        name="experts",
    )(nblk, bstart, nused, xs, wg, wu, wd)


def _combine_kernel(dest_ref, dest_next_ref, ys_ref, hp_ref, w_ref, x1_ref, mod_ref, wgus_ref, wds_ref,
                    gpost_ref, o_ref, gbuf, sem):
    i = pl.program_id(0)
    t_rows = hp_ref.shape[0] // PACK_ROWS
    tile_rows = GATHER_STRIDE * t_rows

    def gather(d_ref, slot):
        def issue(t, _):
            for kk in range(TOP_K):
                _row_copy(ys_ref, d_ref[0, 0, t * TOP_K + kk], gbuf, slot * tile_rows + t * GATHER_STRIDE + kk,
                          sem.at[slot]).start(priority=kk % 2)
            return 0

        lax.fori_loop(0, t_rows, issue, 0)

    slot = lax.rem(i, 2)

    @pl.when(i == 0)
    def _():
        gather(dest_ref, 0)

    @pl.when(i + 1 < pl.num_programs(0))
    def _():
        gather(dest_next_ref, 1 - slot)

    shared = _swiglu(_load_packed_bf16(hp_ref, t_rows), wgus_ref[...], wds_ref[...])

    def drain(t, _):
        for kk in range(TOP_K):
            _row_copy(ys_ref, 0, gbuf, 0, sem.at[slot]).wait()
        return 0

    lax.fori_loop(0, t_rows, drain, 0)

    w = w_ref[...]
    cols = None
    for kk in range(TOP_K):
        wk = w[:, kk:kk + 1]
        parts = [wk * p for p in _load_packed(gbuf, slot * tile_rows + kk, t_rows, GATHER_STRIDE)]
        cols = parts if cols is None else [a + b for a, b in zip(cols, parts)]
    y = jnp.concatenate(cols, axis=1) + shared
    gate_f = mod_ref[0, 5:6, :]
    o_ref[...] = x1_ref[...] + gate_f * (y * _rsqrt_mean_sq(y, -1) * gpost_ref[...])


def _combine(dest_tiles, ys, hp, w_tok, x1, mod, wgus, wds, gpost, tiles_per_batch):
    n, d = x1.shape
    t = T_ROW
    nt = n // t
    full = lambda a: pl.BlockSpec(a.shape, lambda i: (0,) * a.ndim)
    dest_spec = lambda shift: pl.BlockSpec(
        (1, 1, t * TOP_K), lambda i: (jnp.minimum(i + shift, nt - 1), 0, 0), memory_space=pltpu.SMEM)
    return pl.pallas_call(
        _combine_kernel,
        out_shape=jax.ShapeDtypeStruct((n, d), _F32),
        grid=(nt,),
        in_specs=[dest_spec(0), dest_spec(1),
                  pl.BlockSpec(memory_space=pl.ANY),
                  pl.BlockSpec((t * PACK_ROWS, LANES), lambda i: (i, 0)),
                  pl.BlockSpec((t, TOP_K), lambda i: (i, 0)),
                  pl.BlockSpec((t, d), lambda i: (i, 0)),
                  pl.BlockSpec((1, N_MOD, d), lambda i: (i // tiles_per_batch, 0, 0)),
                  full(wgus), full(wds), full(gpost)],
        out_specs=pl.BlockSpec((t, d), lambda i: (i, 0)),
        scratch_shapes=[pltpu.VMEM((2 * GATHER_STRIDE * t * PACK_ROWS, LANES), _U32), pltpu.SemaphoreType.DMA((2,))],
        compiler_params=pltpu.CompilerParams(
            dimension_semantics=("arbitrary",), vmem_limit_bytes=VMEM_LIMIT),
        name="combine",
    )(dest_tiles, dest_tiles, ys, hp, w_tok, x1, mod, wgus, wds, gpost)


def _deinterleave(n):
    return np.concatenate([np.arange(0, n, 2), np.arange(1, n, 2)])


def _rope_tables(s, dim):
    rows = s // GRID_W
    row = jnp.broadcast_to(jnp.arange(rows, dtype=_F32)[:, None], (rows, GRID_W)).reshape(s)
    col = jnp.broadcast_to(jnp.arange(GRID_W, dtype=_F32)[None, :], (rows, GRID_W)).reshape(s)
    n_freq = dim // 4
    inv_freq = ROPE_THETA ** (-jnp.arange(n_freq, dtype=_F32) / n_freq)
    ang = jnp.concatenate([row[:, None] * inv_freq, col[:, None] * inv_freq], axis=-1)
    return jnp.cos(ang).T, jnp.sin(ang).T


def _in_proj_perm():
    perm = list(range(OFF_KROPE))
    perm += list(OFF_KROPE + _deinterleave(MLA_ROPE))
    for hd in range(GQA_HEADS):
        perm += list(OFF_QG + hd * GQA_HEAD_DIM + _deinterleave(GQA_HEAD_DIM))
    for hd in range(GQA_KV_HEADS):
        perm += list(OFF_KG + hd * GQA_HEAD_DIM + _deinterleave(GQA_HEAD_DIM))
    perm += list(range(OFF_VG, IN_COLS))
    return np.asarray(perm)


def _q_b_perm():
    per = MLA_NOPE + MLA_ROPE
    nope = [hd * per + j for hd in range(MLA_HEADS) for j in range(MLA_NOPE)]
    even = [hd * per + MLA_NOPE + 2 * i for hd in range(MLA_HEADS) for i in range(MLA_ROPE // 2)]
    odd = [hd * per + MLA_NOPE + 2 * i + 1 for hd in range(MLA_HEADS) for i in range(MLA_ROPE // 2)]
    return np.asarray(nope + even + odd)


def _kv_b_perm():
    per = MLA_NOPE + MLA_V
    kk = [hd * per + j for hd in range(MLA_HEADS) for j in range(MLA_NOPE)]
    vv = [hd * per + MLA_NOPE + j for hd in range(MLA_HEADS) for j in range(MLA_V)]
    return np.asarray(kk + vv)


def _layer(x, c, w_ada, b_ada, g_pre_mix, g_post_mix, g_pre_ffn, g_post_ffn, w_in, g_q_a, w_q_b,
           g_kv_a, w_kv_b, g_q_head, g_k_head, g_mla_out, g_gqa_out, w_out, w_router, b_router,
           w_gate_e, w_up_e, w_down_e, w_gate_s, w_up_s, w_down_s):
    b, s, d = x.shape
    n = b * s
    col = lambda g: g.reshape(-1, 1).astype(_F32)
    row = lambda g: g.reshape(1, -1).astype(_F32)

    mod = _ada_mod(c, w_ada, b_ada)

    winT = w_in[:, _in_proj_perm()].T.astype(_BF16)
    wqbT = w_q_b[:, _q_b_perm()].T.astype(_BF16)
    wkvbT = w_kv_b[:, _kv_b_perm()].T.astype(_BF16)
    perm_h = _deinterleave(GQA_HEAD_DIM)
    cosm, sinm = _rope_tables(s, MLA_ROPE)
    cosg, sing = _rope_tables(s, GQA_HEAD_DIM)

    qT, k, vT = _project(x, mod, row(g_pre_mix), winT, col(g_q_a), wqbT, col(g_kv_a), wkvbT,
                         col(g_q_head[perm_h]), col(g_k_head[perm_h]), cosm, sinm, cosg, sing)
    oT = _attention(qT, k, vT)

    wr_t = w_router.T.astype(_F32)
    wrh = wr_t.astype(_BF16)
    wrl = (wr_t - wrh.astype(_F32)).astype(_BF16)
    x1, hp, idx_t, w_t, rank_t, cnt = _post_attention(
        x, oT, mod, col(g_mla_out), col(g_gqa_out), w_out.T.astype(_BF16), row(g_post_mix),
        row(g_pre_ffn), wrh, wrl, col(b_router))

    counts = cnt[:, 0].astype(jnp.int32)
    blocks_e = (counts + T_M - 1) // T_M
    blk_end = jnp.cumsum(blocks_e)
    start = (blk_end - blocks_e) * T_M
    n_blocks = n * TOP_K // T_M + N_EXPERTS
    nused = blk_end[-1:].astype(jnp.int32)
    bstart = (blk_end - blocks_e).astype(jnp.int32)
    dest = _slots(idx_t, rank_t, start)
    dest_tiles = dest.T.reshape(n // T_ROW, 1, T_ROW * TOP_K)

    xs = _dispatch((start + counts).astype(jnp.int32), (blocks_e * T_M - counts).astype(jnp.int32), nused,
                   dest_tiles, hp, n_blocks * T_M)
    ys = _experts(blocks_e.astype(jnp.int32), bstart, nused, xs, w_gate_e, w_up_e, w_down_e)
    wgus = jnp.concatenate([w_gate_s, w_up_s], axis=1).astype(_BF16)
    out = _combine(dest_tiles, ys, hp, w_t.T, x1.reshape(n, d), mod, wgus, w_down_s.astype(_BF16),
                   row(g_post_ffn), s // T_ROW)
    return out.reshape(b, s, d)


def kernel(x, c, w_ada, b_ada, g_pre_mix, g_post_mix, g_pre_ffn, g_post_ffn, w_in, g_q_a, w_q_b, g_kv_a, w_kv_b, g_q_head, g_k_head, g_mla_out, g_gqa_out, w_out, w_router, b_router, w_gate_e, w_up_e, w_down_e, w_gate_s, w_up_s, w_down_s):
    depth = w_ada.shape[0]
    for l in range(depth):
        x = _layer(x, c, w_ada[l], b_ada[l], g_pre_mix[l], g_post_mix[l], g_pre_ffn[l], g_post_ffn[l],
                   w_in[l], g_q_a[l], w_q_b[l], g_kv_a[l], w_kv_b[l], g_q_head[l], g_k_head[l],
                   g_mla_out[l], g_gqa_out[l], w_out[l], w_router[l], b_router[l], w_gate_e[l],
                   w_up_e[l], w_down_e[l], w_gate_s[l], w_up_s[l], w_down_s[l])
    return x
```

```python
import functools

import jax
import jax.numpy as jnp
import numpy as np
from jax import lax
from jax.experimental import pallas as pl
from jax.experimental.pallas import tpu as pltpu

D_MODEL = 1024
GRID_W = 64
ROPE_THETA = 10000.0
NORM_EPS = 1e-6

MLA_HEADS = 8
MLA_Q_LORA = 256
MLA_KV_LORA = 128
MLA_NOPE = 64
MLA_ROPE = 32
MLA_V = 64

GQA_HEADS = 8
GQA_KV_HEADS = 2
GQA_HEAD_DIM = 64
GQA_GROUP = GQA_HEADS // GQA_KV_HEADS

LANES = 128
SUBLANES = 8
VMEM_BYTES = 64 * 1024 * 1024

N_HEADS = MLA_HEADS + GQA_HEADS
N_KV = MLA_HEADS + GQA_KV_HEADS
HEAD_PAD = LANES
V_DIM = 64
V_ROWS = 80

N_EXPERTS = 256
TOP_K = 8
N_GROUPS = 8
GROUP_SIZE = N_EXPERTS // N_GROUPS
TOPK_GROUPS = 4
EXPERT_FF = 256
ROUTED_SCALE = 2.5
N_MOD = 6

OFF_QLAT = 0
OFF_KVLAT = OFF_QLAT + MLA_Q_LORA
OFF_KROPE = OFF_KVLAT + MLA_KV_LORA
OFF_QG = OFF_KROPE + MLA_ROPE
OFF_KG = OFF_QG + GQA_HEADS * GQA_HEAD_DIM
OFF_VG = OFF_KG + GQA_KV_HEADS * GQA_HEAD_DIM
IN_COLS = OFF_VG + GQA_KV_HEADS * GQA_HEAD_DIM

T_PROJ = 512
T_POST = 512
T_ROW = 256
T_M = 256
T_ADA = 1536
PACK_ROWS = D_MODEL // (2 * LANES)
N_SCORE_SLOTS = 4
ATTN_GROUP = 16
X_RING = 4
Y_RING = 3
LOG2_E = 1.4426950408889634
VMEM_LIMIT = VMEM_BYTES * 7 // 8

_NT = (((1,), (1,)), ((), ()))
_F32 = jnp.float32
_BF16 = jnp.bfloat16
_U32 = jnp.uint32
_HI_MASK = np.uint32(0xFFFF0000)


def _rsqrt_mean_sq(v, axis):
    return lax.rsqrt(jnp.mean(v * v, axis=axis, keepdims=True) + NORM_EPS)


def _pack_pair(lo, hi):
    lo_b = pltpu.bitcast(lo.astype(_BF16).astype(_F32), _U32) >> 16
    hi_b = pltpu.bitcast(hi.astype(_BF16).astype(_F32), _U32) & _HI_MASK
    return lo_b | hi_b


def _unpack_pair(u):
    return pltpu.bitcast(u << 16, _F32), pltpu.bitcast(u & _HI_MASK, _F32)


def _token_rows(ref, first, count=1):
    return ref.at[pl.ds(pl.multiple_of(first * PACK_ROWS, PACK_ROWS), count * PACK_ROWS), :]


def _chunk_of_tokens(first, count, c):
    return pl.ds(first * PACK_ROWS + c, count, stride=PACK_ROWS)


def _store_packed(ref, val, first=0):
    rows = val.shape[0]
    for c in range(PACK_ROWS):
        lo = 2 * c * LANES
        ref[_chunk_of_tokens(first, rows, c), :] = _pack_pair(val[:, lo:lo + LANES], val[:, lo + LANES:lo + 2 * LANES])


def _load_packed(ref, first, rows):
    parts = []
    for c in range(PACK_ROWS):
        parts += list(_unpack_pair(ref[_chunk_of_tokens(first, rows, c), :]))
    return parts


def _ada_kernel(c_ref, w_ref, b_ref, o_ref):
    c = c_ref[...]
    a = (c * jax.nn.sigmoid(c)).astype(_BF16)
    o_ref[...] = jnp.dot(a, w_ref[...].astype(_BF16), preferred_element_type=_F32) + b_ref[...]


def _ada_mod(c, w_ada, b_ada):
    b, d = c.shape
    cols = w_ada.shape[1]
    rows = SUBLANES
    tn = T_ADA
    c_pad = jnp.zeros((rows, d), _F32).at[:b].set(c)
    out = pl.pallas_call(
        _ada_kernel,
        out_shape=jax.ShapeDtypeStruct((rows, cols), _F32),
        grid=(cols // tn,),
        in_specs=[pl.BlockSpec((rows, d), lambda j: (0, 0)),
                  pl.BlockSpec((d, tn), lambda j: (0, j)),
                  pl.BlockSpec((1, tn), lambda j: (0, j))],
        out_specs=pl.BlockSpec((rows, tn), lambda j: (0, j)),
        compiler_params=pltpu.CompilerParams(vmem_limit_bytes=VMEM_LIMIT),
        name="ada_mod",
    )(c_pad, w_ada, b_ada.reshape(1, cols))
    return out[:b].reshape(b, N_MOD, d)


def _rope(x0, x1, cos, sin):
    return x0 * cos - x1 * sin, x0 * sin + x1 * cos


def _proj_kernel(x_ref, mod_ref, gpre_ref, win_ref, gqa_ref, wqb_ref, gkva_ref, wkvb_ref,
                 gqh_ref, gkh_ref, cosm_ref, sinm_ref, cosg_ref, sing_ref,
                 qT_ref, k_ref, vT_ref):
    x = x_ref[0]
    t = x.shape[0]
    shift = mod_ref[0, 0:1, :]
    scale = mod_ref[0, 1:2, :]
    h = x * _rsqrt_mean_sq(x, -1) * (gpre_ref[...] * (1.0 + scale)) + shift
    pT = lax.dot_general(win_ref[...], h.astype(_BF16), _NT, preferred_element_type=_F32)

    cm, sm = cosm_ref[...], sinm_ref[...]
    cg, sg = cosg_ref[...], sing_ref[...]
    half_m = MLA_ROPE // 2
    half_g = GQA_HEAD_DIM // 2
    pad_rows = lax.broadcasted_iota(jnp.int32, (V_ROWS - V_DIM, t), 0)
    ones_row = jnp.where(pad_rows == 0, 1.0, 0.0).astype(_BF16)

    ql = pT[OFF_QLAT:OFF_QLAT + MLA_Q_LORA]
    qn = (ql * _rsqrt_mean_sq(ql, 0) * gqa_ref[...]).astype(_BF16)
    qT = jnp.dot(wqb_ref[...], qn, preferred_element_type=_F32)
    sc_m = float((MLA_NOPE + MLA_ROPE) ** -0.5 * LOG2_E)
    n_nope = MLA_HEADS * MLA_NOPE
    n_half = MLA_HEADS * half_m
    q0, q1 = _rope(qT[n_nope:n_nope + n_half], qT[n_nope + n_half:],
                   jnp.tile(cm, (MLA_HEADS, 1)), jnp.tile(sm, (MLA_HEADS, 1)))
    zq_m = jnp.zeros((HEAD_PAD - MLA_NOPE - MLA_ROPE, t), _BF16)
    for hd in range(MLA_HEADS):
        qT_ref[0, hd, 0, 0:MLA_NOPE, :] = (qT[hd * MLA_NOPE:(hd + 1) * MLA_NOPE] * sc_m).astype(_BF16)
        qT_ref[0, hd, 0, MLA_NOPE:MLA_NOPE + half_m, :] = (q0[hd * half_m:(hd + 1) * half_m] * sc_m).astype(_BF16)
        qT_ref[0, hd, 0, MLA_NOPE + half_m:MLA_NOPE + MLA_ROPE, :] = (
            q1[hd * half_m:(hd + 1) * half_m] * sc_m).astype(_BF16)
        qT_ref[0, hd, 0, MLA_NOPE + MLA_ROPE:, :] = zq_m

    kl = pT[OFF_KVLAT:OFF_KVLAT + MLA_KV_LORA]
    kn = (kl * _rsqrt_mean_sq(kl, 0) * gkva_ref[...]).astype(_BF16)
    kvT = jnp.dot(wkvb_ref[...], kn, preferred_element_type=_F32)
    k0, k1 = _rope(pT[OFF_KROPE:OFF_KROPE + half_m], pT[OFF_KROPE + half_m:OFF_KROPE + MLA_ROPE], cm, sm)
    zk_m = jnp.zeros((HEAD_PAD - MLA_NOPE - MLA_ROPE, t), _F32)
    for hd in range(MLA_HEADS):
        kTh = jnp.concatenate([kvT[hd * MLA_NOPE:(hd + 1) * MLA_NOPE], k0, k1, zk_m], axis=0)
        k_ref[0, hd] = kTh.T.astype(_BF16)
        vT_ref[0, hd, 0, :V_DIM] = kvT[n_nope + hd * MLA_V:n_nope + (hd + 1) * MLA_V].astype(_BF16)
        vT_ref[0, hd, 0, V_DIM:] = ones_row

    sc_g = float(GQA_HEAD_DIM ** -0.5 * LOG2_E)
    zq_g = jnp.zeros((HEAD_PAD - GQA_HEAD_DIM, t), _BF16)
    for hd in range(GQA_HEADS):
        blk = pT[OFF_QG + hd * GQA_HEAD_DIM:OFF_QG + (hd + 1) * GQA_HEAD_DIM]
        bn = blk * _rsqrt_mean_sq(blk, 0) * gqh_ref[...]
        g0, g1 = _rope(bn[:half_g], bn[half_g:], cg, sg)
        qT_ref[0, MLA_HEADS + hd, 0, 0:half_g, :] = (g0 * sc_g).astype(_BF16)
        qT_ref[0, MLA_HEADS + hd, 0, half_g:GQA_HEAD_DIM, :] = (g1 * sc_g).astype(_BF16)
        qT_ref[0, MLA_HEADS + hd, 0, GQA_HEAD_DIM:, :] = zq_g

    zk_g = jnp.zeros((HEAD_PAD - GQA_HEAD_DIM, t), _F32)
    for hd in range(GQA_KV_HEADS):
        blk = pT[OFF_KG + hd * GQA_HEAD_DIM:OFF_KG + (hd + 1) * GQA_HEAD_DIM]
        bn = blk * _rsqrt_mean_sq(blk, 0) * gkh_ref[...]
        g0, g1 = _rope(bn[:half_g], bn[half_g:], cg, sg)
        kTh = jnp.concatenate([g0, g1, zk_g], axis=0)
        k_ref[0, MLA_HEADS + hd] = kTh.T.astype(_BF16)
        vT_ref[0, MLA_HEADS + hd, 0, :V_DIM] = pT[OFF_VG + hd * V_DIM:OFF_VG + (hd + 1) * V_DIM].astype(_BF16)
        vT_ref[0, MLA_HEADS + hd, 0, V_DIM:] = ones_row


def _project(x, mod, gpre, winT, gqa, wqbT, gkva, wkvbT, gqh, gkh, cosm, sinm, cosg, sing):
    b, s, d = x.shape
    t = T_PROJ
    nt = s // t
    full = lambda a: pl.BlockSpec(a.shape, lambda bi, i: (0,) * a.ndim)
    tab = lambda a: pl.BlockSpec((a.shape[0], t), lambda bi, i: (0, i))
    return pl.pallas_call(
        _proj_kernel,
        out_shape=(jax.ShapeDtypeStruct((b, N_HEADS, nt, HEAD_PAD, t), _BF16),
                   jax.ShapeDtypeStruct((b, N_KV, s, HEAD_PAD), _BF16),
                   jax.ShapeDtypeStruct((b, N_KV, nt, V_ROWS, t), _BF16)),
        grid=(b, nt),
        in_specs=[pl.BlockSpec((1, t, d), lambda bi, i: (bi, i, 0)),
                  pl.BlockSpec((1, N_MOD, d), lambda bi, i: (bi, 0, 0)),
                  full(gpre), full(winT), full(gqa), full(wqbT), full(gkva), full(wkvbT),
                  full(gqh), full(gkh), tab(cosm), tab(sinm), tab(cosg), tab(sing)],
        out_specs=(pl.BlockSpec((1, N_HEADS, 1, HEAD_PAD, t), lambda bi, i: (bi, 0, i, 0, 0)),
                   pl.BlockSpec((1, N_KV, t, HEAD_PAD), lambda bi, i: (bi, 0, i, 0)),
                   pl.BlockSpec((1, N_KV, 1, V_ROWS, t), lambda bi, i: (bi, 0, i, 0, 0))),
        compiler_params=pltpu.CompilerParams(
            dimension_semantics=("arbitrary", "arbitrary"), vmem_limit_bytes=VMEM_LIMIT),
        name="in_proj",
    )(x, mod, gpre, winT, gqa, wqbT, gkva, wkvbT, gqh, gkh, cosm, sinm, cosg, sing)


def _attn_kernel(qT_ref, k_ref, vT_ref, o_ref, *slots, n_tiles, n_chunks, tk, group_size):
    ns = len(slots)
    total = n_tiles * n_chunks
    tq = qT_ref.shape[-1]

    def scores(g, slot):
        tile, j = g // n_chunks, g % n_chunks
        kc = k_ref[0, 0, pl.ds(pl.multiple_of(j * tk, tk), tk), :]
        s = jnp.dot(kc, qT_ref[0, 0, tile], preferred_element_type=_F32)
        slots[slot][...] = s
        return jnp.max(s, axis=0, keepdims=True)

    def accumulate(g, slot, mc, carry, first=False, last=False):
        m, acc = carry
        if first is not False:
            m = jnp.where(first, -1e30, m)
            acc = jnp.where(first, 0.0, acc)
        m_new = jnp.maximum(m, mc)
        alpha = jnp.exp2(m - m_new)
        p = jnp.exp2(slots[slot][...] - m_new).astype(_BF16)
        acc = alpha * acc + jnp.dot(vT_ref[0, 0, g % n_chunks], p, preferred_element_type=_F32)
        if last is not False:
            o_ref[0, 0, g // n_chunks] = acc[:V_DIM] / acc[V_DIM:V_DIM + 1]
        return m_new, acc

    def group(jj, state):
        mcs, carry = state
        mcs = list(mcs)
        g0 = group_size * jj
        opens = (g0 % n_chunks) == 0
        for u in range(group_size):
            nxt = scores(g0 + u + 2, (u + 2) % ns)
            carry = accumulate(g0 + u, u % ns, mcs[u % ns], carry, first=opens if u == 0 else False,
                               last=(u == group_size - 1))
            mcs[(u + 2) % ns] = nxt
        return tuple(mcs), carry

    carry = (jnp.full((1, tq), -1e30, _F32), jnp.zeros((V_ROWS, tq), _F32))
    zero = jnp.zeros((1, tq), _F32)
    mcs = (scores(0, 0), scores(1, 1)) + (zero,) * (ns - 2)
    n_loop = (total - 2) // group_size
    mcs, carry = lax.fori_loop(0, n_loop, group, (mcs, carry))
    mcs = list(mcs)
    for g in range(n_loop * group_size, total):
        if g + 2 < total:
            mcs[(g + 2) % ns] = scores(g + 2, (g + 2) % ns)
        carry = accumulate(g, g % ns, mcs[g % ns], carry, first=(g % n_chunks == 0), last=(g == total - 1))


def _kv_index(h):
    return jnp.where(h < MLA_HEADS, h, MLA_HEADS + (h - MLA_HEADS) // GQA_GROUP)


def _attention(qT, k, vT):
    b, nh, n_tiles, dp, tq = qT.shape
    s = k.shape[2]
    n_chunks, tk = vT.shape[2], vT.shape[4]
    group_size = min(ATTN_GROUP, n_chunks)
    assert group_size % N_SCORE_SLOTS == 0 and n_chunks % group_size == 0 and n_tiles * n_chunks > group_size
    return pl.pallas_call(
        functools.partial(_attn_kernel, n_tiles=n_tiles, n_chunks=n_chunks, tk=tk, group_size=group_size),
        out_shape=jax.ShapeDtypeStruct((b, nh, n_tiles, V_DIM, tq), _F32),
        grid=(b, nh),
        in_specs=[pl.BlockSpec((1, 1, n_tiles, dp, tq), lambda bi, h: (bi, h, 0, 0, 0)),
                  pl.BlockSpec((1, 1, s, dp), lambda bi, h: (bi, _kv_index(h), 0, 0)),
                  pl.BlockSpec((1, 1, n_chunks, V_ROWS, tk), lambda bi, h: (bi, _kv_index(h), 0, 0, 0))],
        out_specs=pl.BlockSpec((1, 1, n_tiles, V_DIM, tq), lambda bi, h: (bi, h, 0, 0, 0)),
        scratch_shapes=[pltpu.VMEM((tk, tq), _F32) for _ in range(N_SCORE_SLOTS)],
        compiler_params=pltpu.CompilerParams(
            dimension_semantics=("arbitrary", "arbitrary"), vmem_limit_bytes=VMEM_LIMIT),
        name="attention",
    )(qT, k, vT)


def _first_argmax(v, rows, n):
    m = jnp.max(v, axis=0, keepdims=True)
    idx = jnp.min(jnp.where(v == m, rows, n), axis=0, keepdims=True)
    return m, idx


def _post_kernel(x_ref, o_ref, mod_ref, gmla_ref, ggqa_ref, wout_ref, gpost_ref, gffn_ref,
                 wrh_ref, wrl_ref, br_ref, upper_ref,
                 x1_ref, hp_ref, idx_ref, w_ref, rank_ref, cnt_ref, carry_ref):
    first = jnp.logical_and(pl.program_id(0) == 0, pl.program_id(1) == 0)

    @pl.when(first)
    def _():
        carry_ref[...] = jnp.zeros_like(carry_ref)

    x = x_ref[0]
    t = x.shape[0]
    gate_m = mod_ref[0, 2:3, :]
    shift_f = mod_ref[0, 3:4, :]
    scale_f = mod_ref[0, 4:5, :]

    o = o_ref[0, :, 0].reshape(N_HEADS * V_DIM, t)
    half = MLA_HEADS * V_DIM
    om, og = o[:half], o[half:]
    mixedT = jnp.concatenate([om * _rsqrt_mean_sq(om, 0) * gmla_ref[...],
                              og * _rsqrt_mean_sq(og, 0) * ggqa_ref[...]], axis=0).astype(_BF16)
    y = jnp.dot(wout_ref[...], mixedT, preferred_element_type=_F32).T
    x1 = x + (y * _rsqrt_mean_sq(y, -1)) * (gate_m * gpost_ref[...])
    x1_ref[0] = x1
    h2 = x1 * _rsqrt_mean_sq(x1, -1) * (gffn_ref[...] * (1.0 + scale_f)) + shift_f

    _store_packed(hp_ref, h2)

    hh = h2.astype(_BF16)
    hl = (h2 - hh.astype(_F32)).astype(_BF16)
    wrh = wrh_ref[...]
    logits = (lax.dot_general(wrh, hh, _NT, preferred_element_type=_F32)
              + lax.dot_general(wrh, hl, _NT, preferred_element_type=_F32)
              + lax.dot_general(wrl_ref[...], hh, _NT, preferred_element_type=_F32))
    scores = jax.nn.sigmoid(logits)
    biased = scores + br_ref[...]
    neg = -jnp.inf

    rows_g = lax.broadcasted_iota(jnp.int32, (GROUP_SIZE, t), 0)
    gscores = []
    for g in range(N_GROUPS):
        v = biased[g * GROUP_SIZE:(g + 1) * GROUP_SIZE]
        m1, i1 = _first_argmax(v, rows_g, GROUP_SIZE)
        m2 = jnp.max(jnp.where(rows_g == i1, neg, v), axis=0, keepdims=True)
        gscores.append(m1 + m2)
    cur = jnp.concatenate(gscores, axis=0)
    rows_8 = lax.broadcasted_iota(jnp.int32, (N_GROUPS, t), 0)
    gsel = jnp.zeros((N_GROUPS, t), _F32)
    for _ in range(TOPK_GROUPS):
        _, gi = _first_argmax(cur, rows_8, N_GROUPS)
        hit = rows_8 == gi
        gsel = jnp.where(hit, 1.0, gsel)
        cur = jnp.where(hit, neg, cur)
    emask = jnp.concatenate(
        [jnp.broadcast_to(gsel[g:g + 1], (GROUP_SIZE, t)) for g in range(N_GROUPS)], axis=0) > 0.5
    masked = jnp.where(emask, biased, neg)

    rows_e = lax.broadcasted_iota(jnp.int32, (N_EXPERTS, t), 0)
    idxs, ws = [], []
    for _ in range(TOP_K):
        _, ei = _first_argmax(masked, rows_e, N_EXPERTS)
        hit = rows_e == ei
        idxs.append(ei)
        ws.append(jnp.sum(jnp.where(hit, scores, 0.0), axis=0, keepdims=True))
        masked = jnp.where(hit, neg, masked)
    wsum = ws[0]
    for wk in ws[1:]:
        wsum = wsum + wk

    onehot = jnp.zeros((N_EXPERTS, t), _F32)
    for ei in idxs:
        onehot = onehot + jnp.where(rows_e == ei, 1.0, 0.0)
    before = jnp.dot(onehot.astype(_BF16), upper_ref[...], preferred_element_type=_F32) + carry_ref[...]
    for kk in range(TOP_K):
        idx_ref[kk:kk + 1, :] = idxs[kk]
        w_ref[kk:kk + 1, :] = ws[kk] / wsum * ROUTED_SCALE
        rank_ref[kk:kk + 1, :] = jnp.sum(
            jnp.where(rows_e == idxs[kk], before, 0.0), axis=0, keepdims=True).astype(jnp.int32)
    total = carry_ref[...] + jnp.sum(onehot, axis=1, keepdims=True)
    carry_ref[...] = total
    cnt_ref[...] = jnp.broadcast_to(total, cnt_ref.shape)


def _post_attention(x, oT, mod, gmla, ggqa, woutT, gpost, gffn, wrh, wrl, br):
    upper = jnp.asarray(np.triu(np.ones((T_POST, T_POST), np.float32), 1), _BF16)
    b, s, d = x.shape
    n = b * s
    t = T_POST
    nt = s // t
    assert oT.shape[2:] == (nt, V_DIM, t)
    full = lambda a: pl.BlockSpec(a.shape, lambda bi, i: (0,) * a.ndim)
    tok = lambda bi, i: (0, bi * nt + i)
    return pl.pallas_call(
        _post_kernel,
        out_shape=(jax.ShapeDtypeStruct((b, s, d), _F32),
                   jax.ShapeDtypeStruct((n * PACK_ROWS, LANES), _U32),
                   jax.ShapeDtypeStruct((TOP_K, n), jnp.int32),
                   jax.ShapeDtypeStruct((TOP_K, n), _F32),
                   jax.ShapeDtypeStruct((TOP_K, n), jnp.int32),
                   jax.ShapeDtypeStruct((N_EXPERTS, LANES), _F32)),
        grid=(b, nt),
        in_specs=[pl.BlockSpec((1, t, d), lambda bi, i: (bi, i, 0)),
                  pl.BlockSpec((1, N_HEADS, 1, V_DIM, t), lambda bi, i: (bi, 0, i, 0, 0)),
                  pl.BlockSpec((1, N_MOD, d), lambda bi, i: (bi, 0, 0)),
                  full(gmla), full(ggqa), full(woutT), full(gpost), full(gffn),
                  full(wrh), full(wrl), full(br), full(upper)],
        out_specs=(pl.BlockSpec((1, t, d), lambda bi, i: (bi, i, 0)),
                   pl.BlockSpec((t * PACK_ROWS, LANES), lambda bi, i: (bi * nt + i, 0)),
                   pl.BlockSpec((TOP_K, t), tok),
                   pl.BlockSpec((TOP_K, t), tok),
                   pl.BlockSpec((TOP_K, t), tok),
                   pl.BlockSpec((N_EXPERTS, LANES), lambda bi, i: (0, 0))),
        scratch_shapes=[pltpu.VMEM((N_EXPERTS, 1), _F32)],
        compiler_params=pltpu.CompilerParams(
            dimension_semantics=("arbitrary", "arbitrary"), vmem_limit_bytes=VMEM_LIMIT),
        name="out_proj_router",
    )(x, oT, mod, gmla, ggqa, woutT, gpost, gffn, wrh, wrl, br, upper)


def _slot_kernel(idx_ref, rank_ref, start_ref, dest_ref):
    t = idx_ref.shape[1]
    rows_e = lax.broadcasted_iota(jnp.int32, (N_EXPERTS, t), 0)
    start = start_ref[...]
    for kk in range(TOP_K):
        base = jnp.sum(jnp.where(rows_e == idx_ref[kk:kk + 1, :], start, 0.0), axis=0, keepdims=True)
        dest_ref[kk:kk + 1, :] = rank_ref[kk:kk + 1, :] + base.astype(jnp.int32)


def _slots(idx_t, rank_t, start):
    n = idx_t.shape[1]
    t = T_POST
    return pl.pallas_call(
        _slot_kernel,
        out_shape=jax.ShapeDtypeStruct((TOP_K, n), jnp.int32),
        grid=(n // t,),
        in_specs=[pl.BlockSpec((TOP_K, t), lambda i: (0, i)),
                  pl.BlockSpec((TOP_K, t), lambda i: (0, i)),
                  pl.BlockSpec((N_EXPERTS, 1), lambda i: (0, 0))],
        out_specs=pl.BlockSpec((TOP_K, t), lambda i: (0, i)),
        compiler_params=pltpu.CompilerParams(dimension_semantics=("arbitrary",)),
        name="slot_index",
    )(idx_t, rank_t, start.astype(_F32).reshape(N_EXPERTS, 1))


def _row_copy(src, src_row, dst, dst_row, sem):
    return pltpu.make_async_copy(_token_rows(src, src_row), _token_rows(dst, dst_row), sem)


def _dispatch_kernel(pad_start_ref, pad_len_ref, nused_ref, dest_ref, h_ref, xs_ref, zbuf, sem, zsem, *,
                     n_blocks):
    t_rows = h_ref.shape[0] // PACK_ROWS

    def issue(t, _):
        for kk in range(TOP_K):
            _row_copy(h_ref, t, xs_ref, dest_ref[0, 0, t * TOP_K + kk], sem).start(priority=kk % 2)
        return 0

    lax.fori_loop(0, t_rows, issue, 0)

    def drain(t, _):
        for kk in range(TOP_K):
            _row_copy(h_ref, 0, xs_ref, 0, sem).wait()
        return 0

    lax.fori_loop(0, t_rows, drain, 0)

    @pl.when(pl.program_id(0) == pl.num_programs(0) - 1)
    def _():
        zbuf[...] = jnp.zeros_like(zbuf)

        def zero_copy(first, count):
            return pltpu.make_async_copy(_token_rows(zbuf, 0, count), _token_rows(xs_ref, first, count), zsem)

        def sweep(act):
            def expert_pad(e, _):
                first = pad_start_ref[e]
                for piece in [T_M >> sh for sh in range(1, T_M.bit_length())]:
                    has = pad_len_ref[e] & piece

                    @pl.when(has != 0)
                    def _():
                        act(zero_copy(first, piece))

                    first = first + has
                return 0

            lax.fori_loop(0, N_EXPERTS, expert_pad, 0)

            def spare_block(blk, _):
                act(zero_copy(blk * T_M, T_M))
                return 0

            lax.fori_loop(nused_ref[0], n_blocks, spare_block, 0)

        sweep(lambda c: c.start())
        sweep(lambda c: c.wait())


def _dispatch(pad_start, pad_len, nused, dest_tiles, hp, n_slots):
    n = hp.shape[0] // PACK_ROWS
    t = T_ROW
    return pl.pallas_call(
        functools.partial(_dispatch_kernel, n_blocks=n_slots // T_M),
        out_shape=jax.ShapeDtypeStruct((n_slots * PACK_ROWS, LANES), _U32),
        grid_spec=pltpu.PrefetchScalarGridSpec(
            num_scalar_prefetch=3,
            grid=(n // t,),
            in_specs=[pl.BlockSpec((1, 1, t * TOP_K), lambda i, ps, pn, nu: (i, 0, 0), memory_space=pltpu.SMEM),
                      pl.BlockSpec((t * PACK_ROWS, LANES), lambda i, ps, pn, nu: (i, 0))],
            out_specs=pl.BlockSpec(memory_space=pl.ANY),
            scratch_shapes=[pltpu.VMEM((T_M * PACK_ROWS, LANES), _U32),
                            pltpu.SemaphoreType.DMA(()), pltpu.SemaphoreType.DMA(())]),
        compiler_params=pltpu.CompilerParams(
            dimension_semantics=("arbitrary",), vmem_limit_bytes=VMEM_LIMIT),
        name="dispatch",
    )(pad_start, pad_len, nused, dest_tiles, hp)


def _swiglu(xb, wgu, wd):
    gu = jnp.dot(xb, wgu, preferred_element_type=_F32)
    g, u = gu[:, :EXPERT_FF], gu[:, EXPERT_FF:]
    mid = (g * jax.nn.sigmoid(g) * u).astype(_BF16)
    return jnp.dot(mid, wd, preferred_element_type=_F32)


def _load_packed_bf16(ref, rows):
    return jnp.concatenate([p.astype(_BF16) for p in _load_packed(ref, 0, rows)], axis=1)


def _expert_kernel(nblk_ref, bstart_ref, nused_ref, xs_ref, wg_ref, wu_ref, wd_ref, ys_ref,
                   xbuf, ybuf, xsem, ysem, wgu_s, wd_s, *, n_blocks):
    e = pl.program_id(0)
    nb = nblk_ref[e]
    b0 = bstart_ref[e]
    nused = nused_ref[0]

    def x_copy(blk):
        slot = lax.rem(blk, X_RING)
        return pltpu.make_async_copy(_token_rows(xs_ref, blk * T_M, T_M), _token_rows(xbuf, slot * T_M, T_M),
                                     xsem.at[slot])

    def y_copy(blk):
        slot = lax.rem(blk, Y_RING)
        return pltpu.make_async_copy(_token_rows(ybuf, slot * T_M, T_M), _token_rows(ys_ref, blk * T_M, T_M),
                                     ysem.at[slot])

    @pl.when(e == 0)
    def _():
        for g in range(X_RING - 1):
            @pl.when(g < nused)
            def _():
                x_copy(g).start()

    @pl.when(nb > 0)
    def _():
        wgu_s[:, :EXPERT_FF] = wg_ref[0].astype(_BF16)
        wgu_s[:, EXPERT_FF:] = wu_ref[0].astype(_BF16)
        wd_s[...] = wd_ref[0].astype(_BF16)

        def block(g, _):
            x_copy(g).wait()

            @pl.when(g + X_RING - 1 < nused)
            def _():
                x_copy(g + X_RING - 1).start(priority=1)

            @pl.when(g >= Y_RING)
            def _():
                y_copy(g - Y_RING).wait()

            xb = jnp.concatenate(
                [p.astype(_BF16) for p in _load_packed(xbuf, lax.rem(g, X_RING) * T_M, T_M)], axis=1)
            _store_packed(ybuf, _swiglu(xb, wgu_s[...], wd_s[...]), lax.rem(g, Y_RING) * T_M)
            y_copy(g).start(priority=1)
            return 0

        lax.fori_loop(b0, b0 + nb, block, 0)

    @pl.when(e == pl.num_programs(0) - 1)
    def _():
        for back in range(Y_RING, 0, -1):
            @pl.when(nused >= back)
            def _():
                y_copy(nused - back).wait()

        ybuf[pl.ds(0, T_M * PACK_ROWS), :] = jnp.zeros((T_M * PACK_ROWS, LANES), _U32)

        def zero_copy(blk):
            return pltpu.make_async_copy(_token_rows(ybuf, 0, T_M), _token_rows(ys_ref, blk * T_M, T_M), ysem.at[0])

        def fill(blk, _):
            zero_copy(blk).start()
            return 0

        def drain(blk, _):
            zero_copy(blk).wait()
            return 0

        lax.fori_loop(nused, n_blocks, fill, 0)
        lax.fori_loop(nused, n_blocks, drain, 0)


def _experts(nblk, bstart, nused, xs, wg, wu, wd):
    n_slots = xs.shape[0] // PACK_ROWS
    n_exp, d, ff = wg.shape
    ring = lambda depth: pltpu.VMEM((depth * T_M * PACK_ROWS, LANES), _U32)
    return pl.pallas_call(
        functools.partial(_expert_kernel, n_blocks=n_slots // T_M),
        out_shape=jax.ShapeDtypeStruct(xs.shape, _U32),
        grid_spec=pltpu.PrefetchScalarGridSpec(
            num_scalar_prefetch=3,
            grid=(n_exp,),
            in_specs=[pl.BlockSpec(memory_space=pl.ANY),
                      pl.BlockSpec((1, d, ff), lambda i, nbk, bst, nu: (i, 0, 0)),
                      pl.BlockSpec((1, d, ff), lambda i, nbk, bst, nu: (i, 0, 0)),
                      pl.BlockSpec((1, ff, d), lambda i, nbk, bst, nu: (i, 0, 0))],
            out_specs=pl.BlockSpec(memory_space=pl.ANY),
            scratch_shapes=[ring(X_RING), ring(Y_RING),
                            pltpu.SemaphoreType.DMA((X_RING,)), pltpu.SemaphoreType.DMA((Y_RING,)),
                            pltpu.VMEM((d, 2 * ff), _BF16), pltpu.VMEM((ff, d), _BF16)]),
        compiler_params=pltpu.CompilerParams(
            dimension_semantics=("arbitrary",), vmem_limit_bytes=VMEM_LIMIT),
        name="experts",
    )(nblk, bstart, nused, xs, wg, wu, wd)


def _combine_kernel(dest_ref, dest_next_ref, ys_ref, hp_ref, w_ref, x1_ref, mod_ref, wgus_ref, wds_ref,
                    gpost_ref, o_ref, gbuf, sem):
    i = pl.program_id(0)
    t_rows = hp_ref.shape[0] // PACK_ROWS
    tile_rows = TOP_K * t_rows

    def gather(d_ref, slot):
        def issue(t, _):
            for kk in range(TOP_K):
                _row_copy(ys_ref, d_ref[0, 0, t * TOP_K + kk], gbuf, slot * tile_rows + kk * t_rows + t,
                          sem.at[slot]).start(priority=kk % 2)
            return 0

        lax.fori_loop(0, t_rows, issue, 0)

    slot = lax.rem(i, 2)

    @pl.when(i == 0)
    def _():
        gather(dest_ref, 0)

    @pl.when(i + 1 < pl.num_programs(0))
    def _():
        gather(dest_next_ref, 1 - slot)

    shared = _swiglu(_load_packed_bf16(hp_ref, t_rows), wgus_ref[...], wds_ref[...])

    def drain(t, _):
        for kk in range(TOP_K):
            _row_copy(ys_ref, 0, gbuf, 0, sem.at[slot]).wait()
        return 0

    lax.fori_loop(0, t_rows, drain, 0)

    w = w_ref[...]
    cols = None
    for kk in range(TOP_K):
        wk = w[:, kk:kk + 1]
        parts = [wk * p for p in _load_packed(gbuf, slot * tile_rows + kk * t_rows, t_rows)]
        cols = parts if cols is None else [a + b for a, b in zip(cols, parts)]
    y = jnp.concatenate(cols, axis=1) + shared
    gate_f = mod_ref[0, 5:6, :]
    o_ref[...] = x1_ref[...] + (y * _rsqrt_mean_sq(y, -1)) * (gate_f * gpost_ref[...])


def _combine(dest_tiles, ys, hp, w_tok, x1, mod, wgus, wds, gpost, tiles_per_batch):
    n, d = x1.shape
    t = T_ROW
    nt = n // t
    full = lambda a: pl.BlockSpec(a.shape, lambda i: (0,) * a.ndim)
    dest_spec = lambda shift: pl.BlockSpec(
        (1, 1, t * TOP_K), lambda i: (jnp.minimum(i + shift, nt - 1), 0, 0), memory_space=pltpu.SMEM)
    return pl.pallas_call(
        _combine_kernel,
        out_shape=jax.ShapeDtypeStruct((n, d), _F32),
        grid=(nt,),
        in_specs=[dest_spec(0), dest_spec(1),
                  pl.BlockSpec(memory_space=pl.ANY),
                  pl.BlockSpec((t * PACK_ROWS, LANES), lambda i: (i, 0)),
                  pl.BlockSpec((t, TOP_K), lambda i: (i, 0)),
                  pl.BlockSpec((t, d), lambda i: (i, 0)),
                  pl.BlockSpec((1, N_MOD, d), lambda i: (i // tiles_per_batch, 0, 0)),
                  full(wgus), full(wds), full(gpost)],
        out_specs=pl.BlockSpec((t, d), lambda i: (i, 0)),
        scratch_shapes=[pltpu.VMEM((2 * TOP_K * t * PACK_ROWS, LANES), _U32), pltpu.SemaphoreType.DMA((2,))],
        compiler_params=pltpu.CompilerParams(
            dimension_semantics=("arbitrary",), vmem_limit_bytes=VMEM_LIMIT),
        name="combine",
    )(dest_tiles, dest_tiles, ys, hp, w_tok, x1, mod, wgus, wds, gpost)


def _deinterleave(n):
    return np.concatenate([np.arange(0, n, 2), np.arange(1, n, 2)])


def _rope_tables(s, dim):
    rows = s // GRID_W
    row = jnp.broadcast_to(jnp.arange(rows, dtype=_F32)[:, None], (rows, GRID_W)).reshape(s)
    col = jnp.broadcast_to(jnp.arange(GRID_W, dtype=_F32)[None, :], (rows, GRID_W)).reshape(s)
    n_freq = dim // 4
    inv_freq = ROPE_THETA ** (-jnp.arange(n_freq, dtype=_F32) / n_freq)
    ang = jnp.concatenate([row[:, None] * inv_freq, col[:, None] * inv_freq], axis=-1)
    return jnp.cos(ang).T, jnp.sin(ang).T


def _in_proj_perm():
    perm = list(range(OFF_KROPE))
    perm += list(OFF_KROPE + _deinterleave(MLA_ROPE))
    for hd in range(GQA_HEADS):
        perm += list(OFF_QG + hd * GQA_HEAD_DIM + _deinterleave(GQA_HEAD_DIM))
    for hd in range(GQA_KV_HEADS):
        perm += list(OFF_KG + hd * GQA_HEAD_DIM + _deinterleave(GQA_HEAD_DIM))
    perm += list(range(OFF_VG, IN_COLS))
    return np.asarray(perm)


def _q_b_perm():
    per = MLA_NOPE + MLA_ROPE
    nope = [hd * per + j for hd in range(MLA_HEADS) for j in range(MLA_NOPE)]
    even = [hd * per + MLA_NOPE + 2 * i for hd in range(MLA_HEADS) for i in range(MLA_ROPE // 2)]
    odd = [hd * per + MLA_NOPE + 2 * i + 1 for hd in range(MLA_HEADS) for i in range(MLA_ROPE // 2)]
    return np.asarray(nope + even + odd)


def _kv_b_perm():
    per = MLA_NOPE + MLA_V
    kk = [hd * per + j for hd in range(MLA_HEADS) for j in range(MLA_NOPE)]
    vv = [hd * per + MLA_NOPE + j for hd in range(MLA_HEADS) for j in range(MLA_V)]
    return np.asarray(kk + vv)


def _layer(x, c, w_ada, b_ada, g_pre_mix, g_post_mix, g_pre_ffn, g_post_ffn, w_in, g_q_a, w_q_b,
           g_kv_a, w_kv_b, g_q_head, g_k_head, g_mla_out, g_gqa_out, w_out, w_router, b_router,
           w_gate_e, w_up_e, w_down_e, w_gate_s, w_up_s, w_down_s):
    b, s, d = x.shape
    n = b * s
    col = lambda g: g.reshape(-1, 1).astype(_F32)
    row = lambda g: g.reshape(1, -1).astype(_F32)

    mod = _ada_mod(c, w_ada, b_ada)

    winT = w_in[:, _in_proj_perm()].T.astype(_BF16)
    wqbT = w_q_b[:, _q_b_perm()].T.astype(_BF16)
    wkvbT = w_kv_b[:, _kv_b_perm()].T.astype(_BF16)
    perm_h = _deinterleave(GQA_HEAD_DIM)
    cosm, sinm = _rope_tables(s, MLA_ROPE)
    cosg, sing = _rope_tables(s, GQA_HEAD_DIM)

    qT, k, vT = _project(x, mod, row(g_pre_mix), winT, col(g_q_a), wqbT, col(g_kv_a), wkvbT,
                         col(g_q_head[perm_h]), col(g_k_head[perm_h]), cosm, sinm, cosg, sing)
    oT = _attention(qT, k, vT)

    wr_t = w_router.T.astype(_F32)
    wrh = wr_t.astype(_BF16)
    wrl = (wr_t - wrh.astype(_F32)).astype(_BF16)
    x1, hp, idx_t, w_t, rank_t, cnt = _post_attention(
        x, oT, mod, col(g_mla_out), col(g_gqa_out), w_out.T.astype(_BF16), row(g_post_mix),
        row(g_pre_ffn), wrh, wrl, col(b_router))

    counts = cnt[:, 0].astype(jnp.int32)
    blocks_e = (counts + T_M - 1) // T_M
    blk_end = jnp.cumsum(blocks_e)
    start = (blk_end - blocks_e) * T_M
    n_blocks = n * TOP_K // T_M + N_EXPERTS
    nused = blk_end[-1:].astype(jnp.int32)
    bstart = (blk_end - blocks_e).astype(jnp.int32)
    dest = _slots(idx_t, rank_t, start)
    dest_tiles = dest.T.reshape(n // T_ROW, 1, T_ROW * TOP_K)

    xs = _dispatch((start + counts).astype(jnp.int32), (blocks_e * T_M - counts).astype(jnp.int32), nused,
                   dest_tiles, hp, n_blocks * T_M)
    ys = _experts(blocks_e.astype(jnp.int32), bstart, nused, xs, w_gate_e, w_up_e, w_down_e)
    wgus = jnp.concatenate([w_gate_s, w_up_s], axis=1).astype(_BF16)
    out = _combine(dest_tiles, ys, hp, w_t.T, x1.reshape(n, d), mod, wgus, w_down_s.astype(_BF16),
                   row(g_post_ffn), s // T_ROW)
    return out.reshape(b, s, d)


def kernel(x, c, w_ada, b_ada, g_pre_mix, g_post_mix, g_pre_ffn, g_post_ffn, w_in, g_q_a, w_q_b, g_kv_a, w_kv_b, g_q_head, g_k_head, g_mla_out, g_gqa_out, w_out, w_router, b_router, w_gate_e, w_up_e, w_down_e, w_gate_s, w_up_s, w_down_s):
    depth = w_ada.shape[0]
    for l in range(depth):
        x = _layer(x, c, w_ada[l], b_ada[l], g_pre_mix[l], g_post_mix[l], g_pre_ffn[l], g_post_ffn[l],
                   w_in[l], g_q_a[l], w_q_b[l], g_kv_a[l], w_kv_b[l], g_q_head[l], g_k_head[l],
                   g_mla_out[l], g_gqa_out[l], w_out[l], w_router[l], b_router[l], w_gate_e[l],
                   w_up_e[l], w_down_e[l], w_gate_s[l], w_up_s[l], w_down_s[l])
    return x
```

```python
import functools

import jax
import jax.numpy as jnp
import numpy as np
from jax import lax
from jax.experimental import pallas as pl
from jax.experimental.pallas import tpu as pltpu

D_MODEL = 1024
GRID_W = 64
ROPE_THETA = 10000.0
NORM_EPS = 1e-6

MLA_HEADS = 8
MLA_Q_LORA = 256
MLA_KV_LORA = 128
MLA_NOPE = 64
MLA_ROPE = 32
MLA_V = 64

GQA_HEADS = 8
GQA_KV_HEADS = 2
GQA_HEAD_DIM = 64
GQA_GROUP = GQA_HEADS // GQA_KV_HEADS

LANES = 128
SUBLANES = 8
VMEM_BYTES = 64 * 1024 * 1024

N_HEADS = MLA_HEADS + GQA_HEADS
N_KV = MLA_HEADS + GQA_KV_HEADS
HEAD_PAD = LANES
V_DIM = 64
V_ROWS = 80

N_EXPERTS = 256
TOP_K = 8
N_GROUPS = 8
GROUP_SIZE = N_EXPERTS // N_GROUPS
TOPK_GROUPS = 4
EXPERT_FF = 256
ROUTED_SCALE = 2.5
N_MOD = 6

OFF_QLAT = 0
OFF_KVLAT = OFF_QLAT + MLA_Q_LORA
OFF_KROPE = OFF_KVLAT + MLA_KV_LORA
OFF_QG = OFF_KROPE + MLA_ROPE
OFF_KG = OFF_QG + GQA_HEADS * GQA_HEAD_DIM
OFF_VG = OFF_KG + GQA_KV_HEADS * GQA_HEAD_DIM
IN_COLS = OFF_VG + GQA_KV_HEADS * GQA_HEAD_DIM

T_PROJ = 512
T_POST = 512
T_ROW = 256
T_M = 256
T_ADA = 1536
PACK_ROWS = D_MODEL // (2 * LANES)
N_SCORE_SLOTS = 4
ATTN_GROUP = 16
X_RING = 6
X_AHEAD = 4
Y_RING = 4
LOG2_E = 1.4426950408889634
VMEM_LIMIT = VMEM_BYTES * 7 // 8

_NT = (((1,), (1,)), ((), ()))
_F32 = jnp.float32
_BF16 = jnp.bfloat16
_U32 = jnp.uint32
_HI_MASK = np.uint32(0xFFFF0000)


def _rsqrt_mean_sq(v, axis):
    return lax.rsqrt(jnp.mean(v * v, axis=axis, keepdims=True) + NORM_EPS)


def _pack_pair(lo, hi):
    lo_b = pltpu.bitcast(lo.astype(_BF16).astype(_F32), _U32) >> 16
    hi_b = pltpu.bitcast(hi.astype(_BF16).astype(_F32), _U32) & _HI_MASK
    return lo_b | hi_b


def _unpack_pair(u):
    return pltpu.bitcast(u << 16, _F32), pltpu.bitcast(u & _HI_MASK, _F32)


def _token_rows(ref, first, count=1):
    return ref.at[pl.ds(pl.multiple_of(first * PACK_ROWS, PACK_ROWS), count * PACK_ROWS), :]


def _chunk_of_tokens(first, count, c):
    return pl.ds(first * PACK_ROWS + c, count, stride=PACK_ROWS)


def _store_packed(ref, val, first=0):
    rows = val.shape[0]
    for c in range(PACK_ROWS):
        lo = 2 * c * LANES
        ref[_chunk_of_tokens(first, rows, c), :] = _pack_pair(val[:, lo:lo + LANES], val[:, lo + LANES:lo + 2 * LANES])


def _load_packed(ref, first, rows):
    parts = []
    for c in range(PACK_ROWS):
        parts += list(_unpack_pair(ref[_chunk_of_tokens(first, rows, c), :]))
    return parts


def _ada_kernel(c_ref, w_ref, b_ref, o_ref):
    c = c_ref[...]
    a = (c * jax.nn.sigmoid(c)).astype(_BF16)
    o_ref[...] = jnp.dot(a, w_ref[...].astype(_BF16), preferred_element_type=_F32) + b_ref[...]


def _ada_mod(c, w_ada, b_ada):
    b, d = c.shape
    cols = w_ada.shape[1]
    rows = SUBLANES
    tn = T_ADA
    c_pad = jnp.zeros((rows, d), _F32).at[:b].set(c)
    out = pl.pallas_call(
        _ada_kernel,
        out_shape=jax.ShapeDtypeStruct((rows, cols), _F32),
        grid=(cols // tn,),
        in_specs=[pl.BlockSpec((rows, d), lambda j: (0, 0)),
                  pl.BlockSpec((d, tn), lambda j: (0, j)),
                  pl.BlockSpec((1, tn), lambda j: (0, j))],
        out_specs=pl.BlockSpec((rows, tn), lambda j: (0, j)),
        compiler_params=pltpu.CompilerParams(vmem_limit_bytes=VMEM_LIMIT),
        name="ada_mod",
    )(c_pad, w_ada, b_ada.reshape(1, cols))
    return out[:b].reshape(b, N_MOD, d)


def _rope(x0, x1, cos, sin):
    return x0 * cos - x1 * sin, x0 * sin + x1 * cos


def _proj_kernel(x_ref, mod_ref, gpre_ref, win_ref, gqa_ref, wqb_ref, gkva_ref, wkvb_ref,
                 gqh_ref, gkh_ref, cosm_ref, sinm_ref, cosg_ref, sing_ref,
                 qT_ref, k_ref, vT_ref):
    x = x_ref[0]
    t = x.shape[0]
    shift = mod_ref[0, 0:1, :]
    scale = mod_ref[0, 1:2, :]
    h = x * _rsqrt_mean_sq(x, -1) * (gpre_ref[...] * (1.0 + scale)) + shift
    pT = lax.dot_general(win_ref[...], h.astype(_BF16), _NT, preferred_element_type=_F32)

    cm, sm = cosm_ref[...], sinm_ref[...]
    cg, sg = cosg_ref[...], sing_ref[...]
    half_m = MLA_ROPE // 2
    half_g = GQA_HEAD_DIM // 2
    pad_rows = lax.broadcasted_iota(jnp.int32, (V_ROWS - V_DIM, t), 0)
    ones_row = jnp.where(pad_rows == 0, 1.0, 0.0).astype(_BF16)

    ql = pT[OFF_QLAT:OFF_QLAT + MLA_Q_LORA]
    qn = (ql * _rsqrt_mean_sq(ql, 0) * gqa_ref[...]).astype(_BF16)
    qT = jnp.dot(wqb_ref[...], qn, preferred_element_type=_F32)
    sc_m = float((MLA_NOPE + MLA_ROPE) ** -0.5 * LOG2_E)
    n_nope = MLA_HEADS * MLA_NOPE
    n_half = MLA_HEADS * half_m
    q0, q1 = _rope(qT[n_nope:n_nope + n_half], qT[n_nope + n_half:],
                   jnp.tile(cm, (MLA_HEADS, 1)), jnp.tile(sm, (MLA_HEADS, 1)))
    zq_m = jnp.zeros((HEAD_PAD - MLA_NOPE - MLA_ROPE, t), _BF16)
    for hd in range(MLA_HEADS):
        qT_ref[0, hd, 0, 0:MLA_NOPE, :] = (qT[hd * MLA_NOPE:(hd + 1) * MLA_NOPE] * sc_m).astype(_BF16)
        qT_ref[0, hd, 0, MLA_NOPE:MLA_NOPE + half_m, :] = (q0[hd * half_m:(hd + 1) * half_m] * sc_m).astype(_BF16)
        qT_ref[0, hd, 0, MLA_NOPE + half_m:MLA_NOPE + MLA_ROPE, :] = (
            q1[hd * half_m:(hd + 1) * half_m] * sc_m).astype(_BF16)
        qT_ref[0, hd, 0, MLA_NOPE + MLA_ROPE:, :] = zq_m

    kl = pT[OFF_KVLAT:OFF_KVLAT + MLA_KV_LORA]
    kn = (kl * _rsqrt_mean_sq(kl, 0) * gkva_ref[...]).astype(_BF16)
    kvT = jnp.dot(wkvb_ref[...], kn, preferred_element_type=_F32)
    k0, k1 = _rope(pT[OFF_KROPE:OFF_KROPE + half_m], pT[OFF_KROPE + half_m:OFF_KROPE + MLA_ROPE], cm, sm)
    zk_m = jnp.zeros((HEAD_PAD - MLA_NOPE - MLA_ROPE, t), _F32)
    for hd in range(MLA_HEADS):
        kTh = jnp.concatenate([kvT[hd * MLA_NOPE:(hd + 1) * MLA_NOPE], k0, k1, zk_m], axis=0)
        k_ref[0, hd] = kTh.T.astype(_BF16)
        vT_ref[0, hd, 0, :V_DIM] = kvT[n_nope + hd * MLA_V:n_nope + (hd + 1) * MLA_V].astype(_BF16)
        vT_ref[0, hd, 0, V_DIM:] = ones_row

    sc_g = float(GQA_HEAD_DIM ** -0.5 * LOG2_E)
    zq_g = jnp.zeros((HEAD_PAD - GQA_HEAD_DIM, t), _BF16)
    for hd in range(GQA_HEADS):
        blk = pT[OFF_QG + hd * GQA_HEAD_DIM:OFF_QG + (hd + 1) * GQA_HEAD_DIM]
        bn = blk * _rsqrt_mean_sq(blk, 0) * gqh_ref[...]
        g0, g1 = _rope(bn[:half_g], bn[half_g:], cg, sg)
        qT_ref[0, MLA_HEADS + hd, 0, 0:half_g, :] = (g0 * sc_g).astype(_BF16)
        qT_ref[0, MLA_HEADS + hd, 0, half_g:GQA_HEAD_DIM, :] = (g1 * sc_g).astype(_BF16)
        qT_ref[0, MLA_HEADS + hd, 0, GQA_HEAD_DIM:, :] = zq_g

    zk_g = jnp.zeros((HEAD_PAD - GQA_HEAD_DIM, t), _F32)
    for hd in range(GQA_KV_HEADS):
        blk = pT[OFF_KG + hd * GQA_HEAD_DIM:OFF_KG + (hd + 1) * GQA_HEAD_DIM]
        bn = blk * _rsqrt_mean_sq(blk, 0) * gkh_ref[...]
        g0, g1 = _rope(bn[:half_g], bn[half_g:], cg, sg)
        kTh = jnp.concatenate([g0, g1, zk_g], axis=0)
        k_ref[0, MLA_HEADS + hd] = kTh.T.astype(_BF16)
        vT_ref[0, MLA_HEADS + hd, 0, :V_DIM] = pT[OFF_VG + hd * V_DIM:OFF_VG + (hd + 1) * V_DIM].astype(_BF16)
        vT_ref[0, MLA_HEADS + hd, 0, V_DIM:] = ones_row


def _project(x, mod, gpre, winT, gqa, wqbT, gkva, wkvbT, gqh, gkh, cosm, sinm, cosg, sing):
    b, s, d = x.shape
    t = T_PROJ
    nt = s // t
    full = lambda a: pl.BlockSpec(a.shape, lambda bi, i: (0,) * a.ndim)
    tab = lambda a: pl.BlockSpec((a.shape[0], t), lambda bi, i: (0, i))
    return pl.pallas_call(
        _proj_kernel,
        out_shape=(jax.ShapeDtypeStruct((b, N_HEADS, nt, HEAD_PAD, t), _BF16),
                   jax.ShapeDtypeStruct((b, N_KV, s, HEAD_PAD), _BF16),
                   jax.ShapeDtypeStruct((b, N_KV, nt, V_ROWS, t), _BF16)),
        grid=(b, nt),
        in_specs=[pl.BlockSpec((1, t, d), lambda bi, i: (bi, i, 0)),
                  pl.BlockSpec((1, N_MOD, d), lambda bi, i: (bi, 0, 0)),
                  full(gpre), full(winT), full(gqa), full(wqbT), full(gkva), full(wkvbT),
                  full(gqh), full(gkh), tab(cosm), tab(sinm), tab(cosg), tab(sing)],
        out_specs=(pl.BlockSpec((1, N_HEADS, 1, HEAD_PAD, t), lambda bi, i: (bi, 0, i, 0, 0)),
                   pl.BlockSpec((1, N_KV, t, HEAD_PAD), lambda bi, i: (bi, 0, i, 0)),
                   pl.BlockSpec((1, N_KV, 1, V_ROWS, t), lambda bi, i: (bi, 0, i, 0, 0))),
        compiler_params=pltpu.CompilerParams(
            dimension_semantics=("arbitrary", "arbitrary"), vmem_limit_bytes=VMEM_LIMIT),
        name="in_proj",
    )(x, mod, gpre, winT, gqa, wqbT, gkva, wkvbT, gqh, gkh, cosm, sinm, cosg, sing)


def _attn_kernel(qT_ref, k_ref, vT_ref, o_ref, *slots, n_tiles, n_chunks, tk, group_size):
    ns = len(slots)
    total = n_tiles * n_chunks
    tq = qT_ref.shape[-1]

    def scores(g, slot):
        tile, j = g // n_chunks, g % n_chunks
        kc = k_ref[0, 0, pl.ds(pl.multiple_of(j * tk, tk), tk), :]
        s = jnp.dot(kc, qT_ref[0, 0, tile], preferred_element_type=_F32)
        slots[slot][...] = s
        return jnp.max(s, axis=0, keepdims=True)

    def accumulate(g, slot, mc, carry, first=False, last=False):
        m, acc = carry
        if first is not False:
            m = jnp.where(first, -1e30, m)
            acc = jnp.where(first, 0.0, acc)
        m_new = jnp.maximum(m, mc)
        alpha = jnp.exp2(m - m_new)
        p = jnp.exp2(slots[slot][...] - m_new).astype(_BF16)
        acc = alpha * acc + jnp.dot(vT_ref[0, 0, g % n_chunks], p, preferred_element_type=_F32)
        if last is not False:
            o_ref[0, 0, g // n_chunks] = acc[:V_DIM] / acc[V_DIM:V_DIM + 1]
        return m_new, acc

    def group(jj, state):
        mcs, carry = state
        mcs = list(mcs)
        g0 = group_size * jj
        opens = (g0 % n_chunks) == 0
        for u in range(group_size):
            nxt = scores(g0 + u + 2, (u + 2) % ns)
            carry = accumulate(g0 + u, u % ns, mcs[u % ns], carry, first=opens if u == 0 else False,
                               last=(u == group_size - 1))
            mcs[(u + 2) % ns] = nxt
        return tuple(mcs), carry

    carry = (jnp.full((1, tq), -1e30, _F32), jnp.zeros((V_ROWS, tq), _F32))
    zero = jnp.zeros((1, tq), _F32)
    mcs = (scores(0, 0), scores(1, 1)) + (zero,) * (ns - 2)
    n_loop = (total - 2) // group_size
    mcs, carry = lax.fori_loop(0, n_loop, group, (mcs, carry))
    mcs = list(mcs)
    for g in range(n_loop * group_size, total):
        if g + 2 < total:
            mcs[(g + 2) % ns] = scores(g + 2, (g + 2) % ns)
        carry = accumulate(g, g % ns, mcs[g % ns], carry, first=(g % n_chunks == 0), last=(g == total - 1))


def _kv_index(h):
    return jnp.where(h < MLA_HEADS, h, MLA_HEADS + (h - MLA_HEADS) // GQA_GROUP)


def _attention(qT, k, vT):
    b, nh, n_tiles, dp, tq = qT.shape
    s = k.shape[2]
    n_chunks, tk = vT.shape[2], vT.shape[4]
    group_size = min(ATTN_GROUP, n_chunks)
    assert group_size % N_SCORE_SLOTS == 0 and n_chunks % group_size == 0 and n_tiles * n_chunks > group_size
    return pl.pallas_call(
        functools.partial(_attn_kernel, n_tiles=n_tiles, n_chunks=n_chunks, tk=tk, group_size=group_size),
        out_shape=jax.ShapeDtypeStruct((b, nh, n_tiles, V_DIM, tq), _F32),
        grid=(b, nh),
        in_specs=[pl.BlockSpec((1, 1, n_tiles, dp, tq), lambda bi, h: (bi, h, 0, 0, 0)),
                  pl.BlockSpec((1, 1, s, dp), lambda bi, h: (bi, _kv_index(h), 0, 0)),
                  pl.BlockSpec((1, 1, n_chunks, V_ROWS, tk), lambda bi, h: (bi, _kv_index(h), 0, 0, 0))],
        out_specs=pl.BlockSpec((1, 1, n_tiles, V_DIM, tq), lambda bi, h: (bi, h, 0, 0, 0)),
        scratch_shapes=[pltpu.VMEM((tk, tq), _F32) for _ in range(N_SCORE_SLOTS)],
        compiler_params=pltpu.CompilerParams(
            dimension_semantics=("arbitrary", "arbitrary"), vmem_limit_bytes=VMEM_LIMIT),
        name="attention",
    )(qT, k, vT)


def _first_argmax(v, rows, n):
    m = jnp.max(v, axis=0, keepdims=True)
    idx = jnp.min(jnp.where(v == m, rows, n), axis=0, keepdims=True)
    return m, idx


def _post_kernel(x_ref, o_ref, mod_ref, gmla_ref, ggqa_ref, wout_ref, gpost_ref, gffn_ref,
                 wrh_ref, wrl_ref, br_ref, upper_ref,
                 x1_ref, hp_ref, idx_ref, w_ref, rank_ref, cnt_ref, carry_ref):
    first = jnp.logical_and(pl.program_id(0) == 0, pl.program_id(1) == 0)

    @pl.when(first)
    def _():
        carry_ref[...] = jnp.zeros_like(carry_ref)

    x = x_ref[0]
    t = x.shape[0]
    gate_m = mod_ref[0, 2:3, :]
    shift_f = mod_ref[0, 3:4, :]
    scale_f = mod_ref[0, 4:5, :]

    o = o_ref[0, :, 0].reshape(N_HEADS * V_DIM, t)
    half = MLA_HEADS * V_DIM
    om, og = o[:half], o[half:]
    mixedT = jnp.concatenate([om * _rsqrt_mean_sq(om, 0) * gmla_ref[...],
                              og * _rsqrt_mean_sq(og, 0) * ggqa_ref[...]], axis=0).astype(_BF16)
    y = jnp.dot(wout_ref[...], mixedT, preferred_element_type=_F32).T
    x1 = x + (y * _rsqrt_mean_sq(y, -1)) * (gate_m * gpost_ref[...])
    x1_ref[0] = x1
    h2 = x1 * _rsqrt_mean_sq(x1, -1) * (gffn_ref[...] * (1.0 + scale_f)) + shift_f

    _store_packed(hp_ref, h2)

    hh = h2.astype(_BF16)
    hl = (h2 - hh.astype(_F32)).astype(_BF16)
    wrh = wrh_ref[...]
    logits = (lax.dot_general(wrh, hh, _NT, preferred_element_type=_F32)
              + lax.dot_general(wrh, hl, _NT, preferred_element_type=_F32)
              + lax.dot_general(wrl_ref[...], hh, _NT, preferred_element_type=_F32))
    scores = jax.nn.sigmoid(logits)
    biased = scores + br_ref[...]
    neg = -jnp.inf

    rows_g = lax.broadcasted_iota(jnp.int32, (GROUP_SIZE, t), 0)
    gscores = []
    for g in range(N_GROUPS):
        v = biased[g * GROUP_SIZE:(g + 1) * GROUP_SIZE]
        m1, i1 = _first_argmax(v, rows_g, GROUP_SIZE)
        m2 = jnp.max(jnp.where(rows_g == i1, neg, v), axis=0, keepdims=True)
        gscores.append(m1 + m2)
    cur = jnp.concatenate(gscores, axis=0)
    rows_8 = lax.broadcasted_iota(jnp.int32, (N_GROUPS, t), 0)
    gsel = jnp.zeros((N_GROUPS, t), _F32)
    for _ in range(TOPK_GROUPS):
        _, gi = _first_argmax(cur, rows_8, N_GROUPS)
        hit = rows_8 == gi
        gsel = jnp.where(hit, 1.0, gsel)
        cur = jnp.where(hit, neg, cur)
    emask = jnp.concatenate(
        [jnp.broadcast_to(gsel[g:g + 1], (GROUP_SIZE, t)) for g in range(N_GROUPS)], axis=0) > 0.5
    masked = jnp.where(emask, biased, neg)

    rows_e = lax.broadcasted_iota(jnp.int32, (N_EXPERTS, t), 0)
    idxs, ws = [], []
    for _ in range(TOP_K):
        _, ei = _first_argmax(masked, rows_e, N_EXPERTS)
        hit = rows_e == ei
        idxs.append(ei)
        ws.append(jnp.sum(jnp.where(hit, scores, 0.0), axis=0, keepdims=True))
        masked = jnp.where(hit, neg, masked)
    wsum = ws[0]
    for wk in ws[1:]:
        wsum = wsum + wk

    onehot = jnp.zeros((N_EXPERTS, t), _F32)
    for ei in idxs:
        onehot = onehot + jnp.where(rows_e == ei, 1.0, 0.0)
    before = jnp.dot(onehot.astype(_BF16), upper_ref[...], preferred_element_type=_F32) + carry_ref[...]
    for kk in range(TOP_K):
        idx_ref[kk:kk + 1, :] = idxs[kk]
        w_ref[kk:kk + 1, :] = ws[kk] / wsum * ROUTED_SCALE
        rank_ref[kk:kk + 1, :] = jnp.sum(
            jnp.where(rows_e == idxs[kk], before, 0.0), axis=0, keepdims=True).astype(jnp.int32)
    total = carry_ref[...] + jnp.sum(onehot, axis=1, keepdims=True)
    carry_ref[...] = total
    cnt_ref[...] = jnp.broadcast_to(total, cnt_ref.shape)


def _post_attention(x, oT, mod, gmla, ggqa, woutT, gpost, gffn, wrh, wrl, br):
    upper = jnp.asarray(np.triu(np.ones((T_POST, T_POST), np.float32), 1), _BF16)
    b, s, d = x.shape
    n = b * s
    t = T_POST
    nt = s // t
    assert oT.shape[2:] == (nt, V_DIM, t)
    full = lambda a: pl.BlockSpec(a.shape, lambda bi, i: (0,) * a.ndim)
    tok = lambda bi, i: (0, bi * nt + i)
    return pl.pallas_call(
        _post_kernel,
        out_shape=(jax.ShapeDtypeStruct((b, s, d), _F32),
                   jax.ShapeDtypeStruct((n * PACK_ROWS, LANES), _U32),
                   jax.ShapeDtypeStruct((TOP_K, n), jnp.int32),
                   jax.ShapeDtypeStruct((TOP_K, n), _F32),
                   jax.ShapeDtypeStruct((TOP_K, n), jnp.int32),
                   jax.ShapeDtypeStruct((N_EXPERTS, LANES), _F32)),
        grid=(b, nt),
        in_specs=[pl.BlockSpec((1, t, d), lambda bi, i: (bi, i, 0)),
                  pl.BlockSpec((1, N_HEADS, 1, V_DIM, t), lambda bi, i: (bi, 0, i, 0, 0)),
                  pl.BlockSpec((1, N_MOD, d), lambda bi, i: (bi, 0, 0)),
                  full(gmla), full(ggqa), full(woutT), full(gpost), full(gffn),
                  full(wrh), full(wrl), full(br), full(upper)],
        out_specs=(pl.BlockSpec((1, t, d), lambda bi, i: (bi, i, 0)),
                   pl.BlockSpec((t * PACK_ROWS, LANES), lambda bi, i: (bi * nt + i, 0)),
                   pl.BlockSpec((TOP_K, t), tok),
                   pl.BlockSpec((TOP_K, t), tok),
                   pl.BlockSpec((TOP_K, t), tok),
                   pl.BlockSpec((N_EXPERTS, LANES), lambda bi, i: (0, 0))),
        scratch_shapes=[pltpu.VMEM((N_EXPERTS, 1), _F32)],
        compiler_params=pltpu.CompilerParams(
            dimension_semantics=("arbitrary", "arbitrary"), vmem_limit_bytes=VMEM_LIMIT),
        name="out_proj_router",
    )(x, oT, mod, gmla, ggqa, woutT, gpost, gffn, wrh, wrl, br, upper)


def _slot_kernel(idx_ref, rank_ref, start_ref, dest_ref):
    t = idx_ref.shape[1]
    rows_e = lax.broadcasted_iota(jnp.int32, (N_EXPERTS, t), 0)
    start = start_ref[...]
    for kk in range(TOP_K):
        base = jnp.sum(jnp.where(rows_e == idx_ref[kk:kk + 1, :], start, 0.0), axis=0, keepdims=True)
        dest_ref[kk:kk + 1, :] = rank_ref[kk:kk + 1, :] + base.astype(jnp.int32)


def _slots(idx_t, rank_t, start):
    n = idx_t.shape[1]
    t = T_POST
    return pl.pallas_call(
        _slot_kernel,
        out_shape=jax.ShapeDtypeStruct((TOP_K, n), jnp.int32),
        grid=(n // t,),
        in_specs=[pl.BlockSpec((TOP_K, t), lambda i: (0, i)),
                  pl.BlockSpec((TOP_K, t), lambda i: (0, i)),
                  pl.BlockSpec((N_EXPERTS, 1), lambda i: (0, 0))],
        out_specs=pl.BlockSpec((TOP_K, t), lambda i: (0, i)),
        compiler_params=pltpu.CompilerParams(dimension_semantics=("arbitrary",)),
        name="slot_index",
    )(idx_t, rank_t, start.astype(_F32).reshape(N_EXPERTS, 1))


def _row_copy(src, src_row, dst, dst_row, sem):
    return pltpu.make_async_copy(_token_rows(src, src_row), _token_rows(dst, dst_row), sem)


def _dispatch_kernel(pad_start_ref, pad_len_ref, nused_ref, dest_ref, h_ref, xs_ref, zbuf, sem, zsem, *,
                     n_blocks):
    t_rows = h_ref.shape[0] // PACK_ROWS

    def issue(t, _):
        for kk in range(TOP_K):
            _row_copy(h_ref, t, xs_ref, dest_ref[0, 0, t * TOP_K + kk], sem).start(priority=kk % 2)
        return 0

    lax.fori_loop(0, t_rows, issue, 0)

    def drain(t, _):
        for kk in range(TOP_K):
            _row_copy(h_ref, 0, xs_ref, 0, sem).wait()
        return 0

    lax.fori_loop(0, t_rows, drain, 0)

    @pl.when(pl.program_id(0) == pl.num_programs(0) - 1)
    def _():
        zbuf[...] = jnp.zeros_like(zbuf)

        def zero_copy(first, count):
            return pltpu.make_async_copy(_token_rows(zbuf, 0, count), _token_rows(xs_ref, first, count), zsem)

        def sweep(act):
            def expert_pad(e, _):
                first = pad_start_ref[e]
                for piece in [T_M >> sh for sh in range(1, T_M.bit_length())]:
                    has = pad_len_ref[e] & piece

                    @pl.when(has != 0)
                    def _():
                        act(zero_copy(first, piece))

                    first = first + has
                return 0

            lax.fori_loop(0, N_EXPERTS, expert_pad, 0)

            def spare_block(blk, _):
                act(zero_copy(blk * T_M, T_M))
                return 0

            lax.fori_loop(nused_ref[0], n_blocks, spare_block, 0)

        sweep(lambda c: c.start())
        sweep(lambda c: c.wait())


def _dispatch(pad_start, pad_len, nused, dest_tiles, hp, n_slots):
    n = hp.shape[0] // PACK_ROWS
    t = T_ROW
    return pl.pallas_call(
        functools.partial(_dispatch_kernel, n_blocks=n_slots // T_M),
        out_shape=jax.ShapeDtypeStruct((n_slots * PACK_ROWS, LANES), _U32),
        grid_spec=pltpu.PrefetchScalarGridSpec(
            num_scalar_prefetch=3,
            grid=(n // t,),
            in_specs=[pl.BlockSpec((1, 1, t * TOP_K), lambda i, ps, pn, nu: (i, 0, 0), memory_space=pltpu.SMEM),
                      pl.BlockSpec((t * PACK_ROWS, LANES), lambda i, ps, pn, nu: (i, 0))],
            out_specs=pl.BlockSpec(memory_space=pl.ANY),
            scratch_shapes=[pltpu.VMEM((T_M * PACK_ROWS, LANES), _U32),
                            pltpu.SemaphoreType.DMA(()), pltpu.SemaphoreType.DMA(())]),
        compiler_params=pltpu.CompilerParams(
            dimension_semantics=("arbitrary",), vmem_limit_bytes=VMEM_LIMIT),
        name="dispatch",
    )(pad_start, pad_len, nused, dest_tiles, hp)


def _swiglu(xb, wgu, wd):
    gu = jnp.dot(xb, wgu, preferred_element_type=_F32)
    g, u = gu[:, :EXPERT_FF], gu[:, EXPERT_FF:]
    mid = (g * jax.nn.sigmoid(g) * u).astype(_BF16)
    return jnp.dot(mid, wd, preferred_element_type=_F32)


def _load_packed_bf16(ref, rows):
    return jnp.concatenate([p.astype(_BF16) for p in _load_packed(ref, 0, rows)], axis=1)


def _expert_kernel(nblk_ref, bstart_ref, nused_ref, xs_ref, wg_ref, wu_ref, wd_ref, ys_ref,
                   xbuf, ybuf, xsem, ysem, wgu_s, wd_s, *, n_blocks):
    e = pl.program_id(0)
    nb = nblk_ref[e]
    b0 = bstart_ref[e]
    nused = nused_ref[0]

    def x_copy(blk):
        slot = lax.rem(blk, X_RING)
        return pltpu.make_async_copy(_token_rows(xs_ref, blk * T_M, T_M), _token_rows(xbuf, slot * T_M, T_M),
                                     xsem.at[slot])

    def y_copy(blk):
        slot = lax.rem(blk, Y_RING)
        return pltpu.make_async_copy(_token_rows(ybuf, slot * T_M, T_M), _token_rows(ys_ref, blk * T_M, T_M),
                                     ysem.at[slot])

    @pl.when(e == 0)
    def _():
        for g in range(X_AHEAD):
            @pl.when(g < nused)
            def _():
                x_copy(g).start()

    @pl.when(nb > 0)
    def _():
        wgu_s[:, :EXPERT_FF] = wg_ref[0].astype(_BF16)
        wgu_s[:, EXPERT_FF:] = wu_ref[0].astype(_BF16)
        wd_s[...] = wd_ref[0].astype(_BF16)

        def begin(g):
            x_copy(g).wait()

            @pl.when(g + X_AHEAD < nused)
            def _():
                x_copy(g + X_AHEAD).start(priority=1)

            @pl.when(g >= Y_RING)
            def _():
                y_copy(g - Y_RING).wait()

        def compute(g):
            xb = jnp.concatenate(
                [p.astype(_BF16) for p in _load_packed(xbuf, lax.rem(g, X_RING) * T_M, T_M)], axis=1)
            return _swiglu(xb, wgu_s[...], wd_s[...])

        def finish(g, y):
            _store_packed(ybuf, y, lax.rem(g, Y_RING) * T_M)
            y_copy(g).start(priority=1)

        def pair(jj, _):
            g = b0 + 2 * jj
            begin(g)
            begin(g + 1)
            y0, y1 = compute(g), compute(g + 1)
            finish(g, y0)
            finish(g + 1, y1)
            return 0

        lax.fori_loop(0, nb // 2, pair, 0)

        @pl.when(lax.rem(nb, 2) == 1)
        def _():
            g = b0 + nb - 1
            begin(g)
            finish(g, compute(g))

    @pl.when(e == pl.num_programs(0) - 1)
    def _():
        for back in range(Y_RING, 0, -1):
            @pl.when(nused >= back)
            def _():
                y_copy(nused - back).wait()

        ybuf[pl.ds(0, T_M * PACK_ROWS), :] = jnp.zeros((T_M * PACK_ROWS, LANES), _U32)

        def zero_copy(blk):
            return pltpu.make_async_copy(_token_rows(ybuf, 0, T_M), _token_rows(ys_ref, blk * T_M, T_M), ysem.at[0])

        def fill(blk, _):
            zero_copy(blk).start()
            return 0

        def drain(blk, _):
            zero_copy(blk).wait()
            return 0

        lax.fori_loop(nused, n_blocks, fill, 0)
        lax.fori_loop(nused, n_blocks, drain, 0)


def _experts(nblk, bstart, nused, xs, wg, wu, wd):
    n_slots = xs.shape[0] // PACK_ROWS
    n_exp, d, ff = wg.shape
    ring = lambda depth: pltpu.VMEM((depth * T_M * PACK_ROWS, LANES), _U32)
    return pl.pallas_call(
        functools.partial(_expert_kernel, n_blocks=n_slots // T_M),
        out_shape=jax.ShapeDtypeStruct(xs.shape, _U32),
        grid_spec=pltpu.PrefetchScalarGridSpec(
            num_scalar_prefetch=3,
            grid=(n_exp,),
            in_specs=[pl.BlockSpec(memory_space=pl.ANY),
                      pl.BlockSpec((1, d, ff), lambda i, nbk, bst, nu: (i, 0, 0)),
                      pl.BlockSpec((1, d, ff), lambda i, nbk, bst, nu: (i, 0, 0)),
                      pl.BlockSpec((1, ff, d), lambda i, nbk, bst, nu: (i, 0, 0))],
            out_specs=pl.BlockSpec(memory_space=pl.ANY),
            scratch_shapes=[ring(X_RING), ring(Y_RING),
                            pltpu.SemaphoreType.DMA((X_RING,)), pltpu.SemaphoreType.DMA((Y_RING,)),
                            pltpu.VMEM((d, 2 * ff), _BF16), pltpu.VMEM((ff, d), _BF16)]),
        compiler_params=pltpu.CompilerParams(
            dimension_semantics=("arbitrary",), vmem_limit_bytes=VMEM_LIMIT),
        name="experts",
    )(nblk, bstart, nused, xs, wg, wu, wd)


def _combine_kernel(dest_ref, dest_next_ref, ys_ref, hp_ref, w_ref, x1_ref, mod_ref, wgus_ref, wds_ref,
                    gpost_ref, o_ref, gbuf, sem):
    i = pl.program_id(0)
    t_rows = hp_ref.shape[0] // PACK_ROWS
    tile_rows = TOP_K * t_rows

    def gather(d_ref, slot):
        def issue(t, _):
            for kk in range(TOP_K):
                _row_copy(ys_ref, d_ref[0, 0, t * TOP_K + kk], gbuf, slot * tile_rows + kk * t_rows + t,
                          sem.at[slot]).start(priority=kk % 2)
            return 0

        lax.fori_loop(0, t_rows, issue, 0)

    slot = lax.rem(i, 2)

    @pl.when(i == 0)
    def _():
        gather(dest_ref, 0)

    @pl.when(i + 1 < pl.num_programs(0))
    def _():
        gather(dest_next_ref, 1 - slot)

    shared = _swiglu(_load_packed_bf16(hp_ref, t_rows), wgus_ref[...], wds_ref[...])

    def drain(t, _):
        for kk in range(TOP_K):
            _row_copy(ys_ref, 0, gbuf, 0, sem.at[slot]).wait()
        return 0

    lax.fori_loop(0, t_rows, drain, 0)

    w = w_ref[...]
    cols = None
    for kk in range(TOP_K):
        wk = w[:, kk:kk + 1]
        parts = [wk * p for p in _load_packed(gbuf, slot * tile_rows + kk * t_rows, t_rows)]
        cols = parts if cols is None else [a + b for a, b in zip(cols, parts)]
    y = jnp.concatenate(cols, axis=1) + shared
    gate_f = mod_ref[0, 5:6, :]
    o_ref[...] = x1_ref[...] + (y * _rsqrt_mean_sq(y, -1)) * (gate_f * gpost_ref[...])


def _combine(dest_tiles, ys, hp, w_tok, x1, mod, wgus, wds, gpost, tiles_per_batch):
    n, d = x1.shape
    t = T_ROW
    nt = n // t
    full = lambda a: pl.BlockSpec(a.shape, lambda i: (0,) * a.ndim)
    dest_spec = lambda shift: pl.BlockSpec(
        (1, 1, t * TOP_K), lambda i: (jnp.minimum(i + shift, nt - 1), 0, 0), memory_space=pltpu.SMEM)
    return pl.pallas_call(
        _combine_kernel,
        out_shape=jax.ShapeDtypeStruct((n, d), _F32),
        grid=(nt,),
        in_specs=[dest_spec(0), dest_spec(1),
                  pl.BlockSpec(memory_space=pl.ANY),
                  pl.BlockSpec((t * PACK_ROWS, LANES), lambda i: (i, 0)),
                  pl.BlockSpec((t, TOP_K), lambda i: (i, 0)),
                  pl.BlockSpec((t, d), lambda i: (i, 0)),
                  pl.BlockSpec((1, N_MOD, d), lambda i: (i // tiles_per_batch, 0, 0)),
                  full(wgus), full(wds), full(gpost)],
        out_specs=pl.BlockSpec((t, d), lambda i: (i, 0)),
        scratch_shapes=[pltpu.VMEM((2 * TOP_K * t * PACK_ROWS, LANES), _U32), pltpu.SemaphoreType.DMA((2,))],
        compiler_params=pltpu.CompilerParams(
            dimension_semantics=("arbitrary",), vmem_limit_bytes=VMEM_LIMIT),
        name="combine",
    )(dest_tiles, dest_tiles, ys, hp, w_tok, x1, mod, wgus, wds, gpost)


def _deinterleave(n):
    return np.concatenate([np.arange(0, n, 2), np.arange(1, n, 2)])


def _rope_tables(s, dim):
    rows = s // GRID_W
    row = jnp.broadcast_to(jnp.arange(rows, dtype=_F32)[:, None], (rows, GRID_W)).reshape(s)
    col = jnp.broadcast_to(jnp.arange(GRID_W, dtype=_F32)[None, :], (rows, GRID_W)).reshape(s)
    n_freq = dim // 4
    inv_freq = ROPE_THETA ** (-jnp.arange(n_freq, dtype=_F32) / n_freq)
    ang = jnp.concatenate([row[:, None] * inv_freq, col[:, None] * inv_freq], axis=-1)
    return jnp.cos(ang).T, jnp.sin(ang).T


def _in_proj_perm():
    perm = list(range(OFF_KROPE))
    perm += list(OFF_KROPE + _deinterleave(MLA_ROPE))
    for hd in range(GQA_HEADS):
        perm += list(OFF_QG + hd * GQA_HEAD_DIM + _deinterleave(GQA_HEAD_DIM))
    for hd in range(GQA_KV_HEADS):
        perm += list(OFF_KG + hd * GQA_HEAD_DIM + _deinterleave(GQA_HEAD_DIM))
    perm += list(range(OFF_VG, IN_COLS))
    return np.asarray(perm)


def _q_b_perm():
    per = MLA_NOPE + MLA_ROPE
    nope = [hd * per + j for hd in range(MLA_HEADS) for j in range(MLA_NOPE)]
    even = [hd * per + MLA_NOPE + 2 * i for hd in range(MLA_HEADS) for i in range(MLA_ROPE // 2)]
    odd = [hd * per + MLA_NOPE + 2 * i + 1 for hd in range(MLA_HEADS) for i in range(MLA_ROPE // 2)]
    return np.asarray(nope + even + odd)


def _kv_b_perm():
    per = MLA_NOPE + MLA_V
    kk = [hd * per + j for hd in range(MLA_HEADS) for j in range(MLA_NOPE)]
    vv = [hd * per + MLA_NOPE + j for hd in range(MLA_HEADS) for j in range(MLA_V)]
    return np.asarray(kk + vv)


def _layer(x, c, w_ada, b_ada, g_pre_mix, g_post_mix, g_pre_ffn, g_post_ffn, w_in, g_q_a, w_q_b,
           g_kv_a, w_kv_b, g_q_head, g_k_head, g_mla_out, g_gqa_out, w_out, w_router, b_router,
           w_gate_e, w_up_e, w_down_e, w_gate_s, w_up_s, w_down_s):
    b, s, d = x.shape
    n = b * s
    col = lambda g: g.reshape(-1, 1).astype(_F32)
    row = lambda g: g.reshape(1, -1).astype(_F32)

    mod = _ada_mod(c, w_ada, b_ada)

    winT = w_in[:, _in_proj_perm()].T.astype(_BF16)
    wqbT = w_q_b[:, _q_b_perm()].T.astype(_BF16)
    wkvbT = w_kv_b[:, _kv_b_perm()].T.astype(_BF16)
    perm_h = _deinterleave(GQA_HEAD_DIM)
    cosm, sinm = _rope_tables(s, MLA_ROPE)
    cosg, sing = _rope_tables(s, GQA_HEAD_DIM)

    qT, k, vT = _project(x, mod, row(g_pre_mix), winT, col(g_q_a), wqbT, col(g_kv_a), wkvbT,
                         col(g_q_head[perm_h]), col(g_k_head[perm_h]), cosm, sinm, cosg, sing)
    oT = _attention(qT, k, vT)

    wr_t = w_router.T.astype(_F32)
    wrh = wr_t.astype(_BF16)
    wrl = (wr_t - wrh.astype(_F32)).astype(_BF16)
    x1, hp, idx_t, w_t, rank_t, cnt = _post_attention(
        x, oT, mod, col(g_mla_out), col(g_gqa_out), w_out.T.astype(_BF16), row(g_post_mix),
        row(g_pre_ffn), wrh, wrl, col(b_router))

    counts = cnt[:, 0].astype(jnp.int32)
    blocks_e = (counts + T_M - 1) // T_M
    blk_end = jnp.cumsum(blocks_e)
    start = (blk_end - blocks_e) * T_M
    n_blocks = n * TOP_K // T_M + N_EXPERTS
    nused = blk_end[-1:].astype(jnp.int32)
    bstart = (blk_end - blocks_e).astype(jnp.int32)
    dest = _slots(idx_t, rank_t, start)
    dest_tiles = dest.T.reshape(n // T_ROW, 1, T_ROW * TOP_K)

    xs = _dispatch((start + counts).astype(jnp.int32), (blocks_e * T_M - counts).astype(jnp.int32), nused,
                   dest_tiles, hp, n_blocks * T_M)
    ys = _experts(blocks_e.astype(jnp.int32), bstart, nused, xs, w_gate_e, w_up_e, w_down_e)
    wgus = jnp.concatenate([w_gate_s, w_up_s], axis=1).astype(_BF16)
    out = _combine(dest_tiles, ys, hp, w_t.T, x1.reshape(n, d), mod, wgus, w_down_s.astype(_BF16),
                   row(g_post_ffn), s // T_ROW)
    return out.reshape(b, s, d)


def kernel(x, c, w_ada, b_ada, g_pre_mix, g_post_mix, g_pre_ffn, g_post_ffn, w_in, g_q_a, w_q_b, g_kv_a, w_kv_b, g_q_head, g_k_head, g_mla_out, g_gqa_out, w_out, w_router, b_router, w_gate_e, w_up_e, w_down_e, w_gate_s, w_up_s, w_down_s):
    depth = w_ada.shape[0]
    for l in range(depth):
        x = _layer(x, c, w_ada[l], b_ada[l], g_pre_mix[l], g_post_mix[l], g_pre_ffn[l], g_post_ffn[l],
                   w_in[l], g_q_a[l], w_q_b[l], g_kv_a[l], w_kv_b[l], g_q_head[l], g_k_head[l],
                   g_mla_out[l], g_gqa_out[l], w_out[l], w_router[l], b_router[l], w_gate_e[l],
                   w_up_e[l], w_down_e[l], w_gate_s[l], w_up_s[l], w_down_s[l])
    return x
```

```python
import functools

import jax
import jax.numpy as jnp
import numpy as np
from jax import lax
from jax.experimental import pallas as pl
from jax.experimental.pallas import tpu as pltpu

D_MODEL = 1024
GRID_W = 64
ROPE_THETA = 10000.0
NORM_EPS = 1e-6

MLA_HEADS = 8
MLA_Q_LORA = 256
MLA_KV_LORA = 128
MLA_NOPE = 64
MLA_ROPE = 32
MLA_V = 64

GQA_HEADS = 8
GQA_KV_HEADS = 2
GQA_HEAD_DIM = 64
GQA_GROUP = GQA_HEADS // GQA_KV_HEADS

LANES = 128
SUBLANES = 8
VMEM_BYTES = 64 * 1024 * 1024

N_HEADS = MLA_HEADS + GQA_HEADS
N_KV = MLA_HEADS + GQA_KV_HEADS
HEAD_PAD = LANES
V_DIM = 64
V_ROWS = 80

N_EXPERTS = 256
TOP_K = 8
N_GROUPS = 8
GROUP_SIZE = N_EXPERTS // N_GROUPS
TOPK_GROUPS = 4
EXPERT_FF = 256
ROUTED_SCALE = 2.5
N_MOD = 6

OFF_QLAT = 0
OFF_KVLAT = OFF_QLAT + MLA_Q_LORA
OFF_KROPE = OFF_KVLAT + MLA_KV_LORA
OFF_QG = OFF_KROPE + MLA_ROPE
OFF_KG = OFF_QG + GQA_HEADS * GQA_HEAD_DIM
OFF_VG = OFF_KG + GQA_KV_HEADS * GQA_HEAD_DIM
IN_COLS = OFF_VG + GQA_KV_HEADS * GQA_HEAD_DIM

T_PROJ = 512
T_POST = 512
T_ROW = 256
T_M = 256
T_ADA = 1536
PACK_ROWS = D_MODEL // (2 * LANES)
N_SCORE_SLOTS = 4
ATTN_GROUP = 16
X_RING = 6
X_AHEAD = 4
Y_RING = 4
LOG2_E = 1.4426950408889634
VMEM_LIMIT = VMEM_BYTES * 7 // 8

_NT = (((1,), (1,)), ((), ()))
_F32 = jnp.float32
_BF16 = jnp.bfloat16
_U32 = jnp.uint32
_HI_MASK = np.uint32(0xFFFF0000)


def _rsqrt_mean_sq(v, axis):
    return lax.rsqrt(jnp.mean(v * v, axis=axis, keepdims=True) + NORM_EPS)


def _pack_pair(lo, hi):
    lo_b = pltpu.bitcast(lo.astype(_BF16).astype(_F32), _U32) >> 16
    hi_b = pltpu.bitcast(hi.astype(_BF16).astype(_F32), _U32) & _HI_MASK
    return lo_b | hi_b


def _unpack_pair(u):
    return pltpu.bitcast(u << 16, _F32), pltpu.bitcast(u & _HI_MASK, _F32)


def _token_rows(ref, first, count=1):
    return ref.at[pl.ds(pl.multiple_of(first * PACK_ROWS, PACK_ROWS), count * PACK_ROWS), :]


def _chunk_of_tokens(first, count, c):
    return pl.ds(first * PACK_ROWS + c, count, stride=PACK_ROWS)


def _store_packed(ref, val, first=0):
    rows = val.shape[0]
    for c in range(PACK_ROWS):
        lo = 2 * c * LANES
        ref[_chunk_of_tokens(first, rows, c), :] = _pack_pair(val[:, lo:lo + LANES], val[:, lo + LANES:lo + 2 * LANES])


def _load_packed(ref, first, rows):
    parts = []
    for c in range(PACK_ROWS):
        parts += list(_unpack_pair(ref[_chunk_of_tokens(first, rows, c), :]))
    return parts


def _ada_kernel(c_ref, w_ref, b_ref, o_ref):
    c = c_ref[...]
    a = (c * jax.nn.sigmoid(c)).astype(_BF16)
    o_ref[...] = jnp.dot(a, w_ref[...].astype(_BF16), preferred_element_type=_F32) + b_ref[...]


def _ada_mod(c, w_ada, b_ada):
    b, d = c.shape
    cols = w_ada.shape[1]
    rows = SUBLANES
    tn = T_ADA
    c_pad = jnp.zeros((rows, d), _F32).at[:b].set(c)
    out = pl.pallas_call(
        _ada_kernel,
        out_shape=jax.ShapeDtypeStruct((rows, cols), _F32),
        grid=(cols // tn,),
        in_specs=[pl.BlockSpec((rows, d), lambda j: (0, 0)),
                  pl.BlockSpec((d, tn), lambda j: (0, j)),
                  pl.BlockSpec((1, tn), lambda j: (0, j))],
        out_specs=pl.BlockSpec((rows, tn), lambda j: (0, j)),
        compiler_params=pltpu.CompilerParams(vmem_limit_bytes=VMEM_LIMIT),
        name="ada_mod",
    )(c_pad, w_ada, b_ada.reshape(1, cols))
    return out[:b].reshape(b, N_MOD, d)


def _rope(x0, x1, cos, sin):
    return x0 * cos - x1 * sin, x0 * sin + x1 * cos


def _proj_kernel(x_ref, mod_ref, gpre_ref, win_ref, gqa_ref, wqb_ref, gkva_ref, wkvb_ref,
                 gqh_ref, gkh_ref, cosm_ref, sinm_ref, cosg_ref, sing_ref,
                 qT_ref, k_ref, vT_ref):
    x = x_ref[0]
    t = x.shape[0]
    shift = mod_ref[0, 0:1, :]
    scale = mod_ref[0, 1:2, :]
    h = x * _rsqrt_mean_sq(x, -1) * (gpre_ref[...] * (1.0 + scale)) + shift
    pT = lax.dot_general(win_ref[...], h.astype(_BF16), _NT, preferred_element_type=_F32)

    cm, sm = cosm_ref[...], sinm_ref[...]
    cg, sg = cosg_ref[...], sing_ref[...]
    half_m = MLA_ROPE // 2
    half_g = GQA_HEAD_DIM // 2
    pad_rows = lax.broadcasted_iota(jnp.int32, (V_ROWS - V_DIM, t), 0)
    ones_row = jnp.where(pad_rows == 0, 1.0, 0.0).astype(_BF16)

    ql = pT[OFF_QLAT:OFF_QLAT + MLA_Q_LORA]
    qn = (ql * _rsqrt_mean_sq(ql, 0) * gqa_ref[...]).astype(_BF16)
    qT = jnp.dot(wqb_ref[...], qn, preferred_element_type=_F32)
    sc_m = float((MLA_NOPE + MLA_ROPE) ** -0.5 * LOG2_E)
    n_nope = MLA_HEADS * MLA_NOPE
    n_half = MLA_HEADS * half_m
    q0, q1 = _rope(qT[n_nope:n_nope + n_half], qT[n_nope + n_half:],
                   jnp.tile(cm, (MLA_HEADS, 1)), jnp.tile(sm, (MLA_HEADS, 1)))
    zq_m = jnp.zeros((HEAD_PAD - MLA_NOPE - MLA_ROPE, t), _BF16)
    for hd in range(MLA_HEADS):
        qT_ref[0, hd, 0, 0:MLA_NOPE, :] = (qT[hd * MLA_NOPE:(hd + 1) * MLA_NOPE] * sc_m).astype(_BF16)
        qT_ref[0, hd, 0, MLA_NOPE:MLA_NOPE + half_m, :] = (q0[hd * half_m:(hd + 1) * half_m] * sc_m).astype(_BF16)
        qT_ref[0, hd, 0, MLA_NOPE + half_m:MLA_NOPE + MLA_ROPE, :] = (
            q1[hd * half_m:(hd + 1) * half_m] * sc_m).astype(_BF16)
        qT_ref[0, hd, 0, MLA_NOPE + MLA_ROPE:, :] = zq_m

    kl = pT[OFF_KVLAT:OFF_KVLAT + MLA_KV_LORA]
    kn = (kl * _rsqrt_mean_sq(kl, 0) * gkva_ref[...]).astype(_BF16)
    kvT = jnp.dot(wkvb_ref[...], kn, preferred_element_type=_F32)
    k0, k1 = _rope(pT[OFF_KROPE:OFF_KROPE + half_m], pT[OFF_KROPE + half_m:OFF_KROPE + MLA_ROPE], cm, sm)
    zk_m = jnp.zeros((HEAD_PAD - MLA_NOPE - MLA_ROPE, t), _F32)
    for hd in range(MLA_HEADS):
        kTh = jnp.concatenate([kvT[hd * MLA_NOPE:(hd + 1) * MLA_NOPE], k0, k1, zk_m], axis=0)
        k_ref[0, hd] = kTh.T.astype(_BF16)
        vT_ref[0, hd, 0, :V_DIM] = kvT[n_nope + hd * MLA_V:n_nope + (hd + 1) * MLA_V].astype(_BF16)
        vT_ref[0, hd, 0, V_DIM:] = ones_row

    sc_g = float(GQA_HEAD_DIM ** -0.5 * LOG2_E)
    zq_g = jnp.zeros((HEAD_PAD - GQA_HEAD_DIM, t), _BF16)
    for hd in range(GQA_HEADS):
        blk = pT[OFF_QG + hd * GQA_HEAD_DIM:OFF_QG + (hd + 1) * GQA_HEAD_DIM]
        bn = blk * _rsqrt_mean_sq(blk, 0) * gqh_ref[...]
        g0, g1 = _rope(bn[:half_g], bn[half_g:], cg, sg)
        qT_ref[0, MLA_HEADS + hd, 0, 0:half_g, :] = (g0 * sc_g).astype(_BF16)
        qT_ref[0, MLA_HEADS + hd, 0, half_g:GQA_HEAD_DIM, :] = (g1 * sc_g).astype(_BF16)
        qT_ref[0, MLA_HEADS + hd, 0, GQA_HEAD_DIM:, :] = zq_g

    zk_g = jnp.zeros((HEAD_PAD - GQA_HEAD_DIM, t), _F32)
    for hd in range(GQA_KV_HEADS):
        blk = pT[OFF_KG + hd * GQA_HEAD_DIM:OFF_KG + (hd + 1) * GQA_HEAD_DIM]
        bn = blk * _rsqrt_mean_sq(blk, 0) * gkh_ref[...]
        g0, g1 = _rope(bn[:half_g], bn[half_g:], cg, sg)
        kTh = jnp.concatenate([g0, g1, zk_g], axis=0)
        k_ref[0, MLA_HEADS + hd] = kTh.T.astype(_BF16)
        vT_ref[0, MLA_HEADS + hd, 0, :V_DIM] = pT[OFF_VG + hd * V_DIM:OFF_VG + (hd + 1) * V_DIM].astype(_BF16)
        vT_ref[0, MLA_HEADS + hd, 0, V_DIM:] = ones_row


def _project(x, mod, gpre, winT, gqa, wqbT, gkva, wkvbT, gqh, gkh, cosm, sinm, cosg, sing):
    b, s, d = x.shape
    t = T_PROJ
    nt = s // t
    full = lambda a: pl.BlockSpec(a.shape, lambda bi, i: (0,) * a.ndim)
    tab = lambda a: pl.BlockSpec((a.shape[0], t), lambda bi, i: (0, i))
    return pl.pallas_call(
        _proj_kernel,
        out_shape=(jax.ShapeDtypeStruct((b, N_HEADS, nt, HEAD_PAD, t), _BF16),
                   jax.ShapeDtypeStruct((b, N_KV, s, HEAD_PAD), _BF16),
                   jax.ShapeDtypeStruct((b, N_KV, nt, V_ROWS, t), _BF16)),
        grid=(b, nt),
        in_specs=[pl.BlockSpec((1, t, d), lambda bi, i: (bi, i, 0)),
                  pl.BlockSpec((1, N_MOD, d), lambda bi, i: (bi, 0, 0)),
                  full(gpre), full(winT), full(gqa), full(wqbT), full(gkva), full(wkvbT),
                  full(gqh), full(gkh), tab(cosm), tab(sinm), tab(cosg), tab(sing)],
        out_specs=(pl.BlockSpec((1, N_HEADS, 1, HEAD_PAD, t), lambda bi, i: (bi, 0, i, 0, 0)),
                   pl.BlockSpec((1, N_KV, t, HEAD_PAD), lambda bi, i: (bi, 0, i, 0)),
                   pl.BlockSpec((1, N_KV, 1, V_ROWS, t), lambda bi, i: (bi, 0, i, 0, 0))),
        compiler_params=pltpu.CompilerParams(
            dimension_semantics=("arbitrary", "arbitrary"), vmem_limit_bytes=VMEM_LIMIT),
        name="in_proj",
    )(x, mod, gpre, winT, gqa, wqbT, gkva, wkvbT, gqh, gkh, cosm, sinm, cosg, sing)


def _attn_kernel(qT_ref, k_ref, vT_ref, o_ref, *slots, n_tiles, n_chunks, tk, group_size):
    ns = len(slots)
    total = n_tiles * n_chunks
    tq = qT_ref.shape[-1]

    def scores(g, slot):
        tile, j = g // n_chunks, g % n_chunks
        kc = k_ref[0, 0, pl.ds(pl.multiple_of(j * tk, tk), tk), :]
        s = jnp.dot(kc, qT_ref[0, 0, tile], preferred_element_type=_F32)
        slots[slot][:, :tq] = s
        return jnp.max(s, axis=0, keepdims=True)

    def accumulate(g, slot, mc, carry, first=False, last=False):
        m, acc = carry
        if first is not False:
            m = jnp.where(first, -1e30, m)
            acc = jnp.where(first, 0.0, acc)
        m_new = jnp.maximum(m, mc)
        alpha = jnp.exp2(m - m_new)
        p = jnp.exp2(slots[slot][:, :tq] - m_new).astype(_BF16)
        acc = alpha * acc + jnp.dot(vT_ref[0, 0, g % n_chunks], p, preferred_element_type=_F32)
        if last is not False:
            o_ref[0, 0, g // n_chunks] = acc[:V_DIM] / acc[V_DIM:V_DIM + 1]
        return m_new, acc

    def group(jj, state):
        mcs, carry = state
        mcs = list(mcs)
        g0 = group_size * jj
        opens = (g0 % n_chunks) == 0
        for u in range(group_size):
            nxt = scores(g0 + u + 2, (u + 2) % ns)
            carry = accumulate(g0 + u, u % ns, mcs[u % ns], carry, first=opens if u == 0 else False,
                               last=(u == group_size - 1))
            mcs[(u + 2) % ns] = nxt
        return tuple(mcs), carry

    carry = (jnp.full((1, tq), -1e30, _F32), jnp.zeros((V_ROWS, tq), _F32))
    zero = jnp.zeros((1, tq), _F32)
    mcs = (scores(0, 0), scores(1, 1)) + (zero,) * (ns - 2)
    n_loop = (total - 2) // group_size
    mcs, carry = lax.fori_loop(0, n_loop, group, (mcs, carry))
    mcs = list(mcs)
    for g in range(n_loop * group_size, total):
        if g + 2 < total:
            mcs[(g + 2) % ns] = scores(g + 2, (g + 2) % ns)
        carry = accumulate(g, g % ns, mcs[g % ns], carry, first=(g % n_chunks == 0), last=(g == total - 1))


def _kv_index(h):
    return jnp.where(h < MLA_HEADS, h, MLA_HEADS + (h - MLA_HEADS) // GQA_GROUP)


def _attention(qT, k, vT):
    b, nh, n_tiles, dp, tq = qT.shape
    s = k.shape[2]
    n_chunks, tk = vT.shape[2], vT.shape[4]
    group_size = min(ATTN_GROUP, n_chunks)
    assert group_size % N_SCORE_SLOTS == 0 and n_chunks % group_size == 0 and n_tiles * n_chunks > group_size
    return pl.pallas_call(
        functools.partial(_attn_kernel, n_tiles=n_tiles, n_chunks=n_chunks, tk=tk, group_size=group_size),
        out_shape=jax.ShapeDtypeStruct((b, nh, n_tiles, V_DIM, tq), _F32),
        grid=(b, nh),
        in_specs=[pl.BlockSpec((1, 1, n_tiles, dp, tq), lambda bi, h: (bi, h, 0, 0, 0)),
                  pl.BlockSpec((1, 1, s, dp), lambda bi, h: (bi, _kv_index(h), 0, 0)),
                  pl.BlockSpec((1, 1, n_chunks, V_ROWS, tk), lambda bi, h: (bi, _kv_index(h), 0, 0, 0))],
        out_specs=pl.BlockSpec((1, 1, n_tiles, V_DIM, tq), lambda bi, h: (bi, h, 0, 0, 0)),
        scratch_shapes=[pltpu.VMEM((tk, tq + 3 * LANES), _F32) for _ in range(N_SCORE_SLOTS)],
        compiler_params=pltpu.CompilerParams(
            dimension_semantics=("arbitrary", "arbitrary"), vmem_limit_bytes=VMEM_LIMIT),
        name="attention",
    )(qT, k, vT)


def _first_argmax(v, rows, n):
    m = jnp.max(v, axis=0, keepdims=True)
    idx = jnp.min(jnp.where(v == m, rows, n), axis=0, keepdims=True)
    return m, idx


def _post_kernel(x_ref, o_ref, mod_ref, gmla_ref, ggqa_ref, wout_ref, gpost_ref, gffn_ref,
                 wrh_ref, wrl_ref, br_ref, upper_ref,
                 x1_ref, hp_ref, idx_ref, w_ref, rank_ref, cnt_ref, carry_ref):
    first = jnp.logical_and(pl.program_id(0) == 0, pl.program_id(1) == 0)

    @pl.when(first)
    def _():
        carry_ref[...] = jnp.zeros_like(carry_ref)

    x = x_ref[0]
    t = x.shape[0]
    gate_m = mod_ref[0, 2:3, :]
    shift_f = mod_ref[0, 3:4, :]
    scale_f = mod_ref[0, 4:5, :]

    o = o_ref[0, :, 0].reshape(N_HEADS * V_DIM, t)
    half = MLA_HEADS * V_DIM
    om, og = o[:half], o[half:]
    mixedT = jnp.concatenate([om * _rsqrt_mean_sq(om, 0) * gmla_ref[...],
                              og * _rsqrt_mean_sq(og, 0) * ggqa_ref[...]], axis=0).astype(_BF16)
    y = jnp.dot(wout_ref[...], mixedT, preferred_element_type=_F32).T
    x1 = x + (y * _rsqrt_mean_sq(y, -1)) * (gate_m * gpost_ref[...])
    x1_ref[0] = x1
    h2 = x1 * _rsqrt_mean_sq(x1, -1) * (gffn_ref[...] * (1.0 + scale_f)) + shift_f

    _store_packed(hp_ref, h2)

    hh = h2.astype(_BF16)
    hl = (h2 - hh.astype(_F32)).astype(_BF16)
    wrh = wrh_ref[...]
    logits = (lax.dot_general(wrh, hh, _NT, preferred_element_type=_F32)
              + lax.dot_general(wrh, hl, _NT, preferred_element_type=_F32)
              + lax.dot_general(wrl_ref[...], hh, _NT, preferred_element_type=_F32))
    scores = jax.nn.sigmoid(logits)
    biased = scores + br_ref[...]
    neg = -jnp.inf

    rows_g = lax.broadcasted_iota(jnp.int32, (GROUP_SIZE, t), 0)
    gscores = []
    for g in range(N_GROUPS):
        v = biased[g * GROUP_SIZE:(g + 1) * GROUP_SIZE]
        m1, i1 = _first_argmax(v, rows_g, GROUP_SIZE)
        m2 = jnp.max(jnp.where(rows_g == i1, neg, v), axis=0, keepdims=True)
        gscores.append(m1 + m2)
    cur = jnp.concatenate(gscores, axis=0)
    rows_8 = lax.broadcasted_iota(jnp.int32, (N_GROUPS, t), 0)
    gsel = jnp.zeros((N_GROUPS, t), _F32)
    for _ in range(TOPK_GROUPS):
        _, gi = _first_argmax(cur, rows_8, N_GROUPS)
        hit = rows_8 == gi
        gsel = jnp.where(hit, 1.0, gsel)
        cur = jnp.where(hit, neg, cur)
    emask = jnp.concatenate(
        [jnp.broadcast_to(gsel[g:g + 1], (GROUP_SIZE, t)) for g in range(N_GROUPS)], axis=0) > 0.5
    masked = jnp.where(emask, biased, neg)

    rows_e = lax.broadcasted_iota(jnp.int32, (N_EXPERTS, t), 0)
    idxs, ws = [], []
    for _ in range(TOP_K):
        _, ei = _first_argmax(masked, rows_e, N_EXPERTS)
        hit = rows_e == ei
        idxs.append(ei)
        ws.append(jnp.sum(jnp.where(hit, scores, 0.0), axis=0, keepdims=True))
        masked = jnp.where(hit, neg, masked)
    wsum = ws[0]
    for wk in ws[1:]:
        wsum = wsum + wk

    onehot = jnp.zeros((N_EXPERTS, t), _F32)
    for ei in idxs:
        onehot = onehot + jnp.where(rows_e == ei, 1.0, 0.0)
    before = jnp.dot(onehot.astype(_BF16), upper_ref[...], preferred_element_type=_F32) + carry_ref[...]
    for kk in range(TOP_K):
        idx_ref[kk:kk + 1, :] = idxs[kk]
        w_ref[kk:kk + 1, :] = ws[kk] / wsum * ROUTED_SCALE
        rank_ref[kk:kk + 1, :] = jnp.sum(
            jnp.where(rows_e == idxs[kk], before, 0.0), axis=0, keepdims=True).astype(jnp.int32)
    total = carry_ref[...] + jnp.sum(onehot, axis=1, keepdims=True)
    carry_ref[...] = total
    cnt_ref[...] = jnp.broadcast_to(total, cnt_ref.shape)


def _post_attention(x, oT, mod, gmla, ggqa, woutT, gpost, gffn, wrh, wrl, br):
    upper = jnp.asarray(np.triu(np.ones((T_POST, T_POST), np.float32), 1), _BF16)
    b, s, d = x.shape
    n = b * s
    t = T_POST
    nt = s // t
    assert oT.shape[2:] == (nt, V_DIM, t)
    full = lambda a: pl.BlockSpec(a.shape, lambda bi, i: (0,) * a.ndim)
    tok = lambda bi, i: (0, bi * nt + i)
    return pl.pallas_call(
        _post_kernel,
        out_shape=(jax.ShapeDtypeStruct((b, s, d), _F32),
                   jax.ShapeDtypeStruct((n * PACK_ROWS, LANES), _U32),
                   jax.ShapeDtypeStruct((TOP_K, n), jnp.int32),
                   jax.ShapeDtypeStruct((TOP_K, n), _F32),
                   jax.ShapeDtypeStruct((TOP_K, n), jnp.int32),
                   jax.ShapeDtypeStruct((N_EXPERTS, LANES), _F32)),
        grid=(b, nt),
        in_specs=[pl.BlockSpec((1, t, d), lambda bi, i: (bi, i, 0)),
                  pl.BlockSpec((1, N_HEADS, 1, V_DIM, t), lambda bi, i: (bi, 0, i, 0, 0)),
                  pl.BlockSpec((1, N_MOD, d), lambda bi, i: (bi, 0, 0)),
                  full(gmla), full(ggqa), full(woutT), full(gpost), full(gffn),
                  full(wrh), full(wrl), full(br), full(upper)],
        out_specs=(pl.BlockSpec((1, t, d), lambda bi, i: (bi, i, 0)),
                   pl.BlockSpec((t * PACK_ROWS, LANES), lambda bi, i: (bi * nt + i, 0)),
                   pl.BlockSpec((TOP_K, t), tok),
                   pl.BlockSpec((TOP_K, t), tok),
                   pl.BlockSpec((TOP_K, t), tok),
                   pl.BlockSpec((N_EXPERTS, LANES), lambda bi, i: (0, 0))),
        scratch_shapes=[pltpu.VMEM((N_EXPERTS, 1), _F32)],
        compiler_params=pltpu.CompilerParams(
            dimension_semantics=("arbitrary", "arbitrary"), vmem_limit_bytes=VMEM_LIMIT),
        name="out_proj_router",
    )(x, oT, mod, gmla, ggqa, woutT, gpost, gffn, wrh, wrl, br, upper)


def _slot_kernel(idx_ref, rank_ref, start_ref, dest_ref):
    t = idx_ref.shape[1]
    rows_e = lax.broadcasted_iota(jnp.int32, (N_EXPERTS, t), 0)
    start = start_ref[...]
    for kk in range(TOP_K):
        base = jnp.sum(jnp.where(rows_e == idx_ref[kk:kk + 1, :], start, 0.0), axis=0, keepdims=True)
        dest_ref[kk:kk + 1, :] = rank_ref[kk:kk + 1, :] + base.astype(jnp.int32)


def _slots(idx_t, rank_t, start):
    n = idx_t.shape[1]
    t = T_POST
    return pl.pallas_call(
        _slot_kernel,
        out_shape=jax.ShapeDtypeStruct((TOP_K, n), jnp.int32),
        grid=(n // t,),
        in_specs=[pl.BlockSpec((TOP_K, t), lambda i: (0, i)),
                  pl.BlockSpec((TOP_K, t), lambda i: (0, i)),
                  pl.BlockSpec((N_EXPERTS, 1), lambda i: (0, 0))],
        out_specs=pl.BlockSpec((TOP_K, t), lambda i: (0, i)),
        compiler_params=pltpu.CompilerParams(dimension_semantics=("arbitrary",)),
        name="slot_index",
    )(idx_t, rank_t, start.astype(_F32).reshape(N_EXPERTS, 1))


def _row_copy(src, src_row, dst, dst_row, sem):
    return pltpu.make_async_copy(_token_rows(src, src_row), _token_rows(dst, dst_row), sem)


def _dispatch_kernel(pad_start_ref, pad_len_ref, nused_ref, dest_ref, h_ref, xs_ref, zbuf, sem, zsem, *,
                     n_blocks):
    t_rows = h_ref.shape[0] // PACK_ROWS

    def issue(t, _):
        for kk in range(TOP_K):
            _row_copy(h_ref, t, xs_ref, dest_ref[0, 0, t * TOP_K + kk], sem).start(priority=kk % 2)
        return 0

    lax.fori_loop(0, t_rows, issue, 0)

    def drain(t, _):
        for kk in range(TOP_K):
            _row_copy(h_ref, 0, xs_ref, 0, sem).wait()
        return 0

    lax.fori_loop(0, t_rows, drain, 0)

    @pl.when(pl.program_id(0) == pl.num_programs(0) - 1)
    def _():
        zbuf[...] = jnp.zeros_like(zbuf)

        def zero_copy(first, count):
            return pltpu.make_async_copy(_token_rows(zbuf, 0, count), _token_rows(xs_ref, first, count), zsem)

        def sweep(act):
            def expert_pad(e, _):
                first = pad_start_ref[e]
                for piece in [T_M >> sh for sh in range(1, T_M.bit_length())]:
                    has = pad_len_ref[e] & piece

                    @pl.when(has != 0)
                    def _():
                        act(zero_copy(first, piece))

                    first = first + has
                return 0

            lax.fori_loop(0, N_EXPERTS, expert_pad, 0)

            def spare_block(blk, _):
                act(zero_copy(blk * T_M, T_M))
                return 0

            lax.fori_loop(nused_ref[0], n_blocks, spare_block, 0)

        sweep(lambda c: c.start())
        sweep(lambda c: c.wait())


def _dispatch(pad_start, pad_len, nused, dest_tiles, hp, n_slots):
    n = hp.shape[0] // PACK_ROWS
    t = T_ROW
    return pl.pallas_call(
        functools.partial(_dispatch_kernel, n_blocks=n_slots // T_M),
        out_shape=jax.ShapeDtypeStruct((n_slots * PACK_ROWS, LANES), _U32),
        grid_spec=pltpu.PrefetchScalarGridSpec(
            num_scalar_prefetch=3,
            grid=(n // t,),
            in_specs=[pl.BlockSpec((1, 1, t * TOP_K), lambda i, ps, pn, nu: (i, 0, 0), memory_space=pltpu.SMEM),
                      pl.BlockSpec((t * PACK_ROWS, LANES), lambda i, ps, pn, nu: (i, 0))],
            out_specs=pl.BlockSpec(memory_space=pl.ANY),
            scratch_shapes=[pltpu.VMEM((T_M * PACK_ROWS, LANES), _U32),
                            pltpu.SemaphoreType.DMA(()), pltpu.SemaphoreType.DMA(())]),
        compiler_params=pltpu.CompilerParams(
            dimension_semantics=("arbitrary",), vmem_limit_bytes=VMEM_LIMIT),
        name="dispatch",
    )(pad_start, pad_len, nused, dest_tiles, hp)


def _swiglu(xb, wgu, wd):
    gu = jnp.dot(xb, wgu, preferred_element_type=_F32)
    g, u = gu[:, :EXPERT_FF], gu[:, EXPERT_FF:]
    mid = (g * jax.nn.sigmoid(g) * u).astype(_BF16)
    return jnp.dot(mid, wd, preferred_element_type=_F32)


def _load_packed_bf16(ref, rows):
    return jnp.concatenate([p.astype(_BF16) for p in _load_packed(ref, 0, rows)], axis=1)


def _expert_kernel(nblk_ref, bstart_ref, nused_ref, xs_ref, wg_ref, wu_ref, wd_ref, ys_ref,
                   xbuf, ybuf, xsem, ysem, wgu_s, wd_s, *, n_blocks):
    e = pl.program_id(0)
    nb = nblk_ref[e]
    b0 = bstart_ref[e]
    nused = nused_ref[0]

    def x_copy(blk):
        slot = lax.rem(blk, X_RING)
        return pltpu.make_async_copy(_token_rows(xs_ref, blk * T_M, T_M), _token_rows(xbuf, slot * T_M, T_M),
                                     xsem.at[slot])

    def y_copy(blk):
        slot = lax.rem(blk, Y_RING)
        return pltpu.make_async_copy(_token_rows(ybuf, slot * T_M, T_M), _token_rows(ys_ref, blk * T_M, T_M),
                                     ysem.at[slot])

    @pl.when(e == 0)
    def _():
        for g in range(X_AHEAD):
            @pl.when(g < nused)
            def _():
                x_copy(g).start()

    @pl.when(nb > 0)
    def _():
        wgu_s[:, :EXPERT_FF] = wg_ref[0].astype(_BF16)
        wgu_s[:, EXPERT_FF:] = wu_ref[0].astype(_BF16)
        wd_s[...] = wd_ref[0].astype(_BF16)

        def begin(g):
            x_copy(g).wait()

            @pl.when(g + X_AHEAD < nused)
            def _():
                x_copy(g + X_AHEAD).start(priority=1)

            @pl.when(g >= Y_RING)
            def _():
                y_copy(g - Y_RING).wait()

        def compute(g):
            xb = jnp.concatenate(
                [p.astype(_BF16) for p in _load_packed(xbuf, lax.rem(g, X_RING) * T_M, T_M)], axis=1)
            return _swiglu(xb, wgu_s[...], wd_s[...])

        def finish(g, y):
            _store_packed(ybuf, y, lax.rem(g, Y_RING) * T_M)
            y_copy(g).start(priority=1)

        def pair(jj, _):
            g = b0 + 2 * jj
            begin(g)
            begin(g + 1)
            y0, y1 = compute(g), compute(g + 1)
            finish(g, y0)
            finish(g + 1, y1)
            return 0

        lax.fori_loop(0, nb // 2, pair, 0)

        @pl.when(lax.rem(nb, 2) == 1)
        def _():
            g = b0 + nb - 1
            begin(g)
            finish(g, compute(g))

    @pl.when(e == pl.num_programs(0) - 1)
    def _():
        for back in range(Y_RING, 0, -1):
            @pl.when(nused >= back)
            def _():
                y_copy(nused - back).wait()

        ybuf[pl.ds(0, T_M * PACK_ROWS), :] = jnp.zeros((T_M * PACK_ROWS, LANES), _U32)

        def zero_copy(blk):
            return pltpu.make_async_copy(_token_rows(ybuf, 0, T_M), _token_rows(ys_ref, blk * T_M, T_M), ysem.at[0])

        def fill(blk, _):
            zero_copy(blk).start()
            return 0

        def drain(blk, _):
            zero_copy(blk).wait()
            return 0

        lax.fori_loop(nused, n_blocks, fill, 0)
        lax.fori_loop(nused, n_blocks, drain, 0)


def _experts(nblk, bstart, nused, xs, wg, wu, wd):
    n_slots = xs.shape[0] // PACK_ROWS
    n_exp, d, ff = wg.shape
    ring = lambda depth: pltpu.VMEM((depth * T_M * PACK_ROWS, LANES), _U32)
    return pl.pallas_call(
        functools.partial(_expert_kernel, n_blocks=n_slots // T_M),
        out_shape=jax.ShapeDtypeStruct(xs.shape, _U32),
        grid_spec=pltpu.PrefetchScalarGridSpec(
            num_scalar_prefetch=3,
            grid=(n_exp,),
            in_specs=[pl.BlockSpec(memory_space=pl.ANY),
                      pl.BlockSpec((1, d, ff), lambda i, nbk, bst, nu: (i, 0, 0)),
                      pl.BlockSpec((1, d, ff), lambda i, nbk, bst, nu: (i, 0, 0)),
                      pl.BlockSpec((1, ff, d), lambda i, nbk, bst, nu: (i, 0, 0))],
            out_specs=pl.BlockSpec(memory_space=pl.ANY),
            scratch_shapes=[ring(X_RING), ring(Y_RING),
                            pltpu.SemaphoreType.DMA((X_RING,)), pltpu.SemaphoreType.DMA((Y_RING,)),
                            pltpu.VMEM((d, 2 * ff), _BF16), pltpu.VMEM((ff, d), _BF16)]),
        compiler_params=pltpu.CompilerParams(
            dimension_semantics=("arbitrary",), vmem_limit_bytes=VMEM_LIMIT),
        name="experts",
    )(nblk, bstart, nused, xs, wg, wu, wd)


def _combine_kernel(dest_ref, dest_next_ref, ys_ref, hp_ref, w_ref, x1_ref, mod_ref, wgus_ref, wds_ref,
                    gpost_ref, o_ref, gbuf, sem):
    i = pl.program_id(0)
    t_rows = hp_ref.shape[0] // PACK_ROWS
    tile_rows = TOP_K * t_rows

    def gather(d_ref, slot):
        def issue(t, _):
            for kk in range(TOP_K):
                _row_copy(ys_ref, d_ref[0, 0, t * TOP_K + kk], gbuf, slot * tile_rows + kk * t_rows + t,
                          sem.at[slot]).start(priority=kk % 2)
            return 0

        lax.fori_loop(0, t_rows, issue, 0)

    slot = lax.rem(i, 2)

    @pl.when(i == 0)
    def _():
        gather(dest_ref, 0)

    @pl.when(i + 1 < pl.num_programs(0))
    def _():
        gather(dest_next_ref, 1 - slot)

    shared = _swiglu(_load_packed_bf16(hp_ref, t_rows), wgus_ref[...], wds_ref[...])

    def drain(t, _):
        for kk in range(TOP_K):
            _row_copy(ys_ref, 0, gbuf, 0, sem.at[slot]).wait()
        return 0

    lax.fori_loop(0, t_rows, drain, 0)

    w = w_ref[...]
    cols = None
    for kk in range(TOP_K):
        wk = w[:, kk:kk + 1]
        parts = [wk * p for p in _load_packed(gbuf, slot * tile_rows + kk * t_rows, t_rows)]
        cols = parts if cols is None else [a + b for a, b in zip(cols, parts)]
    y = jnp.concatenate(cols, axis=1) + shared
    gate_f = mod_ref[0, 5:6, :]
    o_ref[...] = x1_ref[...] + (y * _rsqrt_mean_sq(y, -1)) * (gate_f * gpost_ref[...])


def _combine(dest_tiles, ys, hp, w_tok, x1, mod, wgus, wds, gpost, tiles_per_batch):
    n, d = x1.shape
    t = T_ROW
    nt = n // t
    full = lambda a: pl.BlockSpec(a.shape, lambda i: (0,) * a.ndim)
    dest_spec = lambda shift: pl.BlockSpec(
        (1, 1, t * TOP_K), lambda i: (jnp.minimum(i + shift, nt - 1), 0, 0), memory_space=pltpu.SMEM)
    return pl.pallas_call(
        _combine_kernel,
        out_shape=jax.ShapeDtypeStruct((n, d), _F32),
        grid=(nt,),
        in_specs=[dest_spec(0), dest_spec(1),
                  pl.BlockSpec(memory_space=pl.ANY),
                  pl.BlockSpec((t * PACK_ROWS, LANES), lambda i: (i, 0)),
                  pl.BlockSpec((t, TOP_K), lambda i: (i, 0)),
                  pl.BlockSpec((t, d), lambda i: (i, 0)),
                  pl.BlockSpec((1, N_MOD, d), lambda i: (i // tiles_per_batch, 0, 0)),
                  full(wgus), full(wds), full(gpost)],
        out_specs=pl.BlockSpec((t, d), lambda i: (i, 0)),
        scratch_shapes=[pltpu.VMEM((2 * TOP_K * t * PACK_ROWS, LANES), _U32), pltpu.SemaphoreType.DMA((2,))],
        compiler_params=pltpu.CompilerParams(
            dimension_semantics=("arbitrary",), vmem_limit_bytes=VMEM_LIMIT),
        name="combine",
    )(dest_tiles, dest_tiles, ys, hp, w_tok, x1, mod, wgus, wds, gpost)


def _deinterleave(n):
    return np.concatenate([np.arange(0, n, 2), np.arange(1, n, 2)])


def _rope_tables(s, dim):
    rows = s // GRID_W
    row = jnp.broadcast_to(jnp.arange(rows, dtype=_F32)[:, None], (rows, GRID_W)).reshape(s)
    col = jnp.broadcast_to(jnp.arange(GRID_W, dtype=_F32)[None, :], (rows, GRID_W)).reshape(s)
    n_freq = dim // 4
    inv_freq = ROPE_THETA ** (-jnp.arange(n_freq, dtype=_F32) / n_freq)
    ang = jnp.concatenate([row[:, None] * inv_freq, col[:, None] * inv_freq], axis=-1)
    return jnp.cos(ang).T, jnp.sin(ang).T


def _in_proj_perm():
    perm = list(range(OFF_KROPE))
    perm += list(OFF_KROPE + _deinterleave(MLA_ROPE))
    for hd in range(GQA_HEADS):
        perm += list(OFF_QG + hd * GQA_HEAD_DIM + _deinterleave(GQA_HEAD_DIM))
    for hd in range(GQA_KV_HEADS):
        perm += list(OFF_KG + hd * GQA_HEAD_DIM + _deinterleave(GQA_HEAD_DIM))
    perm += list(range(OFF_VG, IN_COLS))
    return np.asarray(perm)


def _q_b_perm():
    per = MLA_NOPE + MLA_ROPE
    nope = [hd * per + j for hd in range(MLA_HEADS) for j in range(MLA_NOPE)]
    even = [hd * per + MLA_NOPE + 2 * i for hd in range(MLA_HEADS) for i in range(MLA_ROPE // 2)]
    odd = [hd * per + MLA_NOPE + 2 * i + 1 for hd in range(MLA_HEADS) for i in range(MLA_ROPE // 2)]
    return np.asarray(nope + even + odd)


def _kv_b_perm():
    per = MLA_NOPE + MLA_V
    kk = [hd * per + j for hd in range(MLA_HEADS) for j in range(MLA_NOPE)]
    vv = [hd * per + MLA_NOPE + j for hd in range(MLA_HEADS) for j in range(MLA_V)]
    return np.asarray(kk + vv)


def _layer(x, c, w_ada, b_ada, g_pre_mix, g_post_mix, g_pre_ffn, g_post_ffn, w_in, g_q_a, w_q_b,
           g_kv_a, w_kv_b, g_q_head, g_k_head, g_mla_out, g_gqa_out, w_out, w_router, b_router,
           w_gate_e, w_up_e, w_down_e, w_gate_s, w_up_s, w_down_s):
    b, s, d = x.shape
    n = b * s
    col = lambda g: g.reshape(-1, 1).astype(_F32)
    row = lambda g: g.reshape(1, -1).astype(_F32)

    mod = _ada_mod(c, w_ada, b_ada)

    winT = w_in[:, _in_proj_perm()].T.astype(_BF16)
    wqbT = w_q_b[:, _q_b_perm()].T.astype(_BF16)
    wkvbT = w_kv_b[:, _kv_b_perm()].T.astype(_BF16)
    perm_h = _deinterleave(GQA_HEAD_DIM)
    cosm, sinm = _rope_tables(s, MLA_ROPE)
    cosg, sing = _rope_tables(s, GQA_HEAD_DIM)

    qT, k, vT = _project(x, mod, row(g_pre_mix), winT, col(g_q_a), wqbT, col(g_kv_a), wkvbT,
                         col(g_q_head[perm_h]), col(g_k_head[perm_h]), cosm, sinm, cosg, sing)
    oT = _attention(qT, k, vT)

    wr_t = w_router.T.astype(_F32)
    wrh = wr_t.astype(_BF16)
    wrl = (wr_t - wrh.astype(_F32)).astype(_BF16)
    x1, hp, idx_t, w_t, rank_t, cnt = _post_attention(
        x, oT, mod, col(g_mla_out), col(g_gqa_out), w_out.T.astype(_BF16), row(g_post_mix),
        row(g_pre_ffn), wrh, wrl, col(b_router))

    counts = cnt[:, 0].astype(jnp.int32)
    blocks_e = (counts + T_M - 1) // T_M
    blk_end = jnp.cumsum(blocks_e)
    start = (blk_end - blocks_e) * T_M
    n_blocks = n * TOP_K // T_M + N_EXPERTS
    nused = blk_end[-1:].astype(jnp.int32)
    bstart = (blk_end - blocks_e).astype(jnp.int32)
    dest = _slots(idx_t, rank_t, start)
    dest_tiles = dest.T.reshape(n // T_ROW, 1, T_ROW * TOP_K)

    xs = _dispatch((start + counts).astype(jnp.int32), (blocks_e * T_M - counts).astype(jnp.int32), nused,
                   dest_tiles, hp, n_blocks * T_M)
    ys = _experts(blocks_e.astype(jnp.int32), bstart, nused, xs, w_gate_e, w_up_e, w_down_e)
    wgus = jnp.concatenate([w_gate_s, w_up_s], axis=1).astype(_BF16)
    out = _combine(dest_tiles, ys, hp, w_t.T, x1.reshape(n, d), mod, wgus, w_down_s.astype(_BF16),
                   row(g_post_ffn), s // T_ROW)
    return out.reshape(b, s, d)


def kernel(x, c, w_ada, b_ada, g_pre_mix, g_post_mix, g_pre_ffn, g_post_ffn, w_in, g_q_a, w_q_b, g_kv_a, w_kv_b, g_q_head, g_k_head, g_mla_out, g_gqa_out, w_out, w_router, b_router, w_gate_e, w_up_e, w_down_e, w_gate_s, w_up_s, w_down_s):
    depth = w_ada.shape[0]
    for l in range(depth):
        x = _layer(x, c, w_ada[l], b_ada[l], g_pre_mix[l], g_post_mix[l], g_pre_ffn[l], g_post_ffn[l],
                   w_in[l], g_q_a[l], w_q_b[l], g_kv_a[l], w_kv_b[l], g_q_head[l], g_k_head[l],
                   g_mla_out[l], g_gqa_out[l], w_out[l], w_router[l], b_router[l], w_gate_e[l],
                   w_up_e[l], w_down_e[l], w_gate_s[l], w_up_s[l], w_down_s[l])
    return x
```
